```python
import math
import jax, jax.numpy as jnp
from jax import lax
import numpy as np

D_MODEL = 2048
BATCH = 2
SEQ = 4096
DEPTH = 4
DEC_BATCH = 32
DEC_SEQ = 4
PAST_LEN = 16384
PAGE_SIZE = 128

N_MIXERS = 3
N_RET_LAYERS = (DEPTH + 2) // 3
N_CONV_LAYERS = (DEPTH + 1) // 3
N_ATT_LAYERS = DEPTH // 3

RET_HEADS = 8
RET_DK = D_MODEL // RET_HEADS
RET_DV = 2 * D_MODEL // RET_HEADS
RET_CHUNK = 128
ROPE_BASE = 10000.0

CONV_WIDTH = 31
CONV_STATE = CONV_WIDTH - 1

ATT_Q_HEADS = 32
ATT_KV_HEADS = 4
ATT_HEAD_DIM = D_MODEL // ATT_Q_HEADS
ATT_GROUP = ATT_Q_HEADS // ATT_KV_HEADS
WINDOW = 128
ATT_BLOCK = WINDOW
REL_BUCKETS = 32
REL_MAX_DIST = 128

MOE_GROUPS = 4
MOE_EXPERTS_PER_GROUP = 4
MOE_EXPERTS = MOE_GROUPS * MOE_EXPERTS_PER_GROUP
MOE_TOPK = 2
MOE_FF = D_MODEL // 4

NORM_EPS = 1e-6
NEG_INF = -1e30

kernel_name = 'hybrid_retention_conformer_swa_hmoe_step'


def rms_norm(x, gain):
    xf = x.astype(jnp.float32)
    y = xf * lax.rsqrt(jnp.mean(xf * xf, axis=-1, keepdims=True) + NORM_EPS)
    return (y * gain.astype(jnp.float32)).astype(x.dtype)


def layer_norm_f32(x, gain, bias):
    xf = x.astype(jnp.float32)
    mu = jnp.mean(xf, axis=-1, keepdims=True)
    var = jnp.mean(jnp.square(xf - mu), axis=-1, keepdims=True)
    return (xf - mu) * lax.rsqrt(var + NORM_EPS) * gain.astype(jnp.float32) + bias.astype(jnp.float32)


def ada_modulation(c, w, b):
    m = jax.nn.silu(c) @ w + b
    return [t[:, None, :] for t in jnp.split(m, 6, axis=-1)]


def modulated_norm(x, gain, shift, scale):
    return rms_norm(x, gain) * (1 + scale) + shift


def rotary(x, pos):
    half = x.shape[-1] // 2
    inv = ROPE_BASE ** (-jnp.arange(half, dtype=jnp.float32) / half)
    ang = pos.astype(jnp.float32)[:, None] * inv[None, :]
    cos = jnp.cos(ang)[None, :, None, :]
    sin = jnp.sin(ang)[None, :, None, :]
    x1, x2 = x[..., :half], x[..., half:]
    return jnp.concatenate([x1 * cos - x2 * sin, x1 * sin + x2 * cos], axis=-1)


def retention_chunk(state, q, k, v, log_gamma):
    L = q.shape[1]
    idx = jnp.arange(L, dtype=jnp.float32)
    diff = idx[:, None] - idx[None, :]
    decay = jnp.where(diff[None] >= 0, jnp.exp(jnp.maximum(diff, 0.0)[None] * log_gamma[:, None, None]), 0.0)
    scores = jnp.einsum('blhd,bmhd->bhlm', q, k) * decay[None]
    out = jnp.einsum('bhlm,bmhv->blhv', scores, v)
    q_decay = jnp.exp((idx[:, None] + 1.0) * log_gamma[None, :])
    out = out + jnp.einsum('blhd,bhdv->blhv', q * q_decay[None, :, :, None], state)
    k_decay = jnp.exp((L - 1.0 - idx)[:, None] * log_gamma[None, :])
    new_state = (jnp.exp(L * log_gamma)[None, :, None, None] * state
                 + jnp.einsum('blhd,blhv->bhdv', k * k_decay[None, :, :, None], v))
    return new_state, out


def retention_mixer(h, pos, state, w_in, gn_gain, gn_bias, w_out, chunked):
    B, L, _ = h.shape
    hk, hv = RET_HEADS * RET_DK, RET_HEADS * RET_DV
    q, k, v, g = jnp.split(h @ w_in, [hk, 2 * hk, 2 * hk + hv], axis=-1)
    q = rotary(q.reshape(B, L, RET_HEADS, RET_DK).astype(jnp.float32), pos)
    k = rotary(k.reshape(B, L, RET_HEADS, RET_DK).astype(jnp.float32), pos) * (RET_DK ** -0.5)
    v = v.reshape(B, L, RET_HEADS, RET_DV).astype(jnp.float32)
    log_gamma = jnp.log1p(-jnp.exp2(-5.0 - jnp.arange(RET_HEADS, dtype=jnp.float32)))
    if chunked:
        n = L // RET_CHUNK
        to_chunks = lambda t: jnp.moveaxis(t.reshape(B, n, RET_CHUNK, *t.shape[2:]), 1, 0)
        new_state, o = lax.scan(lambda s, qkv: retention_chunk(s, qkv[0], qkv[1], qkv[2], log_gamma),
                                state, (to_chunks(q), to_chunks(k), to_chunks(v)))
        o = jnp.moveaxis(o, 0, 1).reshape(B, L, RET_HEADS, RET_DV)
    else:
        new_state, o = retention_chunk(state, q, k, v, log_gamma)
    mu = jnp.mean(o, axis=-1, keepdims=True)
    var = jnp.mean(jnp.square(o - mu), axis=-1, keepdims=True)
    o = ((o - mu) * lax.rsqrt(var + NORM_EPS)).reshape(B, L, hv) * gn_gain.astype(jnp.float32) + gn_bias.astype(jnp.float32)
    y = (jax.nn.silu(g.astype(jnp.float32)) * o).astype(h.dtype) @ w_out
    return y, new_state


def conv_mixer(h, prev, w_pw1, b_pw1, w_dw, b_dw, ln_gain, ln_bias, w_pw2, b_pw2):
    a, b = jnp.split(h @ w_pw1 + b_pw1, 2, axis=-1)
    z = a * jax.nn.sigmoid(b)
    zpad = jnp.concatenate([prev.astype(z.dtype), z], axis=1)
    u = lax.conv_general_dilated(zpad, w_dw.astype(zpad.dtype)[:, None, :], (1,), 'VALID',
                                 dimension_numbers=('NWC', 'WIO', 'NWC'),
                                 feature_group_count=D_MODEL) + b_dw
    un = layer_norm_f32(u, ln_gain, ln_bias)
    y = jax.nn.silu(un).astype(h.dtype) @ w_pw2 + b_pw2
    return y, zpad[:, -CONV_STATE:]


def t5_bucket(dist):
    n = jnp.maximum(dist, 0)
    max_exact = REL_BUCKETS // 2
    nf = jnp.maximum(n, 1).astype(jnp.float32)
    large = max_exact + (jnp.log(nf / max_exact) / math.log(REL_MAX_DIST / max_exact)
                         * (REL_BUCKETS - max_exact)).astype(jnp.int32)
    large = jnp.minimum(large, REL_BUCKETS - 1)
    return jnp.where(n < max_exact, n, large)


def sink_attention(q, k, v, qpos, kpos, rel_bias, sinks):
    N, Q, S = qpos.shape[0], qpos.shape[1], kpos.shape[1]
    s = jnp.einsum('bnqhgd,bnshd->bnhgqs', q, k).astype(jnp.float32) * (ATT_HEAD_DIM ** -0.5)
    dist = qpos[:, :, None] - kpos[:, None, :]
    bias = rel_bias.astype(jnp.float32)[t5_bucket(dist)]
    bias = jnp.moveaxis(bias, -1, 1).reshape(N, ATT_KV_HEADS, ATT_GROUP, Q, S)
    valid = (kpos[:, None, :] >= 0) & (dist >= 0) & (dist < WINDOW)
    s = jnp.where(valid[None, :, None, None], s + bias[None], NEG_INF)
    sink = sinks.astype(jnp.float32).reshape(1, 1, ATT_KV_HEADS, ATT_GROUP, 1, 1)
    m = jnp.maximum(jnp.max(s, axis=-1, keepdims=True), sink)
    p = jnp.exp(s - m)
    p = p / (jnp.sum(p, axis=-1, keepdims=True) + jnp.exp(sink - m))
    return jnp.einsum('bnhgqs,bnshd->bnqhgd', p.astype(v.dtype), v)


def swa_project(h, w_qkv, q_gain, k_gain):
    B, L, _ = h.shape
    nq, nkv = ATT_Q_HEADS * ATT_HEAD_DIM, ATT_KV_HEADS * ATT_HEAD_DIM
    q, k, v = jnp.split(h @ w_qkv, [nq, nq + nkv], axis=-1)
    q = rms_norm(q.reshape(B, L, ATT_KV_HEADS, ATT_GROUP, ATT_HEAD_DIM), q_gain)
    k = rms_norm(k.reshape(B, L, ATT_KV_HEADS, ATT_HEAD_DIM), k_gain)
    v = v.reshape(B, L, ATT_KV_HEADS, ATT_HEAD_DIM)
    return q, k, v


def swa_mixer_prompt(h, w_qkv, q_gain, k_gain, w_o, rel_bias, sinks):
    B, L, _ = h.shape
    q, k, v = swa_project(h, w_qkv, q_gain, k_gain)
    nb = L // ATT_BLOCK
    blocks = lambda t: t.reshape(B, nb, ATT_BLOCK, *t.shape[2:])

    def banded(t):
        prev = jnp.concatenate([jnp.zeros_like(t[:, :ATT_BLOCK]), t[:, :-ATT_BLOCK]], axis=1)
        return jnp.concatenate([blocks(prev), blocks(t)], axis=2)

    start = jnp.arange(nb, dtype=jnp.int32)[:, None] * ATT_BLOCK
    qpos = start + jnp.arange(ATT_BLOCK, dtype=jnp.int32)[None, :]
    kpos = start - ATT_BLOCK + jnp.arange(2 * ATT_BLOCK, dtype=jnp.int32)[None, :]
    o = sink_attention(blocks(q), banded(k), banded(v), qpos, kpos, rel_bias, sinks)
    y = o.reshape(B, L, D_MODEL).astype(h.dtype) @ w_o
    return y, k[:, -WINDOW:], v[:, -WINDOW:]


def swa_mixer_sample(h, cache_k, cache_v, w_qkv, q_gain, k_gain, w_o, rel_bias, sinks):
    B, L, _ = h.shape
    q, k, v = swa_project(h, w_qkv, q_gain, k_gain)
    kk = jnp.concatenate([cache_k.astype(k.dtype), k], axis=1)
    vv = jnp.concatenate([cache_v.astype(v.dtype), v], axis=1)
    sc = cache_k.shape[1]
    qpos = (PAST_LEN + jnp.arange(L, dtype=jnp.int32))[None, :]
    kpos = (PAST_LEN - sc + jnp.arange(sc + L, dtype=jnp.int32))[None, :]
    o = sink_attention(q[:, None], kk[:, None], vv[:, None], qpos, kpos, rel_bias, sinks)
    y = o.reshape(B, L, D_MODEL).astype(h.dtype) @ w_o
    return y, kk[:, -sc:], vv[:, -sc:]


def hier_moe(h, wg, bg, we, be, w_in, w_out):
    B, L, _ = h.shape
    hf = h.reshape(B * L, D_MODEL)
    pg = jax.nn.softmax((hf @ wg + bg).astype(jnp.float32), axis=-1)
    p_grp, grp = lax.top_k(pg, 1)
    le = (hf @ we + be).astype(jnp.float32).reshape(-1, MOE_GROUPS, MOE_EXPERTS_PER_GROUP)
    le = jnp.take_along_axis(le, grp[:, :, None], axis=1)[:, 0]
    p_exp, e_loc = lax.top_k(jax.nn.softmax(le, axis=-1), MOE_TOPK)
    weights = p_grp * p_exp / jnp.sum(p_exp, axis=-1, keepdims=True)
    expert = grp * MOE_EXPERTS_PER_GROUP + e_loc
    combine = jnp.einsum('nk,nke->ne', weights, jax.nn.one_hot(expert, MOE_EXPERTS, dtype=jnp.float32))
    gate, up = jnp.split(jnp.einsum('nd,edf->nef', hf, w_in), 2, axis=-1)
    act = jax.nn.silu(gate) * up * combine[..., None].astype(h.dtype)
    y = jnp.einsum('nef,efd->nd', act, w_out)
    return y.reshape(B, L, D_MODEL)


def setup_inputs(seed: int = 0) -> dict:
    key = jax.random.key(seed)
    keys = iter(jax.random.split(key, 48))
    D = D_MODEL

    def nrm(shape, scale):
        return scale * jax.random.normal(next(keys), shape, jnp.float32)

    ret_in = 2 * RET_HEADS * RET_DK + 2 * RET_HEADS * RET_DV
    ret_v = RET_HEADS * RET_DV
    att_in = (ATT_Q_HEADS + 2 * ATT_KV_HEADS) * ATT_HEAD_DIM
    win = min(WINDOW, PAST_LEN)
    return {
        'x_prompt': nrm((BATCH, SEQ, D), 1.0),
        'x_sample': nrm((DEC_BATCH, DEC_SEQ, D), 1.0),
        'c_prompt': nrm((BATCH, D), 1.0),
        'c_sample': nrm((DEC_BATCH, D), 1.0),
        'state_ret': nrm((N_RET_LAYERS, DEC_BATCH, RET_HEADS, RET_DK, RET_DV), 0.05),
        'cache_conv': nrm((N_CONV_LAYERS, DEC_BATCH, CONV_STATE, D), 0.5),
        'cache_swa_k': nrm((N_ATT_LAYERS, DEC_BATCH, win, ATT_KV_HEADS, ATT_HEAD_DIM), 1.0),
        'cache_swa_v': nrm((N_ATT_LAYERS, DEC_BATCH, win, ATT_KV_HEADS, ATT_HEAD_DIM), 1.0),
        'ada_w': nrm((DEPTH, D, 6 * D), 0.5 * D ** -0.5),
        'ada_b': nrm((DEPTH, 6 * D), 0.02),
        'norm_gain': 1.0 + nrm((DEPTH, 2, D), 0.02),
        'ret_w_in': nrm((N_RET_LAYERS, D, ret_in), D ** -0.5),
        'ret_gn_gain': 1.0 + nrm((N_RET_LAYERS, ret_v), 0.02),
        'ret_gn_bias': nrm((N_RET_LAYERS, ret_v), 0.02),
        'ret_w_out': nrm((N_RET_LAYERS, ret_v, D), ret_v ** -0.5),
        'conv_w_pw1': nrm((N_CONV_LAYERS, D, 2 * D), D ** -0.5),
        'conv_b_pw1': nrm((N_CONV_LAYERS, 2 * D), 0.02),
        'conv_w_dw': nrm((N_CONV_LAYERS, CONV_WIDTH, D), CONV_WIDTH ** -0.5),
        'conv_b_dw': nrm((N_CONV_LAYERS, D), 0.02),
        'conv_ln_gain': 1.0 + nrm((N_CONV_LAYERS, D), 0.02),
        'conv_ln_bias': nrm((N_CONV_LAYERS, D), 0.02),
        'conv_w_pw2': nrm((N_CONV_LAYERS, D, D), D ** -0.5),
        'conv_b_pw2': nrm((N_CONV_LAYERS, D), 0.02),
        'att_w_qkv': nrm((N_ATT_LAYERS, D, att_in), D ** -0.5),
        'att_q_gain': 1.0 + nrm((N_ATT_LAYERS, ATT_HEAD_DIM), 0.02),
        'att_k_gain': 1.0 + nrm((N_ATT_LAYERS, ATT_HEAD_DIM), 0.02),
        'att_sinks': nrm((N_ATT_LAYERS, ATT_Q_HEADS), 0.5),
        'att_w_o': nrm((N_ATT_LAYERS, D, D), D ** -0.5),
        'rel_bias': nrm((REL_BUCKETS, ATT_Q_HEADS), 0.5),
        'moe_wg': nrm((DEPTH, D, MOE_GROUPS), D ** -0.5),
        'moe_bg': nrm((DEPTH, MOE_GROUPS), 0.01),
        'moe_we': nrm((DEPTH, D, MOE_EXPERTS), D ** -0.5),
        'moe_be': nrm((DEPTH, MOE_EXPERTS), 0.01),
        'moe_w_in': nrm((DEPTH, MOE_EXPERTS, D, 2 * MOE_FF), D ** -0.5),
        'moe_w_out': nrm((DEPTH, MOE_EXPERTS, MOE_FF, D), MOE_FF ** -0.5),
    }


def reference(x_prompt, x_sample, c_prompt, c_sample, state_ret, cache_conv, cache_swa_k, cache_swa_v,
              ada_w, ada_b, norm_gain, ret_w_in, ret_gn_gain, ret_gn_bias, ret_w_out,
              conv_w_pw1, conv_b_pw1, conv_w_dw, conv_b_dw, conv_ln_gain, conv_ln_bias, conv_w_pw2, conv_b_pw2,
              att_w_qkv, att_q_gain, att_k_gain, att_sinks, att_w_o, rel_bias,
              moe_wg, moe_bg, moe_we, moe_be, moe_w_in, moe_w_out):
    Bp, Lp, _ = x_prompt.shape
    Ls = x_sample.shape[1]
    pos_p = jnp.arange(Lp, dtype=jnp.int32)
    pos_s = PAST_LEN + jnp.arange(Ls, dtype=jnp.int32)
    xp, xs = x_prompt, x_sample
    ret_p, ret_s, conv_p, conv_s = [], [], [], []
    kp_l, vp_l, ks_l, vs_l = [], [], [], []
    for l in range(DEPTH):
        kind, j = l % N_MIXERS, l // N_MIXERS
        sh1p, sc1p, g1p, sh2p, sc2p, g2p = ada_modulation(c_prompt, ada_w[l], ada_b[l])
        sh1s, sc1s, g1s, sh2s, sc2s, g2s = ada_modulation(c_sample, ada_w[l], ada_b[l])
        hp = modulated_norm(xp, norm_gain[l, 0], sh1p, sc1p)
        hs = modulated_norm(xs, norm_gain[l, 0], sh1s, sc1s)
        if kind == 0:
            zero_state = jnp.zeros((Bp, RET_HEADS, RET_DK, RET_DV), jnp.float32)
            yp, sp = retention_mixer(hp, pos_p, zero_state, ret_w_in[j], ret_gn_gain[j], ret_gn_bias[j], ret_w_out[j], True)
            ys, ss = retention_mixer(hs, pos_s, state_ret[j].astype(jnp.float32), ret_w_in[j], ret_gn_gain[j], ret_gn_bias[j], ret_w_out[j], False)
            ret_p.append(sp)
            ret_s.append(ss)
        elif kind == 1:
            conv_args = (conv_w_pw1[j], conv_b_pw1[j], conv_w_dw[j], conv_b_dw[j], conv_ln_gain[j], conv_ln_bias[j], conv_w_pw2[j], conv_b_pw2[j])
            yp, cp = conv_mixer(hp, jnp.zeros((Bp, CONV_STATE, D_MODEL), hp.dtype), *conv_args)
            ys, cs = conv_mixer(hs, cache_conv[j], *conv_args)
            conv_p.append(cp)
            conv_s.append(cs)
        else:
            yp, kp, vp = swa_mixer_prompt(hp, att_w_qkv[j], att_q_gain[j], att_k_gain[j], att_w_o[j], rel_bias, att_sinks[j])
            ys, ks, vs = swa_mixer_sample(hs, cache_swa_k[j], cache_swa_v[j], att_w_qkv[j], att_q_gain[j], att_k_gain[j], att_w_o[j], rel_bias, att_sinks[j])
            kp_l.append(kp)
            vp_l.append(vp)
            ks_l.append(ks)
            vs_l.append(vs)
        xp = xp + g1p * yp
        xs = xs + g1s * ys
        hp = modulated_norm(xp, norm_gain[l, 1], sh2p, sc2p)
        hs = modulated_norm(xs, norm_gain[l, 1], sh2s, sc2s)
        xp = xp + g2p * hier_moe(hp, moe_wg[l], moe_bg[l], moe_we[l], moe_be[l], moe_w_in[l], moe_w_out[l])
        xs = xs + g2s * hier_moe(hs, moe_wg[l], moe_bg[l], moe_we[l], moe_be[l], moe_w_in[l], moe_w_out[l])
    return (xp, xs, jnp.stack(ret_p), jnp.stack(ret_s), jnp.stack(conv_p), jnp.stack(conv_s),
            jnp.stack(kp_l), jnp.stack(vp_l), jnp.stack(ks_l), jnp.stack(vs_l))
```

```python
import functools
import math

import jax
import jax.numpy as jnp
from jax import lax
from jax.experimental import pallas as pl
from jax.experimental.pallas import tpu as pltpu

F32 = jnp.float32
BF16 = jnp.bfloat16

NORM_EPS = 1e-6
NEG_INF = -1e30
ROPE_BASE = 10000.0
PAST_LEN = 16384

RET_HEADS = 8
RET_CHUNK = 128
CONV_WIDTH = 31
CONV_STATE = CONV_WIDTH - 1
ATT_Q_HEADS = 32
ATT_KV_HEADS = 4
ATT_GROUP = ATT_Q_HEADS // ATT_KV_HEADS
WINDOW = 128
REL_BUCKETS = 32
REL_MAX_DIST = 128
MOE_GROUPS = 4
MOE_EPG = 4
MOE_TOPK = 2

ROW_TILE = 512
SAMPLE_ROWS = 16
ROUTER_LANES = 128
VMEM_LIMIT_BYTES = 56 * 1024 * 1024


def _cparams(sem):
    return pltpu.CompilerParams(dimension_semantics=sem, vmem_limit_bytes=VMEM_LIMIT_BYTES)


def _ada_body(c_ref, w_ref, b_ref, o_ref):
    c = c_ref[...]
    s = (c * jax.nn.sigmoid(c)).astype(BF16)
    o_ref[...] = jnp.dot(s, w_ref[...].astype(BF16), preferred_element_type=F32) + b_ref[...]


def _ada_call(c_all, ada_w, ada_b):
    depth, d, n = ada_w.shape
    rows = c_all.shape[0]
    tn = 1024
    return pl.pallas_call(
        _ada_body,
        out_shape=jax.ShapeDtypeStruct((depth, rows, n), F32),
        grid=(depth, n // tn),
        in_specs=[
            pl.BlockSpec((rows, d), lambda l, j: (0, 0)),
            pl.BlockSpec((None, d, tn), lambda l, j: (l, 0, j)),
            pl.BlockSpec((None, 1, tn), lambda l, j: (l, 0, j)),
        ],
        out_specs=pl.BlockSpec((None, rows, tn), lambda l, j: (l, 0, j)),
        compiler_params=_cparams(("arbitrary", "arbitrary")),
        name="ada_mod",
    )(c_all, ada_w, ada_b.reshape(depth, 1, n))


def _norm_body(*refs, n_prompt_tiles, has_resid, has_norm, has_router):
    it = iter(refs)
    x_ref = next(it)
    if has_resid:
        y_ref, grow_ref, gtok_ref = next(it), next(it), next(it)
    if has_norm:
        gain_ref, shrow_ref, scrow_ref, shtok_ref, sctok_ref = next(it), next(it), next(it), next(it), next(it)
    if has_router:
        wr_ref = next(it)
    if has_resid:
        xo_ref = next(it)
    if has_norm:
        h_ref = next(it)
    if has_router:
        lg_ref = next(it)

    is_sample = pl.program_id(0) >= n_prompt_tiles
    x = x_ref[...]
    if has_resid:
        gate = jnp.where(is_sample, gtok_ref[...], grow_ref[...])
        x = x + gate * y_ref[...].astype(F32)
        xo_ref[...] = x
    if has_norm:
        ms = jnp.mean(x * x, axis=-1, keepdims=True)
        xn = x * lax.rsqrt(ms + NORM_EPS) * gain_ref[...]
        scale = jnp.where(is_sample, sctok_ref[...], scrow_ref[...])
        shift = jnp.where(is_sample, shtok_ref[...], shrow_ref[...])
        h = xn * (1.0 + scale) + shift
        h_ref[...] = h.astype(BF16)
        if has_router:
            lg_ref[...] = jnp.dot(h, wr_ref[...], precision=lax.Precision.HIGHEST, preferred_element_type=F32)


def _norm_call(x, mod4, modtok, *, seq_tiles, n_batch, resid=None, norm=None, router_w=None):
    r, d = x.shape
    n_tiles = r // ROW_TILE
    n_prompt_tiles = n_tiles - 1

    def row_spec(layer, col):
        return pl.BlockSpec((None, None, 1, d),
                            lambda i: (layer, jnp.minimum(i // seq_tiles, n_batch - 1), 0, col))

    def tok_spec(layer, col):
        return pl.BlockSpec((None, ROW_TILE, d), lambda i: (layer, 0, col))

    tile = pl.BlockSpec((ROW_TILE, d), lambda i: (i, 0))
    args, in_specs, out_shape, out_specs = [x], [tile], [], []
    if resid is not None:
        y, layer, gcol = resid
        args += [y, mod4, modtok]
        in_specs += [tile, row_spec(layer, gcol), tok_spec(layer, gcol)]
        out_shape.append(jax.ShapeDtypeStruct((r, d), F32))
        out_specs.append(tile)
    if norm is not None:
        gain, layer, shcol, sccol = norm
        args += [gain, mod4, mod4, modtok, modtok]
        in_specs += [pl.BlockSpec((1, d), lambda i: (0, 0)), row_spec(layer, shcol), row_spec(layer, sccol),
                     tok_spec(layer, shcol), tok_spec(layer, sccol)]
        out_shape.append(jax.ShapeDtypeStruct((r, d), BF16))
        out_specs.append(tile)
        if router_w is not None:
            args.append(router_w)
            in_specs.append(pl.BlockSpec((d, ROUTER_LANES), lambda i: (0, 0)))
            out_shape.append(jax.ShapeDtypeStruct((r, ROUTER_LANES), F32))
            out_specs.append(pl.BlockSpec((ROW_TILE, ROUTER_LANES), lambda i: (i, 0)))
    body = functools.partial(_norm_body, n_prompt_tiles=n_prompt_tiles, has_resid=resid is not None,
                             has_norm=norm is not None, has_router=router_w is not None)
    return pl.pallas_call(
        body, out_shape=out_shape, grid=(n_tiles,), in_specs=in_specs, out_specs=out_specs,
        compiler_params=_cparams(("arbitrary",)), name="mod_norm",
    )(*args)


def _linear_body(*refs, n_prompt_tiles, pair, glu, has_bias, has_resid):
    it = iter(refs)
    h_ref = next(it)
    hs_ref = next(it) if pair else None
    w_ref = next(it)
    w2_ref = next(it) if glu else None
    b_ref = next(it) if has_bias else None
    b2_ref = next(it) if glu else None
    if has_resid:
        x_ref, grow_ref, gtok_ref = next(it), next(it), next(it)
    o_ref = next(it)
    wb = next(it)
    wb2 = next(it) if glu else None

    i = pl.program_id(1)
    is_sample = i >= n_prompt_tiles

    @pl.when(i == 0)
    def _():
        wb[...] = w_ref[...].astype(BF16)
        if glu:
            wb2[...] = w2_ref[...].astype(BF16)

    def compute(hv):
        acc = jnp.dot(hv, wb[...], preferred_element_type=F32)
        if has_bias:
            acc = acc + b_ref[...]
        if glu:
            acc2 = jnp.dot(hv, wb2[...], preferred_element_type=F32) + b2_ref[...]
            acc = acc * jax.nn.sigmoid(acc2)
        if has_resid:
            gate = jnp.where(is_sample, gtok_ref[...], grow_ref[...])
            acc = x_ref[...] + gate * acc
        o_ref[...] = acc.astype(o_ref.dtype)

    if pair:
        @pl.when(jnp.logical_not(is_sample))
        def _():
            compute(h_ref[...])

        @pl.when(is_sample)
        def _():
            compute(hs_ref[...])
    else:
        compute(h_ref[...])


def _linear_call(h, w, *, n_out, tn, out_dtype, seq_tiles, n_batch, h_sample=None, bias=None, glu=False,
                 resid=None, name="linear"):
    k = w.shape[0]
    pair = h_sample is not None
    n_prompt_tiles = h.shape[0] // ROW_TILE - (0 if pair else 1)
    n_tiles = n_prompt_tiles + 1
    r = n_tiles * ROW_TILE
    nblk = n_out // tn

    args = [h]
    in_specs = [pl.BlockSpec((ROW_TILE, k), lambda j, i: (jnp.minimum(i, n_prompt_tiles - 1) if pair else i, 0))]
    if pair:
        args.append(h_sample)
        in_specs.append(pl.BlockSpec((ROW_TILE, k), lambda j, i: (0, 0)))
    args.append(w)
    in_specs.append(pl.BlockSpec((k, tn), lambda j, i: (0, j)))
    if glu:
        args.append(w)
        in_specs.append(pl.BlockSpec((k, tn), lambda j, i: (0, nblk + j)))
    if bias is not None:
        args.append(bias)
        in_specs.append(pl.BlockSpec((1, tn), lambda j, i: (0, j)))
        if glu:
            args.append(bias)
            in_specs.append(pl.BlockSpec((1, tn), lambda j, i: (0, nblk + j)))
    if resid is not None:
        x, mod4, modtok, layer, gcol = resid
        cb = gcol * nblk
        args += [x, mod4, modtok]
        in_specs += [
            pl.BlockSpec((ROW_TILE, tn), lambda j, i: (i, j)),
            pl.BlockSpec((None, None, 1, tn),
                         lambda j, i: (layer, jnp.minimum(i // seq_tiles, n_batch - 1), 0, cb + j)),
            pl.BlockSpec((None, ROW_TILE, tn), lambda j, i: (layer, 0, cb + j)),
        ]
    scratch = [pltpu.VMEM((k, tn), BF16)] + ([pltpu.VMEM((k, tn), BF16)] if glu else [])
    body = functools.partial(_linear_body, n_prompt_tiles=n_prompt_tiles, pair=pair, glu=glu,
                             has_bias=bias is not None, has_resid=resid is not None)
    return pl.pallas_call(
        body,
        out_shape=jax.ShapeDtypeStruct((r, n_out), out_dtype),
        grid=(nblk, n_tiles),
        in_specs=in_specs,
        out_specs=pl.BlockSpec((ROW_TILE, tn), lambda j, i: (i, j)),
        scratch_shapes=scratch,
        compiler_params=_cparams(("arbitrary", "arbitrary")),
        name=name,
    )(*args)


def _rotate(x, cos, sin):
    half = x.shape[-1] // 2
    x1, x2 = x[:, :half], x[:, half:]
    return jnp.concatenate([x1 * cos - x2 * sin, x1 * sin + x2 * cos], axis=-1)


def _group_norm_gate(o, g, gain, bias):
    mu = jnp.mean(o, axis=-1, keepdims=True)
    var = jnp.mean(jnp.square(o - mu), axis=-1, keepdims=True)
    on = (o - mu) * lax.rsqrt(var + NORM_EPS) * gain + bias
    g = g.astype(F32)
    return (g * jax.nn.sigmoid(g) * on).astype(BF16)


def _ret_prompt_body(q_ref, k_ref, v_ref, g_ref, cos_ref, sin_ref, dm_ref, qd_ref, kd_ref, gl_ref,
                     gng_ref, gnb_ref, o_ref, st_ref, s_acc, *, n_chunks):
    s_acc[...] = jnp.zeros_like(s_acc)
    dmask = dm_ref[...]
    qdec = qd_ref[...]
    kdec = kd_ref[...]
    gl = gl_ref[0:1, 0:1]
    gng = gng_ref[...]
    gnb = gnb_ref[...]

    def chunk(c, carry):
        r0 = pl.multiple_of(c * RET_CHUNK, RET_CHUNK)
        rows = pl.ds(r0, RET_CHUNK)
        cos = cos_ref[rows, :]
        sin = sin_ref[rows, :]
        qr = _rotate(q_ref[rows, :].astype(F32), cos, sin)
        kr = _rotate(k_ref[rows, :].astype(F32), cos, sin)
        v = v_ref[rows, :]
        state = s_acc[...]
        scores = lax.dot_general(qr.astype(BF16), kr.astype(BF16), (((1,), (1,)), ((), ())),
                                 preferred_element_type=F32)
        scores = scores * dmask
        out = jnp.dot(scores.astype(BF16), v, preferred_element_type=F32)
        out = out + jnp.dot((qr * qdec).astype(BF16), state.astype(BF16), preferred_element_type=F32)
        kv = lax.dot_general((kr * kdec).astype(BF16), v, (((0,), (0,)), ((), ())), preferred_element_type=F32)
        s_acc[...] = gl * state + kv
        o_ref[rows, :] = _group_norm_gate(out, g_ref[rows, :], gng, gnb)
        return carry

    lax.fori_loop(0, n_chunks, chunk, 0)
    st_ref[...] = s_acc[...]


def _ret_tables(chunk, dk, n_valid=None):
    n_valid = chunk if n_valid is None else n_valid
    lg = jnp.log1p(-jnp.exp2(-5.0 - jnp.arange(RET_HEADS, dtype=F32)))
    idx = jnp.arange(chunk, dtype=F32)
    diff = idx[:, None] - idx[None, :]
    inside = (idx[:, None] < n_valid) & (idx[None, :] < n_valid)
    dmask = jnp.where((diff[None] >= 0) & inside[None],
                      jnp.exp(jnp.maximum(diff, 0.0)[None] * lg[:, None, None]), 0.0) * (dk ** -0.5)
    qdec = jnp.exp((idx[None, :] + 1.0) * lg[:, None])
    kdec = jnp.where(idx[None, :] < n_valid, jnp.exp((n_valid - 1.0 - idx)[None, :] * lg[:, None]), 0.0) * (dk ** -0.5)
    gl = jnp.exp(n_valid * lg)
    qdec = jnp.broadcast_to(qdec[:, :, None], (RET_HEADS, chunk, dk))
    kdec = jnp.broadcast_to(kdec[:, :, None], (RET_HEADS, chunk, dk))
    gl = jnp.broadcast_to(gl[:, None, None], (RET_HEADS, 8, 128))
    return dmask, qdec, kdec, gl


def _rope_tables(pos, half):
    inv = ROPE_BASE ** (-jnp.arange(half, dtype=F32) / half)
    ang = pos.astype(F32)[:, None] * inv[None, :]
    return jnp.cos(ang), jnp.sin(ang)


def _ret_prompt_call(proj, gn_gain, gn_bias, *, n_batch, seq, dk, dv):
    h = RET_HEADS
    n_chunks = seq // RET_CHUNK
    cos, sin = _rope_tables(jnp.arange(seq, dtype=jnp.int32), dk // 2)
    dmask, qdec, kdec, gl = _ret_tables(RET_CHUNK, dk)
    kcol, vcol, gcol = h, (2 * h * dk) // dv, (2 * h * dk) // dv + h
    body = functools.partial(_ret_prompt_body, n_chunks=n_chunks)
    return pl.pallas_call(
        body,
        out_shape=[jax.ShapeDtypeStruct((n_batch * seq, h * dv), BF16),
                   jax.ShapeDtypeStruct((n_batch, h, dk, dv), F32)],
        grid=(n_batch, h),
        in_specs=[
            pl.BlockSpec((seq, dk), lambda b, hh: (b, hh)),
            pl.BlockSpec((seq, dk), lambda b, hh: (b, kcol + hh)),
            pl.BlockSpec((seq, dv), lambda b, hh: (b, vcol + hh)),
            pl.BlockSpec((seq, dv), lambda b, hh: (b, gcol + hh)),
            pl.BlockSpec((seq, dk // 2), lambda b, hh: (0, 0)),
            pl.BlockSpec((seq, dk // 2), lambda b, hh: (0, 0)),
            pl.BlockSpec((None, RET_CHUNK, RET_CHUNK), lambda b, hh: (hh, 0, 0)),
            pl.BlockSpec((None, RET_CHUNK, dk), lambda b, hh: (hh, 0, 0)),
            pl.BlockSpec((None, RET_CHUNK, dk), lambda b, hh: (hh, 0, 0)),
            pl.BlockSpec((None, 8, 128), lambda b, hh: (hh, 0, 0)),
            pl.BlockSpec((1, dv), lambda b, hh: (0, hh)),
            pl.BlockSpec((1, dv), lambda b, hh: (0, hh)),
        ],
        out_specs=[pl.BlockSpec((seq, dv), lambda b, hh: (b, hh)),
                   pl.BlockSpec((None, None, dk, dv), lambda b, hh: (b, hh, 0, 0))],
        scratch_shapes=[pltpu.VMEM((dk, dv), F32)],
        compiler_params=_cparams(("arbitrary", "arbitrary")),
        name="retention_prompt",
    )(proj, proj, proj, proj, cos, sin, dmask, qdec, kdec, gl, gn_gain, gn_bias)


def _ret_sample_body(p_ref, st_ref, cos_ref, sin_ref, dm_ref, qd_ref, kd_ref, gl_ref, gng_ref, gnb_ref,
                     o_ref, so_ref, *, dk, dv):
    h = RET_HEADS
    cos = cos_ref[...]
    sin = sin_ref[...]
    pad = 128 - SAMPLE_ROWS
    for hh in range(h):
        q = p_ref[:, hh * dk:(hh + 1) * dk].astype(F32)
        k = p_ref[:, h * dk + hh * dk:h * dk + (hh + 1) * dk].astype(F32)
        v = p_ref[:, 2 * h * dk + hh * dv:2 * h * dk + (hh + 1) * dv]
        g = p_ref[:, 2 * h * dk + h * dv + hh * dv:2 * h * dk + h * dv + (hh + 1) * dv]
        qr = _rotate(q, cos, sin)
        kr = _rotate(k, cos, sin)
        k_pad = jnp.concatenate([kr.astype(BF16), jnp.zeros((pad, dk), BF16)], axis=0)
        kd_pad = jnp.concatenate([(kr * kd_ref[hh]).astype(BF16), jnp.zeros((pad, dk), BF16)], axis=0)
        v_pad = jnp.concatenate([v, jnp.zeros((pad, dv), BF16)], axis=0)
        state = st_ref[hh]
        scores = lax.dot_general(qr.astype(BF16), k_pad, (((1,), (1,)), ((), ())), preferred_element_type=F32)
        scores = scores * dm_ref[hh]
        out = jnp.dot(scores.astype(BF16), v_pad, preferred_element_type=F32)
        out = out + jnp.dot((qr * qd_ref[hh]).astype(BF16), state.astype(BF16), preferred_element_type=F32)
        kv = lax.dot_general(kd_pad, v_pad, (((0,), (0,)), ((), ())), preferred_element_type=F32)
        so_ref[hh] = gl_ref[hh, 0:1, 0:1] * state + kv
        o_ref[:, hh * dv:(hh + 1) * dv] = _group_norm_gate(out, g, gng_ref[:, hh * dv:(hh + 1) * dv],
                                                           gnb_ref[:, hh * dv:(hh + 1) * dv])


def _ret_sample_call(proj, state, gn_gain, gn_bias, *, n_prompt_rows, n_seq, dec_seq, dk, dv):
    h = RET_HEADS
    s = SAMPLE_ROWS
    cos, sin = _rope_tables(PAST_LEN + jnp.arange(s, dtype=jnp.int32), dk // 2)
    dmask, qdec, kdec, gl = _ret_tables(s, dk, n_valid=dec_seq)
    dmask = jnp.pad(dmask, ((0, 0), (0, 0), (0, 128 - s)))
    base = n_prompt_rows // s
    body = functools.partial(_ret_sample_body, dk=dk, dv=dv)
    width = proj.shape[1]
    return pl.pallas_call(
        body,
        out_shape=[jax.ShapeDtypeStruct((n_seq * s, h * dv), BF16),
                   jax.ShapeDtypeStruct(state.shape, F32)],
        grid=(n_seq,),
        in_specs=[
            pl.BlockSpec((s, width), lambda b: (base + b, 0)),
            pl.BlockSpec((None, h, dk, dv), lambda b: (b, 0, 0, 0)),
            pl.BlockSpec((s, dk // 2), lambda b: (0, 0)),
            pl.BlockSpec((s, dk // 2), lambda b: (0, 0)),
            pl.BlockSpec((h, s, 128), lambda b: (0, 0, 0)),
            pl.BlockSpec((h, s, dk), lambda b: (0, 0, 0)),
            pl.BlockSpec((h, s, dk), lambda b: (0, 0, 0)),
            pl.BlockSpec((h, 8, 128), lambda b: (0, 0, 0)),
            pl.BlockSpec((1, h * dv), lambda b: (0, 0)),
            pl.BlockSpec((1, h * dv), lambda b: (0, 0)),
        ],
        out_specs=[pl.BlockSpec((s, h * dv), lambda b: (b, 0)),
                   pl.BlockSpec((None, h, dk, dv), lambda b: (b, 0, 0, 0))],
        compiler_params=_cparams(("arbitrary",)),
        name="retention_sample",
    )(proj, state, cos, sin, dmask, qdec, kdec, gl, gn_gain, gn_bias)


CONV_HALO = 32
CONV_ROW_CHUNK = 64
CONV_LANES = 128


def _layer_norm_swish(u, gain, bias):
    mu = jnp.mean(u, axis=-1, keepdims=True)
    var = jnp.mean(jnp.square(u - mu), axis=-1, keepdims=True)
    un = (u - mu) * lax.rsqrt(var + NORM_EPS) * gain + bias
    return (un * jax.nn.sigmoid(un)).astype(BF16)


def _conv_prompt_body(z_ref, w_ref, bdw_ref, lng_ref, lnb_ref, o_ref, zbuf, ubuf, *, tt, d):
    t = pl.program_id(1)

    @pl.when(t == 0)
    def _():
        zbuf[0:CONV_HALO, :] = jnp.zeros((CONV_HALO, d), F32)

    @pl.when(t > 0)
    def _():
        zbuf[0:CONV_HALO, :] = zbuf[tt:tt + CONV_HALO, :]

    zbuf[CONV_HALO:CONV_HALO + tt, :] = z_ref[...].astype(F32)

    span = CONV_ROW_CHUNK + CONV_HALO
    n_row_chunks = tt // CONV_ROW_CHUNK
    n_strips = d // CONV_LANES

    def strip(n, carry):
        r0 = pl.multiple_of((n % n_row_chunks) * CONV_ROW_CHUNK, CONV_ROW_CHUNK)
        c0 = pl.multiple_of((n // n_row_chunks) * CONV_LANES, CONV_LANES)
        cols = pl.ds(c0, CONV_LANES)
        blk = zbuf[pl.ds(r0, span), cols]
        acc = jnp.zeros((CONV_ROW_CHUNK, CONV_LANES), F32)
        for b in range(8):
            rb = blk if b == 0 else pltpu.roll(blk, span - b, axis=0)
            for a in range(5):
                o = 8 * a + b
                if 2 <= o <= CONV_HALO:
                    acc = acc + rb[8 * a:8 * a + CONV_ROW_CHUNK, :] * w_ref[pl.ds(o - 2, 1), cols]
        ubuf[pl.ds(r0, CONV_ROW_CHUNK), cols] = acc
        return carry

    lax.fori_loop(0, n_row_chunks * n_strips, strip, 0)

    ln_rows = 128

    def ln_chunk(c, carry):
        rows = pl.ds(pl.multiple_of(c * ln_rows, ln_rows), ln_rows)
        o_ref[rows, :] = _layer_norm_swish(ubuf[rows, :] + bdw_ref[...], lng_ref[...], lnb_ref[...])
        return carry

    lax.fori_loop(0, tt // ln_rows, ln_chunk, 0)


def _conv_prompt_call(z, w_dw, b_dw, ln_gain, ln_bias, *, n_batch, seq):
    d = z.shape[1]
    tt = ROW_TILE
    nt = seq // tt
    w_pad = jnp.pad(w_dw, ((0, 32 - CONV_WIDTH), (0, 0)))
    body = functools.partial(_conv_prompt_body, tt=tt, d=d)
    vec = pl.BlockSpec((1, d), lambda b, t: (0, 0))
    return pl.pallas_call(
        body,
        out_shape=jax.ShapeDtypeStruct((n_batch * seq, d), BF16),
        grid=(n_batch, nt),
        in_specs=[pl.BlockSpec((tt, d), lambda b, t: (b * nt + t, 0)),
                  pl.BlockSpec((32, d), lambda b, t: (0, 0)), vec, vec, vec],
        out_specs=pl.BlockSpec((tt, d), lambda b, t: (b * nt + t, 0)),
        scratch_shapes=[pltpu.VMEM((tt + CONV_HALO, d), F32), pltpu.VMEM((tt, d), F32)],
        compiler_params=_cparams(("arbitrary", "arbitrary")),
        name="conv_prompt",
    )(z, w_pad, b_dw, ln_gain, ln_bias)


def _conv_sample_body(z_ref, c_ref, w_ref, bdw_ref, lng_ref, lnb_ref, o_ref, zbuf, *, d):
    s = SAMPLE_ROWS
    zbuf[0:CONV_STATE, :] = c_ref[...]
    zbuf[CONV_STATE:CONV_STATE + s, :] = z_ref[...].astype(F32)
    acc = jnp.zeros((s, d), F32)
    for j in range(CONV_WIDTH):
        acc = acc + zbuf[j:j + s, :] * w_ref[j:j + 1, :]
    o_ref[...] = _layer_norm_swish(acc + bdw_ref[...], lng_ref[...], lnb_ref[...])


def _conv_sample_call(z, cache, w_dw, b_dw, ln_gain, ln_bias, *, n_prompt_rows, n_seq):
    d = z.shape[1]
    s = SAMPLE_ROWS
    base = n_prompt_rows // s
    body = functools.partial(_conv_sample_body, d=d)
    vec = pl.BlockSpec((1, d), lambda b: (0, 0))
    return pl.pallas_call(
        body,
        out_shape=jax.ShapeDtypeStruct((n_seq * s, d), BF16),
        grid=(n_seq,),
        in_specs=[pl.BlockSpec((s, d), lambda b: (base + b, 0)),
                  pl.BlockSpec((None, CONV_STATE, d), lambda b: (b, 0, 0)),
                  pl.BlockSpec((CONV_WIDTH, d), lambda b: (0, 0)), vec, vec, vec],
        out_specs=pl.BlockSpec((s, d), lambda b: (b, 0)),
        scratch_shapes=[pltpu.VMEM((CONV_STATE + s, d), F32)],
        compiler_params=_cparams(("arbitrary",)),
        name="conv_sample",
    )(z, cache, w_dw, b_dw, ln_gain, ln_bias)


def _t5_bucket(dist):
    n = jnp.maximum(dist, 0)
    max_exact = REL_BUCKETS // 2
    nf = jnp.maximum(n, 1).astype(F32)
    large = max_exact + (jnp.log(nf / max_exact) / math.log(REL_MAX_DIST / max_exact)
                         * (REL_BUCKETS - max_exact)).astype(jnp.int32)
    large = jnp.minimum(large, REL_BUCKETS - 1)
    return jnp.where(n < max_exact, n, large)


def _bias_table(rel_bias, dist, valid):
    tbl = rel_bias.astype(F32).T[:, _t5_bucket(dist)]
    tbl = jnp.where(valid[None], tbl, NEG_INF)
    return tbl.reshape(-1, dist.shape[-1])


def _softmax_sink_pv(s, sink, vals):
    m = jnp.maximum(jnp.max(s, axis=-1, keepdims=True), sink)
    p = jnp.exp(s - m)
    den = jnp.sum(p, axis=-1, keepdims=True) + jnp.exp(sink - m)
    o = jnp.dot(p.astype(BF16), vals, preferred_element_type=F32)
    return o / den


def _swa_prompt_body(sink_ref, q_ref, kvp_ref, kvc_ref, gq_ref, gk_ref, bias_ref, o_ref, kn_ref, *, hd, n_blocks):
    i = pl.program_id(1)
    blk = WINDOW
    g = ATT_GROUP
    nkv = ATT_KV_HEADS
    gq = gq_ref[...]
    gk = gk_ref[...]
    gqk = gq * gk
    ones_rep = jnp.ones((hd, 128), BF16)
    ones_row = jnp.ones((8, hd), BF16)
    is_first = i == 0
    col = lax.broadcasted_iota(jnp.int32, (g * blk, 2 * blk), 1)
    for hk in range(nkv):
        keys = jnp.concatenate([kvp_ref[:, hk * hd:(hk + 1) * hd], kvc_ref[:, hk * hd:(hk + 1) * hd]], axis=0)
        vals = jnp.concatenate([kvp_ref[:, (nkv + hk) * hd:(nkv + hk + 1) * hd],
                                kvc_ref[:, (nkv + hk) * hd:(nkv + hk + 1) * hd]], axis=0)
        kf = keys.astype(F32)
        ssq_k = lax.dot_general(ones_row, (kf * kf).astype(BF16), (((1,), (1,)), ((), ())),
                                preferred_element_type=F32)[0:1, :]
        rk = lax.rsqrt(ssq_k * (1.0 / hd) + NORM_EPS)
        qs = jnp.concatenate([q_ref[:, (hk * g + gg) * hd:(hk * g + gg + 1) * hd] for gg in range(g)], axis=0)
        qf = qs.astype(F32)
        ssq_q = jnp.dot((qf * qf).astype(BF16), ones_rep, preferred_element_type=F32)
        rq = lax.rsqrt(ssq_q * (1.0 / hd) + NORM_EPS)
        rq2 = jnp.concatenate([rq, rq], axis=1)
        s = lax.dot_general((qf * gqk).astype(BF16), keys, (((1,), (1,)), ((), ())), preferred_element_type=F32)
        s = s * rq2 * rk * (hd ** -0.5) + bias_ref[hk * g * blk:(hk + 1) * g * blk, :]
        s = jnp.where(jnp.logical_and(is_first, col < blk), NEG_INF, s)
        sink = jnp.concatenate([jnp.full((blk, 1), sink_ref[hk * g + gg], F32) for gg in range(g)], axis=0)
        o = _softmax_sink_pv(s, sink, vals)
        for gg in range(g):
            o_ref[:, (hk * g + gg) * hd:(hk * g + gg + 1) * hd] = o[gg * blk:(gg + 1) * blk, :].astype(BF16)

    @pl.when(i == n_blocks - 1)
    def _():
        for hk in range(nkv):
            kc = kvc_ref[:, hk * hd:(hk + 1) * hd].astype(F32)
            ssq = jnp.dot((kc * kc).astype(BF16), ones_rep, preferred_element_type=F32)[:, 0:hd]
            kn_ref[:, hk * hd:(hk + 1) * hd] = kc * lax.rsqrt(ssq * (1.0 / hd) + NORM_EPS) * gk


def _swa_prompt_call(proj, q_gain, k_gain, sinks, rel_bias, *, n_batch, seq, hd):
    blk = WINDOW
    nb = seq // blk
    nq = ATT_Q_HEADS * hd
    nkv2 = 2 * ATT_KV_HEADS * hd
    kvcol = nq // nkv2
    i_idx = jnp.arange(blk, dtype=jnp.int32)[:, None]
    j_idx = jnp.arange(2 * blk, dtype=jnp.int32)[None, :]
    dist = blk + i_idx - j_idx
    bias = _bias_table(rel_bias, dist, (dist >= 0) & (dist < WINDOW))
    body = functools.partial(_swa_prompt_body, hd=hd, n_blocks=nb)
    return pl.pallas_call(
        body,
        out_shape=[jax.ShapeDtypeStruct((n_batch * seq, nq), BF16),
                   jax.ShapeDtypeStruct((n_batch * blk, ATT_KV_HEADS * hd), F32)],
        grid=(n_batch, nb),
        in_specs=[
            pl.BlockSpec(memory_space=pltpu.SMEM),
            pl.BlockSpec((blk, nq), lambda b, i: (b * nb + i, 0)),
            pl.BlockSpec((blk, nkv2), lambda b, i: (b * nb + jnp.maximum(i - 1, 0), kvcol)),
            pl.BlockSpec((blk, nkv2), lambda b, i: (b * nb + i, kvcol)),
            pl.BlockSpec((1, hd), lambda b, i: (0, 0)),
            pl.BlockSpec((1, hd), lambda b, i: (0, 0)),
            pl.BlockSpec((ATT_Q_HEADS * blk, 2 * blk), lambda b, i: (0, 0)),
        ],
        out_specs=[pl.BlockSpec((blk, nq), lambda b, i: (b * nb + i, 0)),
                   pl.BlockSpec((blk, ATT_KV_HEADS * hd), lambda b, i: (b, 0))],
        compiler_params=_cparams(("arbitrary", "arbitrary")),
        name="swa_prompt",
    )(sinks, proj, proj, proj, q_gain, k_gain, bias)


def _swa_sample_body(sink_ref, p_ref, ck_ref, cv_ref, gq_ref, gk_ref, bias_ref, o_ref, kn_ref, *, hd):
    s = SAMPLE_ROWS
    g = ATT_GROUP
    nkv = ATT_KV_HEADS
    nq = ATT_Q_HEADS * hd
    gq = gq_ref[...]
    gk = gk_ref[...]

    def rms(x, gain):
        return x * lax.rsqrt(jnp.mean(x * x, axis=-1, keepdims=True) + NORM_EPS) * gain

    for hk in range(nkv):
        kn = rms(p_ref[:, nq + hk * hd:nq + (hk + 1) * hd].astype(F32), gk)
        kn_ref[:, hk * hd:(hk + 1) * hd] = kn
        keys = jnp.concatenate([ck_ref[:, hk * hd:(hk + 1) * hd], kn], axis=0).astype(BF16)
        vnew = p_ref[:, nq + (nkv + hk) * hd:nq + (nkv + hk + 1) * hd].astype(F32)
        vals = jnp.concatenate([cv_ref[:, hk * hd:(hk + 1) * hd], vnew], axis=0).astype(BF16)
        qs = jnp.concatenate([rms(p_ref[:, (hk * g + gg) * hd:(hk * g + gg + 1) * hd].astype(F32), gq)
                              for gg in range(g)], axis=0)
        sc = lax.dot_general(qs.astype(BF16), keys, (((1,), (1,)), ((), ())), preferred_element_type=F32)
        sc = sc * (hd ** -0.5) + bias_ref[hk * g * s:(hk + 1) * g * s, :]
        sink = jnp.concatenate([jnp.full((s, 1), sink_ref[hk * g + gg], F32) for gg in range(g)], axis=0)
        o = _softmax_sink_pv(sc, sink, vals)
        for gg in range(g):
            o_ref[:, (hk * g + gg) * hd:(hk * g + gg + 1) * hd] = o[gg * s:(gg + 1) * s, :].astype(BF16)


def _swa_sample_call(proj, cache_k, cache_v, q_gain, k_gain, sinks, rel_bias, *, n_prompt_rows, n_seq, dec_seq, hd):
    s = SAMPLE_ROWS
    nq = ATT_Q_HEADS * hd
    nkv = ATT_KV_HEADS * hd
    sc = cache_k.shape[1]
    i_idx = jnp.arange(s, dtype=jnp.int32)[:, None]
    j_idx = jnp.arange(sc + s, dtype=jnp.int32)[None, :]
    dist = sc + i_idx - j_idx
    valid = (dist >= 0) & (dist < WINDOW) & (j_idx < sc + dec_seq)
    bias = _bias_table(rel_bias, dist, valid)
    base = n_prompt_rows // s
    body = functools.partial(_swa_sample_body, hd=hd)
    return pl.pallas_call(
        body,
        out_shape=[jax.ShapeDtypeStruct((n_seq * s, nq), BF16),
                   jax.ShapeDtypeStruct((n_seq * s, nkv), F32)],
        grid=(n_seq,),
        in_specs=[
            pl.BlockSpec(memory_space=pltpu.SMEM),
            pl.BlockSpec((s, proj.shape[1]), lambda b: (base + b, 0)),
            pl.BlockSpec((None, sc, nkv), lambda b: (b, 0, 0)),
            pl.BlockSpec((None, sc, nkv), lambda b: (b, 0, 0)),
            pl.BlockSpec((1, hd), lambda b: (0, 0)),
            pl.BlockSpec((1, hd), lambda b: (0, 0)),
            pl.BlockSpec((ATT_Q_HEADS * s, sc + s), lambda b: (0, 0)),
        ],
        out_specs=[pl.BlockSpec((s, nq), lambda b: (b, 0)),
                   pl.BlockSpec((s, nkv), lambda b: (b, 0))],
        compiler_params=_cparams(("arbitrary",)),
        name="swa_sample",
    )(sinks, proj, cache_k, cache_v, q_gain, k_gain, bias)


def _moe_in_body(tg_ref, tv_ref, x_ref, w_ref, cw_ref, o_ref, wb, *, ff):
    t = pl.program_id(1)
    changed = jnp.logical_or(t == 0, tg_ref[t] != tg_ref[jnp.maximum(t - 1, 0)])

    @pl.when(changed)
    def _():
        wb[...] = w_ref[...].astype(BF16)

    @pl.when(tv_ref[t] > 0)
    def _():
        gu = jnp.dot(x_ref[...], wb[...], preferred_element_type=F32)
        gate, up = gu[:, :ff], gu[:, ff:]
        o_ref[...] = (gate * jax.nn.sigmoid(gate) * up * cw_ref[...]).astype(BF16)

    @pl.when(tv_ref[t] == 0)
    def _():
        o_ref[...] = jnp.zeros_like(o_ref)


def _moe_out_body(tg_ref, tv_ref, a_ref, w_ref, o_ref, wb):
    t = pl.program_id(1)
    changed = jnp.logical_or(t == 0, tg_ref[t] != tg_ref[jnp.maximum(t - 1, 0)])

    @pl.when(changed)
    def _():
        wb[...] = w_ref[...].astype(BF16)

    @pl.when(tv_ref[t] > 0)
    def _():
        o_ref[...] = jnp.dot(a_ref[...], wb[...], preferred_element_type=F32).astype(o_ref.dtype)

    @pl.when(tv_ref[t] == 0)
    def _():
        o_ref[...] = jnp.zeros_like(o_ref)


def _moe_call(xs, cw_sorted, tile_group, tile_valid, w_in, w_out):
    npad, d = xs.shape
    e, _, ff2 = w_in.shape
    ff = ff2 // 2
    n_tiles = npad // ROW_TILE
    act = pl.pallas_call(
        functools.partial(_moe_in_body, ff=ff),
        out_shape=jax.ShapeDtypeStruct((npad, MOE_EPG * ff), BF16),
        grid_spec=pltpu.PrefetchScalarGridSpec(
            num_scalar_prefetch=2,
            grid=(MOE_EPG, n_tiles),
            in_specs=[
                pl.BlockSpec((ROW_TILE, d), lambda el, t, tg, tv: (t, 0)),
                pl.BlockSpec((None, d, ff2), lambda el, t, tg, tv: (tg[t] * MOE_EPG + el, 0, 0)),
                pl.BlockSpec((None, ROW_TILE, 1), lambda el, t, tg, tv: (el, t, 0)),
            ],
            out_specs=pl.BlockSpec((ROW_TILE, ff), lambda el, t, tg, tv: (t, el)),
            scratch_shapes=[pltpu.VMEM((d, ff2), BF16)],
        ),
        compiler_params=_cparams(("arbitrary", "arbitrary")),
        name="moe_in",
    )(tile_group, tile_valid, xs, w_in, cw_sorted)

    w_out_g = w_out.reshape(MOE_GROUPS, MOE_EPG * ff, d)
    tn = d // 2
    return pl.pallas_call(
        _moe_out_body,
        out_shape=jax.ShapeDtypeStruct((npad, d), BF16),
        grid_spec=pltpu.PrefetchScalarGridSpec(
            num_scalar_prefetch=2,
            grid=(d // tn, n_tiles),
            in_specs=[
                pl.BlockSpec((ROW_TILE, MOE_EPG * ff), lambda j, t, tg, tv: (t, 0)),
                pl.BlockSpec((None, MOE_EPG * ff, tn), lambda j, t, tg, tv: (tg[t], 0, j)),
            ],
            out_specs=pl.BlockSpec((ROW_TILE, tn), lambda j, t, tg, tv: (t, j)),
            scratch_shapes=[pltpu.VMEM((MOE_EPG * ff, tn), BF16)],
        ),
        compiler_params=_cparams(("arbitrary", "arbitrary")),
        name="moe_out",
    )(tile_group, tile_valid, act, w_out_g)


def _route(logits, bg, be):
    g, epg = MOE_GROUPS, MOE_EPG
    n = logits.shape[0]
    pg = jax.nn.softmax(logits[:, :g] + bg, axis=-1)
    p_grp, grp = lax.top_k(pg, 1)
    le = (logits[:, g:g + g * epg] + be).reshape(n, g, epg)
    le = jnp.take_along_axis(le, grp[:, :, None], axis=1)[:, 0]
    p_exp, e_loc = lax.top_k(jax.nn.softmax(le, axis=-1), MOE_TOPK)
    weights = p_grp * p_exp / jnp.sum(p_exp, axis=-1, keepdims=True)
    cw = jnp.einsum('nk,nke->ne', weights, jax.nn.one_hot(e_loc, epg, dtype=F32))
    return grp[:, 0], cw


def _moe_layer(h2, logits, bg, be, w_in, w_out):
    r, d = h2.shape
    g = MOE_GROUPS
    grp, cw = _route(logits, bg, be)
    onehot = (grp[:, None] == jnp.arange(g, dtype=grp.dtype)[None, :]).astype(jnp.int32)
    csum = jnp.cumsum(onehot, axis=0)
    counts = csum[-1]
    rank = jnp.sum((csum - onehot) * onehot, axis=1)
    padded = ((counts + ROW_TILE - 1) // ROW_TILE) * ROW_TILE
    ends = jnp.cumsum(padded)
    starts = ends - padded
    pos = starts[grp] + rank
    n_tiles = r // ROW_TILE + g
    npad = n_tiles * ROW_TILE
    src = jnp.zeros((npad,), jnp.int32).at[pos].set(jnp.arange(r, dtype=jnp.int32))
    filled = jnp.zeros((npad,), F32).at[pos].set(1.0)
    tile_start = jnp.arange(n_tiles, dtype=jnp.int32) * ROW_TILE
    tile_group = jnp.minimum(jnp.sum(tile_start[:, None] >= ends[None, :], axis=1), g - 1).astype(jnp.int32)
    tile_valid = (tile_start < ends[-1]).astype(jnp.int32)
    xs = jnp.take(h2, src, axis=0)
    cw_sorted = (jnp.take(cw, src, axis=0) * filled[:, None]).T[:, :, None]
    y_sorted = _moe_call(xs, cw_sorted, tile_group, tile_valid, w_in, w_out)
    return jnp.take(y_sorted, pos, axis=0)


def kernel(x_prompt, x_sample, c_prompt, c_sample, state_ret, cache_conv, cache_swa_k, cache_swa_v, ada_w, ada_b, norm_gain, ret_w_in, ret_gn_gain, ret_gn_bias, ret_w_out, conv_w_pw1, conv_b_pw1, conv_w_dw, conv_b_dw, conv_ln_gain, conv_ln_bias, conv_w_pw2, conv_b_pw2, att_w_qkv, att_q_gain, att_k_gain, att_sinks, att_w_o, rel_bias, moe_wg, moe_bg, moe_we, moe_be, moe_w_in, moe_w_out):
    n_batch, seq, d = x_prompt.shape
    n_seq, dec_seq, _ = x_sample.shape
    depth = ada_w.shape[0]
    s = SAMPLE_ROWS
    assert n_seq * s == ROW_TILE and seq % ROW_TILE == 0 and dec_seq <= s
    n_prompt_rows = n_batch * seq
    seq_tiles = seq // ROW_TILE
    dk = ret_w_in.shape[2] // (6 * RET_HEADS)
    dv = 2 * dk
    hd = d // ATT_Q_HEADS
    geom = dict(seq_tiles=seq_tiles, n_batch=n_batch)

    xs_pad = jnp.pad(x_sample, ((0, 0), (0, s - dec_seq), (0, 0))).reshape(n_seq * s, d)
    x = jnp.concatenate([x_prompt.reshape(n_prompt_rows, d), xs_pad], axis=0)

    n_c = n_batch + n_seq
    c_rows = ((n_c + 7) // 8) * 8
    c_all = jnp.pad(jnp.concatenate([c_prompt, c_sample], axis=0), ((0, c_rows - n_c), (0, 0)))
    mod = _ada_call(c_all, ada_w, ada_b)
    mod4 = mod.reshape(depth, c_rows, 1, 6 * d)
    modtok = jnp.repeat(mod[:, n_batch:n_c], s, axis=1)

    ret_p, ret_s, conv_p, conv_s, kp_l, vp_l, ks_l, vs_l = [], [], [], [], [], [], [], []
    (h,) = _norm_call(x, mod4, modtok, norm=(norm_gain[0, 0][None], 0, 0, 1), **geom)
    for l in range(depth):
        kind, j = l % 3, l // 3
        resid1 = (x, mod4, modtok, l, 2)
        if kind == 0:
            proj = _linear_call(h, ret_w_in[j], n_out=ret_w_in.shape[2], tn=1024, out_dtype=BF16,
                                name="ret_in", **geom)
            gng, gnb = ret_gn_gain[j][None], ret_gn_bias[j][None]
            a_p, st_p = _ret_prompt_call(proj, gng, gnb, n_batch=n_batch, seq=seq, dk=dk, dv=dv)
            a_s, st_s = _ret_sample_call(proj, state_ret[j].astype(F32), gng, gnb, n_prompt_rows=n_prompt_rows,
                                         n_seq=n_seq, dec_seq=dec_seq, dk=dk, dv=dv)
            ret_p.append(st_p)
            ret_s.append(st_s)
            x = _linear_call(a_p, ret_w_out[j], h_sample=a_s, n_out=d, tn=512, out_dtype=F32, resid=resid1,
                             name="ret_out", **geom)
        elif kind == 1:
            z = _linear_call(h, conv_w_pw1[j], n_out=d, tn=512, out_dtype=BF16, bias=conv_b_pw1[j][None],
                             glu=True, name="conv_pw1", **geom)
            cargs = (conv_w_dw[j], conv_b_dw[j][None], conv_ln_gain[j][None], conv_ln_bias[j][None])
            a_p = _conv_prompt_call(z, *cargs, n_batch=n_batch, seq=seq)
            a_s = _conv_sample_call(z, cache_conv[j].astype(F32), *cargs, n_prompt_rows=n_prompt_rows, n_seq=n_seq)
            zf = z.astype(F32)
            conv_p.append(zf[:n_prompt_rows].reshape(n_batch, seq, d)[:, seq - CONV_STATE:])
            z_new = zf[n_prompt_rows:].reshape(n_seq, s, d)[:, :dec_seq]
            conv_s.append(jnp.concatenate([cache_conv[j].astype(F32), z_new], axis=1)[:, -CONV_STATE:])
            x = _linear_call(a_p, conv_w_pw2[j], h_sample=a_s, n_out=d, tn=1024, out_dtype=F32,
                             bias=conv_b_pw2[j][None], resid=resid1, name="conv_pw2", **geom)
        else:
            nkv = ATT_KV_HEADS * hd
            proj = _linear_call(h, att_w_qkv[j], n_out=att_w_qkv.shape[2], tn=512, out_dtype=BF16,
                                name="att_qkv", **geom)
            gq, gk = att_q_gain[j][None].astype(F32), att_k_gain[j][None].astype(F32)
            sinks = att_sinks[j].astype(F32)
            a_p, kn_p = _swa_prompt_call(proj, gq, gk, sinks, rel_bias, n_batch=n_batch, seq=seq, hd=hd)
            win = cache_swa_k.shape[2]
            ck = cache_swa_k[j].astype(F32).reshape(n_seq, win, nkv)
            cv = cache_swa_v[j].astype(F32).reshape(n_seq, win, nkv)
            a_s, kn_s = _swa_sample_call(proj, ck, cv, gq, gk, sinks, rel_bias, n_prompt_rows=n_prompt_rows,
                                         n_seq=n_seq, dec_seq=dec_seq, hd=hd)
            v_all = proj[:, ATT_Q_HEADS * hd + nkv:].astype(F32)
            kp_l.append(kn_p.reshape(n_batch, WINDOW, ATT_KV_HEADS, hd))
            vp_l.append(v_all[:n_prompt_rows].reshape(n_batch, seq, ATT_KV_HEADS, hd)[:, seq - WINDOW:])
            k_new = kn_s.reshape(n_seq, s, nkv)[:, :dec_seq]
            v_new = v_all[n_prompt_rows:].reshape(n_seq, s, nkv)[:, :dec_seq]
            ks_l.append(jnp.concatenate([ck, k_new], axis=1)[:, -win:].reshape(n_seq, win, ATT_KV_HEADS, hd))
            vs_l.append(jnp.concatenate([cv, v_new], axis=1)[:, -win:].reshape(n_seq, win, ATT_KV_HEADS, hd))
            x = _linear_call(a_p, att_w_o[j], h_sample=a_s, n_out=d, tn=1024, out_dtype=F32, resid=resid1,
                             name="att_out", **geom)

        router_w = jnp.pad(jnp.concatenate([moe_wg[l], moe_we[l]], axis=1).astype(F32),
                           ((0, 0), (0, ROUTER_LANES - MOE_GROUPS - MOE_GROUPS * MOE_EPG)))
        h2, logits = _norm_call(x, mod4, modtok, norm=(norm_gain[l, 1][None], l, 3, 4), router_w=router_w, **geom)
        y = _moe_layer(h2, logits, moe_bg[l], moe_be[l], moe_w_in[l], moe_w_out[l])
        if l + 1 < depth:
            x, h = _norm_call(x, mod4, modtok, resid=(y, l, 5), norm=(norm_gain[l + 1, 0][None], l + 1, 0, 1), **geom)
        else:
            (x,) = _norm_call(x, mod4, modtok, resid=(y, l, 5), **geom)

    y_prompt = x[:n_prompt_rows].reshape(n_batch, seq, d)
    y_sample = x[n_prompt_rows:].reshape(n_seq, s, d)[:, :dec_seq]
    return (y_prompt, y_sample, jnp.stack(ret_p), jnp.stack(ret_s), jnp.stack(conv_p), jnp.stack(conv_s),
            jnp.stack(kp_l), jnp.stack(vp_l), jnp.stack(ks_l), jnp.stack(vs_l))
```

```python
import functools
import math

import jax
import jax.numpy as jnp
from jax import lax
from jax.experimental import pallas as pl
from jax.experimental.pallas import tpu as pltpu

F32 = jnp.float32
BF16 = jnp.bfloat16

NORM_EPS = 1e-6
NEG_INF = -1e30
ROPE_BASE = 10000.0
PAST_LEN = 16384

RET_HEADS = 8
RET_CHUNK = 128
CONV_WIDTH = 31
CONV_STATE = CONV_WIDTH - 1
ATT_Q_HEADS = 32
ATT_KV_HEADS = 4
ATT_GROUP = ATT_Q_HEADS // ATT_KV_HEADS
WINDOW = 128
REL_BUCKETS = 32
REL_MAX_DIST = 128
MOE_GROUPS = 4
MOE_EPG = 4
MOE_TOPK = 2

ROW_TILE = 512
SAMPLE_ROWS = 16
ROUTER_LANES = 128
VMEM_LIMIT_BYTES = 56 * 1024 * 1024


def _cparams(sem):
    return pltpu.CompilerParams(dimension_semantics=sem, vmem_limit_bytes=VMEM_LIMIT_BYTES)


def _ada_body(c_ref, w_ref, b_ref, o_ref):
    c = c_ref[...]
    s = (c * jax.nn.sigmoid(c)).astype(BF16)
    o_ref[...] = jnp.dot(s, w_ref[...].astype(BF16), preferred_element_type=F32) + b_ref[...]


def _ada_call(c_all, ada_w, ada_b):
    depth, d, n = ada_w.shape
    rows = c_all.shape[0]
    tn = 1024
    return pl.pallas_call(
        _ada_body,
        out_shape=jax.ShapeDtypeStruct((depth, rows, n), F32),
        grid=(depth, n // tn),
        in_specs=[
            pl.BlockSpec((rows, d), lambda l, j: (0, 0)),
            pl.BlockSpec((None, d, tn), lambda l, j: (l, 0, j)),
            pl.BlockSpec((None, 1, tn), lambda l, j: (l, 0, j)),
        ],
        out_specs=pl.BlockSpec((None, rows, tn), lambda l, j: (l, 0, j)),
        compiler_params=_cparams(("arbitrary", "arbitrary")),
        name="ada_mod",
    )(c_all, ada_w, ada_b.reshape(depth, 1, n))


def _norm_body(*refs, n_prompt_tiles, has_resid, has_norm, has_router, split_out):
    it = iter(refs)
    x_ref = next(it)
    if has_resid:
        y_ref, grow_ref, gtok_ref = next(it), next(it), next(it)
    if has_norm:
        gain_ref, shrow_ref, scrow_ref, shtok_ref, sctok_ref = next(it), next(it), next(it), next(it), next(it)
    if has_router:
        wr_ref = next(it)
    if has_resid:
        xo_ref = next(it)
        if split_out:
            xs_ref = next(it)
    if has_norm:
        h_ref = next(it)
    if has_router:
        lg_ref = next(it)

    is_sample = pl.program_id(0) >= n_prompt_tiles
    x = x_ref[...]
    if has_resid:
        gate = jnp.where(is_sample, gtok_ref[...], grow_ref[...])
        x = x + gate * y_ref[...].astype(F32)
        if split_out:
            @pl.when(jnp.logical_not(is_sample))
            def _():
                xo_ref[...] = x

            @pl.when(is_sample)
            def _():
                xs_ref[...] = x
        else:
            xo_ref[...] = x
    if has_norm:
        ms = jnp.mean(x * x, axis=-1, keepdims=True)
        xn = x * lax.rsqrt(ms + NORM_EPS) * gain_ref[...]
        scale = jnp.where(is_sample, sctok_ref[...], scrow_ref[...])
        shift = jnp.where(is_sample, shtok_ref[...], shrow_ref[...])
        h = xn * (1.0 + scale) + shift
        h_ref[...] = h.astype(BF16)
        if has_router:
            lg_ref[...] = jnp.dot(h, wr_ref[...], precision=lax.Precision.HIGHEST, preferred_element_type=F32)


def _norm_call(x, mod4, modtok, *, seq_tiles, n_batch, resid=None, norm=None, router_w=None, split_out=False):
    r, d = x.shape
    n_tiles = r // ROW_TILE
    n_prompt_tiles = n_tiles - 1

    def row_spec(layer, col):
        return pl.BlockSpec((None, None, 1, d),
                            lambda i: (layer, jnp.minimum(i // seq_tiles, n_batch - 1), 0, col))

    def tok_spec(layer, col):
        return pl.BlockSpec((None, ROW_TILE, d), lambda i: (layer, 0, col))

    tile = pl.BlockSpec((ROW_TILE, d), lambda i: (i, 0))
    args, in_specs, out_shape, out_specs = [x], [tile], [], []
    if resid is not None:
        y, layer, gcol = resid
        args += [y, mod4, modtok]
        in_specs += [tile, row_spec(layer, gcol), tok_spec(layer, gcol)]
        if split_out:
            out_shape += [jax.ShapeDtypeStruct((r - ROW_TILE, d), F32), jax.ShapeDtypeStruct((ROW_TILE, d), F32)]
            out_specs += [pl.BlockSpec((ROW_TILE, d), lambda i: (jnp.minimum(i, n_prompt_tiles - 1), 0)),
                          pl.BlockSpec((ROW_TILE, d), lambda i: (0, 0))]
        else:
            out_shape.append(jax.ShapeDtypeStruct((r, d), F32))
            out_specs.append(tile)
    if norm is not None:
        gains, gidx, layer, shcol, sccol = norm
        args += [gains, mod4, mod4, modtok, modtok]
        in_specs += [pl.BlockSpec((None, 1, d), lambda i: (gidx, 0, 0)), row_spec(layer, shcol),
                     row_spec(layer, sccol), tok_spec(layer, shcol), tok_spec(layer, sccol)]
        out_shape.append(jax.ShapeDtypeStruct((r, d), BF16))
        out_specs.append(tile)
        if router_w is not None:
            args.append(router_w)
            in_specs.append(pl.BlockSpec((d, ROUTER_LANES), lambda i: (0, 0)))
            out_shape.append(jax.ShapeDtypeStruct((r, ROUTER_LANES), F32))
            out_specs.append(pl.BlockSpec((ROW_TILE, ROUTER_LANES), lambda i: (i, 0)))
    body = functools.partial(_norm_body, n_prompt_tiles=n_prompt_tiles, has_resid=resid is not None,
                             has_norm=norm is not None, has_router=router_w is not None, split_out=split_out)
    return pl.pallas_call(
        body, out_shape=out_shape, grid=(n_tiles,), in_specs=in_specs, out_specs=out_specs,
        compiler_params=_cparams(("arbitrary",)), name="mod_norm",
    )(*args)


def _linear_body(*refs, n_prompt_tiles, pair, glu, has_bias, has_resid):
    it = iter(refs)
    h_ref = next(it)
    hs_ref = next(it) if pair else None
    w_ref = next(it)
    w2_ref = next(it) if glu else None
    b_ref = next(it) if has_bias else None
    b2_ref = next(it) if glu else None
    if has_resid:
        x_ref, grow_ref, gtok_ref = next(it), next(it), next(it)
    o_ref = next(it)
    wb = next(it)
    wb2 = next(it) if glu else None

    i = pl.program_id(1)
    is_sample = i >= n_prompt_tiles

    @pl.when(i == 0)
    def _():
        wb[...] = w_ref[...].astype(BF16)
        if glu:
            wb2[...] = w2_ref[...].astype(BF16)

    def compute(hv):
        acc = jnp.dot(hv, wb[...], preferred_element_type=F32)
        if has_bias:
            acc = acc + b_ref[...]
        if glu:
            acc2 = jnp.dot(hv, wb2[...], preferred_element_type=F32) + b2_ref[...]
            acc = acc * jax.nn.sigmoid(acc2)
        if has_resid:
            gate = jnp.where(is_sample, gtok_ref[...], grow_ref[...])
            acc = x_ref[...] + gate * acc
        o_ref[...] = acc.astype(o_ref.dtype)

    if pair:
        @pl.when(jnp.logical_not(is_sample))
        def _():
            compute(h_ref[...])

        @pl.when(is_sample)
        def _():
            compute(hs_ref[...])
    else:
        compute(h_ref[...])


def _linear_call(h, w, wl, *, n_out, tn, out_dtype, seq_tiles, n_batch, h_sample=None, bias=None, glu=False,
                 resid=None, name="linear"):
    k = w.shape[1]
    pair = h_sample is not None
    n_prompt_tiles = h.shape[0] // ROW_TILE - (0 if pair else 1)
    n_tiles = n_prompt_tiles + 1
    r = n_tiles * ROW_TILE
    nblk = n_out // tn

    args = [h]
    in_specs = [pl.BlockSpec((ROW_TILE, k), lambda j, i: (jnp.minimum(i, n_prompt_tiles - 1) if pair else i, 0))]
    if pair:
        args.append(h_sample)
        in_specs.append(pl.BlockSpec((ROW_TILE, k), lambda j, i: (0, 0)))
    args.append(w)
    in_specs.append(pl.BlockSpec((None, k, tn), lambda j, i: (wl, 0, j)))
    if glu:
        args.append(w)
        in_specs.append(pl.BlockSpec((None, k, tn), lambda j, i: (wl, 0, nblk + j)))
    if bias is not None:
        bias = bias.reshape(bias.shape[0], 1, bias.shape[1])
        args.append(bias)
        in_specs.append(pl.BlockSpec((None, 1, tn), lambda j, i: (wl, 0, j)))
        if glu:
            args.append(bias)
            in_specs.append(pl.BlockSpec((None, 1, tn), lambda j, i: (wl, 0, nblk + j)))
    if resid is not None:
        x, mod4, modtok, layer, gcol = resid
        cb = gcol * nblk
        args += [x, mod4, modtok]
        in_specs += [
            pl.BlockSpec((ROW_TILE, tn), lambda j, i: (i, j)),
            pl.BlockSpec((None, None, 1, tn),
                         lambda j, i: (layer, jnp.minimum(i // seq_tiles, n_batch - 1), 0, cb + j)),
            pl.BlockSpec((None, ROW_TILE, tn), lambda j, i: (layer, 0, cb + j)),
        ]
    scratch = [pltpu.VMEM((k, tn), BF16)] + ([pltpu.VMEM((k, tn), BF16)] if glu else [])
    body = functools.partial(_linear_body, n_prompt_tiles=n_prompt_tiles, pair=pair, glu=glu,
                             has_bias=bias is not None, has_resid=resid is not None)
    return pl.pallas_call(
        body,
        out_shape=jax.ShapeDtypeStruct((r, n_out), out_dtype),
        grid=(nblk, n_tiles),
        in_specs=in_specs,
        out_specs=pl.BlockSpec((ROW_TILE, tn), lambda j, i: (i, j)),
        scratch_shapes=scratch,
        compiler_params=_cparams(("arbitrary", "arbitrary")),
        name=name,
    )(*args)


def _rotate(x, cos, sin):
    half = x.shape[-1] // 2
    x1, x2 = x[:, :half], x[:, half:]
    return jnp.concatenate([x1 * cos - x2 * sin, x1 * sin + x2 * cos], axis=-1)


def _group_norm_gate(o, g, gain, bias):
    mu = jnp.mean(o, axis=-1, keepdims=True)
    var = jnp.mean(jnp.square(o - mu), axis=-1, keepdims=True)
    on = (o - mu) * lax.rsqrt(var + NORM_EPS) * gain + bias
    g = g.astype(F32)
    return (g * jax.nn.sigmoid(g) * on).astype(BF16)


def _ret_prompt_body(q_ref, k_ref, v_ref, g_ref, cos_ref, sin_ref, dm_ref, qd_ref, kd_ref, gl_ref,
                     gng_ref, gnb_ref, o_ref, st_ref, s_acc, *, n_chunks):
    s_acc[...] = jnp.zeros_like(s_acc)
    dmask = dm_ref[...]
    qdec = qd_ref[...]
    kdec = kd_ref[...]
    gl = gl_ref[0:1, 0:1]
    gng = gng_ref[...]
    gnb = gnb_ref[...]

    def chunk(c, carry):
        r0 = pl.multiple_of(c * RET_CHUNK, RET_CHUNK)
        rows = pl.ds(r0, RET_CHUNK)
        cos = cos_ref[rows, :]
        sin = sin_ref[rows, :]
        qr = _rotate(q_ref[rows, :].astype(F32), cos, sin)
        kr = _rotate(k_ref[rows, :].astype(F32), cos, sin)
        v = v_ref[rows, :]
        state = s_acc[...]
        scores = lax.dot_general(qr.astype(BF16), kr.astype(BF16), (((1,), (1,)), ((), ())),
                                 preferred_element_type=F32)
        scores = scores * dmask
        out = jnp.dot(scores.astype(BF16), v, preferred_element_type=F32)
        out = out + jnp.dot((qr * qdec).astype(BF16), state.astype(BF16), preferred_element_type=F32)
        kv = lax.dot_general((kr * kdec).astype(BF16), v, (((0,), (0,)), ((), ())), preferred_element_type=F32)
        s_acc[...] = gl * state + kv
        o_ref[rows, :] = _group_norm_gate(out, g_ref[rows, :], gng, gnb)
        return carry

    lax.fori_loop(0, n_chunks, chunk, 0)
    st_ref[...] = s_acc[...]


def _ret_tables(chunk, dk, n_valid=None):
    n_valid = chunk if n_valid is None else n_valid
    lg = jnp.log1p(-jnp.exp2(-5.0 - jnp.arange(RET_HEADS, dtype=F32)))
    idx = jnp.arange(chunk, dtype=F32)
    diff = idx[:, None] - idx[None, :]
    inside = (idx[:, None] < n_valid) & (idx[None, :] < n_valid)
    dmask = jnp.where((diff[None] >= 0) & inside[None],
                      jnp.exp(jnp.maximum(diff, 0.0)[None] * lg[:, None, None]), 0.0) * (dk ** -0.5)
    qdec = jnp.exp((idx[None, :] + 1.0) * lg[:, None])
    kdec = jnp.where(idx[None, :] < n_valid, jnp.exp((n_valid - 1.0 - idx)[None, :] * lg[:, None]), 0.0) * (dk ** -0.5)
    gl = jnp.exp(n_valid * lg)
    qdec = jnp.broadcast_to(qdec[:, :, None], (RET_HEADS, chunk, dk))
    kdec = jnp.broadcast_to(kdec[:, :, None], (RET_HEADS, chunk, dk))
    gl = jnp.broadcast_to(gl[:, None, None], (RET_HEADS, 8, 128))
    return dmask, qdec, kdec, gl


def _rope_tables(pos, half):
    inv = ROPE_BASE ** (-jnp.arange(half, dtype=F32) / half)
    ang = pos.astype(F32)[:, None] * inv[None, :]
    return jnp.cos(ang), jnp.sin(ang)


def _ret_prompt_call(proj, gn_gain, gn_bias, j, *, n_batch, seq, dk, dv):
    h = RET_HEADS
    n_chunks = seq // RET_CHUNK
    cos, sin = _rope_tables(jnp.arange(seq, dtype=jnp.int32), dk // 2)
    dmask, qdec, kdec, gl = _ret_tables(RET_CHUNK, dk)
    kcol, vcol, gcol = h, (2 * h * dk) // dv, (2 * h * dk) // dv + h
    body = functools.partial(_ret_prompt_body, n_chunks=n_chunks)
    return pl.pallas_call(
        body,
        out_shape=[jax.ShapeDtypeStruct((n_batch * seq, h * dv), BF16),
                   jax.ShapeDtypeStruct((n_batch, h, dk, dv), F32)],
        grid=(n_batch, h),
        in_specs=[
            pl.BlockSpec((seq, dk), lambda b, hh: (b, hh)),
            pl.BlockSpec((seq, dk), lambda b, hh: (b, kcol + hh)),
            pl.BlockSpec((seq, dv), lambda b, hh: (b, vcol + hh)),
            pl.BlockSpec((seq, dv), lambda b, hh: (b, gcol + hh)),
            pl.BlockSpec((seq, dk // 2), lambda b, hh: (0, 0)),
            pl.BlockSpec((seq, dk // 2), lambda b, hh: (0, 0)),
            pl.BlockSpec((None, RET_CHUNK, RET_CHUNK), lambda b, hh: (hh, 0, 0)),
            pl.BlockSpec((None, RET_CHUNK, dk), lambda b, hh: (hh, 0, 0)),
            pl.BlockSpec((None, RET_CHUNK, dk), lambda b, hh: (hh, 0, 0)),
            pl.BlockSpec((None, 8, 128), lambda b, hh: (hh, 0, 0)),
            pl.BlockSpec((None, 1, dv), lambda b, hh: (j, 0, hh)),
            pl.BlockSpec((None, 1, dv), lambda b, hh: (j, 0, hh)),
        ],
        out_specs=[pl.BlockSpec((seq, dv), lambda b, hh: (b, hh)),
                   pl.BlockSpec((None, None, dk, dv), lambda b, hh: (b, hh, 0, 0))],
        scratch_shapes=[pltpu.VMEM((dk, dv), F32)],
        compiler_params=_cparams(("arbitrary", "arbitrary")),
        name="retention_prompt",
    )(proj, proj, proj, proj, cos, sin, dmask, qdec, kdec, gl, gn_gain, gn_bias)


def _ret_sample_body(p_ref, st_ref, cos_ref, sin_ref, dm_ref, qd_ref, kd_ref, gl_ref, gng_ref, gnb_ref,
                     *rest, dk, dv):
    o_ref, so_ref = rest[-2], rest[-1]
    h = RET_HEADS
    cos = cos_ref[...]
    sin = sin_ref[...]
    pad = 128 - SAMPLE_ROWS
    for hh in range(h):
        q = p_ref[:, hh * dk:(hh + 1) * dk].astype(F32)
        k = p_ref[:, h * dk + hh * dk:h * dk + (hh + 1) * dk].astype(F32)
        v = p_ref[:, 2 * h * dk + hh * dv:2 * h * dk + (hh + 1) * dv]
        g = p_ref[:, 2 * h * dk + h * dv + hh * dv:2 * h * dk + h * dv + (hh + 1) * dv]
        qr = _rotate(q, cos, sin)
        kr = _rotate(k, cos, sin)
        k_pad = jnp.concatenate([kr.astype(BF16), jnp.zeros((pad, dk), BF16)], axis=0)
        kd_pad = jnp.concatenate([(kr * kd_ref[hh]).astype(BF16), jnp.zeros((pad, dk), BF16)], axis=0)
        v_pad = jnp.concatenate([v, jnp.zeros((pad, dv), BF16)], axis=0)
        state = st_ref[hh]
        scores = lax.dot_general(qr.astype(BF16), k_pad, (((1,), (1,)), ((), ())), preferred_element_type=F32)
        scores = scores * dm_ref[hh]
        out = jnp.dot(scores.astype(BF16), v_pad, preferred_element_type=F32)
        out = out + jnp.dot((qr * qd_ref[hh]).astype(BF16), state.astype(BF16), preferred_element_type=F32)
        kv = lax.dot_general(kd_pad, v_pad, (((0,), (0,)), ((), ())), preferred_element_type=F32)
        so_ref[hh] = gl_ref[hh, 0:1, 0:1] * state + kv
        o_ref[:, hh * dv:(hh + 1) * dv] = _group_norm_gate(out, g, gng_ref[:, hh * dv:(hh + 1) * dv],
                                                           gnb_ref[:, hh * dv:(hh + 1) * dv])


def _ret_sample_call(proj, state, gn_gain, gn_bias, j, new_state, *, n_prompt_rows, n_seq, dec_seq, dk, dv):
    h = RET_HEADS
    s = SAMPLE_ROWS
    cos, sin = _rope_tables(PAST_LEN + jnp.arange(s, dtype=jnp.int32), dk // 2)
    dmask, qdec, kdec, gl = _ret_tables(s, dk, n_valid=dec_seq)
    dmask = jnp.pad(dmask, ((0, 0), (0, 0), (0, 128 - s)))
    base = n_prompt_rows // s
    body = functools.partial(_ret_sample_body, dk=dk, dv=dv)
    width = proj.shape[1]
    args = [proj, state, cos, sin, dmask, qdec, kdec, gl, gn_gain, gn_bias]
    in_specs = [
        pl.BlockSpec((s, width), lambda b: (base + b, 0)),
        pl.BlockSpec((None, None, h, dk, dv), lambda b: (j, b, 0, 0, 0)),
        pl.BlockSpec((s, dk // 2), lambda b: (0, 0)),
        pl.BlockSpec((s, dk // 2), lambda b: (0, 0)),
        pl.BlockSpec((h, s, 128), lambda b: (0, 0, 0)),
        pl.BlockSpec((h, s, dk), lambda b: (0, 0, 0)),
        pl.BlockSpec((h, s, dk), lambda b: (0, 0, 0)),
        pl.BlockSpec((h, 8, 128), lambda b: (0, 0, 0)),
        pl.BlockSpec((None, 1, h * dv), lambda b: (j, 0, 0)),
        pl.BlockSpec((None, 1, h * dv), lambda b: (j, 0, 0)),
    ]
    aliases = {}
    if new_state is not None:
        aliases = {len(args): 1}
        args.append(new_state)
        in_specs.append(pl.BlockSpec(memory_space=pl.ANY))
    return pl.pallas_call(
        body,
        out_shape=[jax.ShapeDtypeStruct((n_seq * s, h * dv), BF16),
                   jax.ShapeDtypeStruct(state.shape, F32)],
        grid=(n_seq,),
        in_specs=in_specs,
        out_specs=[pl.BlockSpec((s, h * dv), lambda b: (b, 0)),
                   pl.BlockSpec((None, None, h, dk, dv), lambda b: (j, b, 0, 0, 0))],
        input_output_aliases=aliases,
        compiler_params=_cparams(("arbitrary",)),
        name="retention_sample",
    )(*args)


CONV_HALO = 32
CONV_ROW_CHUNK = 64
CONV_LANES = 128


def _layer_norm_swish(u, gain, bias):
    mu = jnp.mean(u, axis=-1, keepdims=True)
    var = jnp.mean(jnp.square(u - mu), axis=-1, keepdims=True)
    un = (u - mu) * lax.rsqrt(var + NORM_EPS) * gain + bias
    return (un * jax.nn.sigmoid(un)).astype(BF16)


def _conv_prompt_body(z_ref, w_ref, bdw_ref, lng_ref, lnb_ref, o_ref, zbuf, ubuf, *, tt, d):
    t = pl.program_id(1)

    @pl.when(t == 0)
    def _():
        zbuf[0:CONV_HALO, :] = jnp.zeros((CONV_HALO, d), F32)

    @pl.when(t > 0)
    def _():
        zbuf[0:CONV_HALO, :] = zbuf[tt:tt + CONV_HALO, :]

    zbuf[CONV_HALO:CONV_HALO + tt, :] = z_ref[...].astype(F32)

    span = CONV_ROW_CHUNK + CONV_HALO
    n_row_chunks = tt // CONV_ROW_CHUNK
    n_strips = d // CONV_LANES

    def strip(n, carry):
        r0 = pl.multiple_of((n % n_row_chunks) * CONV_ROW_CHUNK, CONV_ROW_CHUNK)
        c0 = pl.multiple_of((n // n_row_chunks) * CONV_LANES, CONV_LANES)
        cols = pl.ds(c0, CONV_LANES)
        blk = zbuf[pl.ds(r0, span), cols]
        acc = jnp.zeros((CONV_ROW_CHUNK, CONV_LANES), F32)
        for b in range(8):
            rb = blk if b == 0 else pltpu.roll(blk, span - b, axis=0)
            for a in range(5):
                o = 8 * a + b
                if 2 <= o <= CONV_HALO:
                    acc = acc + rb[8 * a:8 * a + CONV_ROW_CHUNK, :] * w_ref[pl.ds(o - 2, 1), cols]
        ubuf[pl.ds(r0, CONV_ROW_CHUNK), cols] = acc
        return carry

    lax.fori_loop(0, n_row_chunks * n_strips, strip, 0)

    ln_rows = 128

    def ln_chunk(c, carry):
        rows = pl.ds(pl.multiple_of(c * ln_rows, ln_rows), ln_rows)
        o_ref[rows, :] = _layer_norm_swish(ubuf[rows, :] + bdw_ref[...], lng_ref[...], lnb_ref[...])
        return carry

    lax.fori_loop(0, tt // ln_rows, ln_chunk, 0)


def _conv_prompt_call(z, w_dw, b_dw, ln_gain, ln_bias, *, n_batch, seq):
    d = z.shape[1]
    tt = ROW_TILE
    nt = seq // tt
    w_pad = jnp.pad(w_dw, ((0, 32 - CONV_WIDTH), (0, 0)))
    body = functools.partial(_conv_prompt_body, tt=tt, d=d)
    vec = pl.BlockSpec((1, d), lambda b, t: (0, 0))
    return pl.pallas_call(
        body,
        out_shape=jax.ShapeDtypeStruct((n_batch * seq, d), BF16),
        grid=(n_batch, nt),
        in_specs=[pl.BlockSpec((tt, d), lambda b, t: (b * nt + t, 0)),
                  pl.BlockSpec((32, d), lambda b, t: (0, 0)), vec, vec, vec],
        out_specs=pl.BlockSpec((tt, d), lambda b, t: (b * nt + t, 0)),
        scratch_shapes=[pltpu.VMEM((tt + CONV_HALO, d), F32), pltpu.VMEM((tt, d), F32)],
        compiler_params=_cparams(("arbitrary", "arbitrary")),
        name="conv_prompt",
    )(z, w_pad, b_dw, ln_gain, ln_bias)


def _conv_sample_body(z_ref, c_ref, w_ref, bdw_ref, lng_ref, lnb_ref, o_ref, zbuf, *, d):
    s = SAMPLE_ROWS
    zbuf[0:CONV_STATE, :] = c_ref[...]
    zbuf[CONV_STATE:CONV_STATE + s, :] = z_ref[...].astype(F32)
    acc = jnp.zeros((s, d), F32)
    for j in range(CONV_WIDTH):
        acc = acc + zbuf[j:j + s, :] * w_ref[j:j + 1, :]
    o_ref[...] = _layer_norm_swish(acc + bdw_ref[...], lng_ref[...], lnb_ref[...])


def _conv_sample_call(z, cache, w_dw, b_dw, ln_gain, ln_bias, *, n_prompt_rows, n_seq):
    d = z.shape[1]
    s = SAMPLE_ROWS
    base = n_prompt_rows // s
    body = functools.partial(_conv_sample_body, d=d)
    vec = pl.BlockSpec((1, d), lambda b: (0, 0))
    return pl.pallas_call(
        body,
        out_shape=jax.ShapeDtypeStruct((n_seq * s, d), BF16),
        grid=(n_seq,),
        in_specs=[pl.BlockSpec((s, d), lambda b: (base + b, 0)),
                  pl.BlockSpec((None, CONV_STATE, d), lambda b: (b, 0, 0)),
                  pl.BlockSpec((CONV_WIDTH, d), lambda b: (0, 0)), vec, vec, vec],
        out_specs=pl.BlockSpec((s, d), lambda b: (b, 0)),
        scratch_shapes=[pltpu.VMEM((CONV_STATE + s, d), F32)],
        compiler_params=_cparams(("arbitrary",)),
        name="conv_sample",
    )(z, cache, w_dw, b_dw, ln_gain, ln_bias)


def _t5_bucket(dist):
    n = jnp.maximum(dist, 0)
    max_exact = REL_BUCKETS // 2
    nf = jnp.maximum(n, 1).astype(F32)
    large = max_exact + (jnp.log(nf / max_exact) / math.log(REL_MAX_DIST / max_exact)
                         * (REL_BUCKETS - max_exact)).astype(jnp.int32)
    large = jnp.minimum(large, REL_BUCKETS - 1)
    return jnp.where(n < max_exact, n, large)


def _bias_table(rel_bias, dist, valid):
    onehot = (_t5_bucket(dist).reshape(-1)[None, :] == jnp.arange(REL_BUCKETS, dtype=jnp.int32)[:, None]).astype(F32)
    tbl = jnp.dot(rel_bias.astype(F32).T, onehot, precision=lax.Precision.HIGHEST)
    tbl = jnp.where(valid.reshape(-1)[None, :], tbl, NEG_INF)
    return tbl.reshape(-1, dist.shape[-1])


def _softmax_sink_pv(s, sink, vals):
    m = jnp.maximum(jnp.max(s, axis=-1, keepdims=True), sink)
    p = jnp.exp(s - m)
    den = jnp.sum(p, axis=-1, keepdims=True) + jnp.exp(sink - m)
    o = jnp.dot(p.astype(BF16), vals, preferred_element_type=F32)
    return o / den


def _swa_prompt_body(sink_ref, q_ref, kvp_ref, kvc_ref, gq_ref, gk_ref, bias_ref, o_ref, kn_ref, *, hd, n_blocks):
    i = pl.program_id(1)
    blk = WINDOW
    g = ATT_GROUP
    nkv = ATT_KV_HEADS
    gq = gq_ref[...]
    gk = gk_ref[...]
    gqk = gq * gk
    ones_rep = jnp.ones((hd, 128), BF16)
    ones_row = jnp.ones((8, hd), BF16)
    is_first = i == 0
    col = lax.broadcasted_iota(jnp.int32, (g * blk, 2 * blk), 1)
    for hk in range(nkv):
        keys = jnp.concatenate([kvp_ref[:, hk * hd:(hk + 1) * hd], kvc_ref[:, hk * hd:(hk + 1) * hd]], axis=0)
        vals = jnp.concatenate([kvp_ref[:, (nkv + hk) * hd:(nkv + hk + 1) * hd],
                                kvc_ref[:, (nkv + hk) * hd:(nkv + hk + 1) * hd]], axis=0)
        kf = keys.astype(F32)
        ssq_k = lax.dot_general(ones_row, (kf * kf).astype(BF16), (((1,), (1,)), ((), ())),
                                preferred_element_type=F32)[0:1, :]
        rk = lax.rsqrt(ssq_k * (1.0 / hd) + NORM_EPS)
        qs = jnp.concatenate([q_ref[:, (hk * g + gg) * hd:(hk * g + gg + 1) * hd] for gg in range(g)], axis=0)
        qf = qs.astype(F32)
        ssq_q = jnp.dot((qf * qf).astype(BF16), ones_rep, preferred_element_type=F32)
        rq = lax.rsqrt(ssq_q * (1.0 / hd) + NORM_EPS)
        rq2 = jnp.concatenate([rq, rq], axis=1)
        s = lax.dot_general((qf * gqk).astype(BF16), keys, (((1,), (1,)), ((), ())), preferred_element_type=F32)
        s = s * rq2 * rk * (hd ** -0.5) + bias_ref[hk * g * blk:(hk + 1) * g * blk, :]
        s = jnp.where(jnp.logical_and(is_first, col < blk), NEG_INF, s)
        sink = jnp.concatenate([jnp.full((blk, 1), sink_ref[hk * g + gg], F32) for gg in range(g)], axis=0)
        o = _softmax_sink_pv(s, sink, vals)
        for gg in range(g):
            o_ref[:, (hk * g + gg) * hd:(hk * g + gg + 1) * hd] = o[gg * blk:(gg + 1) * blk, :].astype(BF16)

    @pl.when(i == n_blocks - 1)
    def _():
        for hk in range(nkv):
            kc = kvc_ref[:, hk * hd:(hk + 1) * hd].astype(F32)
            ssq = jnp.dot((kc * kc).astype(BF16), ones_rep, preferred_element_type=F32)[:, 0:hd]
            kn_ref[:, hk * hd:(hk + 1) * hd] = kc * lax.rsqrt(ssq * (1.0 / hd) + NORM_EPS) * gk


def _swa_prompt_call(proj, q_gain, k_gain, sinks, rel_bias, *, n_batch, seq, hd):
    blk = WINDOW
    nb = seq // blk
    nq = ATT_Q_HEADS * hd
    nkv2 = 2 * ATT_KV_HEADS * hd
    kvcol = nq // nkv2
    i_idx = jnp.arange(blk, dtype=jnp.int32)[:, None]
    j_idx = jnp.arange(2 * blk, dtype=jnp.int32)[None, :]
    dist = blk + i_idx - j_idx
    bias = _bias_table(rel_bias, dist, (dist >= 0) & (dist < WINDOW))
    body = functools.partial(_swa_prompt_body, hd=hd, n_blocks=nb)
    return pl.pallas_call(
        body,
        out_shape=[jax.ShapeDtypeStruct((n_batch * seq, nq), BF16),
                   jax.ShapeDtypeStruct((n_batch * blk, ATT_KV_HEADS * hd), F32)],
        grid=(n_batch, nb),
        in_specs=[
            pl.BlockSpec(memory_space=pltpu.SMEM),
            pl.BlockSpec((blk, nq), lambda b, i: (b * nb + i, 0)),
            pl.BlockSpec((blk, nkv2), lambda b, i: (b * nb + jnp.maximum(i - 1, 0), kvcol)),
            pl.BlockSpec((blk, nkv2), lambda b, i: (b * nb + i, kvcol)),
            pl.BlockSpec((1, hd), lambda b, i: (0, 0)),
            pl.BlockSpec((1, hd), lambda b, i: (0, 0)),
            pl.BlockSpec((ATT_Q_HEADS * blk, 2 * blk), lambda b, i: (0, 0)),
        ],
        out_specs=[pl.BlockSpec((blk, nq), lambda b, i: (b * nb + i, 0)),
                   pl.BlockSpec((blk, ATT_KV_HEADS * hd), lambda b, i: (b, 0))],
        compiler_params=_cparams(("arbitrary", "arbitrary")),
        name="swa_prompt",
    )(sinks, proj, proj, proj, q_gain, k_gain, bias)


def _swa_sample_body(sink_ref, p_ref, ck_ref, cv_ref, gq_ref, gk_ref, bias_ref, o_ref, kn_ref, *, hd):
    s = SAMPLE_ROWS
    g = ATT_GROUP
    nkv = ATT_KV_HEADS
    nq = ATT_Q_HEADS * hd
    gq = gq_ref[...]
    gk = gk_ref[...]

    def rms(x, gain):
        return x * lax.rsqrt(jnp.mean(x * x, axis=-1, keepdims=True) + NORM_EPS) * gain

    for hk in range(nkv):
        kn = rms(p_ref[:, nq + hk * hd:nq + (hk + 1) * hd].astype(F32), gk)
        kn_ref[:, hk * hd:(hk + 1) * hd] = kn
        keys = jnp.concatenate([ck_ref[:, hk * hd:(hk + 1) * hd], kn], axis=0).astype(BF16)
        vnew = p_ref[:, nq + (nkv + hk) * hd:nq + (nkv + hk + 1) * hd].astype(F32)
        vals = jnp.concatenate([cv_ref[:, hk * hd:(hk + 1) * hd], vnew], axis=0).astype(BF16)
        qs = jnp.concatenate([rms(p_ref[:, (hk * g + gg) * hd:(hk * g + gg + 1) * hd].astype(F32), gq)
                              for gg in range(g)], axis=0)
        sc = lax.dot_general(qs.astype(BF16), keys, (((1,), (1,)), ((), ())), preferred_element_type=F32)
        sc = sc * (hd ** -0.5) + bias_ref[hk * g * s:(hk + 1) * g * s, :]
        sink = jnp.concatenate([jnp.full((s, 1), sink_ref[hk * g + gg], F32) for gg in range(g)], axis=0)
        o = _softmax_sink_pv(sc, sink, vals)
        for gg in range(g):
            o_ref[:, (hk * g + gg) * hd:(hk * g + gg + 1) * hd] = o[gg * s:(gg + 1) * s, :].astype(BF16)


def _swa_sample_call(proj, cache_k, cache_v, q_gain, k_gain, sinks, rel_bias, *, n_prompt_rows, n_seq, dec_seq, hd):
    s = SAMPLE_ROWS
    nq = ATT_Q_HEADS * hd
    nkv = ATT_KV_HEADS * hd
    sc = cache_k.shape[1]
    i_idx = jnp.arange(s, dtype=jnp.int32)[:, None]
    j_idx = jnp.arange(sc + s, dtype=jnp.int32)[None, :]
    dist = sc + i_idx - j_idx
    valid = (dist >= 0) & (dist < WINDOW) & (j_idx < sc + dec_seq)
    bias = _bias_table(rel_bias, dist, valid)
    base = n_prompt_rows // s
    body = functools.partial(_swa_sample_body, hd=hd)
    return pl.pallas_call(
        body,
        out_shape=[jax.ShapeDtypeStruct((n_seq * s, nq), BF16),
                   jax.ShapeDtypeStruct((n_seq * s, nkv), F32)],
        grid=(n_seq,),
        in_specs=[
            pl.BlockSpec(memory_space=pltpu.SMEM),
            pl.BlockSpec((s, proj.shape[1]), lambda b: (base + b, 0)),
            pl.BlockSpec((None, sc, nkv), lambda b: (b, 0, 0)),
            pl.BlockSpec((None, sc, nkv), lambda b: (b, 0, 0)),
            pl.BlockSpec((1, hd), lambda b: (0, 0)),
            pl.BlockSpec((1, hd), lambda b: (0, 0)),
            pl.BlockSpec((ATT_Q_HEADS * s, sc + s), lambda b: (0, 0)),
        ],
        out_specs=[pl.BlockSpec((s, nq), lambda b: (b, 0)),
                   pl.BlockSpec((s, nkv), lambda b: (b, 0))],
        compiler_params=_cparams(("arbitrary",)),
        name="swa_sample",
    )(sinks, proj, cache_k, cache_v, q_gain, k_gain, bias)


def _moe_in_body(tg_ref, tv_ref, x_ref, w_ref, cw_ref, o_ref, wb, *, ff):
    t = pl.program_id(1)
    changed = jnp.logical_or(t == 0, tg_ref[t] != tg_ref[jnp.maximum(t - 1, 0)])

    @pl.when(changed)
    def _():
        wb[...] = w_ref[...].astype(BF16)

    @pl.when(tv_ref[t] > 0)
    def _():
        gu = jnp.dot(x_ref[...], wb[...], preferred_element_type=F32)
        gate, up = gu[:, :ff], gu[:, ff:]
        cw_all = cw_ref[...]
        lane = lax.broadcasted_iota(jnp.int32, cw_all.shape, 1)
        cw = jnp.sum(jnp.where(lane == pl.program_id(0), cw_all, 0.0), axis=-1, keepdims=True)
        o_ref[...] = (gate * jax.nn.sigmoid(gate) * up * cw).astype(BF16)

    @pl.when(tv_ref[t] == 0)
    def _():
        o_ref[...] = jnp.zeros_like(o_ref)


def _moe_out_body(tg_ref, tv_ref, a_ref, w_ref, o_ref, wb):
    t = pl.program_id(1)
    changed = jnp.logical_or(t == 0, tg_ref[t] != tg_ref[jnp.maximum(t - 1, 0)])

    @pl.when(changed)
    def _():
        wb[...] = w_ref[...].astype(BF16)

    @pl.when(tv_ref[t] > 0)
    def _():
        o_ref[...] = jnp.dot(a_ref[...], wb[...], preferred_element_type=F32).astype(o_ref.dtype)

    @pl.when(tv_ref[t] == 0)
    def _():
        o_ref[...] = jnp.zeros_like(o_ref)


def _moe_call(xs, cw_sorted, tile_group, tile_valid, w_in, w_out, layer):
    npad, d = xs.shape
    depth, e, _, ff2 = w_in.shape
    ff = ff2 // 2
    n_tiles = npad // ROW_TILE
    act = pl.pallas_call(
        functools.partial(_moe_in_body, ff=ff),
        out_shape=jax.ShapeDtypeStruct((npad, MOE_EPG * ff), BF16),
        grid_spec=pltpu.PrefetchScalarGridSpec(
            num_scalar_prefetch=2,
            grid=(MOE_EPG, n_tiles),
            in_specs=[
                pl.BlockSpec((ROW_TILE, d), lambda el, t, tg, tv: (t, 0)),
                pl.BlockSpec((None, None, d, ff2), lambda el, t, tg, tv: (layer, tg[t] * MOE_EPG + el, 0, 0)),
                pl.BlockSpec((ROW_TILE, ROUTER_LANES), lambda el, t, tg, tv: (t, 0)),
            ],
            out_specs=pl.BlockSpec((ROW_TILE, ff), lambda el, t, tg, tv: (t, el)),
            scratch_shapes=[pltpu.VMEM((d, ff2), BF16)],
        ),
        compiler_params=_cparams(("arbitrary", "arbitrary")),
        name="moe_in",
    )(tile_group, tile_valid, xs, w_in, cw_sorted)

    w_out_g = w_out.reshape(depth, MOE_GROUPS, MOE_EPG * ff, d)
    tn = d // 2
    return pl.pallas_call(
        _moe_out_body,
        out_shape=jax.ShapeDtypeStruct((npad, d), BF16),
        grid_spec=pltpu.PrefetchScalarGridSpec(
            num_scalar_prefetch=2,
            grid=(d // tn, n_tiles),
            in_specs=[
                pl.BlockSpec((ROW_TILE, MOE_EPG * ff), lambda j, t, tg, tv: (t, 0)),
                pl.BlockSpec((None, None, MOE_EPG * ff, tn), lambda j, t, tg, tv: (layer, tg[t], 0, j)),
            ],
            out_specs=pl.BlockSpec((ROW_TILE, tn), lambda j, t, tg, tv: (t, j)),
            scratch_shapes=[pltpu.VMEM((MOE_EPG * ff, tn), BF16)],
        ),
        compiler_params=_cparams(("arbitrary", "arbitrary")),
        name="moe_out",
    )(tile_group, tile_valid, act, w_out_g)


def _route(logits, bg, be):
    g, epg = MOE_GROUPS, MOE_EPG
    n = logits.shape[0]
    pg = jax.nn.softmax(logits[:, :g] + bg, axis=-1)
    grp = jnp.argmax(pg, axis=-1)
    p_grp = jnp.max(pg, axis=-1, keepdims=True)
    le = (logits[:, g:g + g * epg] + be).reshape(n, g, epg)
    le = jnp.sum(jnp.where((jnp.arange(g)[None, :] == grp[:, None])[:, :, None], le, 0.0), axis=1)
    pe = jax.nn.softmax(le, axis=-1)
    lane = jnp.arange(epg)[None, :]
    e1 = jnp.argmax(pe, axis=-1)
    pe_rest = jnp.where(lane == e1[:, None], -1.0, pe)
    e2 = jnp.argmax(pe_rest, axis=-1)
    p1 = jnp.max(pe, axis=-1, keepdims=True)
    p2 = jnp.max(pe_rest, axis=-1, keepdims=True)
    tot = p1 + p2
    cw = (jnp.where(lane == e1[:, None], p_grp * p1 / tot, 0.0)
          + jnp.where(lane == e2[:, None], p_grp * p2 / tot, 0.0))
    return grp.astype(jnp.int32), cw


def _moe_layer(h2, logits, bg, be, w_in, w_out, layer):
    r, d = h2.shape
    g = MOE_GROUPS
    grp, cw = _route(logits, bg, be)
    onehot = (grp[:, None] == jnp.arange(g, dtype=grp.dtype)[None, :]).astype(jnp.int32)
    csum = jnp.cumsum(onehot, axis=0)
    counts = csum[-1]
    rank = jnp.sum((csum - onehot) * onehot, axis=1)
    padded = ((counts + ROW_TILE - 1) // ROW_TILE) * ROW_TILE
    ends = jnp.cumsum(padded)
    starts = ends - padded
    pos = starts[grp] + rank
    n_tiles = r // ROW_TILE + g
    npad = n_tiles * ROW_TILE
    src = jnp.zeros((npad,), jnp.int32).at[pos].set(jnp.arange(r, dtype=jnp.int32))
    filled = jnp.zeros((npad,), F32).at[pos].set(1.0)
    tile_start = jnp.arange(n_tiles, dtype=jnp.int32) * ROW_TILE
    tile_group = jnp.minimum(jnp.sum(tile_start[:, None] >= ends[None, :], axis=1), g - 1).astype(jnp.int32)
    tile_valid = (tile_start < ends[-1]).astype(jnp.int32)
    xs = jnp.take(h2, src, axis=0, mode="clip")
    cw_lanes = jnp.pad(cw, ((0, 0), (0, ROUTER_LANES - MOE_EPG)))
    cw_sorted = jnp.take(cw_lanes, src, axis=0, mode="clip") * filled[:, None]
    y_sorted = _moe_call(xs, cw_sorted, tile_group, tile_valid, w_in, w_out, layer)
    return jnp.take(y_sorted, pos, axis=0, mode="clip")


def kernel(x_prompt, x_sample, c_prompt, c_sample, state_ret, cache_conv, cache_swa_k, cache_swa_v, ada_w, ada_b, norm_gain, ret_w_in, ret_gn_gain, ret_gn_bias, ret_w_out, conv_w_pw1, conv_b_pw1, conv_w_dw, conv_b_dw, conv_ln_gain, conv_ln_bias, conv_w_pw2, conv_b_pw2, att_w_qkv, att_q_gain, att_k_gain, att_sinks, att_w_o, rel_bias, moe_wg, moe_bg, moe_we, moe_be, moe_w_in, moe_w_out):
    n_batch, seq, d = x_prompt.shape
    n_seq, dec_seq, _ = x_sample.shape
    depth = ada_w.shape[0]
    s = SAMPLE_ROWS
    assert n_seq * s == ROW_TILE and seq % ROW_TILE == 0 and dec_seq <= s
    n_prompt_rows = n_batch * seq
    seq_tiles = seq // ROW_TILE
    dk = ret_w_in.shape[2] // (6 * RET_HEADS)
    dv = 2 * dk
    hd = d // ATT_Q_HEADS
    geom = dict(seq_tiles=seq_tiles, n_batch=n_batch)

    xs_pad = jnp.pad(x_sample, ((0, 0), (0, s - dec_seq), (0, 0))).reshape(n_seq * s, d)
    x = jnp.concatenate([x_prompt.reshape(n_prompt_rows, d), xs_pad], axis=0)

    n_c = n_batch + n_seq
    c_rows = ((n_c + 7) // 8) * 8
    c_all = jnp.pad(jnp.concatenate([c_prompt, c_sample], axis=0), ((0, c_rows - n_c), (0, 0)))
    mod = _ada_call(c_all, ada_w, ada_b)
    mod4 = mod.reshape(depth, c_rows, 1, 6 * d)
    modtok = jnp.repeat(mod[:, n_batch:n_c], s, axis=1)

    gains = norm_gain.astype(F32).reshape(2 * depth, 1, d)
    gng_all = ret_gn_gain.astype(F32)[:, None, :]
    gnb_all = ret_gn_bias.astype(F32)[:, None, :]
    state_all = state_ret.astype(F32)

    ret_p, conv_p, conv_s, kp_l, vp_l, ks_l, vs_l = [], [], [], [], [], [], []
    ret_s = None
    (h,) = _norm_call(x, mod4, modtok, norm=(gains, 0, 0, 0, 1), **geom)
    for l in range(depth):
        kind, j = l % 3, l // 3
        resid1 = (x, mod4, modtok, l, 2)
        if kind == 0:
            proj = _linear_call(h, ret_w_in, j, n_out=ret_w_in.shape[2], tn=1024, out_dtype=BF16,
                                name="ret_in", **geom)
            a_p, st_p = _ret_prompt_call(proj, gng_all, gnb_all, j, n_batch=n_batch, seq=seq, dk=dk, dv=dv)
            a_s, ret_s = _ret_sample_call(proj, state_all, gng_all, gnb_all, j, ret_s, n_prompt_rows=n_prompt_rows,
                                          n_seq=n_seq, dec_seq=dec_seq, dk=dk, dv=dv)
            ret_p.append(st_p)
            x = _linear_call(a_p, ret_w_out, j, h_sample=a_s, n_out=d, tn=512, out_dtype=F32, resid=resid1,
                             name="ret_out", **geom)
        elif kind == 1:
            z = _linear_call(h, conv_w_pw1, j, n_out=d, tn=512, out_dtype=BF16, bias=conv_b_pw1,
                             glu=True, name="conv_pw1", **geom)
            cargs = (conv_w_dw[j], conv_b_dw[j][None], conv_ln_gain[j][None], conv_ln_bias[j][None])
            a_p = _conv_prompt_call(z, *cargs, n_batch=n_batch, seq=seq)
            a_s = _conv_sample_call(z, cache_conv[j].astype(F32), *cargs, n_prompt_rows=n_prompt_rows, n_seq=n_seq)
            z_tail = z[:n_prompt_rows].reshape(n_batch, seq, d)[:, seq - CONV_STATE:].astype(F32)
            conv_p.append(z_tail)
            z_new = z[n_prompt_rows:].reshape(n_seq, s, d)[:, :dec_seq].astype(F32)
            conv_s.append(jnp.concatenate([cache_conv[j].astype(F32), z_new], axis=1)[:, -CONV_STATE:])
            x = _linear_call(a_p, conv_w_pw2, j, h_sample=a_s, n_out=d, tn=1024, out_dtype=F32,
                             bias=conv_b_pw2, resid=resid1, name="conv_pw2", **geom)
        else:
            nkv = ATT_KV_HEADS * hd
            proj = _linear_call(h, att_w_qkv, j, n_out=att_w_qkv.shape[2], tn=512, out_dtype=BF16,
                                name="att_qkv", **geom)
            gq, gk = att_q_gain[j][None].astype(F32), att_k_gain[j][None].astype(F32)
            sinks = att_sinks[j].astype(F32)
            a_p, kn_p = _swa_prompt_call(proj, gq, gk, sinks, rel_bias, n_batch=n_batch, seq=seq, hd=hd)
            win = cache_swa_k.shape[2]
            ck = cache_swa_k[j].astype(F32).reshape(n_seq, win, nkv)
            cv = cache_swa_v[j].astype(F32).reshape(n_seq, win, nkv)
            a_s, kn_s = _swa_sample_call(proj, ck, cv, gq, gk, sinks, rel_bias, n_prompt_rows=n_prompt_rows,
                                         n_seq=n_seq, dec_seq=dec_seq, hd=hd)
            vcol = ATT_Q_HEADS * hd + nkv
            v_tail = proj[:n_prompt_rows, vcol:].reshape(n_batch, seq, nkv)[:, seq - WINDOW:].astype(F32)
            kp_l.append(kn_p.reshape(n_batch, WINDOW, ATT_KV_HEADS, hd))
            vp_l.append(v_tail.reshape(n_batch, WINDOW, ATT_KV_HEADS, hd))
            k_new = kn_s.reshape(n_seq, s, nkv)[:, :dec_seq]
            v_new = proj[n_prompt_rows:, vcol:].reshape(n_seq, s, nkv)[:, :dec_seq].astype(F32)
            ks_l.append(jnp.concatenate([ck, k_new], axis=1)[:, -win:].reshape(n_seq, win, ATT_KV_HEADS, hd))
            vs_l.append(jnp.concatenate([cv, v_new], axis=1)[:, -win:].reshape(n_seq, win, ATT_KV_HEADS, hd))
            x = _linear_call(a_p, att_w_o, j, h_sample=a_s, n_out=d, tn=1024, out_dtype=F32, resid=resid1,
                             name="att_out", **geom)

        router_w = jnp.pad(jnp.concatenate([moe_wg[l], moe_we[l]], axis=1).astype(F32),
                           ((0, 0), (0, ROUTER_LANES - MOE_GROUPS - MOE_GROUPS * MOE_EPG)))
        h2, logits = _norm_call(x, mod4, modtok, norm=(gains, 2 * l + 1, l, 3, 4), router_w=router_w, **geom)
        y = _moe_layer(h2, logits, moe_bg[l], moe_be[l], moe_w_in, moe_w_out, l)
        if l + 1 < depth:
            x, h = _norm_call(x, mod4, modtok, resid=(y, l, 5), norm=(gains, 2 * l + 2, l + 1, 0, 1), **geom)
        else:
            x_p, x_s = _norm_call(x, mod4, modtok, resid=(y, l, 5), split_out=True, **geom)

    y_prompt = x_p.reshape(n_batch, seq, d)
    y_sample = x_s.reshape(n_seq, s, d)[:, :dec_seq]
    return (y_prompt, y_sample, jnp.stack(ret_p), ret_s, jnp.stack(conv_p), jnp.stack(conv_s),
            jnp.stack(kp_l), jnp.stack(vp_l), jnp.stack(ks_l), jnp.stack(vs_l))
```

```python
import functools
import math

import jax
import jax.numpy as jnp
from jax import lax
from jax.experimental import pallas as pl
from jax.experimental.pallas import tpu as pltpu

F32 = jnp.float32
BF16 = jnp.bfloat16

NORM_EPS = 1e-6
NEG_INF = -1e30
ROPE_BASE = 10000.0
PAST_LEN = 16384

RET_HEADS = 8
RET_CHUNK = 128
CONV_WIDTH = 31
CONV_STATE = CONV_WIDTH - 1
ATT_Q_HEADS = 32
ATT_KV_HEADS = 4
ATT_GROUP = ATT_Q_HEADS // ATT_KV_HEADS
WINDOW = 128
REL_BUCKETS = 32
REL_MAX_DIST = 128
MOE_GROUPS = 4
MOE_EPG = 4
MOE_TOPK = 2

ROW_TILE = 512
SAMPLE_ROWS = 16
ROUTER_LANES = 128
VMEM_LIMIT_BYTES = 56 * 1024 * 1024


def _cparams(sem):
    return pltpu.CompilerParams(dimension_semantics=sem, vmem_limit_bytes=VMEM_LIMIT_BYTES)


def _ada_body(c_ref, w_ref, b_ref, o_ref):
    c = c_ref[...]
    s = (c * jax.nn.sigmoid(c)).astype(BF16)
    o_ref[...] = jnp.dot(s, w_ref[...].astype(BF16), preferred_element_type=F32) + b_ref[...]


def _ada_call(c_all, ada_w, ada_b):
    depth, d, n = ada_w.shape
    rows = c_all.shape[0]
    tn = 1024
    return pl.pallas_call(
        _ada_body,
        out_shape=jax.ShapeDtypeStruct((depth, rows, n), F32),
        grid=(depth, n // tn),
        in_specs=[
            pl.BlockSpec((rows, d), lambda l, j: (0, 0)),
            pl.BlockSpec((None, d, tn), lambda l, j: (l, 0, j)),
            pl.BlockSpec((None, 1, tn), lambda l, j: (l, 0, j)),
        ],
        out_specs=pl.BlockSpec((None, rows, tn), lambda l, j: (l, 0, j)),
        compiler_params=_cparams(("arbitrary", "arbitrary")),
        name="ada_mod",
    )(c_all, ada_w, ada_b.reshape(depth, 1, n))


ROUTE_E1, ROUTE_E2, ROUTE_W1, ROUTE_W2, ROUTE_R1, ROUTE_R2 = range(6)


def _route_tile(logits, rb, carry):
    tm = logits.shape[0]
    g, epg = MOE_GROUPS, MOE_EPG
    lg = logits + rb
    lane = lax.broadcasted_iota(jnp.int32, lg.shape, 1)
    lane_f = lane.astype(F32)

    def first_lane(mask):
        return jnp.min(jnp.where(mask, lane_f, float(ROUTER_LANES)), axis=-1, keepdims=True).astype(jnp.int32)

    is_g = lane < g
    mg = jnp.max(jnp.where(is_g, lg, NEG_INF), axis=-1, keepdims=True)
    eg = jnp.where(is_g, jnp.exp(lg - mg), 0.0)
    p_grp = 1.0 / jnp.sum(eg, axis=-1, keepdims=True)
    grp = first_lane(jnp.logical_and(is_g, lg == mg))
    lo = g + epg * grp
    is_e = jnp.logical_and(lane >= lo, lane < lo + epg)
    me = jnp.max(jnp.where(is_e, lg, NEG_INF), axis=-1, keepdims=True)
    ee = jnp.where(is_e, jnp.exp(lg - me), 0.0)
    se = jnp.sum(ee, axis=-1, keepdims=True)
    i1 = first_lane(jnp.logical_and(is_e, lg == me))
    rest = jnp.logical_and(is_e, lane != i1)
    m2 = jnp.max(jnp.where(rest, ee, -1.0), axis=-1, keepdims=True)
    i2 = first_lane(jnp.logical_and(rest, ee == m2))
    p1 = 1.0 / se
    p2 = m2 / se
    w1 = p_grp * p1 / (p1 + p2)
    w2 = p_grp * p2 / (p1 + p2)
    sel1 = lane == i1
    sel2 = lane == i2
    onehot = jnp.where(jnp.logical_or(sel1, sel2), 1.0, 0.0)
    row = lax.broadcasted_iota(jnp.int32, (tm, tm), 0)
    col = lax.broadcasted_iota(jnp.int32, (tm, tm), 1)
    tri = jnp.where(col <= row, 1.0, 0.0).astype(BF16)
    incl = jnp.dot(tri, onehot.astype(BF16), preferred_element_type=F32)
    rank = incl - 1.0 + carry
    r1 = jnp.sum(jnp.where(sel1, rank, 0.0), axis=-1, keepdims=True)
    r2 = jnp.sum(jnp.where(sel2, rank, 0.0), axis=-1, keepdims=True)
    rec = jnp.zeros_like(lg)
    for ln, val in ((ROUTE_E1, (i1 - g).astype(F32)), (ROUTE_E2, (i2 - g).astype(F32)), (ROUTE_W1, w1),
                    (ROUTE_W2, w2), (ROUTE_R1, r1), (ROUTE_R2, r2)):
        rec = jnp.where(lane == ln, val, rec)
    return rec, carry + jnp.sum(onehot, axis=0, keepdims=True)


def _norm_body(*refs, n_prompt_tiles, has_resid, has_norm, has_router, split_out):
    it = iter(refs)
    x_ref = next(it)
    if has_resid:
        y0_ref, y1_ref, rt_ref, grow_ref, gtok_ref = next(it), next(it), next(it), next(it), next(it)
    if has_norm:
        gain_ref, shrow_ref, scrow_ref, shtok_ref, sctok_ref = next(it), next(it), next(it), next(it), next(it)
    if has_router:
        wr_ref, rb_ref = next(it), next(it)
    if has_resid:
        xo_ref = next(it)
        if split_out:
            xs_ref = next(it)
    if has_norm:
        h_ref = next(it)
    if has_router:
        rec_ref, cnt_ref, carry = next(it), next(it), next(it)

    is_sample = pl.program_id(0) >= n_prompt_tiles
    x = x_ref[...]
    if has_resid:
        gate = jnp.where(is_sample, gtok_ref[...], grow_ref[...])
        rt = rt_ref[...]
        y = rt[:, ROUTE_W1:ROUTE_W1 + 1] * y0_ref[...] + rt[:, ROUTE_W2:ROUTE_W2 + 1] * y1_ref[...]
        x = x + gate * y
        if split_out:
            @pl.when(jnp.logical_not(is_sample))
            def _():
                xo_ref[...] = x

            @pl.when(is_sample)
            def _():
                xs_ref[...] = x
        else:
            xo_ref[...] = x
    if has_norm:
        ms = jnp.mean(x * x, axis=-1, keepdims=True)
        xn = x * lax.rsqrt(ms + NORM_EPS) * gain_ref[...]
        scale = jnp.where(is_sample, sctok_ref[...], scrow_ref[...])
        shift = jnp.where(is_sample, shtok_ref[...], shrow_ref[...])
        h = xn * (1.0 + scale) + shift
        h_ref[...] = h.astype(h_ref.dtype)
        if has_router:
            @pl.when(pl.program_id(0) == 0)
            def _():
                carry[...] = jnp.zeros_like(carry)

            logits = jnp.dot(h, wr_ref[...], precision=lax.Precision.HIGHEST, preferred_element_type=F32)
            rec, new_carry = _route_tile(logits, rb_ref[...], carry[0:1, :])
            rec_ref[...] = rec
            carry[...] = jnp.broadcast_to(new_carry, carry.shape)
            cnt_ref[...] = carry[...]


def _norm_call(x, mod4, modtok, *, seq_tiles, n_batch, resid=None, norm=None, router=None, split_out=False):
    r, d = x.shape
    n_tiles = r // ROW_TILE
    n_prompt_tiles = n_tiles - 1

    def row_spec(layer, col):
        return pl.BlockSpec((None, None, 1, d),
                            lambda i: (layer, jnp.minimum(i // seq_tiles, n_batch - 1), 0, col))

    def tok_spec(layer, col):
        return pl.BlockSpec((None, ROW_TILE, d), lambda i: (layer, 0, col))

    tile = pl.BlockSpec((ROW_TILE, d), lambda i: (i, 0))
    args, in_specs, out_shape, out_specs = [x], [tile], [], []
    scratch = []
    if resid is not None:
        ypair, route, layer, gcol = resid
        args += [ypair, ypair, route, mod4, modtok]
        in_specs += [tile, pl.BlockSpec((ROW_TILE, d), lambda i: (n_tiles + i, 0)),
                     pl.BlockSpec((ROW_TILE, ROUTER_LANES), lambda i: (i, 0)),
                     row_spec(layer, gcol), tok_spec(layer, gcol)]
        if split_out:
            out_shape += [jax.ShapeDtypeStruct((r - ROW_TILE, d), F32), jax.ShapeDtypeStruct((ROW_TILE, d), F32)]
            out_specs += [pl.BlockSpec((ROW_TILE, d), lambda i: (jnp.minimum(i, n_prompt_tiles - 1), 0)),
                          pl.BlockSpec((ROW_TILE, d), lambda i: (0, 0))]
        else:
            out_shape.append(jax.ShapeDtypeStruct((r, d), F32))
            out_specs.append(tile)
    if norm is not None:
        gains, gidx, layer, shcol, sccol = norm
        args += [gains, mod4, mod4, modtok, modtok]
        in_specs += [pl.BlockSpec((None, 1, d), lambda i: (gidx, 0, 0)), row_spec(layer, shcol),
                     row_spec(layer, sccol), tok_spec(layer, shcol), tok_spec(layer, sccol)]
        out_shape.append(jax.ShapeDtypeStruct((r, d), F32 if router is not None else BF16))
        out_specs.append(tile)
        if router is not None:
            args += list(router)
            in_specs += [pl.BlockSpec((d, ROUTER_LANES), lambda i: (0, 0)),
                         pl.BlockSpec((1, ROUTER_LANES), lambda i: (0, 0))]
            out_shape += [jax.ShapeDtypeStruct((r, ROUTER_LANES), F32), jax.ShapeDtypeStruct((8, ROUTER_LANES), F32)]
            out_specs += [pl.BlockSpec((ROW_TILE, ROUTER_LANES), lambda i: (i, 0)),
                          pl.BlockSpec((8, ROUTER_LANES), lambda i: (0, 0))]
            scratch.append(pltpu.VMEM((8, ROUTER_LANES), F32))
    body = functools.partial(_norm_body, n_prompt_tiles=n_prompt_tiles, has_resid=resid is not None,
                             has_norm=norm is not None, has_router=router is not None, split_out=split_out)
    return pl.pallas_call(
        body, out_shape=out_shape, grid=(n_tiles,), in_specs=in_specs, out_specs=out_specs,
        scratch_shapes=scratch, compiler_params=_cparams(("arbitrary",)), name="mod_norm",
    )(*args)


def _linear_body(*refs, n_prompt_tiles, pair, glu, has_bias, has_resid):
    it = iter(refs)
    h_ref = next(it)
    hs_ref = next(it) if pair else None
    w_ref = next(it)
    w2_ref = next(it) if glu else None
    b_ref = next(it) if has_bias else None
    b2_ref = next(it) if glu else None
    if has_resid:
        x_ref, grow_ref, gtok_ref = next(it), next(it), next(it)
    o_ref = next(it)
    wb = next(it)
    wb2 = next(it) if glu else None

    i = pl.program_id(1)
    is_sample = i >= n_prompt_tiles

    @pl.when(i == 0)
    def _():
        wb[...] = w_ref[...].astype(BF16)
        if glu:
            wb2[...] = w2_ref[...].astype(BF16)

    def compute(hv):
        acc = jnp.dot(hv, wb[...], preferred_element_type=F32)
        if has_bias:
            acc = acc + b_ref[...]
        if glu:
            acc2 = jnp.dot(hv, wb2[...], preferred_element_type=F32) + b2_ref[...]
            acc = acc * jax.nn.sigmoid(acc2)
        if has_resid:
            gate = jnp.where(is_sample, gtok_ref[...], grow_ref[...])
            acc = x_ref[...] + gate * acc
        o_ref[...] = acc.astype(o_ref.dtype)

    if pair:
        @pl.when(jnp.logical_not(is_sample))
        def _():
            compute(h_ref[...])

        @pl.when(is_sample)
        def _():
            compute(hs_ref[...])
    else:
        compute(h_ref[...])


def _linear_call(h, w, wl, *, n_out, tn, out_dtype, seq_tiles, n_batch, h_sample=None, bias=None, glu=False,
                 resid=None, name="linear"):
    k = w.shape[1]
    pair = h_sample is not None
    n_prompt_tiles = h.shape[0] // ROW_TILE - (0 if pair else 1)
    n_tiles = n_prompt_tiles + 1
    r = n_tiles * ROW_TILE
    nblk = n_out // tn

    args = [h]
    in_specs = [pl.BlockSpec((ROW_TILE, k), lambda j, i: (jnp.minimum(i, n_prompt_tiles - 1) if pair else i, 0))]
    if pair:
        args.append(h_sample)
        in_specs.append(pl.BlockSpec((ROW_TILE, k), lambda j, i: (0, 0)))
    args.append(w)
    in_specs.append(pl.BlockSpec((None, k, tn), lambda j, i: (wl, 0, j)))
    if glu:
        args.append(w)
        in_specs.append(pl.BlockSpec((None, k, tn), lambda j, i: (wl, 0, nblk + j)))
    if bias is not None:
        bias = bias.reshape(bias.shape[0], 1, bias.shape[1])
        args.append(bias)
        in_specs.append(pl.BlockSpec((None, 1, tn), lambda j, i: (wl, 0, j)))
        if glu:
            args.append(bias)
            in_specs.append(pl.BlockSpec((None, 1, tn), lambda j, i: (wl, 0, nblk + j)))
    if resid is not None:
        x, mod4, modtok, layer, gcol = resid
        cb = gcol * nblk
        args += [x, mod4, modtok]
        in_specs += [
            pl.BlockSpec((ROW_TILE, tn), lambda j, i: (i, j)),
            pl.BlockSpec((None, None, 1, tn),
                         lambda j, i: (layer, jnp.minimum(i // seq_tiles, n_batch - 1), 0, cb + j)),
            pl.BlockSpec((None, ROW_TILE, tn), lambda j, i: (layer, 0, cb + j)),
        ]
    scratch = [pltpu.VMEM((k, tn), BF16)] + ([pltpu.VMEM((k, tn), BF16)] if glu else [])
    body = functools.partial(_linear_body, n_prompt_tiles=n_prompt_tiles, pair=pair, glu=glu,
                             has_bias=bias is not None, has_resid=resid is not None)
    return pl.pallas_call(
        body,
        out_shape=jax.ShapeDtypeStruct((r, n_out), out_dtype),
        grid=(nblk, n_tiles),
        in_specs=in_specs,
        out_specs=pl.BlockSpec((ROW_TILE, tn), lambda j, i: (i, j)),
        scratch_shapes=scratch,
        compiler_params=_cparams(("arbitrary", "arbitrary")),
        name=name,
    )(*args)


def _rotate(x, cos, sin):
    half = x.shape[-1] // 2
    x1, x2 = x[:, :half], x[:, half:]
    return jnp.concatenate([x1 * cos - x2 * sin, x1 * sin + x2 * cos], axis=-1)


def _group_norm_gate(o, g, gain, bias):
    mu = jnp.mean(o, axis=-1, keepdims=True)
    var = jnp.mean(jnp.square(o - mu), axis=-1, keepdims=True)
    on = (o - mu) * lax.rsqrt(var + NORM_EPS) * gain + bias
    g = g.astype(F32)
    return (g * jax.nn.sigmoid(g) * on).astype(BF16)


def _ret_prompt_body(q_ref, k_ref, v_ref, g_ref, cos_ref, sin_ref, dm_ref, qd_ref, kd_ref, gl_ref,
                     gng_ref, gnb_ref, o_ref, st_ref, s_acc, *, n_chunks):
    s_acc[...] = jnp.zeros_like(s_acc)
    dmask = dm_ref[...]
    qdec = qd_ref[...]
    kdec = kd_ref[...]
    gl = gl_ref[0:1, 0:1]
    gng = gng_ref[...]
    gnb = gnb_ref[...]

    def chunk(c, carry):
        r0 = pl.multiple_of(c * RET_CHUNK, RET_CHUNK)
        rows = pl.ds(r0, RET_CHUNK)
        cos = cos_ref[rows, :]
        sin = sin_ref[rows, :]
        qr = _rotate(q_ref[rows, :].astype(F32), cos, sin)
        kr = _rotate(k_ref[rows, :].astype(F32), cos, sin)
        v = v_ref[rows, :]
        state = s_acc[...]
        scores = lax.dot_general(qr.astype(BF16), kr.astype(BF16), (((1,), (1,)), ((), ())),
                                 preferred_element_type=F32)
        scores = scores * dmask
        out = jnp.dot(scores.astype(BF16), v, preferred_element_type=F32)
        out = out + jnp.dot((qr * qdec).astype(BF16), state.astype(BF16), preferred_element_type=F32)
        kv = lax.dot_general((kr * kdec).astype(BF16), v, (((0,), (0,)), ((), ())), preferred_element_type=F32)
        s_acc[...] = gl * state + kv
        o_ref[rows, :] = _group_norm_gate(out, g_ref[rows, :], gng, gnb)
        return carry

    lax.fori_loop(0, n_chunks, chunk, 0)
    st_ref[...] = s_acc[...]


def _ret_tables(chunk, dk, n_valid=None):
    n_valid = chunk if n_valid is None else n_valid
    lg = jnp.log1p(-jnp.exp2(-5.0 - jnp.arange(RET_HEADS, dtype=F32)))
    idx = jnp.arange(chunk, dtype=F32)
    diff = idx[:, None] - idx[None, :]
    inside = (idx[:, None] < n_valid) & (idx[None, :] < n_valid)
    dmask = jnp.where((diff[None] >= 0) & inside[None],
                      jnp.exp(jnp.maximum(diff, 0.0)[None] * lg[:, None, None]), 0.0) * (dk ** -0.5)
    qdec = jnp.exp((idx[None, :] + 1.0) * lg[:, None])
    kdec = jnp.where(idx[None, :] < n_valid, jnp.exp((n_valid - 1.0 - idx)[None, :] * lg[:, None]), 0.0) * (dk ** -0.5)
    gl = jnp.exp(n_valid * lg)
    qdec = jnp.broadcast_to(qdec[:, :, None], (RET_HEADS, chunk, dk))
    kdec = jnp.broadcast_to(kdec[:, :, None], (RET_HEADS, chunk, dk))
    gl = jnp.broadcast_to(gl[:, None, None], (RET_HEADS, 8, 128))
    return dmask, qdec, kdec, gl


def _rope_tables(pos, half):
    inv = ROPE_BASE ** (-jnp.arange(half, dtype=F32) / half)
    ang = pos.astype(F32)[:, None] * inv[None, :]
    return jnp.cos(ang), jnp.sin(ang)


def _ret_prompt_call(proj, gn_gain, gn_bias, j, *, n_batch, seq, dk, dv):
    h = RET_HEADS
    n_chunks = seq // RET_CHUNK
    cos, sin = _rope_tables(jnp.arange(seq, dtype=jnp.int32), dk // 2)
    dmask, qdec, kdec, gl = _ret_tables(RET_CHUNK, dk)
    kcol, vcol, gcol = h, (2 * h * dk) // dv, (2 * h * dk) // dv + h
    body = functools.partial(_ret_prompt_body, n_chunks=n_chunks)
    return pl.pallas_call(
        body,
        out_shape=[jax.ShapeDtypeStruct((n_batch * seq, h * dv), BF16),
                   jax.ShapeDtypeStruct((n_batch, h, dk, dv), F32)],
        grid=(n_batch, h),
        in_specs=[
            pl.BlockSpec((seq, dk), lambda b, hh: (b, hh)),
            pl.BlockSpec((seq, dk), lambda b, hh: (b, kcol + hh)),
            pl.BlockSpec((seq, dv), lambda b, hh: (b, vcol + hh)),
            pl.BlockSpec((seq, dv), lambda b, hh: (b, gcol + hh)),
            pl.BlockSpec((seq, dk // 2), lambda b, hh: (0, 0)),
            pl.BlockSpec((seq, dk // 2), lambda b, hh: (0, 0)),
            pl.BlockSpec((None, RET_CHUNK, RET_CHUNK), lambda b, hh: (hh, 0, 0)),
            pl.BlockSpec((None, RET_CHUNK, dk), lambda b, hh: (hh, 0, 0)),
            pl.BlockSpec((None, RET_CHUNK, dk), lambda b, hh: (hh, 0, 0)),
            pl.BlockSpec((None, 8, 128), lambda b, hh: (hh, 0, 0)),
            pl.BlockSpec((None, 1, dv), lambda b, hh: (j, 0, hh)),
            pl.BlockSpec((None, 1, dv), lambda b, hh: (j, 0, hh)),
        ],
        out_specs=[pl.BlockSpec((seq, dv), lambda b, hh: (b, hh)),
                   pl.BlockSpec((None, None, dk, dv), lambda b, hh: (b, hh, 0, 0))],
        scratch_shapes=[pltpu.VMEM((dk, dv), F32)],
        compiler_params=_cparams(("arbitrary", "arbitrary")),
        name="retention_prompt",
    )(proj, proj, proj, proj, cos, sin, dmask, qdec, kdec, gl, gn_gain, gn_bias)


def _ret_sample_body(p_ref, st_ref, cos_ref, sin_ref, dm_ref, qd_ref, kd_ref, gl_ref, gng_ref, gnb_ref,
                     *rest, dk, dv):
    o_ref, so_ref = rest[-2], rest[-1]
    h = RET_HEADS
    cos = cos_ref[...]
    sin = sin_ref[...]
    pad = 128 - SAMPLE_ROWS
    for hh in range(h):
        q = p_ref[:, hh * dk:(hh + 1) * dk].astype(F32)
        k = p_ref[:, h * dk + hh * dk:h * dk + (hh + 1) * dk].astype(F32)
        v = p_ref[:, 2 * h * dk + hh * dv:2 * h * dk + (hh + 1) * dv]
        g = p_ref[:, 2 * h * dk + h * dv + hh * dv:2 * h * dk + h * dv + (hh + 1) * dv]
        qr = _rotate(q, cos, sin)
        kr = _rotate(k, cos, sin)
        k_pad = jnp.concatenate([kr.astype(BF16), jnp.zeros((pad, dk), BF16)], axis=0)
        kd_pad = jnp.concatenate([(kr * kd_ref[hh]).astype(BF16), jnp.zeros((pad, dk), BF16)], axis=0)
        v_pad = jnp.concatenate([v, jnp.zeros((pad, dv), BF16)], axis=0)
        state = st_ref[hh]
        scores = lax.dot_general(qr.astype(BF16), k_pad, (((1,), (1,)), ((), ())), preferred_element_type=F32)
        scores = scores * dm_ref[hh]
        out = jnp.dot(scores.astype(BF16), v_pad, preferred_element_type=F32)
        out = out + jnp.dot((qr * qd_ref[hh]).astype(BF16), state.astype(BF16), preferred_element_type=F32)
        kv = lax.dot_general(kd_pad, v_pad, (((0,), (0,)), ((), ())), preferred_element_type=F32)
        so_ref[hh] = gl_ref[hh, 0:1, 0:1] * state + kv
        o_ref[:, hh * dv:(hh + 1) * dv] = _group_norm_gate(out, g, gng_ref[:, hh * dv:(hh + 1) * dv],
                                                           gnb_ref[:, hh * dv:(hh + 1) * dv])


def _ret_sample_call(proj, state, gn_gain, gn_bias, j, new_state, *, n_prompt_rows, n_seq, dec_seq, dk, dv):
    h = RET_HEADS
    s = SAMPLE_ROWS
    cos, sin = _rope_tables(PAST_LEN + jnp.arange(s, dtype=jnp.int32), dk // 2)
    dmask, qdec, kdec, gl = _ret_tables(s, dk, n_valid=dec_seq)
    dmask = jnp.pad(dmask, ((0, 0), (0, 0), (0, 128 - s)))
    base = n_prompt_rows // s
    body = functools.partial(_ret_sample_body, dk=dk, dv=dv)
    width = proj.shape[1]
    args = [proj, state, cos, sin, dmask, qdec, kdec, gl, gn_gain, gn_bias]
    in_specs = [
        pl.BlockSpec((s, width), lambda b: (base + b, 0)),
        pl.BlockSpec((None, None, h, dk, dv), lambda b: (j, b, 0, 0, 0)),
        pl.BlockSpec((s, dk // 2), lambda b: (0, 0)),
        pl.BlockSpec((s, dk // 2), lambda b: (0, 0)),
        pl.BlockSpec((h, s, 128), lambda b: (0, 0, 0)),
        pl.BlockSpec((h, s, dk), lambda b: (0, 0, 0)),
        pl.BlockSpec((h, s, dk), lambda b: (0, 0, 0)),
        pl.BlockSpec((h, 8, 128), lambda b: (0, 0, 0)),
        pl.BlockSpec((None, 1, h * dv), lambda b: (j, 0, 0)),
        pl.BlockSpec((None, 1, h * dv), lambda b: (j, 0, 0)),
    ]
    aliases = {}
    if new_state is not None:
        aliases = {len(args): 1}
        args.append(new_state)
        in_specs.append(pl.BlockSpec(memory_space=pl.ANY))
    return pl.pallas_call(
        body,
        out_shape=[jax.ShapeDtypeStruct((n_seq * s, h * dv), BF16),
                   jax.ShapeDtypeStruct(state.shape, F32)],
        grid=(n_seq,),
        in_specs=in_specs,
        out_specs=[pl.BlockSpec((s, h * dv), lambda b: (b, 0)),
                   pl.BlockSpec((None, None, h, dk, dv), lambda b: (j, b, 0, 0, 0))],
        input_output_aliases=aliases,
        compiler_params=_cparams(("arbitrary",)),
        name="retention_sample",
    )(*args)


CONV_HALO = 32
CONV_ROW_CHUNK = 64
CONV_LANES = 128


def _layer_norm_swish(u, gain, bias):
    mu = jnp.mean(u, axis=-1, keepdims=True)
    var = jnp.mean(jnp.square(u - mu), axis=-1, keepdims=True)
    un = (u - mu) * lax.rsqrt(var + NORM_EPS) * gain + bias
    return (un * jax.nn.sigmoid(un)).astype(BF16)


def _conv_prompt_body(z_ref, w_ref, bdw_ref, lng_ref, lnb_ref, o_ref, zbuf, ubuf, *, tt, d):
    t = pl.program_id(1)

    @pl.when(t == 0)
    def _():
        zbuf[0:CONV_HALO, :] = jnp.zeros((CONV_HALO, d), F32)

    @pl.when(t > 0)
    def _():
        zbuf[0:CONV_HALO, :] = zbuf[tt:tt + CONV_HALO, :]

    zbuf[CONV_HALO:CONV_HALO + tt, :] = z_ref[...].astype(F32)

    span = CONV_ROW_CHUNK + CONV_HALO
    n_row_chunks = tt // CONV_ROW_CHUNK
    n_strips = d // CONV_LANES

    def strip(n, carry):
        r0 = pl.multiple_of((n % n_row_chunks) * CONV_ROW_CHUNK, CONV_ROW_CHUNK)
        c0 = pl.multiple_of((n // n_row_chunks) * CONV_LANES, CONV_LANES)
        cols = pl.ds(c0, CONV_LANES)
        blk = zbuf[pl.ds(r0, span), cols]
        acc = jnp.zeros((CONV_ROW_CHUNK, CONV_LANES), F32)
        for b in range(8):
            rb = blk if b == 0 else pltpu.roll(blk, span - b, axis=0)
            for a in range(5):
                o = 8 * a + b
                if 2 <= o <= CONV_HALO:
                    acc = acc + rb[8 * a:8 * a + CONV_ROW_CHUNK, :] * w_ref[pl.ds(o - 2, 1), cols]
        ubuf[pl.ds(r0, CONV_ROW_CHUNK), cols] = acc
        return carry

    lax.fori_loop(0, n_row_chunks * n_strips, strip, 0)

    ln_rows = 128

    def ln_chunk(c, carry):
        rows = pl.ds(pl.multiple_of(c * ln_rows, ln_rows), ln_rows)
        o_ref[rows, :] = _layer_norm_swish(ubuf[rows, :] + bdw_ref[...], lng_ref[...], lnb_ref[...])
        return carry

    lax.fori_loop(0, tt // ln_rows, ln_chunk, 0)


def _conv_prompt_call(z, w_dw, b_dw, ln_gain, ln_bias, *, n_batch, seq):
    d = z.shape[1]
    tt = ROW_TILE
    nt = seq // tt
    w_pad = jnp.pad(w_dw, ((0, 32 - CONV_WIDTH), (0, 0)))
    body = functools.partial(_conv_prompt_body, tt=tt, d=d)
    vec = pl.BlockSpec((1, d), lambda b, t: (0, 0))
    return pl.pallas_call(
        body,
        out_shape=jax.ShapeDtypeStruct((n_batch * seq, d), BF16),
        grid=(n_batch, nt),
        in_specs=[pl.BlockSpec((tt, d), lambda b, t: (b * nt + t, 0)),
                  pl.BlockSpec((32, d), lambda b, t: (0, 0)), vec, vec, vec],
        out_specs=pl.BlockSpec((tt, d), lambda b, t: (b * nt + t, 0)),
        scratch_shapes=[pltpu.VMEM((tt + CONV_HALO, d), F32), pltpu.VMEM((tt, d), F32)],
        compiler_params=_cparams(("arbitrary", "arbitrary")),
        name="conv_prompt",
    )(z, w_pad, b_dw, ln_gain, ln_bias)


def _conv_sample_body(z_ref, c_ref, w_ref, bdw_ref, lng_ref, lnb_ref, o_ref, zbuf, *, d):
    s = SAMPLE_ROWS
    zbuf[0:CONV_STATE, :] = c_ref[...]
    zbuf[CONV_STATE:CONV_STATE + s, :] = z_ref[...].astype(F32)
    acc = jnp.zeros((s, d), F32)
    for j in range(CONV_WIDTH):
        acc = acc + zbuf[j:j + s, :] * w_ref[j:j + 1, :]
    o_ref[...] = _layer_norm_swish(acc + bdw_ref[...], lng_ref[...], lnb_ref[...])


def _conv_sample_call(z, cache, w_dw, b_dw, ln_gain, ln_bias, *, n_prompt_rows, n_seq):
    d = z.shape[1]
    s = SAMPLE_ROWS
    base = n_prompt_rows // s
    body = functools.partial(_conv_sample_body, d=d)
    vec = pl.BlockSpec((1, d), lambda b: (0, 0))
    return pl.pallas_call(
        body,
        out_shape=jax.ShapeDtypeStruct((n_seq * s, d), BF16),
        grid=(n_seq,),
        in_specs=[pl.BlockSpec((s, d), lambda b: (base + b, 0)),
                  pl.BlockSpec((None, CONV_STATE, d), lambda b: (b, 0, 0)),
                  pl.BlockSpec((CONV_WIDTH, d), lambda b: (0, 0)), vec, vec, vec],
        out_specs=pl.BlockSpec((s, d), lambda b: (b, 0)),
        scratch_shapes=[pltpu.VMEM((CONV_STATE + s, d), F32)],
        compiler_params=_cparams(("arbitrary",)),
        name="conv_sample",
    )(z, cache, w_dw, b_dw, ln_gain, ln_bias)


def _t5_bucket(dist):
    n = jnp.maximum(dist, 0)
    max_exact = REL_BUCKETS // 2
    nf = jnp.maximum(n, 1).astype(F32)
    large = max_exact + (jnp.log(nf / max_exact) / math.log(REL_MAX_DIST / max_exact)
                         * (REL_BUCKETS - max_exact)).astype(jnp.int32)
    large = jnp.minimum(large, REL_BUCKETS - 1)
    return jnp.where(n < max_exact, n, large)


def _bias_table(rel_bias, dist, valid):
    onehot = (_t5_bucket(dist).reshape(-1)[None, :] == jnp.arange(REL_BUCKETS, dtype=jnp.int32)[:, None]).astype(F32)
    tbl = jnp.dot(rel_bias.astype(F32).T, onehot, precision=lax.Precision.HIGHEST)
    tbl = jnp.where(valid.reshape(-1)[None, :], tbl, NEG_INF)
    return tbl.reshape(-1, dist.shape[-1])


def _softmax_sink_pv(s, sink, vals):
    m = jnp.maximum(jnp.max(s, axis=-1, keepdims=True), sink)
    p = jnp.exp(s - m)
    den = jnp.sum(p, axis=-1, keepdims=True) + jnp.exp(sink - m)
    o = jnp.dot(p.astype(BF16), vals, preferred_element_type=F32)
    return o / den


def _swa_prompt_body(sink_ref, q_ref, kvp_ref, kvc_ref, gq_ref, gk_ref, bias_ref, o_ref, kn_ref, *, hd, n_blocks):
    i = pl.program_id(1)
    blk = WINDOW
    g = ATT_GROUP
    nkv = ATT_KV_HEADS
    gq = gq_ref[...]
    gk = gk_ref[...]
    gqk = gq * gk
    ones_rep = jnp.ones((hd, 128), BF16)
    ones_row = jnp.ones((8, hd), BF16)
    is_first = i == 0
    col = lax.broadcasted_iota(jnp.int32, (g * blk, 2 * blk), 1)
    for hk in range(nkv):
        keys = jnp.concatenate([kvp_ref[:, hk * hd:(hk + 1) * hd], kvc_ref[:, hk * hd:(hk + 1) * hd]], axis=0)
        vals = jnp.concatenate([kvp_ref[:, (nkv + hk) * hd:(nkv + hk + 1) * hd],
                                kvc_ref[:, (nkv + hk) * hd:(nkv + hk + 1) * hd]], axis=0)
        kf = keys.astype(F32)
        ssq_k = lax.dot_general(ones_row, (kf * kf).astype(BF16), (((1,), (1,)), ((), ())),
                                preferred_element_type=F32)[0:1, :]
        rk = lax.rsqrt(ssq_k * (1.0 / hd) + NORM_EPS)
        qs = jnp.concatenate([q_ref[:, (hk * g + gg) * hd:(hk * g + gg + 1) * hd] for gg in range(g)], axis=0)
        qf = qs.astype(F32)
        ssq_q = jnp.dot((qf * qf).astype(BF16), ones_rep, preferred_element_type=F32)
        rq = lax.rsqrt(ssq_q * (1.0 / hd) + NORM_EPS)
        rq2 = jnp.concatenate([rq, rq], axis=1)
        s = lax.dot_general((qf * gqk).astype(BF16), keys, (((1,), (1,)), ((), ())), preferred_element_type=F32)
        s = s * rq2 * rk * (hd ** -0.5) + bias_ref[hk * g * blk:(hk + 1) * g * blk, :]
        s = jnp.where(jnp.logical_and(is_first, col < blk), NEG_INF, s)
        sink = jnp.concatenate([jnp.full((blk, 1), sink_ref[hk * g + gg], F32) for gg in range(g)], axis=0)
        o = _softmax_sink_pv(s, sink, vals)
        for gg in range(g):
            o_ref[:, (hk * g + gg) * hd:(hk * g + gg + 1) * hd] = o[gg * blk:(gg + 1) * blk, :].astype(BF16)

    @pl.when(i == n_blocks - 1)
    def _():
        for hk in range(nkv):
            kc = kvc_ref[:, hk * hd:(hk + 1) * hd].astype(F32)
            ssq = jnp.dot((kc * kc).astype(BF16), ones_rep, preferred_element_type=F32)[:, 0:hd]
            kn_ref[:, hk * hd:(hk + 1) * hd] = kc * lax.rsqrt(ssq * (1.0 / hd) + NORM_EPS) * gk


def _swa_prompt_call(proj, q_gain, k_gain, sinks, rel_bias, *, n_batch, seq, hd):
    blk = WINDOW
    nb = seq // blk
    nq = ATT_Q_HEADS * hd
    nkv2 = 2 * ATT_KV_HEADS * hd
    kvcol = nq // nkv2
    i_idx = jnp.arange(blk, dtype=jnp.int32)[:, None]
    j_idx = jnp.arange(2 * blk, dtype=jnp.int32)[None, :]
    dist = blk + i_idx - j_idx
    bias = _bias_table(rel_bias, dist, (dist >= 0) & (dist < WINDOW))
    body = functools.partial(_swa_prompt_body, hd=hd, n_blocks=nb)
    return pl.pallas_call(
        body,
        out_shape=[jax.ShapeDtypeStruct((n_batch * seq, nq), BF16),
                   jax.ShapeDtypeStruct((n_batch * blk, ATT_KV_HEADS * hd), F32)],
        grid=(n_batch, nb),
        in_specs=[
            pl.BlockSpec(memory_space=pltpu.SMEM),
            pl.BlockSpec((blk, nq), lambda b, i: (b * nb + i, 0)),
            pl.BlockSpec((blk, nkv2), lambda b, i: (b * nb + jnp.maximum(i - 1, 0), kvcol)),
            pl.BlockSpec((blk, nkv2), lambda b, i: (b * nb + i, kvcol)),
            pl.BlockSpec((1, hd), lambda b, i: (0, 0)),
            pl.BlockSpec((1, hd), lambda b, i: (0, 0)),
            pl.BlockSpec((ATT_Q_HEADS * blk, 2 * blk), lambda b, i: (0, 0)),
        ],
        out_specs=[pl.BlockSpec((blk, nq), lambda b, i: (b * nb + i, 0)),
                   pl.BlockSpec((blk, ATT_KV_HEADS * hd), lambda b, i: (b, 0))],
        compiler_params=_cparams(("arbitrary", "arbitrary")),
        name="swa_prompt",
    )(sinks, proj, proj, proj, q_gain, k_gain, bias)


def _swa_sample_body(sink_ref, p_ref, ck_ref, cv_ref, gq_ref, gk_ref, bias_ref, o_ref, kn_ref, *, hd):
    s = SAMPLE_ROWS
    g = ATT_GROUP
    nkv = ATT_KV_HEADS
    nq = ATT_Q_HEADS * hd
    gq = gq_ref[...]
    gk = gk_ref[...]

    def rms(x, gain):
        return x * lax.rsqrt(jnp.mean(x * x, axis=-1, keepdims=True) + NORM_EPS) * gain

    for hk in range(nkv):
        kn = rms(p_ref[:, nq + hk * hd:nq + (hk + 1) * hd].astype(F32), gk)
        kn_ref[:, hk * hd:(hk + 1) * hd] = kn
        keys = jnp.concatenate([ck_ref[:, hk * hd:(hk + 1) * hd], kn], axis=0).astype(BF16)
        vnew = p_ref[:, nq + (nkv + hk) * hd:nq + (nkv + hk + 1) * hd].astype(F32)
        vals = jnp.concatenate([cv_ref[:, hk * hd:(hk + 1) * hd], vnew], axis=0).astype(BF16)
        qs = jnp.concatenate([rms(p_ref[:, (hk * g + gg) * hd:(hk * g + gg + 1) * hd].astype(F32), gq)
                              for gg in range(g)], axis=0)
        sc = lax.dot_general(qs.astype(BF16), keys, (((1,), (1,)), ((), ())), preferred_element_type=F32)
        sc = sc * (hd ** -0.5) + bias_ref[hk * g * s:(hk + 1) * g * s, :]
        sink = jnp.concatenate([jnp.full((s, 1), sink_ref[hk * g + gg], F32) for gg in range(g)], axis=0)
        o = _softmax_sink_pv(sc, sink, vals)
        for gg in range(g):
            o_ref[:, (hk * g + gg) * hd:(hk * g + gg + 1) * hd] = o[gg * s:(gg + 1) * s, :].astype(BF16)


def _swa_sample_call(proj, cache_k, cache_v, q_gain, k_gain, sinks, rel_bias, *, n_prompt_rows, n_seq, dec_seq, hd):
    s = SAMPLE_ROWS
    nq = ATT_Q_HEADS * hd
    nkv = ATT_KV_HEADS * hd
    sc = cache_k.shape[1]
    i_idx = jnp.arange(s, dtype=jnp.int32)[:, None]
    j_idx = jnp.arange(sc + s, dtype=jnp.int32)[None, :]
    dist = sc + i_idx - j_idx
    valid = (dist >= 0) & (dist < WINDOW) & (j_idx < sc + dec_seq)
    bias = _bias_table(rel_bias, dist, valid)
    base = n_prompt_rows // s
    body = functools.partial(_swa_sample_body, hd=hd)
    return pl.pallas_call(
        body,
        out_shape=[jax.ShapeDtypeStruct((n_seq * s, nq), BF16),
                   jax.ShapeDtypeStruct((n_seq * s, nkv), F32)],
        grid=(n_seq,),
        in_specs=[
            pl.BlockSpec(memory_space=pltpu.SMEM),
            pl.BlockSpec((s, proj.shape[1]), lambda b: (base + b, 0)),
            pl.BlockSpec((None, sc, nkv), lambda b: (b, 0, 0)),
            pl.BlockSpec((None, sc, nkv), lambda b: (b, 0, 0)),
            pl.BlockSpec((1, hd), lambda b: (0, 0)),
            pl.BlockSpec((1, hd), lambda b: (0, 0)),
            pl.BlockSpec((ATT_Q_HEADS * s, sc + s), lambda b: (0, 0)),
        ],
        out_specs=[pl.BlockSpec((s, nq), lambda b: (b, 0)),
                   pl.BlockSpec((s, nkv), lambda b: (b, 0))],
        compiler_params=_cparams(("arbitrary",)),
        name="swa_sample",
    )(sinks, proj, cache_k, cache_v, q_gain, k_gain, bias)


MOE_TILE = 256


def _moe_body(src_ref, dst_ref, te_ref, nu_ref, h_hbm, win_ref, wout_ref, y_hbm, xg, yb, wib, wob, gsem, ssem,
              *, ff):
    t = pl.program_id(0)
    nt = pl.num_programs(0)
    n_used = nu_ref[0]
    tm = MOE_TILE
    slot = lax.rem(t, 2)

    def gather_start(tile, sl):
        base = tile * tm

        def row(r, carry):
            pltpu.make_async_copy(h_hbm.at[pl.ds(src_ref[base + r], 1), :], xg.at[sl, pl.ds(r, 1), :],
                                  gsem.at[sl]).start()
            return carry

        lax.fori_loop(0, tm, row, 0, unroll=8)

    def gather_wait(sl):
        pltpu.make_async_copy(h_hbm.at[pl.ds(0, tm), :], xg.at[sl], gsem.at[sl]).wait()

    def scatter_start(tile, sl):
        base = tile * tm

        def row(r, carry):
            pltpu.make_async_copy(yb.at[sl, pl.ds(r, 1), :], y_hbm.at[pl.ds(dst_ref[base + r], 1), :],
                                  ssem.at[sl]).start()
            return carry

        lax.fori_loop(0, tm, row, 0, unroll=8)

    def scatter_wait(sl):
        pltpu.make_async_copy(yb.at[sl], y_hbm.at[pl.ds(0, tm), :], ssem.at[sl]).wait()

    valid = t < n_used

    @pl.when(t == 0)
    def _():
        gather_start(0, 0)

    @pl.when(valid)
    def _():
        @pl.when(t >= 2)
        def _():
            scatter_wait(slot)

        gather_wait(slot)

        @pl.when(t + 1 < n_used)
        def _():
            gather_start(t + 1, 1 - slot)

        @pl.when(jnp.logical_or(t == 0, te_ref[t] != te_ref[jnp.maximum(t - 1, 0)]))
        def _():
            wib[...] = win_ref[...].astype(BF16)
            wob[...] = wout_ref[...].astype(BF16)

        gu = jnp.dot(xg[slot].astype(BF16), wib[...], preferred_element_type=F32)
        gate, up = gu[:, :ff], gu[:, ff:]
        act = (gate * jax.nn.sigmoid(gate) * up).astype(BF16)
        yb[slot] = jnp.dot(act, wob[...], preferred_element_type=F32)
        scatter_start(t, slot)

    @pl.when(t == nt - 1)
    def _():
        scatter_wait(lax.rem(n_used - 1, 2))
        scatter_wait(lax.rem(n_used, 2))


def _moe_layer(h2, route, counts, w_in, w_out, layer):
    r, d = h2.shape
    depth, n_exp, _, ff2 = w_in.shape
    ff = ff2 // 2
    tm = MOE_TILE
    n_tiles = 2 * r // tm + n_exp
    n_slots = n_tiles * tm

    cnt = counts[0, MOE_GROUPS:MOE_GROUPS + n_exp].astype(jnp.int32)
    padded = ((cnt + tm - 1) // tm) * tm
    ends = jnp.cumsum(padded)
    starts = ends - padded
    rt = route[:, :ROUTE_R2 + 1].astype(jnp.int32)
    pos = jnp.concatenate([starts[rt[:, ROUTE_E1]] + rt[:, ROUTE_R1], starts[rt[:, ROUTE_E2]] + rt[:, ROUTE_R2]])
    tok = jnp.arange(r, dtype=jnp.int32)
    src = jnp.zeros((n_slots,), jnp.int32).at[pos].set(jnp.concatenate([tok, tok]))
    dst = (2 * r + jnp.arange(n_slots, dtype=jnp.int32)).at[pos].set(jnp.concatenate([tok, r + tok]))
    tile_start = jnp.arange(n_tiles, dtype=jnp.int32) * tm
    tile_expert = jnp.minimum(jnp.sum(tile_start[:, None] >= ends[None, :], axis=1), n_exp - 1).astype(jnp.int32)
    n_used = (ends[-1:] // tm).astype(jnp.int32)

    return pl.pallas_call(
        functools.partial(_moe_body, ff=ff),
        out_shape=jax.ShapeDtypeStruct((2 * r + n_slots, d), F32),
        grid_spec=pltpu.PrefetchScalarGridSpec(
            num_scalar_prefetch=4,
            grid=(n_tiles,),
            in_specs=[
                pl.BlockSpec(memory_space=pl.ANY),
                pl.BlockSpec((None, None, d, ff2), lambda t, src, dst, te, nu: (layer, te[t], 0, 0)),
                pl.BlockSpec((None, None, ff, d), lambda t, src, dst, te, nu: (layer, te[t], 0, 0)),
            ],
            out_specs=pl.BlockSpec(memory_space=pl.ANY),
            scratch_shapes=[pltpu.VMEM((2, tm, d), F32), pltpu.VMEM((2, tm, d), F32),
                            pltpu.VMEM((d, ff2), BF16), pltpu.VMEM((ff, d), BF16),
                            pltpu.SemaphoreType.DMA((2,)), pltpu.SemaphoreType.DMA((2,))],
        ),
        compiler_params=_cparams(("arbitrary",)),
        name="moe_experts",
    )(src, dst, tile_expert, n_used, h2, w_in, w_out)


def kernel(x_prompt, x_sample, c_prompt, c_sample, state_ret, cache_conv, cache_swa_k, cache_swa_v, ada_w, ada_b, norm_gain, ret_w_in, ret_gn_gain, ret_gn_bias, ret_w_out, conv_w_pw1, conv_b_pw1, conv_w_dw, conv_b_dw, conv_ln_gain, conv_ln_bias, conv_w_pw2, conv_b_pw2, att_w_qkv, att_q_gain, att_k_gain, att_sinks, att_w_o, rel_bias, moe_wg, moe_bg, moe_we, moe_be, moe_w_in, moe_w_out):
    n_batch, seq, d = x_prompt.shape
    n_seq, dec_seq, _ = x_sample.shape
    depth = ada_w.shape[0]
    s = SAMPLE_ROWS
    assert n_seq * s == ROW_TILE and seq % ROW_TILE == 0 and dec_seq <= s
    n_prompt_rows = n_batch * seq
    seq_tiles = seq // ROW_TILE
    dk = ret_w_in.shape[2] // (6 * RET_HEADS)
    dv = 2 * dk
    hd = d // ATT_Q_HEADS
    geom = dict(seq_tiles=seq_tiles, n_batch=n_batch)

    xs_pad = jnp.pad(x_sample, ((0, 0), (0, s - dec_seq), (0, 0))).reshape(n_seq * s, d)
    x = jnp.concatenate([x_prompt.reshape(n_prompt_rows, d), xs_pad], axis=0)

    n_c = n_batch + n_seq
    c_rows = ((n_c + 7) // 8) * 8
    c_all = jnp.pad(jnp.concatenate([c_prompt, c_sample], axis=0), ((0, c_rows - n_c), (0, 0)))
    mod = _ada_call(c_all, ada_w, ada_b)
    mod4 = mod.reshape(depth, c_rows, 1, 6 * d)
    modtok = jnp.repeat(mod[:, n_batch:n_c], s, axis=1)

    gains = norm_gain.astype(F32).reshape(2 * depth, 1, d)
    gng_all = ret_gn_gain.astype(F32)[:, None, :]
    gnb_all = ret_gn_bias.astype(F32)[:, None, :]
    state_all = state_ret.astype(F32)

    ret_p, conv_p, conv_s, kp_l, vp_l, ks_l, vs_l = [], [], [], [], [], [], []
    ret_s = None
    (h,) = _norm_call(x, mod4, modtok, norm=(gains, 0, 0, 0, 1), **geom)
    for l in range(depth):
        kind, j = l % 3, l // 3
        resid1 = (x, mod4, modtok, l, 2)
        if kind == 0:
            proj = _linear_call(h, ret_w_in, j, n_out=ret_w_in.shape[2], tn=1024, out_dtype=BF16,
                                name="ret_in", **geom)
            a_p, st_p = _ret_prompt_call(proj, gng_all, gnb_all, j, n_batch=n_batch, seq=seq, dk=dk, dv=dv)
            a_s, ret_s = _ret_sample_call(proj, state_all, gng_all, gnb_all, j, ret_s, n_prompt_rows=n_prompt_rows,
                                          n_seq=n_seq, dec_seq=dec_seq, dk=dk, dv=dv)
            ret_p.append(st_p)
            x = _linear_call(a_p, ret_w_out, j, h_sample=a_s, n_out=d, tn=512, out_dtype=F32, resid=resid1,
                             name="ret_out", **geom)
        elif kind == 1:
            z = _linear_call(h, conv_w_pw1, j, n_out=d, tn=512, out_dtype=BF16, bias=conv_b_pw1,
                             glu=True, name="conv_pw1", **geom)
            cargs = (conv_w_dw[j], conv_b_dw[j][None], conv_ln_gain[j][None], conv_ln_bias[j][None])
            a_p = _conv_prompt_call(z, *cargs, n_batch=n_batch, seq=seq)
            a_s = _conv_sample_call(z, cache_conv[j].astype(F32), *cargs, n_prompt_rows=n_prompt_rows, n_seq=n_seq)
            z_tail = z[:n_prompt_rows].reshape(n_batch, seq, d)[:, seq - CONV_STATE:].astype(F32)
            conv_p.append(z_tail)
            z_new = z[n_prompt_rows:].reshape(n_seq, s, d)[:, :dec_seq].astype(F32)
            conv_s.append(jnp.concatenate([cache_conv[j].astype(F32), z_new], axis=1)[:, -CONV_STATE:])
            x = _linear_call(a_p, conv_w_pw2, j, h_sample=a_s, n_out=d, tn=1024, out_dtype=F32,
                             bias=conv_b_pw2, resid=resid1, name="conv_pw2", **geom)
        else:
            nkv = ATT_KV_HEADS * hd
            proj = _linear_call(h, att_w_qkv, j, n_out=att_w_qkv.shape[2], tn=512, out_dtype=BF16,
                                name="att_qkv", **geom)
            gq, gk = att_q_gain[j][None].astype(F32), att_k_gain[j][None].astype(F32)
            sinks = att_sinks[j].astype(F32)
            a_p, kn_p = _swa_prompt_call(proj, gq, gk, sinks, rel_bias, n_batch=n_batch, seq=seq, hd=hd)
            win = cache_swa_k.shape[2]
            ck = cache_swa_k[j].astype(F32).reshape(n_seq, win, nkv)
            cv = cache_swa_v[j].astype(F32).reshape(n_seq, win, nkv)
            a_s, kn_s = _swa_sample_call(proj, ck, cv, gq, gk, sinks, rel_bias, n_prompt_rows=n_prompt_rows,
                                         n_seq=n_seq, dec_seq=dec_seq, hd=hd)
            vcol = ATT_Q_HEADS * hd + nkv
            v_tail = proj[:n_prompt_rows, vcol:].reshape(n_batch, seq, nkv)[:, seq - WINDOW:].astype(F32)
            kp_l.append(kn_p.reshape(n_batch, WINDOW, ATT_KV_HEADS, hd))
            vp_l.append(v_tail.reshape(n_batch, WINDOW, ATT_KV_HEADS, hd))
            k_new = kn_s.reshape(n_seq, s, nkv)[:, :dec_seq]
            v_new = proj[n_prompt_rows:, vcol:].reshape(n_seq, s, nkv)[:, :dec_seq].astype(F32)
            ks_l.append(jnp.concatenate([ck, k_new], axis=1)[:, -win:].reshape(n_seq, win, ATT_KV_HEADS, hd))
            vs_l.append(jnp.concatenate([cv, v_new], axis=1)[:, -win:].reshape(n_seq, win, ATT_KV_HEADS, hd))
            x = _linear_call(a_p, att_w_o, j, h_sample=a_s, n_out=d, tn=1024, out_dtype=F32, resid=resid1,
                             name="att_out", **geom)

        lane_pad = ROUTER_LANES - MOE_GROUPS - MOE_GROUPS * MOE_EPG
        router_w = jnp.pad(jnp.concatenate([moe_wg[l], moe_we[l]], axis=1).astype(F32), ((0, 0), (0, lane_pad)))
        router_b = jnp.pad(jnp.concatenate([moe_bg[l], moe_be[l]]).astype(F32), (0, lane_pad))[None]
        h2, route, counts = _norm_call(x, mod4, modtok, norm=(gains, 2 * l + 1, l, 3, 4),
                                       router=(router_w, router_b), **geom)
        ypair = _moe_layer(h2, route, counts, moe_w_in, moe_w_out, l)
        if l + 1 < depth:
            x, h = _norm_call(x, mod4, modtok, resid=(ypair, route, l, 5),
                              norm=(gains, 2 * l + 2, l + 1, 0, 1), **geom)
        else:
            x_p, x_s = _norm_call(x, mod4, modtok, resid=(ypair, route, l, 5), split_out=True, **geom)

    y_prompt = x_p.reshape(n_batch, seq, d)
    y_sample = x_s.reshape(n_seq, s, d)[:, :dec_seq]
    return (y_prompt, y_sample, jnp.stack(ret_p), ret_s, jnp.stack(conv_p), jnp.stack(conv_s),
            jnp.stack(kp_l), jnp.stack(vp_l), jnp.stack(ks_l), jnp.stack(vs_l))
```

```python
import functools
import math

import jax
import jax.numpy as jnp
from jax import lax
from jax.experimental import pallas as pl
from jax.experimental.pallas import tpu as pltpu

F32 = jnp.float32
BF16 = jnp.bfloat16

NORM_EPS = 1e-6
NEG_INF = -1e30
ROPE_BASE = 10000.0
PAST_LEN = 16384

RET_HEADS = 8
RET_CHUNK = 128
CONV_WIDTH = 31
CONV_STATE = CONV_WIDTH - 1
ATT_Q_HEADS = 32
ATT_KV_HEADS = 4
ATT_GROUP = ATT_Q_HEADS // ATT_KV_HEADS
WINDOW = 128
REL_BUCKETS = 32
REL_MAX_DIST = 128
MOE_GROUPS = 4
MOE_EPG = 4
MOE_TOPK = 2

ROW_TILE = 512
SAMPLE_ROWS = 16
ROUTER_LANES = 128
VMEM_LIMIT_BYTES = 56 * 1024 * 1024


def _cparams(sem):
    return pltpu.CompilerParams(dimension_semantics=sem, vmem_limit_bytes=VMEM_LIMIT_BYTES)


def _ada_body(c_ref, w_ref, b_ref, o_ref):
    c = c_ref[...]
    s = (c * jax.nn.sigmoid(c)).astype(BF16)
    o_ref[...] = jnp.dot(s, w_ref[...].astype(BF16), preferred_element_type=F32) + b_ref[...]


def _ada_call(c_all, ada_w, ada_b):
    depth, d, n = ada_w.shape
    rows = c_all.shape[0]
    tn = 1024
    return pl.pallas_call(
        _ada_body,
        out_shape=jax.ShapeDtypeStruct((depth, rows, n), F32),
        grid=(depth, n // tn),
        in_specs=[
            pl.BlockSpec((rows, d), lambda l, j: (0, 0)),
            pl.BlockSpec((None, d, tn), lambda l, j: (l, 0, j)),
            pl.BlockSpec((None, 1, tn), lambda l, j: (l, 0, j)),
        ],
        out_specs=pl.BlockSpec((None, rows, tn), lambda l, j: (l, 0, j)),
        compiler_params=_cparams(("arbitrary", "arbitrary")),
        name="ada_mod",
    )(c_all, ada_w, ada_b.reshape(depth, 1, n))


ROUTE_E1, ROUTE_E2, ROUTE_W1, ROUTE_W2, ROUTE_R1, ROUTE_R2 = range(6)


def _route_tile(logits, rb, carry):
    tm = logits.shape[0]
    g, epg = MOE_GROUPS, MOE_EPG
    lg = logits + rb
    lane = lax.broadcasted_iota(jnp.int32, lg.shape, 1)
    lane_f = lane.astype(F32)

    def first_lane(mask):
        return jnp.min(jnp.where(mask, lane_f, float(ROUTER_LANES)), axis=-1, keepdims=True).astype(jnp.int32)

    is_g = lane < g
    mg = jnp.max(jnp.where(is_g, lg, NEG_INF), axis=-1, keepdims=True)
    eg = jnp.where(is_g, jnp.exp(lg - mg), 0.0)
    p_grp = 1.0 / jnp.sum(eg, axis=-1, keepdims=True)
    grp = first_lane(jnp.logical_and(is_g, lg == mg))
    lo = g + epg * grp
    is_e = jnp.logical_and(lane >= lo, lane < lo + epg)
    me = jnp.max(jnp.where(is_e, lg, NEG_INF), axis=-1, keepdims=True)
    ee = jnp.where(is_e, jnp.exp(lg - me), 0.0)
    se = jnp.sum(ee, axis=-1, keepdims=True)
    i1 = first_lane(jnp.logical_and(is_e, lg == me))
    rest = jnp.logical_and(is_e, lane != i1)
    m2 = jnp.max(jnp.where(rest, ee, -1.0), axis=-1, keepdims=True)
    i2 = first_lane(jnp.logical_and(rest, ee == m2))
    p1 = 1.0 / se
    p2 = m2 / se
    w1 = p_grp * p1 / (p1 + p2)
    w2 = p_grp * p2 / (p1 + p2)
    sel1 = lane == i1
    sel2 = lane == i2
    onehot = jnp.where(jnp.logical_or(sel1, sel2), 1.0, 0.0)
    row = lax.broadcasted_iota(jnp.int32, (tm, tm), 0)
    col = lax.broadcasted_iota(jnp.int32, (tm, tm), 1)
    tri = jnp.where(col <= row, 1.0, 0.0).astype(BF16)
    incl = jnp.dot(tri, onehot.astype(BF16), preferred_element_type=F32)
    rank = incl - 1.0 + carry
    r1 = jnp.sum(jnp.where(sel1, rank, 0.0), axis=-1, keepdims=True)
    r2 = jnp.sum(jnp.where(sel2, rank, 0.0), axis=-1, keepdims=True)
    rec = jnp.zeros_like(lg)
    for ln, val in ((ROUTE_E1, (i1 - g).astype(F32)), (ROUTE_E2, (i2 - g).astype(F32)), (ROUTE_W1, w1),
                    (ROUTE_W2, w2), (ROUTE_R1, r1), (ROUTE_R2, r2)):
        rec = jnp.where(lane == ln, val, rec)
    return rec, carry + jnp.sum(onehot, axis=0, keepdims=True)


def _norm_body(*refs, n_prompt_tiles, has_resid, has_norm, has_router, split_out):
    it = iter(refs)
    x_ref = next(it)
    if has_resid:
        y0_ref, y1_ref, rt_ref, grow_ref, gtok_ref = next(it), next(it), next(it), next(it), next(it)
    if has_norm:
        gain_ref, shrow_ref, scrow_ref, shtok_ref, sctok_ref = next(it), next(it), next(it), next(it), next(it)
    if has_router:
        wr_ref, rb_ref = next(it), next(it)
    if has_resid:
        xo_ref = next(it)
        if split_out:
            xs_ref = next(it)
    if has_norm:
        h_ref = next(it)
    if has_router:
        rec_ref, cnt_ref, carry = next(it), next(it), next(it)

    is_sample = pl.program_id(0) >= n_prompt_tiles
    x = x_ref[...]
    if has_resid:
        gate = jnp.where(is_sample, gtok_ref[...], grow_ref[...])
        rt = rt_ref[...]
        y = rt[:, ROUTE_W1:ROUTE_W1 + 1] * y0_ref[...] + rt[:, ROUTE_W2:ROUTE_W2 + 1] * y1_ref[...]
        x = x + gate * y
        if split_out:
            @pl.when(jnp.logical_not(is_sample))
            def _():
                xo_ref[...] = x

            @pl.when(is_sample)
            def _():
                xs_ref[...] = x
        else:
            xo_ref[...] = x
    if has_norm:
        ms = jnp.mean(x * x, axis=-1, keepdims=True)
        xn = x * lax.rsqrt(ms + NORM_EPS) * gain_ref[...]
        scale = jnp.where(is_sample, sctok_ref[...], scrow_ref[...])
        shift = jnp.where(is_sample, shtok_ref[...], shrow_ref[...])
        h = xn * (1.0 + scale) + shift
        h_ref[...] = h.astype(h_ref.dtype)
        if has_router:
            @pl.when(pl.program_id(0) == 0)
            def _():
                carry[...] = jnp.zeros_like(carry)

            logits = jnp.dot(h, wr_ref[...], precision=lax.Precision.HIGHEST, preferred_element_type=F32)
            rec, new_carry = _route_tile(logits, rb_ref[...], carry[0:1, :])
            rec_ref[...] = rec
            carry[...] = jnp.broadcast_to(new_carry, carry.shape)
            cnt_ref[...] = carry[...]


def _norm_call(x, mod4, modtok, *, seq_tiles, n_batch, resid=None, norm=None, router=None, split_out=False):
    r, d = x.shape
    n_tiles = r // ROW_TILE
    n_prompt_tiles = n_tiles - 1

    def row_spec(layer, col):
        return pl.BlockSpec((None, None, 1, d),
                            lambda i: (layer, jnp.minimum(i // seq_tiles, n_batch - 1), 0, col))

    def tok_spec(layer, col):
        return pl.BlockSpec((None, ROW_TILE, d), lambda i: (layer, 0, col))

    tile = pl.BlockSpec((ROW_TILE, d), lambda i: (i, 0))
    args, in_specs, out_shape, out_specs = [x], [tile], [], []
    scratch = []
    if resid is not None:
        ypair, route, layer, gcol = resid
        args += [ypair, ypair, route, mod4, modtok]
        in_specs += [tile, pl.BlockSpec((ROW_TILE, d), lambda i: (n_tiles + i, 0)),
                     pl.BlockSpec((ROW_TILE, ROUTER_LANES), lambda i: (i, 0)),
                     row_spec(layer, gcol), tok_spec(layer, gcol)]
        if split_out:
            out_shape += [jax.ShapeDtypeStruct((r - ROW_TILE, d), F32), jax.ShapeDtypeStruct((ROW_TILE, d), F32)]
            out_specs += [pl.BlockSpec((ROW_TILE, d), lambda i: (jnp.minimum(i, n_prompt_tiles - 1), 0)),
                          pl.BlockSpec((ROW_TILE, d), lambda i: (0, 0))]
        else:
            out_shape.append(jax.ShapeDtypeStruct((r, d), F32))
            out_specs.append(tile)
    if norm is not None:
        gains, gidx, layer, shcol, sccol = norm
        args += [gains, mod4, mod4, modtok, modtok]
        in_specs += [pl.BlockSpec((None, 1, d), lambda i: (gidx, 0, 0)), row_spec(layer, shcol),
                     row_spec(layer, sccol), tok_spec(layer, shcol), tok_spec(layer, sccol)]
        out_shape.append(jax.ShapeDtypeStruct((r, d), F32 if router is not None else BF16))
        out_specs.append(tile)
        if router is not None:
            args += list(router)
            in_specs += [pl.BlockSpec((d, ROUTER_LANES), lambda i: (0, 0)),
                         pl.BlockSpec((1, ROUTER_LANES), lambda i: (0, 0))]
            out_shape += [jax.ShapeDtypeStruct((r, ROUTER_LANES), F32), jax.ShapeDtypeStruct((8, ROUTER_LANES), F32)]
            out_specs += [pl.BlockSpec((ROW_TILE, ROUTER_LANES), lambda i: (i, 0)),
                          pl.BlockSpec((8, ROUTER_LANES), lambda i: (0, 0))]
            scratch.append(pltpu.VMEM((8, ROUTER_LANES), F32))
    body = functools.partial(_norm_body, n_prompt_tiles=n_prompt_tiles, has_resid=resid is not None,
                             has_norm=norm is not None, has_router=router is not None, split_out=split_out)
    return pl.pallas_call(
        body, out_shape=out_shape, grid=(n_tiles,), in_specs=in_specs, out_specs=out_specs,
        scratch_shapes=scratch, compiler_params=_cparams(("arbitrary",)), name="mod_norm",
    )(*args)


def _linear_body(*refs, n_prompt_tiles, pair, glu, has_bias, has_resid):
    it = iter(refs)
    h_ref = next(it)
    hs_ref = next(it) if pair else None
    w_ref = next(it)
    w2_ref = next(it) if glu else None
    b_ref = next(it) if has_bias else None
    b2_ref = next(it) if glu else None
    if has_resid:
        x_ref, grow_ref, gtok_ref = next(it), next(it), next(it)
    o_ref = next(it)
    wb = next(it)
    wb2 = next(it) if glu else None

    i = pl.program_id(1)
    is_sample = i >= n_prompt_tiles

    @pl.when(i == 0)
    def _():
        wb[...] = w_ref[...].astype(BF16)
        if glu:
            wb2[...] = w2_ref[...].astype(BF16)

    def compute(hv):
        acc = jnp.dot(hv, wb[...], preferred_element_type=F32)
        if has_bias:
            acc = acc + b_ref[...]
        if glu:
            acc2 = jnp.dot(hv, wb2[...], preferred_element_type=F32) + b2_ref[...]
            acc = acc * jax.nn.sigmoid(acc2)
        if has_resid:
            gate = jnp.where(is_sample, gtok_ref[...], grow_ref[...])
            acc = x_ref[...] + gate * acc
        o_ref[...] = acc.astype(o_ref.dtype)

    if pair:
        @pl.when(jnp.logical_not(is_sample))
        def _():
            compute(h_ref[...])

        @pl.when(is_sample)
        def _():
            compute(hs_ref[...])
    else:
        compute(h_ref[...])


def _linear_call(h, w, wl, *, n_out, tn, out_dtype, seq_tiles, n_batch, h_sample=None, bias=None, glu=False,
                 resid=None, name="linear"):
    k = w.shape[1]
    pair = h_sample is not None
    n_prompt_tiles = h.shape[0] // ROW_TILE - (0 if pair else 1)
    n_tiles = n_prompt_tiles + 1
    r = n_tiles * ROW_TILE
    nblk = n_out // tn

    args = [h]
    in_specs = [pl.BlockSpec((ROW_TILE, k), lambda j, i: (jnp.minimum(i, n_prompt_tiles - 1) if pair else i, 0))]
    if pair:
        args.append(h_sample)
        in_specs.append(pl.BlockSpec((ROW_TILE, k), lambda j, i: (0, 0)))
    args.append(w)
    in_specs.append(pl.BlockSpec((None, k, tn), lambda j, i: (wl, 0, j)))
    if glu:
        args.append(w)
        in_specs.append(pl.BlockSpec((None, k, tn), lambda j, i: (wl, 0, nblk + j)))
    if bias is not None:
        bias = bias.reshape(bias.shape[0], 1, bias.shape[1])
        args.append(bias)
        in_specs.append(pl.BlockSpec((None, 1, tn), lambda j, i: (wl, 0, j)))
        if glu:
            args.append(bias)
            in_specs.append(pl.BlockSpec((None, 1, tn), lambda j, i: (wl, 0, nblk + j)))
    if resid is not None:
        x, mod4, modtok, layer, gcol = resid
        cb = gcol * nblk
        args += [x, mod4, modtok]
        in_specs += [
            pl.BlockSpec((ROW_TILE, tn), lambda j, i: (i, j)),
            pl.BlockSpec((None, None, 1, tn),
                         lambda j, i: (layer, jnp.minimum(i // seq_tiles, n_batch - 1), 0, cb + j)),
            pl.BlockSpec((None, ROW_TILE, tn), lambda j, i: (layer, 0, cb + j)),
        ]
    scratch = [pltpu.VMEM((k, tn), BF16)] + ([pltpu.VMEM((k, tn), BF16)] if glu else [])
    body = functools.partial(_linear_body, n_prompt_tiles=n_prompt_tiles, pair=pair, glu=glu,
                             has_bias=bias is not None, has_resid=resid is not None)
    return pl.pallas_call(
        body,
        out_shape=jax.ShapeDtypeStruct((r, n_out), out_dtype),
        grid=(nblk, n_tiles),
        in_specs=in_specs,
        out_specs=pl.BlockSpec((ROW_TILE, tn), lambda j, i: (i, j)),
        scratch_shapes=scratch,
        compiler_params=_cparams(("arbitrary", "arbitrary")),
        name=name,
    )(*args)


def _rotate(x, cos, sin):
    half = x.shape[-1] // 2
    x1, x2 = x[:, :half], x[:, half:]
    return jnp.concatenate([x1 * cos - x2 * sin, x1 * sin + x2 * cos], axis=-1)


def _group_norm_gate(o, g, gain, bias):
    mu = jnp.mean(o, axis=-1, keepdims=True)
    var = jnp.mean(jnp.square(o - mu), axis=-1, keepdims=True)
    on = (o - mu) * lax.rsqrt(var + NORM_EPS) * gain + bias
    g = g.astype(F32)
    return (g * jax.nn.sigmoid(g) * on).astype(BF16)


def _ret_prompt_body(q_ref, k_ref, v_ref, g_ref, cos_ref, sin_ref, dm_ref, qd_ref, kd_ref, gl_ref,
                     gng_ref, gnb_ref, o_ref, st_ref, s_acc, *, n_chunks):
    s_acc[...] = jnp.zeros_like(s_acc)
    dmask = dm_ref[...]
    qdec = qd_ref[...]
    kdec = kd_ref[...]
    gl = gl_ref[0:1, 0:1]
    gng = gng_ref[...]
    gnb = gnb_ref[...]

    def chunk(c, carry):
        r0 = pl.multiple_of(c * RET_CHUNK, RET_CHUNK)
        rows = pl.ds(r0, RET_CHUNK)
        cos = cos_ref[rows, :]
        sin = sin_ref[rows, :]
        qr = _rotate(q_ref[rows, :].astype(F32), cos, sin)
        kr = _rotate(k_ref[rows, :].astype(F32), cos, sin)
        v = v_ref[rows, :]
        state = s_acc[...]
        scores = lax.dot_general(qr.astype(BF16), kr.astype(BF16), (((1,), (1,)), ((), ())),
                                 preferred_element_type=F32)
        scores = scores * dmask
        out = jnp.dot(scores.astype(BF16), v, preferred_element_type=F32)
        out = out + jnp.dot((qr * qdec).astype(BF16), state.astype(BF16), preferred_element_type=F32)
        kv = lax.dot_general((kr * kdec).astype(BF16), v, (((0,), (0,)), ((), ())), preferred_element_type=F32)
        s_acc[...] = gl * state + kv
        o_ref[rows, :] = _group_norm_gate(out, g_ref[rows, :], gng, gnb)
        return carry

    lax.fori_loop(0, n_chunks, chunk, 0)
    st_ref[...] = s_acc[...]


def _ret_tables(chunk, dk, n_valid=None):
    n_valid = chunk if n_valid is None else n_valid
    lg = jnp.log1p(-jnp.exp2(-5.0 - jnp.arange(RET_HEADS, dtype=F32)))
    idx = jnp.arange(chunk, dtype=F32)
    diff = idx[:, None] - idx[None, :]
    inside = (idx[:, None] < n_valid) & (idx[None, :] < n_valid)
    dmask = jnp.where((diff[None] >= 0) & inside[None],
                      jnp.exp(jnp.maximum(diff, 0.0)[None] * lg[:, None, None]), 0.0) * (dk ** -0.5)
    qdec = jnp.exp((idx[None, :] + 1.0) * lg[:, None])
    kdec = jnp.where(idx[None, :] < n_valid, jnp.exp((n_valid - 1.0 - idx)[None, :] * lg[:, None]), 0.0) * (dk ** -0.5)
    gl = jnp.exp(n_valid * lg)
    qdec = jnp.broadcast_to(qdec[:, :, None], (RET_HEADS, chunk, dk))
    kdec = jnp.broadcast_to(kdec[:, :, None], (RET_HEADS, chunk, dk))
    gl = jnp.broadcast_to(gl[:, None, None], (RET_HEADS, 8, 128))
    return dmask, qdec, kdec, gl


def _rope_tables(pos, half):
    inv = ROPE_BASE ** (-jnp.arange(half, dtype=F32) / half)
    ang = pos.astype(F32)[:, None] * inv[None, :]
    return jnp.cos(ang), jnp.sin(ang)


def _ret_prompt_call(proj, gn_gain, gn_bias, j, *, n_batch, seq, dk, dv):
    h = RET_HEADS
    n_chunks = seq // RET_CHUNK
    cos, sin = _rope_tables(jnp.arange(seq, dtype=jnp.int32), dk // 2)
    dmask, qdec, kdec, gl = _ret_tables(RET_CHUNK, dk)
    kcol, vcol, gcol = h, (2 * h * dk) // dv, (2 * h * dk) // dv + h
    body = functools.partial(_ret_prompt_body, n_chunks=n_chunks)
    return pl.pallas_call(
        body,
        out_shape=[jax.ShapeDtypeStruct((n_batch * seq, h * dv), BF16),
                   jax.ShapeDtypeStruct((n_batch, h, dk, dv), F32)],
        grid=(n_batch, h),
        in_specs=[
            pl.BlockSpec((seq, dk), lambda b, hh: (b, hh)),
            pl.BlockSpec((seq, dk), lambda b, hh: (b, kcol + hh)),
            pl.BlockSpec((seq, dv), lambda b, hh: (b, vcol + hh)),
            pl.BlockSpec((seq, dv), lambda b, hh: (b, gcol + hh)),
            pl.BlockSpec((seq, dk // 2), lambda b, hh: (0, 0)),
            pl.BlockSpec((seq, dk // 2), lambda b, hh: (0, 0)),
            pl.BlockSpec((None, RET_CHUNK, RET_CHUNK), lambda b, hh: (hh, 0, 0)),
            pl.BlockSpec((None, RET_CHUNK, dk), lambda b, hh: (hh, 0, 0)),
            pl.BlockSpec((None, RET_CHUNK, dk), lambda b, hh: (hh, 0, 0)),
            pl.BlockSpec((None, 8, 128), lambda b, hh: (hh, 0, 0)),
            pl.BlockSpec((None, 1, dv), lambda b, hh: (j, 0, hh)),
            pl.BlockSpec((None, 1, dv), lambda b, hh: (j, 0, hh)),
        ],
        out_specs=[pl.BlockSpec((seq, dv), lambda b, hh: (b, hh)),
                   pl.BlockSpec((None, None, dk, dv), lambda b, hh: (b, hh, 0, 0))],
        scratch_shapes=[pltpu.VMEM((dk, dv), F32)],
        compiler_params=_cparams(("arbitrary", "arbitrary")),
        name="retention_prompt",
    )(proj, proj, proj, proj, cos, sin, dmask, qdec, kdec, gl, gn_gain, gn_bias)


def _ret_sample_body(p_ref, st_ref, cos_ref, sin_ref, dm_ref, qd_ref, kd_ref, gl_ref, gng_ref, gnb_ref,
                     *rest, dk, dv):
    o_ref, so_ref = rest[-2], rest[-1]
    h = RET_HEADS
    cos = cos_ref[...]
    sin = sin_ref[...]
    pad = 128 - SAMPLE_ROWS
    for hh in range(h):
        q = p_ref[:, hh * dk:(hh + 1) * dk].astype(F32)
        k = p_ref[:, h * dk + hh * dk:h * dk + (hh + 1) * dk].astype(F32)
        v = p_ref[:, 2 * h * dk + hh * dv:2 * h * dk + (hh + 1) * dv]
        g = p_ref[:, 2 * h * dk + h * dv + hh * dv:2 * h * dk + h * dv + (hh + 1) * dv]
        qr = _rotate(q, cos, sin)
        kr = _rotate(k, cos, sin)
        k_pad = jnp.concatenate([kr.astype(BF16), jnp.zeros((pad, dk), BF16)], axis=0)
        kd_pad = jnp.concatenate([(kr * kd_ref[hh]).astype(BF16), jnp.zeros((pad, dk), BF16)], axis=0)
        v_pad = jnp.concatenate([v, jnp.zeros((pad, dv), BF16)], axis=0)
        state = st_ref[hh]
        scores = lax.dot_general(qr.astype(BF16), k_pad, (((1,), (1,)), ((), ())), preferred_element_type=F32)
        scores = scores * dm_ref[hh]
        out = jnp.dot(scores.astype(BF16), v_pad, preferred_element_type=F32)
        out = out + jnp.dot((qr * qd_ref[hh]).astype(BF16), state.astype(BF16), preferred_element_type=F32)
        kv = lax.dot_general(kd_pad, v_pad, (((0,), (0,)), ((), ())), preferred_element_type=F32)
        so_ref[hh] = gl_ref[hh, 0:1, 0:1] * state + kv
        o_ref[:, hh * dv:(hh + 1) * dv] = _group_norm_gate(out, g, gng_ref[:, hh * dv:(hh + 1) * dv],
                                                           gnb_ref[:, hh * dv:(hh + 1) * dv])


def _ret_sample_call(proj, state, gn_gain, gn_bias, j, new_state, *, n_prompt_rows, n_seq, dec_seq, dk, dv):
    h = RET_HEADS
    s = SAMPLE_ROWS
    cos, sin = _rope_tables(PAST_LEN + jnp.arange(s, dtype=jnp.int32), dk // 2)
    dmask, qdec, kdec, gl = _ret_tables(s, dk, n_valid=dec_seq)
    dmask = jnp.pad(dmask, ((0, 0), (0, 0), (0, 128 - s)))
    base = n_prompt_rows // s
    body = functools.partial(_ret_sample_body, dk=dk, dv=dv)
    width = proj.shape[1]
    args = [proj, state, cos, sin, dmask, qdec, kdec, gl, gn_gain, gn_bias]
    in_specs = [
        pl.BlockSpec((s, width), lambda b: (base + b, 0)),
        pl.BlockSpec((None, None, h, dk, dv), lambda b: (j, b, 0, 0, 0)),
        pl.BlockSpec((s, dk // 2), lambda b: (0, 0)),
        pl.BlockSpec((s, dk // 2), lambda b: (0, 0)),
        pl.BlockSpec((h, s, 128), lambda b: (0, 0, 0)),
        pl.BlockSpec((h, s, dk), lambda b: (0, 0, 0)),
        pl.BlockSpec((h, s, dk), lambda b: (0, 0, 0)),
        pl.BlockSpec((h, 8, 128), lambda b: (0, 0, 0)),
        pl.BlockSpec((None, 1, h * dv), lambda b: (j, 0, 0)),
        pl.BlockSpec((None, 1, h * dv), lambda b: (j, 0, 0)),
    ]
    aliases = {}
    if new_state is not None:
        aliases = {len(args): 1}
        args.append(new_state)
        in_specs.append(pl.BlockSpec(memory_space=pl.ANY))
    return pl.pallas_call(
        body,
        out_shape=[jax.ShapeDtypeStruct((n_seq * s, h * dv), BF16),
                   jax.ShapeDtypeStruct(state.shape, F32)],
        grid=(n_seq,),
        in_specs=in_specs,
        out_specs=[pl.BlockSpec((s, h * dv), lambda b: (b, 0)),
                   pl.BlockSpec((None, None, h, dk, dv), lambda b: (j, b, 0, 0, 0))],
        input_output_aliases=aliases,
        compiler_params=_cparams(("arbitrary",)),
        name="retention_sample",
    )(*args)


CONV_HALO = 32
CONV_ROW_CHUNK = 64
CONV_LANES = 128


def _layer_norm_swish(u, gain, bias):
    mu = jnp.mean(u, axis=-1, keepdims=True)
    var = jnp.mean(jnp.square(u - mu), axis=-1, keepdims=True)
    un = (u - mu) * lax.rsqrt(var + NORM_EPS) * gain + bias
    return (un * jax.nn.sigmoid(un)).astype(BF16)


def _conv_prompt_body(z_ref, w_ref, bdw_ref, lng_ref, lnb_ref, o_ref, zbuf, ubuf, *, tt, d):
    t = pl.program_id(1)

    @pl.when(t == 0)
    def _():
        zbuf[0:CONV_HALO, :] = jnp.zeros((CONV_HALO, d), F32)

    @pl.when(t > 0)
    def _():
        zbuf[0:CONV_HALO, :] = zbuf[tt:tt + CONV_HALO, :]

    zbuf[CONV_HALO:CONV_HALO + tt, :] = z_ref[...].astype(F32)

    span = CONV_ROW_CHUNK + CONV_HALO
    n_row_chunks = tt // CONV_ROW_CHUNK
    n_strips = d // CONV_LANES

    def strip(n, carry):
        r0 = pl.multiple_of((n % n_row_chunks) * CONV_ROW_CHUNK, CONV_ROW_CHUNK)
        c0 = pl.multiple_of((n // n_row_chunks) * CONV_LANES, CONV_LANES)
        cols = pl.ds(c0, CONV_LANES)
        blk = zbuf[pl.ds(r0, span), cols]
        acc = jnp.zeros((CONV_ROW_CHUNK, CONV_LANES), F32)
        for b in range(8):
            rb = blk if b == 0 else pltpu.roll(blk, span - b, axis=0)
            for a in range(5):
                o = 8 * a + b
                if 2 <= o <= CONV_HALO:
                    acc = acc + rb[8 * a:8 * a + CONV_ROW_CHUNK, :] * w_ref[pl.ds(o - 2, 1), cols]
        ubuf[pl.ds(r0, CONV_ROW_CHUNK), cols] = acc
        return carry

    lax.fori_loop(0, n_row_chunks * n_strips, strip, 0)

    ln_rows = 128

    def ln_chunk(c, carry):
        rows = pl.ds(pl.multiple_of(c * ln_rows, ln_rows), ln_rows)
        o_ref[rows, :] = _layer_norm_swish(ubuf[rows, :] + bdw_ref[...], lng_ref[...], lnb_ref[...])
        return carry

    lax.fori_loop(0, tt // ln_rows, ln_chunk, 0)


def _conv_prompt_call(z, w_dw, b_dw, ln_gain, ln_bias, *, n_batch, seq):
    d = z.shape[1]
    tt = ROW_TILE
    nt = seq // tt
    w_pad = jnp.pad(w_dw, ((0, 32 - CONV_WIDTH), (0, 0)))
    body = functools.partial(_conv_prompt_body, tt=tt, d=d)
    vec = pl.BlockSpec((1, d), lambda b, t: (0, 0))
    return pl.pallas_call(
        body,
        out_shape=jax.ShapeDtypeStruct((n_batch * seq, d), BF16),
        grid=(n_batch, nt),
        in_specs=[pl.BlockSpec((tt, d), lambda b, t: (b * nt + t, 0)),
                  pl.BlockSpec((32, d), lambda b, t: (0, 0)), vec, vec, vec],
        out_specs=pl.BlockSpec((tt, d), lambda b, t: (b * nt + t, 0)),
        scratch_shapes=[pltpu.VMEM((tt + CONV_HALO, d), F32), pltpu.VMEM((tt, d), F32)],
        compiler_params=_cparams(("arbitrary", "arbitrary")),
        name="conv_prompt",
    )(z, w_pad, b_dw, ln_gain, ln_bias)


def _conv_sample_body(z_ref, c_ref, w_ref, bdw_ref, lng_ref, lnb_ref, o_ref, zbuf, *, d):
    s = SAMPLE_ROWS
    zbuf[0:CONV_STATE, :] = c_ref[...]
    zbuf[CONV_STATE:CONV_STATE + s, :] = z_ref[...].astype(F32)
    acc = jnp.zeros((s, d), F32)
    for j in range(CONV_WIDTH):
        acc = acc + zbuf[j:j + s, :] * w_ref[j:j + 1, :]
    o_ref[...] = _layer_norm_swish(acc + bdw_ref[...], lng_ref[...], lnb_ref[...])


def _conv_sample_call(z, cache, w_dw, b_dw, ln_gain, ln_bias, *, n_prompt_rows, n_seq):
    d = z.shape[1]
    s = SAMPLE_ROWS
    base = n_prompt_rows // s
    body = functools.partial(_conv_sample_body, d=d)
    vec = pl.BlockSpec((1, d), lambda b: (0, 0))
    return pl.pallas_call(
        body,
        out_shape=jax.ShapeDtypeStruct((n_seq * s, d), BF16),
        grid=(n_seq,),
        in_specs=[pl.BlockSpec((s, d), lambda b: (base + b, 0)),
                  pl.BlockSpec((None, CONV_STATE, d), lambda b: (b, 0, 0)),
                  pl.BlockSpec((CONV_WIDTH, d), lambda b: (0, 0)), vec, vec, vec],
        out_specs=pl.BlockSpec((s, d), lambda b: (b, 0)),
        scratch_shapes=[pltpu.VMEM((CONV_STATE + s, d), F32)],
        compiler_params=_cparams(("arbitrary",)),
        name="conv_sample",
    )(z, cache, w_dw, b_dw, ln_gain, ln_bias)


def _t5_bucket(dist):
    n = jnp.maximum(dist, 0)
    max_exact = REL_BUCKETS // 2
    nf = jnp.maximum(n, 1).astype(F32)
    large = max_exact + (jnp.log(nf / max_exact) / math.log(REL_MAX_DIST / max_exact)
                         * (REL_BUCKETS - max_exact)).astype(jnp.int32)
    large = jnp.minimum(large, REL_BUCKETS - 1)
    return jnp.where(n < max_exact, n, large)


def _bias_table(rel_bias, dist, valid):
    onehot = (_t5_bucket(dist).reshape(-1)[None, :] == jnp.arange(REL_BUCKETS, dtype=jnp.int32)[:, None]).astype(F32)
    tbl = jnp.dot(rel_bias.astype(F32).T, onehot, precision=lax.Precision.HIGHEST)
    tbl = jnp.where(valid.reshape(-1)[None, :], tbl, NEG_INF)
    return tbl.reshape(-1, dist.shape[-1])


def _softmax_sink_pv(s, sink, vals):
    m = jnp.maximum(jnp.max(s, axis=-1, keepdims=True), sink)
    p = jnp.exp(s - m)
    den = jnp.sum(p, axis=-1, keepdims=True) + jnp.exp(sink - m)
    o = jnp.dot(p.astype(BF16), vals, preferred_element_type=F32)
    return o / den


def _swa_prompt_body(sink_ref, q_ref, kvp_ref, kvc_ref, gq_ref, gk_ref, bias_ref, o_ref, kn_ref, *, hd, n_blocks):
    i = pl.program_id(1)
    blk = WINDOW
    g = ATT_GROUP
    nkv = ATT_KV_HEADS
    gq = gq_ref[...] * (hd ** -0.5)
    gk = gk_ref[...]
    ones_rep = jnp.ones((hd, 128), BF16)
    is_first = i == 0
    col = lax.broadcasted_iota(jnp.int32, (g * blk, 2 * blk), 1)

    def rms_rows(x, gain):
        ssq = jnp.dot((x * x).astype(BF16), ones_rep, preferred_element_type=F32)[:, 0:hd]
        return x * lax.rsqrt(ssq * (1.0 / hd) + NORM_EPS) * gain

    for hk in range(nkv):
        keys = jnp.concatenate([kvp_ref[:, hk * hd:(hk + 1) * hd], kvc_ref[:, hk * hd:(hk + 1) * hd]], axis=0)
        vals = jnp.concatenate([kvp_ref[:, (nkv + hk) * hd:(nkv + hk + 1) * hd],
                                kvc_ref[:, (nkv + hk) * hd:(nkv + hk + 1) * hd]], axis=0)
        kn = rms_rows(keys.astype(F32), gk)
        qs = jnp.concatenate([q_ref[:, (hk * g + gg) * hd:(hk * g + gg + 1) * hd] for gg in range(g)], axis=0)
        qn = rms_rows(qs.astype(F32), gq)
        s = lax.dot_general(qn.astype(BF16), kn.astype(BF16), (((1,), (1,)), ((), ())),
                            preferred_element_type=F32)
        s = s + bias_ref[hk * g * blk:(hk + 1) * g * blk, :]
        s = jnp.where(jnp.logical_and(is_first, col < blk), NEG_INF, s)
        sink = jnp.concatenate([jnp.full((blk, 1), sink_ref[hk * g + gg], F32) for gg in range(g)], axis=0)
        o = _softmax_sink_pv(s, sink, vals)
        for gg in range(g):
            o_ref[:, (hk * g + gg) * hd:(hk * g + gg + 1) * hd] = o[gg * blk:(gg + 1) * blk, :].astype(BF16)

        @pl.when(i == n_blocks - 1)
        def _():
            kn_ref[:, hk * hd:(hk + 1) * hd] = kn[blk:, :]


def _swa_prompt_call(proj, q_gain, k_gain, sinks, rel_bias, *, n_batch, seq, hd):
    blk = WINDOW
    nb = seq // blk
    nq = ATT_Q_HEADS * hd
    nkv2 = 2 * ATT_KV_HEADS * hd
    kvcol = nq // nkv2
    i_idx = jnp.arange(blk, dtype=jnp.int32)[:, None]
    j_idx = jnp.arange(2 * blk, dtype=jnp.int32)[None, :]
    dist = blk + i_idx - j_idx
    bias = _bias_table(rel_bias, dist, (dist >= 0) & (dist < WINDOW))
    body = functools.partial(_swa_prompt_body, hd=hd, n_blocks=nb)
    return pl.pallas_call(
        body,
        out_shape=[jax.ShapeDtypeStruct((n_batch * seq, nq), BF16),
                   jax.ShapeDtypeStruct((n_batch * blk, ATT_KV_HEADS * hd), F32)],
        grid=(n_batch, nb),
        in_specs=[
            pl.BlockSpec(memory_space=pltpu.SMEM),
            pl.BlockSpec((blk, nq), lambda b, i: (b * nb + i, 0)),
            pl.BlockSpec((blk, nkv2), lambda b, i: (b * nb + jnp.maximum(i - 1, 0), kvcol)),
            pl.BlockSpec((blk, nkv2), lambda b, i: (b * nb + i, kvcol)),
            pl.BlockSpec((1, hd), lambda b, i: (0, 0)),
            pl.BlockSpec((1, hd), lambda b, i: (0, 0)),
            pl.BlockSpec((ATT_Q_HEADS * blk, 2 * blk), lambda b, i: (0, 0)),
        ],
        out_specs=[pl.BlockSpec((blk, nq), lambda b, i: (b * nb + i, 0)),
                   pl.BlockSpec((blk, ATT_KV_HEADS * hd), lambda b, i: (b, 0))],
        compiler_params=_cparams(("arbitrary", "arbitrary")),
        name="swa_prompt",
    )(sinks, proj, proj, proj, q_gain, k_gain, bias)


def _swa_sample_body(sink_ref, p_ref, ck_ref, cv_ref, gq_ref, gk_ref, bias_ref, o_ref, kn_ref, *, hd):
    s = SAMPLE_ROWS
    g = ATT_GROUP
    nkv = ATT_KV_HEADS
    nq = ATT_Q_HEADS * hd
    gq = gq_ref[...]
    gk = gk_ref[...]

    def rms(x, gain):
        return x * lax.rsqrt(jnp.mean(x * x, axis=-1, keepdims=True) + NORM_EPS) * gain

    for hk in range(nkv):
        kn = rms(p_ref[:, nq + hk * hd:nq + (hk + 1) * hd].astype(F32), gk)
        kn_ref[:, hk * hd:(hk + 1) * hd] = kn
        keys = jnp.concatenate([ck_ref[:, hk * hd:(hk + 1) * hd], kn], axis=0).astype(BF16)
        vnew = p_ref[:, nq + (nkv + hk) * hd:nq + (nkv + hk + 1) * hd].astype(F32)
        vals = jnp.concatenate([cv_ref[:, hk * hd:(hk + 1) * hd], vnew], axis=0).astype(BF16)
        qs = jnp.concatenate([rms(p_ref[:, (hk * g + gg) * hd:(hk * g + gg + 1) * hd].astype(F32), gq)
                              for gg in range(g)], axis=0)
        sc = lax.dot_general(qs.astype(BF16), keys, (((1,), (1,)), ((), ())), preferred_element_type=F32)
        sc = sc * (hd ** -0.5) + bias_ref[hk * g * s:(hk + 1) * g * s, :]
        sink = jnp.concatenate([jnp.full((s, 1), sink_ref[hk * g + gg], F32) for gg in range(g)], axis=0)
        o = _softmax_sink_pv(sc, sink, vals)
        for gg in range(g):
            o_ref[:, (hk * g + gg) * hd:(hk * g + gg + 1) * hd] = o[gg * s:(gg + 1) * s, :].astype(BF16)


def _swa_sample_call(proj, cache_k, cache_v, q_gain, k_gain, sinks, rel_bias, *, n_prompt_rows, n_seq, dec_seq, hd):
    s = SAMPLE_ROWS
    nq = ATT_Q_HEADS * hd
    nkv = ATT_KV_HEADS * hd
    sc = cache_k.shape[1]
    i_idx = jnp.arange(s, dtype=jnp.int32)[:, None]
    j_idx = jnp.arange(sc + s, dtype=jnp.int32)[None, :]
    dist = sc + i_idx - j_idx
    valid = (dist >= 0) & (dist < WINDOW) & (j_idx < sc + dec_seq)
    bias = _bias_table(rel_bias, dist, valid)
    base = n_prompt_rows // s
    body = functools.partial(_swa_sample_body, hd=hd)
    return pl.pallas_call(
        body,
        out_shape=[jax.ShapeDtypeStruct((n_seq * s, nq), BF16),
                   jax.ShapeDtypeStruct((n_seq * s, nkv), F32)],
        grid=(n_seq,),
        in_specs=[
            pl.BlockSpec(memory_space=pltpu.SMEM),
            pl.BlockSpec((s, proj.shape[1]), lambda b: (base + b, 0)),
            pl.BlockSpec((None, sc, nkv), lambda b: (b, 0, 0)),
            pl.BlockSpec((None, sc, nkv), lambda b: (b, 0, 0)),
            pl.BlockSpec((1, hd), lambda b: (0, 0)),
            pl.BlockSpec((1, hd), lambda b: (0, 0)),
            pl.BlockSpec((ATT_Q_HEADS * s, sc + s), lambda b: (0, 0)),
        ],
        out_specs=[pl.BlockSpec((s, nq), lambda b: (b, 0)),
                   pl.BlockSpec((s, nkv), lambda b: (b, 0))],
        compiler_params=_cparams(("arbitrary",)),
        name="swa_sample",
    )(sinks, proj, cache_k, cache_v, q_gain, k_gain, bias)


MOE_TILE = 256


def _moe_body(src_ref, dst_ref, te_ref, nu_ref, h_hbm, win_ref, wout_ref, y_hbm, xg, yb, wib, wob, gsem, ssem,
              *, ff):
    t = pl.program_id(0)
    nt = pl.num_programs(0)
    n_used = nu_ref[0]
    tm = MOE_TILE
    slot = lax.rem(t, 2)

    def gather_start(tile, sl):
        base = tile * tm
        for r in range(tm):
            pltpu.make_async_copy(h_hbm.at[pl.ds(src_ref[base + r], 1), :], xg.at[sl, pl.ds(r, 1), :],
                                  gsem.at[sl]).start()

    def gather_wait(sl):
        pltpu.make_async_copy(h_hbm.at[pl.ds(0, tm), :], xg.at[sl], gsem.at[sl]).wait()

    def scatter_start(tile, sl):
        base = tile * tm
        for r in range(tm):
            pltpu.make_async_copy(yb.at[sl, pl.ds(r, 1), :], y_hbm.at[pl.ds(dst_ref[base + r], 1), :],
                                  ssem.at[sl]).start()

    def scatter_wait(sl):
        pltpu.make_async_copy(yb.at[sl], y_hbm.at[pl.ds(0, tm), :], ssem.at[sl]).wait()

    valid = t < n_used

    @pl.when(t == 0)
    def _():
        gather_start(0, 0)

    @pl.when(valid)
    def _():
        @pl.when(t >= 2)
        def _():
            scatter_wait(slot)

        gather_wait(slot)

        @pl.when(t + 1 < n_used)
        def _():
            gather_start(t + 1, 1 - slot)

        @pl.when(jnp.logical_or(t == 0, te_ref[t] != te_ref[jnp.maximum(t - 1, 0)]))
        def _():
            wib[...] = win_ref[...].astype(BF16)
            wob[...] = wout_ref[...].astype(BF16)

        gu = jnp.dot(xg[slot].astype(BF16), wib[...], preferred_element_type=F32)
        gate, up = gu[:, :ff], gu[:, ff:]
        act = (gate * jax.nn.sigmoid(gate) * up).astype(BF16)
        yb[slot] = jnp.dot(act, wob[...], preferred_element_type=F32)
        scatter_start(t, slot)

    @pl.when(t == nt - 1)
    def _():
        scatter_wait(lax.rem(n_used - 1, 2))
        scatter_wait(lax.rem(n_used, 2))


def _moe_layer(h2, route, counts, w_in, w_out, layer):
    r, d = h2.shape
    depth, n_exp, _, ff2 = w_in.shape
    ff = ff2 // 2
    tm = MOE_TILE
    n_tiles = 2 * r // tm + n_exp
    n_slots = n_tiles * tm

    cnt = counts[0, MOE_GROUPS:MOE_GROUPS + n_exp].astype(jnp.int32)
    padded = ((cnt + tm - 1) // tm) * tm
    ends = jnp.cumsum(padded)
    starts = ends - padded
    rt = route[:, :ROUTE_R2 + 1].astype(jnp.int32)
    pos = jnp.concatenate([starts[rt[:, ROUTE_E1]] + rt[:, ROUTE_R1], starts[rt[:, ROUTE_E2]] + rt[:, ROUTE_R2]])
    tok = jnp.arange(r, dtype=jnp.int32)
    dst = (2 * r + jnp.arange(n_slots, dtype=jnp.int32)).at[pos].set(
        jnp.concatenate([tok, r + tok]), unique_indices=True, indices_are_sorted=False)
    src = jnp.where(dst < r, dst, jnp.where(dst < 2 * r, dst - r, 0))
    tile_start = jnp.arange(n_tiles, dtype=jnp.int32) * tm
    tile_expert = jnp.minimum(jnp.sum(tile_start[:, None] >= ends[None, :], axis=1), n_exp - 1).astype(jnp.int32)
    n_used = (ends[-1:] // tm).astype(jnp.int32)

    return pl.pallas_call(
        functools.partial(_moe_body, ff=ff),
        out_shape=jax.ShapeDtypeStruct((2 * r + n_slots, d), F32),
        grid_spec=pltpu.PrefetchScalarGridSpec(
            num_scalar_prefetch=4,
            grid=(n_tiles,),
            in_specs=[
                pl.BlockSpec(memory_space=pl.ANY),
                pl.BlockSpec((None, None, d, ff2), lambda t, src, dst, te, nu: (layer, te[t], 0, 0)),
                pl.BlockSpec((None, None, ff, d), lambda t, src, dst, te, nu: (layer, te[t], 0, 0)),
            ],
            out_specs=pl.BlockSpec(memory_space=pl.ANY),
            scratch_shapes=[pltpu.VMEM((2, tm, d), F32), pltpu.VMEM((2, tm, d), F32),
                            pltpu.VMEM((d, ff2), BF16), pltpu.VMEM((ff, d), BF16),
                            pltpu.SemaphoreType.DMA((2,)), pltpu.SemaphoreType.DMA((2,))],
        ),
        compiler_params=_cparams(("arbitrary",)),
        name="moe_experts",
    )(src, dst, tile_expert, n_used, h2, w_in, w_out)


def kernel(x_prompt, x_sample, c_prompt, c_sample, state_ret, cache_conv, cache_swa_k, cache_swa_v, ada_w, ada_b, norm_gain, ret_w_in, ret_gn_gain, ret_gn_bias, ret_w_out, conv_w_pw1, conv_b_pw1, conv_w_dw, conv_b_dw, conv_ln_gain, conv_ln_bias, conv_w_pw2, conv_b_pw2, att_w_qkv, att_q_gain, att_k_gain, att_sinks, att_w_o, rel_bias, moe_wg, moe_bg, moe_we, moe_be, moe_w_in, moe_w_out):
    n_batch, seq, d = x_prompt.shape
    n_seq, dec_seq, _ = x_sample.shape
    depth = ada_w.shape[0]
    s = SAMPLE_ROWS
    assert n_seq * s == ROW_TILE and seq % ROW_TILE == 0 and dec_seq <= s
    n_prompt_rows = n_batch * seq
    seq_tiles = seq // ROW_TILE
    dk = ret_w_in.shape[2] // (6 * RET_HEADS)
    dv = 2 * dk
    hd = d // ATT_Q_HEADS
    geom = dict(seq_tiles=seq_tiles, n_batch=n_batch)

    xs_pad = jnp.pad(x_sample, ((0, 0), (0, s - dec_seq), (0, 0))).reshape(n_seq * s, d)
    x = jnp.concatenate([x_prompt.reshape(n_prompt_rows, d), xs_pad], axis=0)

    n_c = n_batch + n_seq
    c_rows = ((n_c + 7) // 8) * 8
    c_all = jnp.pad(jnp.concatenate([c_prompt, c_sample], axis=0), ((0, c_rows - n_c), (0, 0)))
    mod = _ada_call(c_all, ada_w, ada_b)
    mod4 = mod.reshape(depth, c_rows, 1, 6 * d)
    modtok = jnp.repeat(mod[:, n_batch:n_c], s, axis=1)

    gains = norm_gain.astype(F32).reshape(2 * depth, 1, d)
    gng_all = ret_gn_gain.astype(F32)[:, None, :]
    gnb_all = ret_gn_bias.astype(F32)[:, None, :]
    state_all = state_ret.astype(F32)

    ret_p, conv_p, conv_s, kp_l, vp_l, ks_l, vs_l = [], [], [], [], [], [], []
    ret_s = None
    (h,) = _norm_call(x, mod4, modtok, norm=(gains, 0, 0, 0, 1), **geom)
    for l in range(depth):
        kind, j = l % 3, l // 3
        resid1 = (x, mod4, modtok, l, 2)
        if kind == 0:
            proj = _linear_call(h, ret_w_in, j, n_out=ret_w_in.shape[2], tn=1024, out_dtype=BF16,
                                name="ret_in", **geom)
            a_p, st_p = _ret_prompt_call(proj, gng_all, gnb_all, j, n_batch=n_batch, seq=seq, dk=dk, dv=dv)
            a_s, ret_s = _ret_sample_call(proj, state_all, gng_all, gnb_all, j, ret_s, n_prompt_rows=n_prompt_rows,
                                          n_seq=n_seq, dec_seq=dec_seq, dk=dk, dv=dv)
            ret_p.append(st_p)
            x = _linear_call(a_p, ret_w_out, j, h_sample=a_s, n_out=d, tn=512, out_dtype=F32, resid=resid1,
                             name="ret_out", **geom)
        elif kind == 1:
            z = _linear_call(h, conv_w_pw1, j, n_out=d, tn=512, out_dtype=BF16, bias=conv_b_pw1,
                             glu=True, name="conv_pw1", **geom)
            cargs = (conv_w_dw[j], conv_b_dw[j][None], conv_ln_gain[j][None], conv_ln_bias[j][None])
            a_p = _conv_prompt_call(z, *cargs, n_batch=n_batch, seq=seq)
            a_s = _conv_sample_call(z, cache_conv[j].astype(F32), *cargs, n_prompt_rows=n_prompt_rows, n_seq=n_seq)
            z_tail = z[:n_prompt_rows].reshape(n_batch, seq, d)[:, seq - CONV_STATE:].astype(F32)
            conv_p.append(z_tail)
            z_new = z[n_prompt_rows:].reshape(n_seq, s, d)[:, :dec_seq].astype(F32)
            conv_s.append(jnp.concatenate([cache_conv[j].astype(F32), z_new], axis=1)[:, -CONV_STATE:])
            x = _linear_call(a_p, conv_w_pw2, j, h_sample=a_s, n_out=d, tn=1024, out_dtype=F32,
                             bias=conv_b_pw2, resid=resid1, name="conv_pw2", **geom)
        else:
            nkv = ATT_KV_HEADS * hd
            proj = _linear_call(h, att_w_qkv, j, n_out=att_w_qkv.shape[2], tn=512, out_dtype=BF16,
                                name="att_qkv", **geom)
            gq, gk = att_q_gain[j][None].astype(F32), att_k_gain[j][None].astype(F32)
            sinks = att_sinks[j].astype(F32)
            a_p, kn_p = _swa_prompt_call(proj, gq, gk, sinks, rel_bias, n_batch=n_batch, seq=seq, hd=hd)
            win = cache_swa_k.shape[2]
            ck = cache_swa_k[j].astype(F32).reshape(n_seq, win, nkv)
            cv = cache_swa_v[j].astype(F32).reshape(n_seq, win, nkv)
            a_s, kn_s = _swa_sample_call(proj, ck, cv, gq, gk, sinks, rel_bias, n_prompt_rows=n_prompt_rows,
                                         n_seq=n_seq, dec_seq=dec_seq, hd=hd)
            vcol = ATT_Q_HEADS * hd + nkv
            v_tail = proj[:n_prompt_rows, vcol:].reshape(n_batch, seq, nkv)[:, seq - WINDOW:].astype(F32)
            kp_l.append(kn_p.reshape(n_batch, WINDOW, ATT_KV_HEADS, hd))
            vp_l.append(v_tail.reshape(n_batch, WINDOW, ATT_KV_HEADS, hd))
            k_new = kn_s.reshape(n_seq, s, nkv)[:, :dec_seq]
            v_new = proj[n_prompt_rows:, vcol:].reshape(n_seq, s, nkv)[:, :dec_seq].astype(F32)
            ks_l.append(jnp.concatenate([ck, k_new], axis=1)[:, -win:].reshape(n_seq, win, ATT_KV_HEADS, hd))
            vs_l.append(jnp.concatenate([cv, v_new], axis=1)[:, -win:].reshape(n_seq, win, ATT_KV_HEADS, hd))
            x = _linear_call(a_p, att_w_o, j, h_sample=a_s, n_out=d, tn=1024, out_dtype=F32, resid=resid1,
                             name="att_out", **geom)

        lane_pad = ROUTER_LANES - MOE_GROUPS - MOE_GROUPS * MOE_EPG
        router_w = jnp.pad(jnp.concatenate([moe_wg[l], moe_we[l]], axis=1).astype(F32), ((0, 0), (0, lane_pad)))
        router_b = jnp.pad(jnp.concatenate([moe_bg[l], moe_be[l]]).astype(F32), (0, lane_pad))[None]
        h2, route, counts = _norm_call(x, mod4, modtok, norm=(gains, 2 * l + 1, l, 3, 4),
                                       router=(router_w, router_b), **geom)
        ypair = _moe_layer(h2, route, counts, moe_w_in, moe_w_out, l)
        if l + 1 < depth:
            x, h = _norm_call(x, mod4, modtok, resid=(ypair, route, l, 5),
                              norm=(gains, 2 * l + 2, l + 1, 0, 1), **geom)
        else:
            x_p, x_s = _norm_call(x, mod4, modtok, resid=(ypair, route, l, 5), split_out=True, **geom)

    y_prompt = x_p.reshape(n_batch, seq, d)
    y_sample = x_s.reshape(n_seq, s, d)[:, :dec_seq]
    return (y_prompt, y_sample, jnp.stack(ret_p), ret_s, jnp.stack(conv_p), jnp.stack(conv_s),
            jnp.stack(kp_l), jnp.stack(vp_l), jnp.stack(ks_l), jnp.stack(vs_l))
```

```python
import functools
import math

import jax
import jax.numpy as jnp
from jax import lax
from jax.experimental import pallas as pl
from jax.experimental.pallas import tpu as pltpu

F32 = jnp.float32
BF16 = jnp.bfloat16

NORM_EPS = 1e-6
NEG_INF = -1e30
ROPE_BASE = 10000.0
PAST_LEN = 16384

RET_HEADS = 8
RET_CHUNK = 128
CONV_WIDTH = 31
CONV_STATE = CONV_WIDTH - 1
ATT_Q_HEADS = 32
ATT_KV_HEADS = 4
ATT_GROUP = ATT_Q_HEADS // ATT_KV_HEADS
WINDOW = 128
REL_BUCKETS = 32
REL_MAX_DIST = 128
MOE_GROUPS = 4
MOE_EPG = 4
MOE_TOPK = 2

ROW_TILE = 512
SAMPLE_ROWS = 16
ROUTER_LANES = 128
VMEM_LIMIT_BYTES = 56 * 1024 * 1024


def _cparams(sem):
    return pltpu.CompilerParams(dimension_semantics=sem, vmem_limit_bytes=VMEM_LIMIT_BYTES)


def _ada_body(c_ref, w_ref, b_ref, o_ref):
    c = c_ref[...]
    s = (c * jax.nn.sigmoid(c)).astype(BF16)
    o_ref[...] = jnp.dot(s, w_ref[...].astype(BF16), preferred_element_type=F32) + b_ref[...]


def _ada_call(c_all, ada_w, ada_b):
    depth, d, n = ada_w.shape
    rows = c_all.shape[0]
    tn = 1024
    return pl.pallas_call(
        _ada_body,
        out_shape=jax.ShapeDtypeStruct((depth, rows, n), F32),
        grid=(depth, n // tn),
        in_specs=[
            pl.BlockSpec((rows, d), lambda l, j: (0, 0)),
            pl.BlockSpec((None, d, tn), lambda l, j: (l, 0, j)),
            pl.BlockSpec((None, 1, tn), lambda l, j: (l, 0, j)),
        ],
        out_specs=pl.BlockSpec((None, rows, tn), lambda l, j: (l, 0, j)),
        compiler_params=_cparams(("arbitrary", "arbitrary")),
        name="ada_mod",
    )(c_all, ada_w, ada_b.reshape(depth, 1, n))


ROUTE_E1, ROUTE_E2, ROUTE_W1, ROUTE_W2, ROUTE_R1, ROUTE_R2 = range(6)


def _route_tile(logits, rb, carry):
    tm = logits.shape[0]
    g, epg = MOE_GROUPS, MOE_EPG
    lg = logits + rb
    lane = lax.broadcasted_iota(jnp.int32, lg.shape, 1)
    lane_f = lane.astype(F32)

    def first_lane(mask):
        return jnp.min(jnp.where(mask, lane_f, float(ROUTER_LANES)), axis=-1, keepdims=True).astype(jnp.int32)

    is_g = lane < g
    mg = jnp.max(jnp.where(is_g, lg, NEG_INF), axis=-1, keepdims=True)
    eg = jnp.where(is_g, jnp.exp(lg - mg), 0.0)
    p_grp = 1.0 / jnp.sum(eg, axis=-1, keepdims=True)
    grp = first_lane(jnp.logical_and(is_g, lg == mg))
    lo = g + epg * grp
    is_e = jnp.logical_and(lane >= lo, lane < lo + epg)
    me = jnp.max(jnp.where(is_e, lg, NEG_INF), axis=-1, keepdims=True)
    ee = jnp.where(is_e, jnp.exp(lg - me), 0.0)
    se = jnp.sum(ee, axis=-1, keepdims=True)
    i1 = first_lane(jnp.logical_and(is_e, lg == me))
    rest = jnp.logical_and(is_e, lane != i1)
    m2 = jnp.max(jnp.where(rest, ee, -1.0), axis=-1, keepdims=True)
    i2 = first_lane(jnp.logical_and(rest, ee == m2))
    p1 = 1.0 / se
    p2 = m2 / se
    w1 = p_grp * p1 / (p1 + p2)
    w2 = p_grp * p2 / (p1 + p2)
    sel1 = lane == i1
    sel2 = lane == i2
    onehot = jnp.where(jnp.logical_or(sel1, sel2), 1.0, 0.0)
    row = lax.broadcasted_iota(jnp.int32, (tm, tm), 0)
    col = lax.broadcasted_iota(jnp.int32, (tm, tm), 1)
    tri = jnp.where(col <= row, 1.0, 0.0).astype(BF16)
    incl = jnp.dot(tri, onehot.astype(BF16), preferred_element_type=F32)
    rank = incl - 1.0 + carry
    r1 = jnp.sum(jnp.where(sel1, rank, 0.0), axis=-1, keepdims=True)
    r2 = jnp.sum(jnp.where(sel2, rank, 0.0), axis=-1, keepdims=True)
    rec = jnp.zeros_like(lg)
    for ln, val in ((ROUTE_E1, (i1 - g).astype(F32)), (ROUTE_E2, (i2 - g).astype(F32)), (ROUTE_W1, w1),
                    (ROUTE_W2, w2), (ROUTE_R1, r1), (ROUTE_R2, r2)):
        rec = jnp.where(lane == ln, val, rec)
    return rec, carry + jnp.sum(onehot, axis=0, keepdims=True)


def _norm_body(*refs, n_prompt_tiles, has_resid, has_norm, has_router, split_out):
    it = iter(refs)
    x_ref = next(it)
    if has_resid:
        y0_ref, y1_ref, rt_ref, grow_ref, gtok_ref = next(it), next(it), next(it), next(it), next(it)
    if has_norm:
        gain_ref, shrow_ref, scrow_ref, shtok_ref, sctok_ref = next(it), next(it), next(it), next(it), next(it)
    if has_router:
        wr_ref, rb_ref = next(it), next(it)
    if has_resid:
        xo_ref = next(it)
        if split_out:
            xs_ref = next(it)
    if has_norm:
        h_ref = next(it)
    if has_router:
        rec_ref, cnt_ref, carry = next(it), next(it), next(it)

    is_sample = pl.program_id(0) >= n_prompt_tiles
    x = x_ref[...]
    if has_resid:
        gate = jnp.where(is_sample, gtok_ref[...], grow_ref[...])
        rt = rt_ref[...]
        y = rt[:, ROUTE_W1:ROUTE_W1 + 1] * y0_ref[...] + rt[:, ROUTE_W2:ROUTE_W2 + 1] * y1_ref[...]
        x = x + gate * y
        if split_out:
            @pl.when(jnp.logical_not(is_sample))
            def _():
                xo_ref[...] = x

            @pl.when(is_sample)
            def _():
                xs_ref[...] = x
        else:
            xo_ref[...] = x
    if has_norm:
        ms = jnp.mean(x * x, axis=-1, keepdims=True)
        xn = x * lax.rsqrt(ms + NORM_EPS) * gain_ref[...]
        scale = jnp.where(is_sample, sctok_ref[...], scrow_ref[...])
        shift = jnp.where(is_sample, shtok_ref[...], shrow_ref[...])
        h = xn * (1.0 + scale) + shift
        h_ref[...] = h.astype(h_ref.dtype)
        if has_router:
            @pl.when(pl.program_id(0) == 0)
            def _():
                carry[...] = jnp.zeros_like(carry)

            h_hi = h.astype(BF16)
            h_lo = (h - h_hi.astype(F32)).astype(BF16)
            logits = (jnp.dot(h_hi, wr_ref[0], preferred_element_type=F32)
                      + jnp.dot(h_lo, wr_ref[0], preferred_element_type=F32)
                      + jnp.dot(h_hi, wr_ref[1], preferred_element_type=F32))
            rec, new_carry = _route_tile(logits, rb_ref[...], carry[0:1, :])
            rec_ref[...] = rec
            carry[...] = jnp.broadcast_to(new_carry, carry.shape)
            cnt_ref[...] = carry[...]


def _norm_call(x, mod4, modtok, *, seq_tiles, n_batch, resid=None, norm=None, router=None, split_out=False):
    r, d = x.shape
    n_tiles = r // ROW_TILE
    n_prompt_tiles = n_tiles - 1

    def row_spec(layer, col):
        return pl.BlockSpec((None, None, 1, d),
                            lambda i: (layer, jnp.minimum(i // seq_tiles, n_batch - 1), 0, col))

    def tok_spec(layer, col):
        return pl.BlockSpec((None, ROW_TILE, d), lambda i: (layer, 0, col))

    tile = pl.BlockSpec((ROW_TILE, d), lambda i: (i, 0))
    args, in_specs, out_shape, out_specs = [x], [tile], [], []
    scratch = []
    if resid is not None:
        ypair, route, layer, gcol = resid
        args += [ypair, ypair, route, mod4, modtok]
        in_specs += [tile, pl.BlockSpec((ROW_TILE, d), lambda i: (n_tiles + i, 0)),
                     pl.BlockSpec((ROW_TILE, ROUTER_LANES), lambda i: (i, 0)),
                     row_spec(layer, gcol), tok_spec(layer, gcol)]
        if split_out:
            out_shape += [jax.ShapeDtypeStruct((r - ROW_TILE, d), F32), jax.ShapeDtypeStruct((ROW_TILE, d), F32)]
            out_specs += [pl.BlockSpec((ROW_TILE, d), lambda i: (jnp.minimum(i, n_prompt_tiles - 1), 0)),
                          pl.BlockSpec((ROW_TILE, d), lambda i: (0, 0))]
        else:
            out_shape.append(jax.ShapeDtypeStruct((r, d), F32))
            out_specs.append(tile)
    if norm is not None:
        gains, gidx, layer, shcol, sccol = norm
        args += [gains, mod4, mod4, modtok, modtok]
        in_specs += [pl.BlockSpec((None, 1, d), lambda i: (gidx, 0, 0)), row_spec(layer, shcol),
                     row_spec(layer, sccol), tok_spec(layer, shcol), tok_spec(layer, sccol)]
        out_shape.append(jax.ShapeDtypeStruct((r, d), F32 if router is not None else BF16))
        out_specs.append(tile)
        if router is not None:
            args += list(router)
            in_specs += [pl.BlockSpec((2, d, ROUTER_LANES), lambda i: (0, 0, 0)),
                         pl.BlockSpec((1, ROUTER_LANES), lambda i: (0, 0))]
            out_shape += [jax.ShapeDtypeStruct((r, ROUTER_LANES), F32), jax.ShapeDtypeStruct((8, ROUTER_LANES), F32)]
            out_specs += [pl.BlockSpec((ROW_TILE, ROUTER_LANES), lambda i: (i, 0)),
                          pl.BlockSpec((8, ROUTER_LANES), lambda i: (0, 0))]
            scratch.append(pltpu.VMEM((8, ROUTER_LANES), F32))
    body = functools.partial(_norm_body, n_prompt_tiles=n_prompt_tiles, has_resid=resid is not None,
                             has_norm=norm is not None, has_router=router is not None, split_out=split_out)
    return pl.pallas_call(
        body, out_shape=out_shape, grid=(n_tiles,), in_specs=in_specs, out_specs=out_specs,
        scratch_shapes=scratch, compiler_params=_cparams(("arbitrary",)), name="mod_norm",
    )(*args)


def _linear_body(*refs, n_prompt_tiles, pair, glu, has_bias, has_resid):
    it = iter(refs)
    h_ref = next(it)
    hs_ref = next(it) if pair else None
    w_ref = next(it)
    w2_ref = next(it) if glu else None
    b_ref = next(it) if has_bias else None
    b2_ref = next(it) if glu else None
    if has_resid:
        x_ref, grow_ref, gtok_ref = next(it), next(it), next(it)
    o_ref = next(it)
    wb = next(it)
    wb2 = next(it) if glu else None

    i = pl.program_id(1)
    is_sample = i >= n_prompt_tiles

    @pl.when(i == 0)
    def _():
        wb[...] = w_ref[...].astype(BF16)
        if glu:
            wb2[...] = w2_ref[...].astype(BF16)

    def compute(hv):
        acc = jnp.dot(hv, wb[...], preferred_element_type=F32)
        if has_bias:
            acc = acc + b_ref[...]
        if glu:
            acc2 = jnp.dot(hv, wb2[...], preferred_element_type=F32) + b2_ref[...]
            acc = acc * jax.nn.sigmoid(acc2)
        if has_resid:
            gate = jnp.where(is_sample, gtok_ref[...], grow_ref[...])
            acc = x_ref[...] + gate * acc
        o_ref[...] = acc.astype(o_ref.dtype)

    if pair:
        @pl.when(jnp.logical_not(is_sample))
        def _():
            compute(h_ref[...])

        @pl.when(is_sample)
        def _():
            compute(hs_ref[...])
    else:
        compute(h_ref[...])


def _linear_call(h, w, wl, *, n_out, tn, out_dtype, seq_tiles, n_batch, h_sample=None, bias=None, glu=False,
                 resid=None, name="linear"):
    k = w.shape[1]
    pair = h_sample is not None
    n_prompt_tiles = h.shape[0] // ROW_TILE - (0 if pair else 1)
    n_tiles = n_prompt_tiles + 1
    r = n_tiles * ROW_TILE
    nblk = n_out // tn

    args = [h]
    in_specs = [pl.BlockSpec((ROW_TILE, k), lambda j, i: (jnp.minimum(i, n_prompt_tiles - 1) if pair else i, 0))]
    if pair:
        args.append(h_sample)
        in_specs.append(pl.BlockSpec((ROW_TILE, k), lambda j, i: (0, 0)))
    args.append(w)
    in_specs.append(pl.BlockSpec((None, k, tn), lambda j, i: (wl, 0, j)))
    if glu:
        args.append(w)
        in_specs.append(pl.BlockSpec((None, k, tn), lambda j, i: (wl, 0, nblk + j)))
    if bias is not None:
        bias = bias.reshape(bias.shape[0], 1, bias.shape[1])
        args.append(bias)
        in_specs.append(pl.BlockSpec((None, 1, tn), lambda j, i: (wl, 0, j)))
        if glu:
            args.append(bias)
            in_specs.append(pl.BlockSpec((None, 1, tn), lambda j, i: (wl, 0, nblk + j)))
    if resid is not None:
        x, mod4, modtok, layer, gcol = resid
        cb = gcol * nblk
        args += [x, mod4, modtok]
        in_specs += [
            pl.BlockSpec((ROW_TILE, tn), lambda j, i: (i, j)),
            pl.BlockSpec((None, None, 1, tn),
                         lambda j, i: (layer, jnp.minimum(i // seq_tiles, n_batch - 1), 0, cb + j)),
            pl.BlockSpec((None, ROW_TILE, tn), lambda j, i: (layer, 0, cb + j)),
        ]
    scratch = [pltpu.VMEM((k, tn), BF16)] + ([pltpu.VMEM((k, tn), BF16)] if glu else [])
    body = functools.partial(_linear_body, n_prompt_tiles=n_prompt_tiles, pair=pair, glu=glu,
                             has_bias=bias is not None, has_resid=resid is not None)
    return pl.pallas_call(
        body,
        out_shape=jax.ShapeDtypeStruct((r, n_out), out_dtype),
        grid=(nblk, n_tiles),
        in_specs=in_specs,
        out_specs=pl.BlockSpec((ROW_TILE, tn), lambda j, i: (i, j)),
        scratch_shapes=scratch,
        compiler_params=_cparams(("arbitrary", "arbitrary")),
        name=name,
    )(*args)


def _rotate(x, cos, sin):
    half = x.shape[-1] // 2
    x1, x2 = x[:, :half], x[:, half:]
    return jnp.concatenate([x1 * cos - x2 * sin, x1 * sin + x2 * cos], axis=-1)


def _group_norm_gate(o, g, gain, bias):
    mu = jnp.mean(o, axis=-1, keepdims=True)
    var = jnp.mean(jnp.square(o - mu), axis=-1, keepdims=True)
    on = (o - mu) * lax.rsqrt(var + NORM_EPS) * gain + bias
    g = g.astype(F32)
    return (g * jax.nn.sigmoid(g) * on).astype(BF16)


def _ret_prompt_body(q_ref, k_ref, v_ref, g_ref, cos_ref, sin_ref, dm_ref, qd_ref, kd_ref, gl_ref,
                     gng_ref, gnb_ref, o_ref, st_ref, s_acc, *, n_chunks):
    s_acc[...] = jnp.zeros_like(s_acc)
    dmask = dm_ref[...]
    qdec = qd_ref[...]
    kdec = kd_ref[...]
    gl = gl_ref[0:1, 0:1]
    gng = gng_ref[...]
    gnb = gnb_ref[...]

    def chunk(c, carry):
        r0 = pl.multiple_of(c * RET_CHUNK, RET_CHUNK)
        rows = pl.ds(r0, RET_CHUNK)
        cos = cos_ref[rows, :]
        sin = sin_ref[rows, :]
        qr = _rotate(q_ref[rows, :].astype(F32), cos, sin)
        kr = _rotate(k_ref[rows, :].astype(F32), cos, sin)
        v = v_ref[rows, :]
        state = s_acc[...]
        scores = lax.dot_general(qr.astype(BF16), kr.astype(BF16), (((1,), (1,)), ((), ())),
                                 preferred_element_type=F32)
        scores = scores * dmask
        out = jnp.dot(scores.astype(BF16), v, preferred_element_type=F32)
        out = out + jnp.dot((qr * qdec).astype(BF16), state.astype(BF16), preferred_element_type=F32)
        kv = lax.dot_general((kr * kdec).astype(BF16), v, (((0,), (0,)), ((), ())), preferred_element_type=F32)
        s_acc[...] = gl * state + kv
        o_ref[rows, :] = _group_norm_gate(out, g_ref[rows, :], gng, gnb)
        return carry

    lax.fori_loop(0, n_chunks, chunk, 0, unroll=4)
    st_ref[...] = s_acc[...]


def _ret_tables(chunk, dk, n_valid=None):
    n_valid = chunk if n_valid is None else n_valid
    lg = jnp.log1p(-jnp.exp2(-5.0 - jnp.arange(RET_HEADS, dtype=F32)))
    idx = jnp.arange(chunk, dtype=F32)
    diff = idx[:, None] - idx[None, :]
    inside = (idx[:, None] < n_valid) & (idx[None, :] < n_valid)
    dmask = jnp.where((diff[None] >= 0) & inside[None],
                      jnp.exp(jnp.maximum(diff, 0.0)[None] * lg[:, None, None]), 0.0) * (dk ** -0.5)
    qdec = jnp.exp((idx[None, :] + 1.0) * lg[:, None])
    kdec = jnp.where(idx[None, :] < n_valid, jnp.exp((n_valid - 1.0 - idx)[None, :] * lg[:, None]), 0.0) * (dk ** -0.5)
    gl = jnp.exp(n_valid * lg)
    qdec = jnp.broadcast_to(qdec[:, :, None], (RET_HEADS, chunk, dk))
    kdec = jnp.broadcast_to(kdec[:, :, None], (RET_HEADS, chunk, dk))
    gl = jnp.broadcast_to(gl[:, None, None], (RET_HEADS, 8, 128))
    return dmask, qdec, kdec, gl


def _rope_tables(pos, half):
    inv = ROPE_BASE ** (-jnp.arange(half, dtype=F32) / half)
    ang = pos.astype(F32)[:, None] * inv[None, :]
    return jnp.cos(ang), jnp.sin(ang)


def _ret_prompt_call(proj, gn_gain, gn_bias, j, *, n_batch, seq, dk, dv):
    h = RET_HEADS
    n_chunks = seq // RET_CHUNK
    cos, sin = _rope_tables(jnp.arange(seq, dtype=jnp.int32), dk // 2)
    dmask, qdec, kdec, gl = _ret_tables(RET_CHUNK, dk)
    kcol, vcol, gcol = h, (2 * h * dk) // dv, (2 * h * dk) // dv + h
    body = functools.partial(_ret_prompt_body, n_chunks=n_chunks)
    return pl.pallas_call(
        body,
        out_shape=[jax.ShapeDtypeStruct((n_batch * seq, h * dv), BF16),
                   jax.ShapeDtypeStruct((n_batch, h, dk, dv), F32)],
        grid=(n_batch, h),
        in_specs=[
            pl.BlockSpec((seq, dk), lambda b, hh: (b, hh)),
            pl.BlockSpec((seq, dk), lambda b, hh: (b, kcol + hh)),
            pl.BlockSpec((seq, dv), lambda b, hh: (b, vcol + hh)),
            pl.BlockSpec((seq, dv), lambda b, hh: (b, gcol + hh)),
            pl.BlockSpec((seq, dk // 2), lambda b, hh: (0, 0)),
            pl.BlockSpec((seq, dk // 2), lambda b, hh: (0, 0)),
            pl.BlockSpec((None, RET_CHUNK, RET_CHUNK), lambda b, hh: (hh, 0, 0)),
            pl.BlockSpec((None, RET_CHUNK, dk), lambda b, hh: (hh, 0, 0)),
            pl.BlockSpec((None, RET_CHUNK, dk), lambda b, hh: (hh, 0, 0)),
            pl.BlockSpec((None, 8, 128), lambda b, hh: (hh, 0, 0)),
            pl.BlockSpec((None, 1, dv), lambda b, hh: (j, 0, hh)),
            pl.BlockSpec((None, 1, dv), lambda b, hh: (j, 0, hh)),
        ],
        out_specs=[pl.BlockSpec((seq, dv), lambda b, hh: (b, hh)),
                   pl.BlockSpec((None, None, dk, dv), lambda b, hh: (b, hh, 0, 0))],
        scratch_shapes=[pltpu.VMEM((dk, dv), F32)],
        compiler_params=_cparams(("arbitrary", "arbitrary")),
        name="retention_prompt",
    )(proj, proj, proj, proj, cos, sin, dmask, qdec, kdec, gl, gn_gain, gn_bias)


def _ret_sample_body(p_ref, st_ref, cos_ref, sin_ref, dm_ref, qd_ref, kd_ref, gl_ref, gng_ref, gnb_ref,
                     *rest, dk, dv):
    o_ref, so_ref = rest[-2], rest[-1]
    h = RET_HEADS
    cos = cos_ref[...]
    sin = sin_ref[...]
    pad = 128 - SAMPLE_ROWS
    for hh in range(h):
        q = p_ref[:, hh * dk:(hh + 1) * dk].astype(F32)
        k = p_ref[:, h * dk + hh * dk:h * dk + (hh + 1) * dk].astype(F32)
        v = p_ref[:, 2 * h * dk + hh * dv:2 * h * dk + (hh + 1) * dv]
        g = p_ref[:, 2 * h * dk + h * dv + hh * dv:2 * h * dk + h * dv + (hh + 1) * dv]
        qr = _rotate(q, cos, sin)
        kr = _rotate(k, cos, sin)
        k_pad = jnp.concatenate([kr.astype(BF16), jnp.zeros((pad, dk), BF16)], axis=0)
        kd_pad = jnp.concatenate([(kr * kd_ref[hh]).astype(BF16), jnp.zeros((pad, dk), BF16)], axis=0)
        v_pad = jnp.concatenate([v, jnp.zeros((pad, dv), BF16)], axis=0)
        state = st_ref[hh]
        scores = lax.dot_general(qr.astype(BF16), k_pad, (((1,), (1,)), ((), ())), preferred_element_type=F32)
        scores = scores * dm_ref[hh]
        out = jnp.dot(scores.astype(BF16), v_pad, preferred_element_type=F32)
        out = out + jnp.dot((qr * qd_ref[hh]).astype(BF16), state.astype(BF16), preferred_element_type=F32)
        kv = lax.dot_general(kd_pad, v_pad, (((0,), (0,)), ((), ())), preferred_element_type=F32)
        so_ref[hh] = gl_ref[hh, 0:1, 0:1] * state + kv
        o_ref[:, hh * dv:(hh + 1) * dv] = _group_norm_gate(out, g, gng_ref[:, hh * dv:(hh + 1) * dv],
                                                           gnb_ref[:, hh * dv:(hh + 1) * dv])


def _ret_sample_call(proj, state, gn_gain, gn_bias, j, new_state, *, n_prompt_rows, n_seq, dec_seq, dk, dv):
    h = RET_HEADS
    s = SAMPLE_ROWS
    cos, sin = _rope_tables(PAST_LEN + jnp.arange(s, dtype=jnp.int32), dk // 2)
    dmask, qdec, kdec, gl = _ret_tables(s, dk, n_valid=dec_seq)
    dmask = jnp.pad(dmask, ((0, 0), (0, 0), (0, 128 - s)))
    base = n_prompt_rows // s
    body = functools.partial(_ret_sample_body, dk=dk, dv=dv)
    width = proj.shape[1]
    args = [proj, state, cos, sin, dmask, qdec, kdec, gl, gn_gain, gn_bias]
    in_specs = [
        pl.BlockSpec((s, width), lambda b: (base + b, 0)),
        pl.BlockSpec((None, None, h, dk, dv), lambda b: (j, b, 0, 0, 0)),
        pl.BlockSpec((s, dk // 2), lambda b: (0, 0)),
        pl.BlockSpec((s, dk // 2), lambda b: (0, 0)),
        pl.BlockSpec((h, s, 128), lambda b: (0, 0, 0)),
        pl.BlockSpec((h, s, dk), lambda b: (0, 0, 0)),
        pl.BlockSpec((h, s, dk), lambda b: (0, 0, 0)),
        pl.BlockSpec((h, 8, 128), lambda b: (0, 0, 0)),
        pl.BlockSpec((None, 1, h * dv), lambda b: (j, 0, 0)),
        pl.BlockSpec((None, 1, h * dv), lambda b: (j, 0, 0)),
    ]
    aliases = {}
    if new_state is not None:
        aliases = {len(args): 1}
        args.append(new_state)
        in_specs.append(pl.BlockSpec(memory_space=pl.ANY))
    return pl.pallas_call(
        body,
        out_shape=[jax.ShapeDtypeStruct((n_seq * s, h * dv), BF16),
                   jax.ShapeDtypeStruct(state.shape, F32)],
        grid=(n_seq,),
        in_specs=in_specs,
        out_specs=[pl.BlockSpec((s, h * dv), lambda b: (b, 0)),
                   pl.BlockSpec((None, None, h, dk, dv), lambda b: (j, b, 0, 0, 0))],
        input_output_aliases=aliases,
        compiler_params=_cparams(("arbitrary",)),
        name="retention_sample",
    )(*args)


CONV_HALO = 32
CONV_ROW_CHUNK = 64
CONV_LANES = 128


def _layer_norm_swish(u, gain, bias):
    mu = jnp.mean(u, axis=-1, keepdims=True)
    var = jnp.mean(jnp.square(u - mu), axis=-1, keepdims=True)
    un = (u - mu) * lax.rsqrt(var + NORM_EPS) * gain + bias
    return (un * jax.nn.sigmoid(un)).astype(BF16)


def _conv_prompt_body(z_ref, w_ref, bdw_ref, lng_ref, lnb_ref, o_ref, zbuf, ubuf, *, tt, d):
    t = pl.program_id(1)

    @pl.when(t == 0)
    def _():
        zbuf[0:CONV_HALO, :] = jnp.zeros((CONV_HALO, d), F32)

    @pl.when(t > 0)
    def _():
        zbuf[0:CONV_HALO, :] = zbuf[tt:tt + CONV_HALO, :]

    zbuf[CONV_HALO:CONV_HALO + tt, :] = z_ref[...].astype(F32)

    span = CONV_ROW_CHUNK + CONV_HALO
    n_row_chunks = tt // CONV_ROW_CHUNK
    n_strips = d // CONV_LANES

    def strip(n, carry):
        r0 = pl.multiple_of((n % n_row_chunks) * CONV_ROW_CHUNK, CONV_ROW_CHUNK)
        c0 = pl.multiple_of((n // n_row_chunks) * CONV_LANES, CONV_LANES)
        cols = pl.ds(c0, CONV_LANES)
        blk = zbuf[pl.ds(r0, span), cols]
        acc = jnp.zeros((CONV_ROW_CHUNK, CONV_LANES), F32)
        for b in range(8):
            rb = blk if b == 0 else pltpu.roll(blk, span - b, axis=0)
            for a in range(5):
                o = 8 * a + b
                if 2 <= o <= CONV_HALO:
                    acc = acc + rb[8 * a:8 * a + CONV_ROW_CHUNK, :] * w_ref[pl.ds(o - 2, 1), cols]
        ubuf[pl.ds(r0, CONV_ROW_CHUNK), cols] = acc
        return carry

    lax.fori_loop(0, n_row_chunks * n_strips, strip, 0)

    ln_rows = 128

    def ln_chunk(c, carry):
        rows = pl.ds(pl.multiple_of(c * ln_rows, ln_rows), ln_rows)
        o_ref[rows, :] = _layer_norm_swish(ubuf[rows, :] + bdw_ref[...], lng_ref[...], lnb_ref[...])
        return carry

    lax.fori_loop(0, tt // ln_rows, ln_chunk, 0)


def _conv_prompt_call(z, w_dw, b_dw, ln_gain, ln_bias, *, n_batch, seq):
    d = z.shape[1]
    tt = ROW_TILE
    nt = seq // tt
    w_pad = jnp.pad(w_dw, ((0, 32 - CONV_WIDTH), (0, 0)))
    body = functools.partial(_conv_prompt_body, tt=tt, d=d)
    vec = pl.BlockSpec((1, d), lambda b, t: (0, 0))
    return pl.pallas_call(
        body,
        out_shape=jax.ShapeDtypeStruct((n_batch * seq, d), BF16),
        grid=(n_batch, nt),
        in_specs=[pl.BlockSpec((tt, d), lambda b, t: (b * nt + t, 0)),
                  pl.BlockSpec((32, d), lambda b, t: (0, 0)), vec, vec, vec],
        out_specs=pl.BlockSpec((tt, d), lambda b, t: (b * nt + t, 0)),
        scratch_shapes=[pltpu.VMEM((tt + CONV_HALO, d), F32), pltpu.VMEM((tt, d), F32)],
        compiler_params=_cparams(("arbitrary", "arbitrary")),
        name="conv_prompt",
    )(z, w_pad, b_dw, ln_gain, ln_bias)


def _conv_sample_body(z_ref, c_ref, w_ref, bdw_ref, lng_ref, lnb_ref, o_ref, zbuf, *, d):
    s = SAMPLE_ROWS
    zbuf[0:CONV_STATE, :] = c_ref[...]
    zbuf[CONV_STATE:CONV_STATE + s, :] = z_ref[...].astype(F32)
    acc = jnp.zeros((s, d), F32)
    for j in range(CONV_WIDTH):
        acc = acc + zbuf[j:j + s, :] * w_ref[j:j + 1, :]
    o_ref[...] = _layer_norm_swish(acc + bdw_ref[...], lng_ref[...], lnb_ref[...])


def _conv_sample_call(z, cache, w_dw, b_dw, ln_gain, ln_bias, *, n_prompt_rows, n_seq):
    d = z.shape[1]
    s = SAMPLE_ROWS
    base = n_prompt_rows // s
    body = functools.partial(_conv_sample_body, d=d)
    vec = pl.BlockSpec((1, d), lambda b: (0, 0))
    return pl.pallas_call(
        body,
        out_shape=jax.ShapeDtypeStruct((n_seq * s, d), BF16),
        grid=(n_seq,),
        in_specs=[pl.BlockSpec((s, d), lambda b: (base + b, 0)),
                  pl.BlockSpec((None, CONV_STATE, d), lambda b: (b, 0, 0)),
                  pl.BlockSpec((CONV_WIDTH, d), lambda b: (0, 0)), vec, vec, vec],
        out_specs=pl.BlockSpec((s, d), lambda b: (b, 0)),
        scratch_shapes=[pltpu.VMEM((CONV_STATE + s, d), F32)],
        compiler_params=_cparams(("arbitrary",)),
        name="conv_sample",
    )(z, cache, w_dw, b_dw, ln_gain, ln_bias)


def _t5_bucket(dist):
    n = jnp.maximum(dist, 0)
    max_exact = REL_BUCKETS // 2
    nf = jnp.maximum(n, 1).astype(F32)
    large = max_exact + (jnp.log(nf / max_exact) / math.log(REL_MAX_DIST / max_exact)
                         * (REL_BUCKETS - max_exact)).astype(jnp.int32)
    large = jnp.minimum(large, REL_BUCKETS - 1)
    return jnp.where(n < max_exact, n, large)


def _bias_table(rel_bias, dist, valid):
    onehot = (_t5_bucket(dist).reshape(-1)[None, :] == jnp.arange(REL_BUCKETS, dtype=jnp.int32)[:, None]).astype(F32)
    tbl = jnp.dot(rel_bias.astype(F32).T, onehot, precision=lax.Precision.HIGHEST)
    tbl = jnp.where(valid.reshape(-1)[None, :], tbl, NEG_INF)
    return tbl.reshape(-1, dist.shape[-1])


def _softmax_sink_pv(s, sink, vals):
    m = jnp.maximum(jnp.max(s, axis=-1, keepdims=True), sink)
    p = jnp.exp(s - m).astype(BF16)
    den = jnp.dot(p, jnp.ones(vals.shape, BF16), preferred_element_type=F32) + jnp.exp(sink - m)
    o = jnp.dot(p, vals, preferred_element_type=F32)
    return o / den


def _head_rms_norm(x, gain_row, seg_ref, exp_ref, hd):
    n = x.shape[1]
    ssq = jnp.dot((x * x).astype(BF16), seg_ref[0:n, :], preferred_element_type=F32)
    r = lax.rsqrt(ssq * (1.0 / hd) + NORM_EPS)
    r_hi = r.astype(BF16)
    r_lo = (r - r_hi.astype(F32)).astype(BF16)
    e = exp_ref[:, 0:n]
    scale = jnp.dot(r_hi, e, preferred_element_type=F32) + jnp.dot(r_lo, e, preferred_element_type=F32)
    return x * scale * gain_row


def _swa_prompt_body(sink_ref, q_ref, kvp_ref, kvc_ref, gq_ref, gk_ref, seg_ref, exp_ref, bias_ref, o_ref, kn_ref,
                     *, hd, n_blocks):
    i = pl.program_id(1)
    blk = WINDOW
    g = ATT_GROUP
    nkv = ATT_KV_HEADS
    nk = nkv * hd
    qn = _head_rms_norm(q_ref[...].astype(F32), gq_ref[...], seg_ref, exp_ref, hd).astype(BF16)
    kn_prev = _head_rms_norm(kvp_ref[:, 0:nk].astype(F32), gk_ref[...], seg_ref, exp_ref, hd)
    kn_cur = _head_rms_norm(kvc_ref[:, 0:nk].astype(F32), gk_ref[...], seg_ref, exp_ref, hd)
    kn = jnp.concatenate([kn_prev, kn_cur], axis=0).astype(BF16)
    col = lax.broadcasted_iota(jnp.int32, (1, 2 * blk), 1)
    first_mask = jnp.where(jnp.logical_and(i == 0, col < blk), NEG_INF, 0.0)
    for hk in range(nkv):
        vals = jnp.concatenate([kvp_ref[:, nk + hk * hd:nk + (hk + 1) * hd],
                                kvc_ref[:, nk + hk * hd:nk + (hk + 1) * hd]], axis=0)
        qs = jnp.concatenate([qn[:, (hk * g + gg) * hd:(hk * g + gg + 1) * hd] for gg in range(g)], axis=0)
        s = lax.dot_general(qs, kn[:, hk * hd:(hk + 1) * hd], (((1,), (1,)), ((), ())),
                            preferred_element_type=F32)
        s = s + bias_ref[hk * g * blk:(hk + 1) * g * blk, :] + first_mask
        sink = jnp.concatenate([jnp.full((blk, 1), sink_ref[hk * g + gg], F32) for gg in range(g)], axis=0)
        o = _softmax_sink_pv(s, sink, vals)
        for gg in range(g):
            o_ref[:, (hk * g + gg) * hd:(hk * g + gg + 1) * hd] = o[gg * blk:(gg + 1) * blk, :].astype(BF16)

    @pl.when(i == n_blocks - 1)
    def _():
        kn_ref[...] = kn_cur


def _swa_prompt_call(proj, q_gain, k_gain, sinks, rel_bias, *, n_batch, seq, hd):
    blk = WINDOW
    nb = seq // blk
    nq = ATT_Q_HEADS * hd
    nkv2 = 2 * ATT_KV_HEADS * hd
    kvcol = nq // nkv2
    i_idx = jnp.arange(blk, dtype=jnp.int32)[:, None]
    j_idx = jnp.arange(2 * blk, dtype=jnp.int32)[None, :]
    dist = blk + i_idx - j_idx
    bias = _bias_table(rel_bias, dist, (dist >= 0) & (dist < WINDOW))
    seg = (jnp.arange(nq, dtype=jnp.int32)[:, None] // hd == jnp.arange(128, dtype=jnp.int32)[None, :]).astype(BF16)
    gq_row = jnp.tile(q_gain * (hd ** -0.5), (1, ATT_Q_HEADS))
    gk_row = jnp.tile(k_gain, (1, ATT_KV_HEADS))
    body = functools.partial(_swa_prompt_body, hd=hd, n_blocks=nb)
    const = lambda b, i: (0, 0)
    return pl.pallas_call(
        body,
        out_shape=[jax.ShapeDtypeStruct((n_batch * seq, nq), BF16),
                   jax.ShapeDtypeStruct((n_batch * blk, ATT_KV_HEADS * hd), F32)],
        grid=(n_batch, nb),
        in_specs=[
            pl.BlockSpec(memory_space=pltpu.SMEM),
            pl.BlockSpec((blk, nq), lambda b, i: (b * nb + i, 0)),
            pl.BlockSpec((blk, nkv2), lambda b, i: (b * nb + jnp.maximum(i - 1, 0), kvcol)),
            pl.BlockSpec((blk, nkv2), lambda b, i: (b * nb + i, kvcol)),
            pl.BlockSpec((1, nq), const),
            pl.BlockSpec((1, ATT_KV_HEADS * hd), const),
            pl.BlockSpec((nq, 128), const),
            pl.BlockSpec((128, nq), const),
            pl.BlockSpec((ATT_Q_HEADS * blk, 2 * blk), const),
        ],
        out_specs=[pl.BlockSpec((blk, nq), lambda b, i: (b * nb + i, 0)),
                   pl.BlockSpec((blk, ATT_KV_HEADS * hd), lambda b, i: (b, 0))],
        compiler_params=_cparams(("arbitrary", "arbitrary")),
        name="swa_prompt",
    )(sinks, proj, proj, proj, gq_row, gk_row, seg, seg.T, bias)


def _swa_sample_body(sink_ref, p_ref, ck_ref, cv_ref, gq_ref, gk_ref, bias_ref, o_ref, kn_ref, *, hd):
    s = SAMPLE_ROWS
    g = ATT_GROUP
    nkv = ATT_KV_HEADS
    nq = ATT_Q_HEADS * hd
    gq = gq_ref[...]
    gk = gk_ref[...]

    def rms(x, gain):
        return x * lax.rsqrt(jnp.mean(x * x, axis=-1, keepdims=True) + NORM_EPS) * gain

    for hk in range(nkv):
        kn = rms(p_ref[:, nq + hk * hd:nq + (hk + 1) * hd].astype(F32), gk)
        kn_ref[:, hk * hd:(hk + 1) * hd] = kn
        keys = jnp.concatenate([ck_ref[:, hk * hd:(hk + 1) * hd], kn], axis=0).astype(BF16)
        vnew = p_ref[:, nq + (nkv + hk) * hd:nq + (nkv + hk + 1) * hd].astype(F32)
        vals = jnp.concatenate([cv_ref[:, hk * hd:(hk + 1) * hd], vnew], axis=0).astype(BF16)
        qs = jnp.concatenate([rms(p_ref[:, (hk * g + gg) * hd:(hk * g + gg + 1) * hd].astype(F32), gq)
                              for gg in range(g)], axis=0)
        sc = lax.dot_general(qs.astype(BF16), keys, (((1,), (1,)), ((), ())), preferred_element_type=F32)
        sc = sc * (hd ** -0.5) + bias_ref[hk * g * s:(hk + 1) * g * s, :]
        sink = jnp.concatenate([jnp.full((s, 1), sink_ref[hk * g + gg], F32) for gg in range(g)], axis=0)
        o = _softmax_sink_pv(sc, sink, vals)
        for gg in range(g):
            o_ref[:, (hk * g + gg) * hd:(hk * g + gg + 1) * hd] = o[gg * s:(gg + 1) * s, :].astype(BF16)


def _swa_sample_call(proj, cache_k, cache_v, q_gain, k_gain, sinks, rel_bias, *, n_prompt_rows, n_seq, dec_seq, hd):
    s = SAMPLE_ROWS
    nq = ATT_Q_HEADS * hd
    nkv = ATT_KV_HEADS * hd
    sc = cache_k.shape[1]
    i_idx = jnp.arange(s, dtype=jnp.int32)[:, None]
    j_idx = jnp.arange(sc + s, dtype=jnp.int32)[None, :]
    dist = sc + i_idx - j_idx
    valid = (dist >= 0) & (dist < WINDOW) & (j_idx < sc + dec_seq)
    bias = _bias_table(rel_bias, dist, valid)
    base = n_prompt_rows // s
    body = functools.partial(_swa_sample_body, hd=hd)
    return pl.pallas_call(
        body,
        out_shape=[jax.ShapeDtypeStruct((n_seq * s, nq), BF16),
                   jax.ShapeDtypeStruct((n_seq * s, nkv), F32)],
        grid=(n_seq,),
        in_specs=[
            pl.BlockSpec(memory_space=pltpu.SMEM),
            pl.BlockSpec((s, proj.shape[1]), lambda b: (base + b, 0)),
            pl.BlockSpec((None, sc, nkv), lambda b: (b, 0, 0)),
            pl.BlockSpec((None, sc, nkv), lambda b: (b, 0, 0)),
            pl.BlockSpec((1, hd), lambda b: (0, 0)),
            pl.BlockSpec((1, hd), lambda b: (0, 0)),
            pl.BlockSpec((ATT_Q_HEADS * s, sc + s), lambda b: (0, 0)),
        ],
        out_specs=[pl.BlockSpec((s, nq), lambda b: (b, 0)),
                   pl.BlockSpec((s, nkv), lambda b: (b, 0))],
        compiler_params=_cparams(("arbitrary",)),
        name="swa_sample",
    )(sinks, proj, cache_k, cache_v, q_gain, k_gain, bias)


MOE_TILE = 256


def _moe_body(src_ref, dst_ref, te_ref, nu_ref, h_hbm, win_ref, wout_ref, y_hbm, xg, yb, wib, wob, gsem, ssem,
              *, ff):
    t = pl.program_id(0)
    nt = pl.num_programs(0)
    n_used = nu_ref[0]
    tm = MOE_TILE
    slot = lax.rem(t, 2)

    def gather_start(tile, sl):
        base = tile * tm
        for r in range(tm):
            pltpu.make_async_copy(h_hbm.at[pl.ds(src_ref[base + r], 1), :], xg.at[sl, pl.ds(r, 1), :],
                                  gsem.at[sl]).start()

    def gather_wait(sl):
        pltpu.make_async_copy(h_hbm.at[pl.ds(0, tm), :], xg.at[sl], gsem.at[sl]).wait()

    def scatter_start(tile, sl):
        base = tile * tm
        for r in range(tm):
            pltpu.make_async_copy(yb.at[sl, pl.ds(r, 1), :], y_hbm.at[pl.ds(dst_ref[base + r], 1), :],
                                  ssem.at[sl]).start()

    def scatter_wait(sl):
        pltpu.make_async_copy(yb.at[sl], y_hbm.at[pl.ds(0, tm), :], ssem.at[sl]).wait()

    valid = t < n_used

    @pl.when(t == 0)
    def _():
        gather_start(0, 0)

    @pl.when(valid)
    def _():
        @pl.when(t >= 2)
        def _():
            scatter_wait(slot)

        gather_wait(slot)

        @pl.when(t + 1 < n_used)
        def _():
            gather_start(t + 1, 1 - slot)

        @pl.when(jnp.logical_or(t == 0, te_ref[t] != te_ref[jnp.maximum(t - 1, 0)]))
        def _():
            wib[...] = win_ref[...].astype(BF16)
            wob[...] = wout_ref[...].astype(BF16)

        gu = jnp.dot(xg[slot].astype(BF16), wib[...], preferred_element_type=F32)
        gate, up = gu[:, :ff], gu[:, ff:]
        act = (gate * jax.nn.sigmoid(gate) * up).astype(BF16)
        yb[slot] = jnp.dot(act, wob[...], preferred_element_type=F32)
        scatter_start(t, slot)

    @pl.when(t == nt - 1)
    def _():
        scatter_wait(lax.rem(n_used - 1, 2))
        scatter_wait(lax.rem(n_used, 2))


def _moe_layer(h2, route, counts, w_in, w_out, layer):
    r, d = h2.shape
    depth, n_exp, _, ff2 = w_in.shape
    ff = ff2 // 2
    tm = MOE_TILE
    n_tiles = 2 * r // tm + n_exp
    n_slots = n_tiles * tm

    cnt = counts[0, MOE_GROUPS:MOE_GROUPS + n_exp].astype(jnp.int32)
    padded = ((cnt + tm - 1) // tm) * tm
    ends = jnp.cumsum(padded)
    starts = ends - padded
    rt = route[:, :ROUTE_R2 + 1].astype(jnp.int32)
    pos = jnp.concatenate([starts[rt[:, ROUTE_E1]] + rt[:, ROUTE_R1], starts[rt[:, ROUTE_E2]] + rt[:, ROUTE_R2]])
    tok = jnp.arange(r, dtype=jnp.int32)
    dst = (2 * r + jnp.arange(n_slots, dtype=jnp.int32)).at[pos].set(
        jnp.concatenate([tok, r + tok]), unique_indices=True, indices_are_sorted=False)
    src = jnp.where(dst < r, dst, jnp.where(dst < 2 * r, dst - r, 0))
    tile_start = jnp.arange(n_tiles, dtype=jnp.int32) * tm
    tile_expert = jnp.minimum(jnp.sum(tile_start[:, None] >= ends[None, :], axis=1), n_exp - 1).astype(jnp.int32)
    n_used = (ends[-1:] // tm).astype(jnp.int32)

    return pl.pallas_call(
        functools.partial(_moe_body, ff=ff),
        out_shape=jax.ShapeDtypeStruct((2 * r + n_slots, d), F32),
        grid_spec=pltpu.PrefetchScalarGridSpec(
            num_scalar_prefetch=4,
            grid=(n_tiles,),
            in_specs=[
                pl.BlockSpec(memory_space=pl.ANY),
                pl.BlockSpec((None, None, d, ff2), lambda t, src, dst, te, nu: (layer, te[t], 0, 0)),
                pl.BlockSpec((None, None, ff, d), lambda t, src, dst, te, nu: (layer, te[t], 0, 0)),
            ],
            out_specs=pl.BlockSpec(memory_space=pl.ANY),
            scratch_shapes=[pltpu.VMEM((2, tm, d), F32), pltpu.VMEM((2, tm, d), F32),
                            pltpu.VMEM((d, ff2), BF16), pltpu.VMEM((ff, d), BF16),
                            pltpu.SemaphoreType.DMA((2,)), pltpu.SemaphoreType.DMA((2,))],
        ),
        compiler_params=_cparams(("arbitrary",)),
        name="moe_experts",
    )(src, dst, tile_expert, n_used, h2, w_in, w_out)


def kernel(x_prompt, x_sample, c_prompt, c_sample, state_ret, cache_conv, cache_swa_k, cache_swa_v, ada_w, ada_b, norm_gain, ret_w_in, ret_gn_gain, ret_gn_bias, ret_w_out, conv_w_pw1, conv_b_pw1, conv_w_dw, conv_b_dw, conv_ln_gain, conv_ln_bias, conv_w_pw2, conv_b_pw2, att_w_qkv, att_q_gain, att_k_gain, att_sinks, att_w_o, rel_bias, moe_wg, moe_bg, moe_we, moe_be, moe_w_in, moe_w_out):
    n_batch, seq, d = x_prompt.shape
    n_seq, dec_seq, _ = x_sample.shape
    depth = ada_w.shape[0]
    s = SAMPLE_ROWS
    assert n_seq * s == ROW_TILE and seq % ROW_TILE == 0 and dec_seq <= s
    n_prompt_rows = n_batch * seq
    seq_tiles = seq // ROW_TILE
    dk = ret_w_in.shape[2] // (6 * RET_HEADS)
    dv = 2 * dk
    hd = d // ATT_Q_HEADS
    geom = dict(seq_tiles=seq_tiles, n_batch=n_batch)

    xs_pad = jnp.pad(x_sample, ((0, 0), (0, s - dec_seq), (0, 0))).reshape(n_seq * s, d)
    x = jnp.concatenate([x_prompt.reshape(n_prompt_rows, d), xs_pad], axis=0)

    n_c = n_batch + n_seq
    c_rows = ((n_c + 7) // 8) * 8
    c_all = jnp.pad(jnp.concatenate([c_prompt, c_sample], axis=0), ((0, c_rows - n_c), (0, 0)))
    mod = _ada_call(c_all, ada_w, ada_b)
    mod4 = mod.reshape(depth, c_rows, 1, 6 * d)
    modtok = jnp.repeat(mod[:, n_batch:n_c], s, axis=1)

    gains = norm_gain.astype(F32).reshape(2 * depth, 1, d)
    gng_all = ret_gn_gain.astype(F32)[:, None, :]
    gnb_all = ret_gn_bias.astype(F32)[:, None, :]
    state_all = state_ret.astype(F32)

    ret_p, conv_p, conv_s, kp_l, vp_l, ks_l, vs_l = [], [], [], [], [], [], []
    ret_s = None
    (h,) = _norm_call(x, mod4, modtok, norm=(gains, 0, 0, 0, 1), **geom)
    for l in range(depth):
        kind, j = l % 3, l // 3
        resid1 = (x, mod4, modtok, l, 2)
        if kind == 0:
            proj = _linear_call(h, ret_w_in, j, n_out=ret_w_in.shape[2], tn=1024, out_dtype=BF16,
                                name="ret_in", **geom)
            a_p, st_p = _ret_prompt_call(proj, gng_all, gnb_all, j, n_batch=n_batch, seq=seq, dk=dk, dv=dv)
            a_s, ret_s = _ret_sample_call(proj, state_all, gng_all, gnb_all, j, ret_s, n_prompt_rows=n_prompt_rows,
                                          n_seq=n_seq, dec_seq=dec_seq, dk=dk, dv=dv)
            ret_p.append(st_p)
            x = _linear_call(a_p, ret_w_out, j, h_sample=a_s, n_out=d, tn=512, out_dtype=F32, resid=resid1,
                             name="ret_out", **geom)
        elif kind == 1:
            z = _linear_call(h, conv_w_pw1, j, n_out=d, tn=512, out_dtype=BF16, bias=conv_b_pw1,
                             glu=True, name="conv_pw1", **geom)
            cargs = (conv_w_dw[j], conv_b_dw[j][None], conv_ln_gain[j][None], conv_ln_bias[j][None])
            a_p = _conv_prompt_call(z, *cargs, n_batch=n_batch, seq=seq)
            a_s = _conv_sample_call(z, cache_conv[j].astype(F32), *cargs, n_prompt_rows=n_prompt_rows, n_seq=n_seq)
            z_tail = z[:n_prompt_rows].reshape(n_batch, seq, d)[:, seq - CONV_STATE:].astype(F32)
            conv_p.append(z_tail)
            z_new = z[n_prompt_rows:].reshape(n_seq, s, d)[:, :dec_seq].astype(F32)
            conv_s.append(jnp.concatenate([cache_conv[j].astype(F32), z_new], axis=1)[:, -CONV_STATE:])
            x = _linear_call(a_p, conv_w_pw2, j, h_sample=a_s, n_out=d, tn=1024, out_dtype=F32,
                             bias=conv_b_pw2, resid=resid1, name="conv_pw2", **geom)
        else:
            nkv = ATT_KV_HEADS * hd
            proj = _linear_call(h, att_w_qkv, j, n_out=att_w_qkv.shape[2], tn=512, out_dtype=BF16,
                                name="att_qkv", **geom)
            gq, gk = att_q_gain[j][None].astype(F32), att_k_gain[j][None].astype(F32)
            sinks = att_sinks[j].astype(F32)
            a_p, kn_p = _swa_prompt_call(proj, gq, gk, sinks, rel_bias, n_batch=n_batch, seq=seq, hd=hd)
            win = cache_swa_k.shape[2]
            ck = cache_swa_k[j].astype(F32).reshape(n_seq, win, nkv)
            cv = cache_swa_v[j].astype(F32).reshape(n_seq, win, nkv)
            a_s, kn_s = _swa_sample_call(proj, ck, cv, gq, gk, sinks, rel_bias, n_prompt_rows=n_prompt_rows,
                                         n_seq=n_seq, dec_seq=dec_seq, hd=hd)
            vcol = ATT_Q_HEADS * hd + nkv
            v_tail = proj[:n_prompt_rows, vcol:].reshape(n_batch, seq, nkv)[:, seq - WINDOW:].astype(F32)
            kp_l.append(kn_p.reshape(n_batch, WINDOW, ATT_KV_HEADS, hd))
            vp_l.append(v_tail.reshape(n_batch, WINDOW, ATT_KV_HEADS, hd))
            k_new = kn_s.reshape(n_seq, s, nkv)[:, :dec_seq]
            v_new = proj[n_prompt_rows:, vcol:].reshape(n_seq, s, nkv)[:, :dec_seq].astype(F32)
            ks_l.append(jnp.concatenate([ck, k_new], axis=1)[:, -win:].reshape(n_seq, win, ATT_KV_HEADS, hd))
            vs_l.append(jnp.concatenate([cv, v_new], axis=1)[:, -win:].reshape(n_seq, win, ATT_KV_HEADS, hd))
            x = _linear_call(a_p, att_w_o, j, h_sample=a_s, n_out=d, tn=1024, out_dtype=F32, resid=resid1,
                             name="att_out", **geom)

        lane_pad = ROUTER_LANES - MOE_GROUPS - MOE_GROUPS * MOE_EPG
        router_w = jnp.pad(jnp.concatenate([moe_wg[l], moe_we[l]], axis=1).astype(F32), ((0, 0), (0, lane_pad)))
        router_hi = router_w.astype(BF16)
        router_w = jnp.stack([router_hi, (router_w - router_hi.astype(F32)).astype(BF16)])
        router_b = jnp.pad(jnp.concatenate([moe_bg[l], moe_be[l]]).astype(F32), (0, lane_pad))[None]
        h2, route, counts = _norm_call(x, mod4, modtok, norm=(gains, 2 * l + 1, l, 3, 4),
                                       router=(router_w, router_b), **geom)
        ypair = _moe_layer(h2, route, counts, moe_w_in, moe_w_out, l)
        if l + 1 < depth:
            x, h = _norm_call(x, mod4, modtok, resid=(ypair, route, l, 5),
                              norm=(gains, 2 * l + 2, l + 1, 0, 1), **geom)
        else:
            x_p, x_s = _norm_call(x, mod4, modtok, resid=(ypair, route, l, 5), split_out=True, **geom)

    y_prompt = x_p.reshape(n_batch, seq, d)
    y_sample = x_s.reshape(n_seq, s, d)[:, :dec_seq]
    return (y_prompt, y_sample, jnp.stack(ret_p), ret_s, jnp.stack(conv_p), jnp.stack(conv_s),
            jnp.stack(kp_l), jnp.stack(vp_l), jnp.stack(ks_l), jnp.stack(vs_l))
```

```python
import functools
import math

import jax
import jax.numpy as jnp
from jax import lax
from jax.experimental import pallas as pl
from jax.experimental.pallas import tpu as pltpu

F32 = jnp.float32
BF16 = jnp.bfloat16

NORM_EPS = 1e-6
NEG_INF = -1e30
ROPE_BASE = 10000.0
PAST_LEN = 16384

RET_HEADS = 8
RET_CHUNK = 128
CONV_WIDTH = 31
CONV_STATE = CONV_WIDTH - 1
ATT_Q_HEADS = 32
ATT_KV_HEADS = 4
ATT_GROUP = ATT_Q_HEADS // ATT_KV_HEADS
WINDOW = 128
REL_BUCKETS = 32
REL_MAX_DIST = 128
MOE_GROUPS = 4
MOE_EPG = 4
MOE_TOPK = 2

ROW_TILE = 512
SAMPLE_ROWS = 16
ROUTER_LANES = 128
VMEM_LIMIT_BYTES = 56 * 1024 * 1024


def _cparams(sem):
    return pltpu.CompilerParams(dimension_semantics=sem, vmem_limit_bytes=VMEM_LIMIT_BYTES)


def _ada_body(c_ref, w_ref, b_ref, o_ref):
    c = c_ref[...]
    s = (c * jax.nn.sigmoid(c)).astype(BF16)
    o_ref[...] = jnp.dot(s, w_ref[...].astype(BF16), preferred_element_type=F32) + b_ref[...]


def _ada_call(c_all, ada_w, ada_b):
    depth, d, n = ada_w.shape
    rows = c_all.shape[0]
    tn = 1024
    return pl.pallas_call(
        _ada_body,
        out_shape=jax.ShapeDtypeStruct((depth, rows, n), F32),
        grid=(depth, n // tn),
        in_specs=[
            pl.BlockSpec((rows, d), lambda l, j: (0, 0)),
            pl.BlockSpec((None, d, tn), lambda l, j: (l, 0, j)),
            pl.BlockSpec((None, 1, tn), lambda l, j: (l, 0, j)),
        ],
        out_specs=pl.BlockSpec((None, rows, tn), lambda l, j: (l, 0, j)),
        compiler_params=_cparams(("arbitrary", "arbitrary")),
        name="ada_mod",
    )(c_all, ada_w, ada_b.reshape(depth, 1, n))


ROUTE_E1, ROUTE_E2, ROUTE_W1, ROUTE_W2, ROUTE_R1, ROUTE_R2 = range(6)


def _route_tile(logits, rb, carry):
    tm = logits.shape[0]
    g, epg = MOE_GROUPS, MOE_EPG
    lg = logits + rb
    lane = lax.broadcasted_iota(jnp.int32, lg.shape, 1)
    lane_f = lane.astype(F32)

    def first_lane(mask):
        return jnp.min(jnp.where(mask, lane_f, float(ROUTER_LANES)), axis=-1, keepdims=True).astype(jnp.int32)

    is_g = lane < g
    mg = jnp.max(jnp.where(is_g, lg, NEG_INF), axis=-1, keepdims=True)
    eg = jnp.where(is_g, jnp.exp(lg - mg), 0.0)
    p_grp = 1.0 / jnp.sum(eg, axis=-1, keepdims=True)
    grp = first_lane(jnp.logical_and(is_g, lg == mg))
    lo = g + epg * grp
    is_e = jnp.logical_and(lane >= lo, lane < lo + epg)
    me = jnp.max(jnp.where(is_e, lg, NEG_INF), axis=-1, keepdims=True)
    ee = jnp.where(is_e, jnp.exp(lg - me), 0.0)
    se = jnp.sum(ee, axis=-1, keepdims=True)
    i1 = first_lane(jnp.logical_and(is_e, lg == me))
    rest = jnp.logical_and(is_e, lane != i1)
    m2 = jnp.max(jnp.where(rest, ee, -1.0), axis=-1, keepdims=True)
    i2 = first_lane(jnp.logical_and(rest, ee == m2))
    p1 = 1.0 / se
    p2 = m2 / se
    w1 = p_grp * p1 / (p1 + p2)
    w2 = p_grp * p2 / (p1 + p2)
    sel1 = lane == i1
    sel2 = lane == i2
    onehot = jnp.where(jnp.logical_or(sel1, sel2), 1.0, 0.0)
    row = lax.broadcasted_iota(jnp.int32, (tm, tm), 0)
    col = lax.broadcasted_iota(jnp.int32, (tm, tm), 1)
    tri = jnp.where(col <= row, 1.0, 0.0).astype(BF16)
    incl = jnp.dot(tri, onehot.astype(BF16), preferred_element_type=F32)
    rank = incl - 1.0 + carry
    r1 = jnp.sum(jnp.where(sel1, rank, 0.0), axis=-1, keepdims=True)
    r2 = jnp.sum(jnp.where(sel2, rank, 0.0), axis=-1, keepdims=True)
    rec = jnp.zeros_like(lg)
    for ln, val in ((ROUTE_E1, (i1 - g).astype(F32)), (ROUTE_E2, (i2 - g).astype(F32)), (ROUTE_W1, w1),
                    (ROUTE_W2, w2), (ROUTE_R1, r1), (ROUTE_R2, r2)):
        rec = jnp.where(lane == ln, val, rec)
    return rec, carry + jnp.sum(onehot, axis=0, keepdims=True)


def _norm_body(*refs, n_prompt_tiles, has_resid, has_norm, has_router, split_out):
    it = iter(refs)
    x_ref = next(it)
    if has_resid:
        y0_ref, y1_ref, rt_ref, grow_ref, gtok_ref = next(it), next(it), next(it), next(it), next(it)
    if has_norm:
        gain_ref, shrow_ref, scrow_ref, shtok_ref, sctok_ref = next(it), next(it), next(it), next(it), next(it)
    if has_router:
        wr_ref, rb_ref = next(it), next(it)
    if has_resid:
        xo_ref = next(it)
        if split_out:
            xs_ref = next(it)
    if has_norm:
        h_ref = next(it)
    if has_router:
        rec_ref, cnt_ref, carry = next(it), next(it), next(it)

    is_sample = pl.program_id(0) >= n_prompt_tiles
    x = x_ref[...]
    if has_resid:
        gate = jnp.where(is_sample, gtok_ref[...], grow_ref[...])
        rt = rt_ref[...]
        y = rt[:, ROUTE_W1:ROUTE_W1 + 1] * y0_ref[...] + rt[:, ROUTE_W2:ROUTE_W2 + 1] * y1_ref[...]
        x = x + gate * y
        if split_out:
            @pl.when(jnp.logical_not(is_sample))
            def _():
                xo_ref[...] = x

            @pl.when(is_sample)
            def _():
                xs_ref[...] = x
        else:
            xo_ref[...] = x
    if has_norm:
        ms = jnp.mean(x * x, axis=-1, keepdims=True)
        xn = x * lax.rsqrt(ms + NORM_EPS) * gain_ref[...]
        scale = jnp.where(is_sample, sctok_ref[...], scrow_ref[...])
        shift = jnp.where(is_sample, shtok_ref[...], shrow_ref[...])
        h = xn * (1.0 + scale) + shift
        h_ref[...] = h.astype(h_ref.dtype)
        if has_router:
            @pl.when(pl.program_id(0) == 0)
            def _():
                carry[...] = jnp.zeros_like(carry)

            h_hi = h.astype(BF16)
            h_lo = (h - h_hi.astype(F32)).astype(BF16)
            logits = (jnp.dot(h_hi, wr_ref[0], preferred_element_type=F32)
                      + jnp.dot(h_lo, wr_ref[0], preferred_element_type=F32)
                      + jnp.dot(h_hi, wr_ref[1], preferred_element_type=F32))
            rec, new_carry = _route_tile(logits, rb_ref[...], carry[0:1, :])
            rec_ref[...] = rec
            carry[...] = jnp.broadcast_to(new_carry, carry.shape)
            cnt_ref[...] = carry[...]


def _norm_call(x, mod4, modtok, *, seq_tiles, n_batch, resid=None, norm=None, router=None, split_out=False):
    r, d = x.shape
    n_tiles = r // ROW_TILE
    n_prompt_tiles = n_tiles - 1

    def row_spec(layer, col):
        return pl.BlockSpec((None, None, 1, d),
                            lambda i: (layer, jnp.minimum(i // seq_tiles, n_batch - 1), 0, col))

    def tok_spec(layer, col):
        return pl.BlockSpec((None, ROW_TILE, d), lambda i: (layer, 0, col))

    tile = pl.BlockSpec((ROW_TILE, d), lambda i: (i, 0))
    args, in_specs, out_shape, out_specs = [x], [tile], [], []
    scratch = []
    if resid is not None:
        ypair, route, layer, gcol = resid
        args += [ypair, ypair, route, mod4, modtok]
        in_specs += [tile, pl.BlockSpec((ROW_TILE, d), lambda i: (n_tiles + i, 0)),
                     pl.BlockSpec((ROW_TILE, ROUTER_LANES), lambda i: (i, 0)),
                     row_spec(layer, gcol), tok_spec(layer, gcol)]
        if split_out:
            out_shape += [jax.ShapeDtypeStruct((r - ROW_TILE, d), F32), jax.ShapeDtypeStruct((ROW_TILE, d), F32)]
            out_specs += [pl.BlockSpec((ROW_TILE, d), lambda i: (jnp.minimum(i, n_prompt_tiles - 1), 0)),
                          pl.BlockSpec((ROW_TILE, d), lambda i: (0, 0))]
        else:
            out_shape.append(jax.ShapeDtypeStruct((r, d), F32))
            out_specs.append(tile)
    if norm is not None:
        gains, gidx, layer, shcol, sccol = norm
        args += [gains, mod4, mod4, modtok, modtok]
        in_specs += [pl.BlockSpec((None, 1, d), lambda i: (gidx, 0, 0)), row_spec(layer, shcol),
                     row_spec(layer, sccol), tok_spec(layer, shcol), tok_spec(layer, sccol)]
        out_shape.append(jax.ShapeDtypeStruct((r, d), F32 if router is not None else BF16))
        out_specs.append(tile)
        if router is not None:
            args += list(router)
            in_specs += [pl.BlockSpec((2, d, ROUTER_LANES), lambda i: (0, 0, 0)),
                         pl.BlockSpec((1, ROUTER_LANES), lambda i: (0, 0))]
            out_shape += [jax.ShapeDtypeStruct((r, ROUTER_LANES), F32), jax.ShapeDtypeStruct((8, ROUTER_LANES), F32)]
            out_specs += [pl.BlockSpec((ROW_TILE, ROUTER_LANES), lambda i: (i, 0)),
                          pl.BlockSpec((8, ROUTER_LANES), lambda i: (0, 0))]
            scratch.append(pltpu.VMEM((8, ROUTER_LANES), F32))
    body = functools.partial(_norm_body, n_prompt_tiles=n_prompt_tiles, has_resid=resid is not None,
                             has_norm=norm is not None, has_router=router is not None, split_out=split_out)
    return pl.pallas_call(
        body, out_shape=out_shape, grid=(n_tiles,), in_specs=in_specs, out_specs=out_specs,
        scratch_shapes=scratch, compiler_params=_cparams(("arbitrary",)), name="mod_norm",
    )(*args)


def _linear_body(*refs, n_prompt_tiles, pair, glu, has_bias, has_resid):
    it = iter(refs)
    h_ref = next(it)
    hs_ref = next(it) if pair else None
    w_ref = next(it)
    w2_ref = next(it) if glu else None
    b_ref = next(it) if has_bias else None
    b2_ref = next(it) if glu else None
    if has_resid:
        x_ref, grow_ref, gtok_ref = next(it), next(it), next(it)
    o_ref = next(it)
    wb = next(it)
    wb2 = next(it) if glu else None

    i = pl.program_id(1)
    is_sample = i >= n_prompt_tiles

    @pl.when(i == 0)
    def _():
        wb[...] = w_ref[...].astype(BF16)
        if glu:
            wb2[...] = w2_ref[...].astype(BF16)

    def compute(hv):
        acc = jnp.dot(hv, wb[...], preferred_element_type=F32)
        if has_bias:
            acc = acc + b_ref[...]
        if glu:
            acc2 = jnp.dot(hv, wb2[...], preferred_element_type=F32) + b2_ref[...]
            acc = acc * jax.nn.sigmoid(acc2)
        if has_resid:
            gate = jnp.where(is_sample, gtok_ref[...], grow_ref[...])
            acc = x_ref[...] + gate * acc
        o_ref[...] = acc.astype(o_ref.dtype)

    if pair:
        @pl.when(jnp.logical_not(is_sample))
        def _():
            compute(h_ref[...])

        @pl.when(is_sample)
        def _():
            compute(hs_ref[...])
    else:
        compute(h_ref[...])


def _linear_call(h, w, wl, *, n_out, tn, out_dtype, seq_tiles, n_batch, h_sample=None, bias=None, glu=False,
                 resid=None, name="linear"):
    k = w.shape[1]
    pair = h_sample is not None
    n_prompt_tiles = h.shape[0] // ROW_TILE - (0 if pair else 1)
    n_tiles = n_prompt_tiles + 1
    r = n_tiles * ROW_TILE
    nblk = n_out // tn

    args = [h]
    in_specs = [pl.BlockSpec((ROW_TILE, k), lambda j, i: (jnp.minimum(i, n_prompt_tiles - 1) if pair else i, 0))]
    if pair:
        args.append(h_sample)
        in_specs.append(pl.BlockSpec((ROW_TILE, k), lambda j, i: (0, 0)))
    args.append(w)
    in_specs.append(pl.BlockSpec((None, k, tn), lambda j, i: (wl, 0, j)))
    if glu:
        args.append(w)
        in_specs.append(pl.BlockSpec((None, k, tn), lambda j, i: (wl, 0, nblk + j)))
    if bias is not None:
        bias = bias.reshape(bias.shape[0], 1, bias.shape[1])
        args.append(bias)
        in_specs.append(pl.BlockSpec((None, 1, tn), lambda j, i: (wl, 0, j)))
        if glu:
            args.append(bias)
            in_specs.append(pl.BlockSpec((None, 1, tn), lambda j, i: (wl, 0, nblk + j)))
    if resid is not None:
        x, mod4, modtok, layer, gcol = resid
        cb = gcol * nblk
        args += [x, mod4, modtok]
        in_specs += [
            pl.BlockSpec((ROW_TILE, tn), lambda j, i: (i, j)),
            pl.BlockSpec((None, None, 1, tn),
                         lambda j, i: (layer, jnp.minimum(i // seq_tiles, n_batch - 1), 0, cb + j)),
            pl.BlockSpec((None, ROW_TILE, tn), lambda j, i: (layer, 0, cb + j)),
        ]
    scratch = [pltpu.VMEM((k, tn), BF16)] + ([pltpu.VMEM((k, tn), BF16)] if glu else [])
    body = functools.partial(_linear_body, n_prompt_tiles=n_prompt_tiles, pair=pair, glu=glu,
                             has_bias=bias is not None, has_resid=resid is not None)
    return pl.pallas_call(
        body,
        out_shape=jax.ShapeDtypeStruct((r, n_out), out_dtype),
        grid=(nblk, n_tiles),
        in_specs=in_specs,
        out_specs=pl.BlockSpec((ROW_TILE, tn), lambda j, i: (i, j)),
        scratch_shapes=scratch,
        compiler_params=_cparams(("arbitrary", "arbitrary")),
        name=name,
    )(*args)


def _rotate(x, cos, sin):
    half = x.shape[-1] // 2
    x1, x2 = x[:, :half], x[:, half:]
    return jnp.concatenate([x1 * cos - x2 * sin, x1 * sin + x2 * cos], axis=-1)


def _group_norm_gate(o, g, gain, bias):
    mu = jnp.mean(o, axis=-1, keepdims=True)
    var = jnp.mean(jnp.square(o - mu), axis=-1, keepdims=True)
    on = (o - mu) * lax.rsqrt(var + NORM_EPS) * gain + bias
    g = g.astype(F32)
    return (g * jax.nn.sigmoid(g) * on).astype(BF16)


def _ret_prompt_body(q_ref, k_ref, v_ref, g_ref, cos_ref, sin_ref, dm_ref, qd_ref, kd_ref, gl_ref,
                     gng_ref, gnb_ref, o_ref, st_ref, s_acc, *, n_chunks):
    s_acc[...] = jnp.zeros_like(s_acc)
    dmask = dm_ref[...]
    qdec = qd_ref[...]
    kdec = kd_ref[...]
    gl = gl_ref[0:1, 0:1]
    gng = gng_ref[...]
    gnb = gnb_ref[...]

    def chunk(c, carry):
        r0 = pl.multiple_of(c * RET_CHUNK, RET_CHUNK)
        rows = pl.ds(r0, RET_CHUNK)
        cos = cos_ref[rows, :]
        sin = sin_ref[rows, :]
        qr = _rotate(q_ref[rows, :].astype(F32), cos, sin)
        kr = _rotate(k_ref[rows, :].astype(F32), cos, sin)
        v = v_ref[rows, :]
        state = s_acc[...]
        scores = lax.dot_general(qr.astype(BF16), kr.astype(BF16), (((1,), (1,)), ((), ())),
                                 preferred_element_type=F32)
        scores = scores * dmask
        out = jnp.dot(scores.astype(BF16), v, preferred_element_type=F32)
        out = out + jnp.dot((qr * qdec).astype(BF16), state.astype(BF16), preferred_element_type=F32)
        kv = lax.dot_general((kr * kdec).astype(BF16), v, (((0,), (0,)), ((), ())), preferred_element_type=F32)
        s_acc[...] = gl * state + kv
        o_ref[rows, :] = _group_norm_gate(out, g_ref[rows, :], gng, gnb)
        return carry

    lax.fori_loop(0, n_chunks, chunk, 0, unroll=4)
    st_ref[...] = s_acc[...]


def _ret_tables(chunk, dk, n_valid=None):
    n_valid = chunk if n_valid is None else n_valid
    lg = jnp.log1p(-jnp.exp2(-5.0 - jnp.arange(RET_HEADS, dtype=F32)))
    idx = jnp.arange(chunk, dtype=F32)
    diff = idx[:, None] - idx[None, :]
    inside = (idx[:, None] < n_valid) & (idx[None, :] < n_valid)
    dmask = jnp.where((diff[None] >= 0) & inside[None],
                      jnp.exp(jnp.maximum(diff, 0.0)[None] * lg[:, None, None]), 0.0) * (dk ** -0.5)
    qdec = jnp.exp((idx[None, :] + 1.0) * lg[:, None])
    kdec = jnp.where(idx[None, :] < n_valid, jnp.exp((n_valid - 1.0 - idx)[None, :] * lg[:, None]), 0.0) * (dk ** -0.5)
    gl = jnp.exp(n_valid * lg)
    qdec = jnp.broadcast_to(qdec[:, :, None], (RET_HEADS, chunk, dk))
    kdec = jnp.broadcast_to(kdec[:, :, None], (RET_HEADS, chunk, dk))
    gl = jnp.broadcast_to(gl[:, None, None], (RET_HEADS, 8, 128))
    return dmask, qdec, kdec, gl


def _rope_tables(pos, half):
    inv = ROPE_BASE ** (-jnp.arange(half, dtype=F32) / half)
    ang = pos.astype(F32)[:, None] * inv[None, :]
    return jnp.cos(ang), jnp.sin(ang)


def _ret_prompt_call(proj, gn_gain, gn_bias, j, *, n_batch, seq, dk, dv):
    h = RET_HEADS
    n_chunks = seq // RET_CHUNK
    cos, sin = _rope_tables(jnp.arange(seq, dtype=jnp.int32), dk // 2)
    dmask, qdec, kdec, gl = _ret_tables(RET_CHUNK, dk)
    kcol, vcol, gcol = h, (2 * h * dk) // dv, (2 * h * dk) // dv + h
    body = functools.partial(_ret_prompt_body, n_chunks=n_chunks)
    return pl.pallas_call(
        body,
        out_shape=[jax.ShapeDtypeStruct((n_batch * seq, h * dv), BF16),
                   jax.ShapeDtypeStruct((n_batch, h, dk, dv), F32)],
        grid=(n_batch, h),
        in_specs=[
            pl.BlockSpec((seq, dk), lambda b, hh: (b, hh)),
            pl.BlockSpec((seq, dk), lambda b, hh: (b, kcol + hh)),
            pl.BlockSpec((seq, dv), lambda b, hh: (b, vcol + hh)),
            pl.BlockSpec((seq, dv), lambda b, hh: (b, gcol + hh)),
            pl.BlockSpec((seq, dk // 2), lambda b, hh: (0, 0)),
            pl.BlockSpec((seq, dk // 2), lambda b, hh: (0, 0)),
            pl.BlockSpec((None, RET_CHUNK, RET_CHUNK), lambda b, hh: (hh, 0, 0)),
            pl.BlockSpec((None, RET_CHUNK, dk), lambda b, hh: (hh, 0, 0)),
            pl.BlockSpec((None, RET_CHUNK, dk), lambda b, hh: (hh, 0, 0)),
            pl.BlockSpec((None, 8, 128), lambda b, hh: (hh, 0, 0)),
            pl.BlockSpec((None, 1, dv), lambda b, hh: (j, 0, hh)),
            pl.BlockSpec((None, 1, dv), lambda b, hh: (j, 0, hh)),
        ],
        out_specs=[pl.BlockSpec((seq, dv), lambda b, hh: (b, hh)),
                   pl.BlockSpec((None, None, dk, dv), lambda b, hh: (b, hh, 0, 0))],
        scratch_shapes=[pltpu.VMEM((dk, dv), F32)],
        compiler_params=_cparams(("arbitrary", "arbitrary")),
        name="retention_prompt",
    )(proj, proj, proj, proj, cos, sin, dmask, qdec, kdec, gl, gn_gain, gn_bias)


def _ret_sample_body(p_ref, st_ref, cos_ref, sin_ref, dm_ref, qd_ref, kd_ref, gl_ref, gng_ref, gnb_ref,
                     *rest, dk, dv):
    o_ref, so_ref = rest[-2], rest[-1]
    h = RET_HEADS
    cos = cos_ref[...]
    sin = sin_ref[...]
    pad = 128 - SAMPLE_ROWS
    for hh in range(h):
        q = p_ref[:, hh * dk:(hh + 1) * dk].astype(F32)
        k = p_ref[:, h * dk + hh * dk:h * dk + (hh + 1) * dk].astype(F32)
        v = p_ref[:, 2 * h * dk + hh * dv:2 * h * dk + (hh + 1) * dv]
        g = p_ref[:, 2 * h * dk + h * dv + hh * dv:2 * h * dk + h * dv + (hh + 1) * dv]
        qr = _rotate(q, cos, sin)
        kr = _rotate(k, cos, sin)
        k_pad = jnp.concatenate([kr.astype(BF16), jnp.zeros((pad, dk), BF16)], axis=0)
        kd_pad = jnp.concatenate([(kr * kd_ref[hh]).astype(BF16), jnp.zeros((pad, dk), BF16)], axis=0)
        v_pad = jnp.concatenate([v, jnp.zeros((pad, dv), BF16)], axis=0)
        state = st_ref[hh]
        scores = lax.dot_general(qr.astype(BF16), k_pad, (((1,), (1,)), ((), ())), preferred_element_type=F32)
        scores = scores * dm_ref[hh]
        out = jnp.dot(scores.astype(BF16), v_pad, preferred_element_type=F32)
        out = out + jnp.dot((qr * qd_ref[hh]).astype(BF16), state.astype(BF16), preferred_element_type=F32)
        kv = lax.dot_general(kd_pad, v_pad, (((0,), (0,)), ((), ())), preferred_element_type=F32)
        so_ref[hh] = gl_ref[hh, 0:1, 0:1] * state + kv
        o_ref[:, hh * dv:(hh + 1) * dv] = _group_norm_gate(out, g, gng_ref[:, hh * dv:(hh + 1) * dv],
                                                           gnb_ref[:, hh * dv:(hh + 1) * dv])


def _ret_sample_call(proj, state, gn_gain, gn_bias, j, new_state, *, n_prompt_rows, n_seq, dec_seq, dk, dv):
    h = RET_HEADS
    s = SAMPLE_ROWS
    cos, sin = _rope_tables(PAST_LEN + jnp.arange(s, dtype=jnp.int32), dk // 2)
    dmask, qdec, kdec, gl = _ret_tables(s, dk, n_valid=dec_seq)
    dmask = jnp.pad(dmask, ((0, 0), (0, 0), (0, 128 - s)))
    base = n_prompt_rows // s
    body = functools.partial(_ret_sample_body, dk=dk, dv=dv)
    width = proj.shape[1]
    args = [proj, state, cos, sin, dmask, qdec, kdec, gl, gn_gain, gn_bias]
    in_specs = [
        pl.BlockSpec((s, width), lambda b: (base + b, 0)),
        pl.BlockSpec((None, None, h, dk, dv), lambda b: (j, b, 0, 0, 0)),
        pl.BlockSpec((s, dk // 2), lambda b: (0, 0)),
        pl.BlockSpec((s, dk // 2), lambda b: (0, 0)),
        pl.BlockSpec((h, s, 128), lambda b: (0, 0, 0)),
        pl.BlockSpec((h, s, dk), lambda b: (0, 0, 0)),
        pl.BlockSpec((h, s, dk), lambda b: (0, 0, 0)),
        pl.BlockSpec((h, 8, 128), lambda b: (0, 0, 0)),
        pl.BlockSpec((None, 1, h * dv), lambda b: (j, 0, 0)),
        pl.BlockSpec((None, 1, h * dv), lambda b: (j, 0, 0)),
    ]
    aliases = {}
    if new_state is not None:
        aliases = {len(args): 1}
        args.append(new_state)
        in_specs.append(pl.BlockSpec(memory_space=pl.ANY))
    return pl.pallas_call(
        body,
        out_shape=[jax.ShapeDtypeStruct((n_seq * s, h * dv), BF16),
                   jax.ShapeDtypeStruct(state.shape, F32)],
        grid=(n_seq,),
        in_specs=in_specs,
        out_specs=[pl.BlockSpec((s, h * dv), lambda b: (b, 0)),
                   pl.BlockSpec((None, None, h, dk, dv), lambda b: (j, b, 0, 0, 0))],
        input_output_aliases=aliases,
        compiler_params=_cparams(("arbitrary",)),
        name="retention_sample",
    )(*args)


CONV_HALO = 32
CONV_ROW_CHUNK = 64
CONV_LANES = 128


def _layer_norm_swish(u, gain, bias):
    mu = jnp.mean(u, axis=-1, keepdims=True)
    var = jnp.mean(jnp.square(u - mu), axis=-1, keepdims=True)
    un = (u - mu) * lax.rsqrt(var + NORM_EPS) * gain + bias
    return (un * jax.nn.sigmoid(un)).astype(BF16)


def _conv_prompt_body(z_ref, w_ref, bdw_ref, lng_ref, lnb_ref, o_ref, zbuf, ubuf, *, tt, d):
    t = pl.program_id(1)

    @pl.when(t == 0)
    def _():
        zbuf[0:CONV_HALO, :] = jnp.zeros((CONV_HALO, d), F32)

    @pl.when(t > 0)
    def _():
        zbuf[0:CONV_HALO, :] = zbuf[tt:tt + CONV_HALO, :]

    zbuf[CONV_HALO:CONV_HALO + tt, :] = z_ref[...].astype(F32)

    span = CONV_ROW_CHUNK + CONV_HALO
    n_row_chunks = tt // CONV_ROW_CHUNK
    n_strips = d // CONV_LANES

    def strip(n, carry):
        r0 = pl.multiple_of((n % n_row_chunks) * CONV_ROW_CHUNK, CONV_ROW_CHUNK)
        c0 = pl.multiple_of((n // n_row_chunks) * CONV_LANES, CONV_LANES)
        cols = pl.ds(c0, CONV_LANES)
        blk = zbuf[pl.ds(r0, span), cols]
        acc = jnp.zeros((CONV_ROW_CHUNK, CONV_LANES), F32)
        for b in range(8):
            rb = blk if b == 0 else pltpu.roll(blk, span - b, axis=0)
            for a in range(5):
                o = 8 * a + b
                if 2 <= o <= CONV_HALO:
                    acc = acc + rb[8 * a:8 * a + CONV_ROW_CHUNK, :] * w_ref[pl.ds(o - 2, 1), cols]
        ubuf[pl.ds(r0, CONV_ROW_CHUNK), cols] = acc
        return carry

    lax.fori_loop(0, n_row_chunks * n_strips, strip, 0)

    ln_rows = 128

    def ln_chunk(c, carry):
        rows = pl.ds(pl.multiple_of(c * ln_rows, ln_rows), ln_rows)
        o_ref[rows, :] = _layer_norm_swish(ubuf[rows, :] + bdw_ref[...], lng_ref[...], lnb_ref[...])
        return carry

    lax.fori_loop(0, tt // ln_rows, ln_chunk, 0)


def _conv_prompt_call(z, w_dw, b_dw, ln_gain, ln_bias, *, n_batch, seq):
    d = z.shape[1]
    tt = ROW_TILE
    nt = seq // tt
    w_pad = jnp.pad(w_dw, ((0, 32 - CONV_WIDTH), (0, 0)))
    body = functools.partial(_conv_prompt_body, tt=tt, d=d)
    vec = pl.BlockSpec((1, d), lambda b, t: (0, 0))
    return pl.pallas_call(
        body,
        out_shape=jax.ShapeDtypeStruct((n_batch * seq, d), BF16),
        grid=(n_batch, nt),
        in_specs=[pl.BlockSpec((tt, d), lambda b, t: (b * nt + t, 0)),
                  pl.BlockSpec((32, d), lambda b, t: (0, 0)), vec, vec, vec],
        out_specs=pl.BlockSpec((tt, d), lambda b, t: (b * nt + t, 0)),
        scratch_shapes=[pltpu.VMEM((tt + CONV_HALO, d), F32), pltpu.VMEM((tt, d), F32)],
        compiler_params=_cparams(("arbitrary", "arbitrary")),
        name="conv_prompt",
    )(z, w_pad, b_dw, ln_gain, ln_bias)


def _conv_sample_body(z_ref, c_ref, w_ref, bdw_ref, lng_ref, lnb_ref, o_ref, zbuf, *, d):
    s = SAMPLE_ROWS
    zbuf[0:CONV_STATE, :] = c_ref[...]
    zbuf[CONV_STATE:CONV_STATE + s, :] = z_ref[...].astype(F32)
    acc = jnp.zeros((s, d), F32)
    for j in range(CONV_WIDTH):
        acc = acc + zbuf[j:j + s, :] * w_ref[j:j + 1, :]
    o_ref[...] = _layer_norm_swish(acc + bdw_ref[...], lng_ref[...], lnb_ref[...])


def _conv_sample_call(z, cache, w_dw, b_dw, ln_gain, ln_bias, *, n_prompt_rows, n_seq):
    d = z.shape[1]
    s = SAMPLE_ROWS
    base = n_prompt_rows // s
    body = functools.partial(_conv_sample_body, d=d)
    vec = pl.BlockSpec((1, d), lambda b: (0, 0))
    return pl.pallas_call(
        body,
        out_shape=jax.ShapeDtypeStruct((n_seq * s, d), BF16),
        grid=(n_seq,),
        in_specs=[pl.BlockSpec((s, d), lambda b: (base + b, 0)),
                  pl.BlockSpec((None, CONV_STATE, d), lambda b: (b, 0, 0)),
                  pl.BlockSpec((CONV_WIDTH, d), lambda b: (0, 0)), vec, vec, vec],
        out_specs=pl.BlockSpec((s, d), lambda b: (b, 0)),
        scratch_shapes=[pltpu.VMEM((CONV_STATE + s, d), F32)],
        compiler_params=_cparams(("arbitrary",)),
        name="conv_sample",
    )(z, cache, w_dw, b_dw, ln_gain, ln_bias)


def _t5_bucket(dist):
    n = jnp.maximum(dist, 0)
    max_exact = REL_BUCKETS // 2
    nf = jnp.maximum(n, 1).astype(F32)
    large = max_exact + (jnp.log(nf / max_exact) / math.log(REL_MAX_DIST / max_exact)
                         * (REL_BUCKETS - max_exact)).astype(jnp.int32)
    large = jnp.minimum(large, REL_BUCKETS - 1)
    return jnp.where(n < max_exact, n, large)


def _bias_table(rel_bias, dist, valid):
    onehot = (_t5_bucket(dist).reshape(-1)[None, :] == jnp.arange(REL_BUCKETS, dtype=jnp.int32)[:, None]).astype(F32)
    tbl = jnp.dot(rel_bias.astype(F32).T, onehot, precision=lax.Precision.HIGHEST)
    tbl = jnp.where(valid.reshape(-1)[None, :], tbl, NEG_INF)
    return tbl.reshape(-1, dist.shape[-1])


def _softmax_sink_pv(s, sink, vals):
    m = jnp.maximum(jnp.max(s, axis=-1, keepdims=True), sink)
    p = jnp.exp(s - m).astype(BF16)
    den = jnp.dot(p, jnp.ones(vals.shape, BF16), preferred_element_type=F32) + jnp.exp(sink - m)
    o = jnp.dot(p, vals, preferred_element_type=F32)
    return o / den


def _head_rms_norm(x, gain_row, seg_ref, exp_ref, hd):
    n = x.shape[1]
    ssq = jnp.dot((x * x).astype(BF16), seg_ref[0:n, :], preferred_element_type=F32)
    r = lax.rsqrt(ssq * (1.0 / hd) + NORM_EPS)
    r_hi = r.astype(BF16)
    r_lo = (r - r_hi.astype(F32)).astype(BF16)
    e = exp_ref[:, 0:n]
    scale = jnp.dot(r_hi, e, preferred_element_type=F32) + jnp.dot(r_lo, e, preferred_element_type=F32)
    return x * scale * gain_row


def _swa_prompt_body(sink_ref, q_ref, kvp_ref, kvc_ref, gq_ref, gk_ref, seg_ref, exp_ref, bias_ref, o_ref, kn_ref,
                     *, hd, n_blocks):
    i = pl.program_id(1)
    blk = WINDOW
    g = ATT_GROUP
    nkv = ATT_KV_HEADS
    nk = nkv * hd
    qn = _head_rms_norm(q_ref[...].astype(F32), gq_ref[...], seg_ref, exp_ref, hd).astype(BF16)
    kn_prev = _head_rms_norm(kvp_ref[:, 0:nk].astype(F32), gk_ref[...], seg_ref, exp_ref, hd)
    kn_cur = _head_rms_norm(kvc_ref[:, 0:nk].astype(F32), gk_ref[...], seg_ref, exp_ref, hd)
    kn = jnp.concatenate([kn_prev, kn_cur], axis=0).astype(BF16)
    col = lax.broadcasted_iota(jnp.int32, (1, 2 * blk), 1)
    first_mask = jnp.where(jnp.logical_and(i == 0, col < blk), NEG_INF, 0.0)
    for hk in range(nkv):
        vals = jnp.concatenate([kvp_ref[:, nk + hk * hd:nk + (hk + 1) * hd],
                                kvc_ref[:, nk + hk * hd:nk + (hk + 1) * hd]], axis=0)
        qs = jnp.concatenate([qn[:, (hk * g + gg) * hd:(hk * g + gg + 1) * hd] for gg in range(g)], axis=0)
        s = lax.dot_general(qs, kn[:, hk * hd:(hk + 1) * hd], (((1,), (1,)), ((), ())),
                            preferred_element_type=F32)
        s = s + bias_ref[hk * g * blk:(hk + 1) * g * blk, :] + first_mask
        sink = jnp.concatenate([jnp.full((blk, 1), sink_ref[hk * g + gg], F32) for gg in range(g)], axis=0)
        o = _softmax_sink_pv(s, sink, vals)
        for gg in range(g):
            o_ref[:, (hk * g + gg) * hd:(hk * g + gg + 1) * hd] = o[gg * blk:(gg + 1) * blk, :].astype(BF16)

    @pl.when(i == n_blocks - 1)
    def _():
        kn_ref[...] = kn_cur


def _swa_prompt_call(proj, q_gain, k_gain, sinks, rel_bias, *, n_batch, seq, hd):
    blk = WINDOW
    nb = seq // blk
    nq = ATT_Q_HEADS * hd
    nkv2 = 2 * ATT_KV_HEADS * hd
    kvcol = nq // nkv2
    i_idx = jnp.arange(blk, dtype=jnp.int32)[:, None]
    j_idx = jnp.arange(2 * blk, dtype=jnp.int32)[None, :]
    dist = blk + i_idx - j_idx
    bias = _bias_table(rel_bias, dist, (dist >= 0) & (dist < WINDOW))
    seg = (jnp.arange(nq, dtype=jnp.int32)[:, None] // hd == jnp.arange(128, dtype=jnp.int32)[None, :]).astype(BF16)
    gq_row = jnp.tile(q_gain * (hd ** -0.5), (1, ATT_Q_HEADS))
    gk_row = jnp.tile(k_gain, (1, ATT_KV_HEADS))
    body = functools.partial(_swa_prompt_body, hd=hd, n_blocks=nb)
    const = lambda b, i: (0, 0)
    return pl.pallas_call(
        body,
        out_shape=[jax.ShapeDtypeStruct((n_batch * seq, nq), BF16),
                   jax.ShapeDtypeStruct((n_batch * blk, ATT_KV_HEADS * hd), F32)],
        grid=(n_batch, nb),
        in_specs=[
            pl.BlockSpec(memory_space=pltpu.SMEM),
            pl.BlockSpec((blk, nq), lambda b, i: (b * nb + i, 0)),
            pl.BlockSpec((blk, nkv2), lambda b, i: (b * nb + jnp.maximum(i - 1, 0), kvcol)),
            pl.BlockSpec((blk, nkv2), lambda b, i: (b * nb + i, kvcol)),
            pl.BlockSpec((1, nq), const),
            pl.BlockSpec((1, ATT_KV_HEADS * hd), const),
            pl.BlockSpec((nq, 128), const),
            pl.BlockSpec((128, nq), const),
            pl.BlockSpec((ATT_Q_HEADS * blk, 2 * blk), const),
        ],
        out_specs=[pl.BlockSpec((blk, nq), lambda b, i: (b * nb + i, 0)),
                   pl.BlockSpec((blk, ATT_KV_HEADS * hd), lambda b, i: (b, 0))],
        compiler_params=_cparams(("arbitrary", "arbitrary")),
        name="swa_prompt",
    )(sinks, proj, proj, proj, gq_row, gk_row, seg, seg.T, bias)


def _swa_sample_body(sink_ref, p_ref, ck_ref, cv_ref, gq_ref, gk_ref, bias_ref, o_ref, kn_ref, *, hd):
    s = SAMPLE_ROWS
    g = ATT_GROUP
    nkv = ATT_KV_HEADS
    nq = ATT_Q_HEADS * hd
    gq = gq_ref[...]
    gk = gk_ref[...]

    def rms(x, gain):
        return x * lax.rsqrt(jnp.mean(x * x, axis=-1, keepdims=True) + NORM_EPS) * gain

    for hk in range(nkv):
        kn = rms(p_ref[:, nq + hk * hd:nq + (hk + 1) * hd].astype(F32), gk)
        kn_ref[:, hk * hd:(hk + 1) * hd] = kn
        keys = jnp.concatenate([ck_ref[:, hk * hd:(hk + 1) * hd], kn], axis=0).astype(BF16)
        vnew = p_ref[:, nq + (nkv + hk) * hd:nq + (nkv + hk + 1) * hd].astype(F32)
        vals = jnp.concatenate([cv_ref[:, hk * hd:(hk + 1) * hd], vnew], axis=0).astype(BF16)
        qs = jnp.concatenate([rms(p_ref[:, (hk * g + gg) * hd:(hk * g + gg + 1) * hd].astype(F32), gq)
                              for gg in range(g)], axis=0)
        sc = lax.dot_general(qs.astype(BF16), keys, (((1,), (1,)), ((), ())), preferred_element_type=F32)
        sc = sc * (hd ** -0.5) + bias_ref[hk * g * s:(hk + 1) * g * s, :]
        sink = jnp.concatenate([jnp.full((s, 1), sink_ref[hk * g + gg], F32) for gg in range(g)], axis=0)
        o = _softmax_sink_pv(sc, sink, vals)
        for gg in range(g):
            o_ref[:, (hk * g + gg) * hd:(hk * g + gg + 1) * hd] = o[gg * s:(gg + 1) * s, :].astype(BF16)


def _swa_sample_call(proj, cache_k, cache_v, q_gain, k_gain, sinks, rel_bias, *, n_prompt_rows, n_seq, dec_seq, hd):
    s = SAMPLE_ROWS
    nq = ATT_Q_HEADS * hd
    nkv = ATT_KV_HEADS * hd
    sc = cache_k.shape[1]
    i_idx = jnp.arange(s, dtype=jnp.int32)[:, None]
    j_idx = jnp.arange(sc + s, dtype=jnp.int32)[None, :]
    dist = sc + i_idx - j_idx
    valid = (dist >= 0) & (dist < WINDOW) & (j_idx < sc + dec_seq)
    bias = _bias_table(rel_bias, dist, valid)
    base = n_prompt_rows // s
    body = functools.partial(_swa_sample_body, hd=hd)
    return pl.pallas_call(
        body,
        out_shape=[jax.ShapeDtypeStruct((n_seq * s, nq), BF16),
                   jax.ShapeDtypeStruct((n_seq * s, nkv), F32)],
        grid=(n_seq,),
        in_specs=[
            pl.BlockSpec(memory_space=pltpu.SMEM),
            pl.BlockSpec((s, proj.shape[1]), lambda b: (base + b, 0)),
            pl.BlockSpec((None, sc, nkv), lambda b: (b, 0, 0)),
            pl.BlockSpec((None, sc, nkv), lambda b: (b, 0, 0)),
            pl.BlockSpec((1, hd), lambda b: (0, 0)),
            pl.BlockSpec((1, hd), lambda b: (0, 0)),
            pl.BlockSpec((ATT_Q_HEADS * s, sc + s), lambda b: (0, 0)),
        ],
        out_specs=[pl.BlockSpec((s, nq), lambda b: (b, 0)),
                   pl.BlockSpec((s, nkv), lambda b: (b, 0))],
        compiler_params=_cparams(("arbitrary",)),
        name="swa_sample",
    )(sinks, proj, cache_k, cache_v, q_gain, k_gain, bias)


MOE_TILE = 256


def _moe_body(src_ref, dst_ref, te_ref, nu_ref, h_hbm, win_ref, wout_ref, y_hbm, xg, yb, wib, wob, gsem, ssem,
              *, ff):
    t = pl.program_id(0)
    nt = pl.num_programs(0)
    n_used = nu_ref[0]
    tm = MOE_TILE
    slot = lax.rem(t, 2)

    def gather_start(tile, sl):
        base = tile * tm
        for r in range(tm):
            pltpu.make_async_copy(h_hbm.at[pl.ds(src_ref[base + r], 1), :], xg.at[sl, pl.ds(r, 1), :],
                                  gsem.at[sl]).start()

    def gather_wait(sl):
        pltpu.make_async_copy(h_hbm.at[pl.ds(0, tm), :], xg.at[sl], gsem.at[sl]).wait()

    def scatter_start(tile, sl):
        base = tile * tm
        for r in range(tm):
            pltpu.make_async_copy(yb.at[sl, pl.ds(r, 1), :], y_hbm.at[pl.ds(dst_ref[base + r], 1), :],
                                  ssem.at[sl]).start()

    def scatter_wait(sl):
        pltpu.make_async_copy(yb.at[sl], y_hbm.at[pl.ds(0, tm), :], ssem.at[sl]).wait()

    valid = t < n_used

    @pl.when(t == 0)
    def _():
        gather_start(0, 0)

    @pl.when(jnp.logical_and(valid, jnp.logical_or(t == 0, te_ref[t] != te_ref[jnp.maximum(t - 1, 0)])))
    def _():
        wib[...] = win_ref[...].astype(BF16)
        wob[...] = wout_ref[...].astype(BF16)

    def step(sl):
        @pl.when(t >= 2)
        def _():
            scatter_wait(sl)

        gather_wait(sl)

        @pl.when(t + 1 < n_used)
        def _():
            gather_start(t + 1, 1 - sl)

        gu = jnp.dot(xg[sl].astype(BF16), wib[...], preferred_element_type=F32)
        gate, up = gu[:, :ff], gu[:, ff:]
        act = (gate * jax.nn.sigmoid(gate) * up).astype(BF16)
        yb[sl] = jnp.dot(act, wob[...], preferred_element_type=F32)
        scatter_start(t, sl)

    for sl in range(2):
        pl.when(jnp.logical_and(valid, slot == sl))(functools.partial(step, sl))

    @pl.when(t == nt - 1)
    def _():
        scatter_wait(lax.rem(n_used - 1, 2))
        scatter_wait(lax.rem(n_used, 2))


def _moe_layer(h2, route, counts, w_in, w_out, layer):
    r, d = h2.shape
    depth, n_exp, _, ff2 = w_in.shape
    ff = ff2 // 2
    tm = MOE_TILE
    n_tiles = 2 * r // tm + n_exp
    n_slots = n_tiles * tm

    cnt = counts[0, MOE_GROUPS:MOE_GROUPS + n_exp].astype(jnp.int32)
    padded = ((cnt + tm - 1) // tm) * tm
    ends = jnp.cumsum(padded)
    starts = ends - padded
    rt = route[:, :ROUTE_R2 + 1].astype(jnp.int32)
    pos = jnp.concatenate([starts[rt[:, ROUTE_E1]] + rt[:, ROUTE_R1], starts[rt[:, ROUTE_E2]] + rt[:, ROUTE_R2]])
    tok = jnp.arange(r, dtype=jnp.int32)
    dst = (2 * r + jnp.arange(n_slots, dtype=jnp.int32)).at[pos].set(
        jnp.concatenate([tok, r + tok]), unique_indices=True, indices_are_sorted=False)
    src = jnp.where(dst < r, dst, jnp.where(dst < 2 * r, dst - r, 0))
    tile_start = jnp.arange(n_tiles, dtype=jnp.int32) * tm
    tile_expert = jnp.minimum(jnp.sum(tile_start[:, None] >= ends[None, :], axis=1), n_exp - 1).astype(jnp.int32)
    n_used = (ends[-1:] // tm).astype(jnp.int32)

    return pl.pallas_call(
        functools.partial(_moe_body, ff=ff),
        out_shape=jax.ShapeDtypeStruct((2 * r + n_slots, d), F32),
        grid_spec=pltpu.PrefetchScalarGridSpec(
            num_scalar_prefetch=4,
            grid=(n_tiles,),
            in_specs=[
                pl.BlockSpec(memory_space=pl.ANY),
                pl.BlockSpec((None, None, d, ff2), lambda t, src, dst, te, nu: (layer, te[t], 0, 0)),
                pl.BlockSpec((None, None, ff, d), lambda t, src, dst, te, nu: (layer, te[t], 0, 0)),
            ],
            out_specs=pl.BlockSpec(memory_space=pl.ANY),
            scratch_shapes=[pltpu.VMEM((2, tm, d), F32), pltpu.VMEM((2, tm, d), F32),
                            pltpu.VMEM((d, ff2), BF16), pltpu.VMEM((ff, d), BF16),
                            pltpu.SemaphoreType.DMA((2,)), pltpu.SemaphoreType.DMA((2,))],
        ),
        compiler_params=_cparams(("arbitrary",)),
        name="moe_experts",
    )(src, dst, tile_expert, n_used, h2, w_in, w_out)


def kernel(x_prompt, x_sample, c_prompt, c_sample, state_ret, cache_conv, cache_swa_k, cache_swa_v, ada_w, ada_b, norm_gain, ret_w_in, ret_gn_gain, ret_gn_bias, ret_w_out, conv_w_pw1, conv_b_pw1, conv_w_dw, conv_b_dw, conv_ln_gain, conv_ln_bias, conv_w_pw2, conv_b_pw2, att_w_qkv, att_q_gain, att_k_gain, att_sinks, att_w_o, rel_bias, moe_wg, moe_bg, moe_we, moe_be, moe_w_in, moe_w_out):
    n_batch, seq, d = x_prompt.shape
    n_seq, dec_seq, _ = x_sample.shape
    depth = ada_w.shape[0]
    s = SAMPLE_ROWS
    assert n_seq * s == ROW_TILE and seq % ROW_TILE == 0 and dec_seq <= s
    n_prompt_rows = n_batch * seq
    seq_tiles = seq // ROW_TILE
    dk = ret_w_in.shape[2] // (6 * RET_HEADS)
    dv = 2 * dk
    hd = d // ATT_Q_HEADS
    geom = dict(seq_tiles=seq_tiles, n_batch=n_batch)

    xs_pad = jnp.pad(x_sample, ((0, 0), (0, s - dec_seq), (0, 0))).reshape(n_seq * s, d)
    x = jnp.concatenate([x_prompt.reshape(n_prompt_rows, d), xs_pad], axis=0)

    n_c = n_batch + n_seq
    c_rows = ((n_c + 7) // 8) * 8
    c_all = jnp.pad(jnp.concatenate([c_prompt, c_sample], axis=0), ((0, c_rows - n_c), (0, 0)))
    mod = _ada_call(c_all, ada_w, ada_b)
    mod4 = mod.reshape(depth, c_rows, 1, 6 * d)
    modtok = jnp.repeat(mod[:, n_batch:n_c], s, axis=1)

    gains = norm_gain.astype(F32).reshape(2 * depth, 1, d)
    gng_all = ret_gn_gain.astype(F32)[:, None, :]
    gnb_all = ret_gn_bias.astype(F32)[:, None, :]
    state_all = state_ret.astype(F32)

    ret_p, conv_p, conv_s, kp_l, vp_l, ks_l, vs_l = [], [], [], [], [], [], []
    ret_s = None
    (h,) = _norm_call(x, mod4, modtok, norm=(gains, 0, 0, 0, 1), **geom)
    for l in range(depth):
        kind, j = l % 3, l // 3
        resid1 = (x, mod4, modtok, l, 2)
        if kind == 0:
            proj = _linear_call(h, ret_w_in, j, n_out=ret_w_in.shape[2], tn=2048, out_dtype=BF16,
                                name="ret_in", **geom)
            a_p, st_p = _ret_prompt_call(proj, gng_all, gnb_all, j, n_batch=n_batch, seq=seq, dk=dk, dv=dv)
            a_s, ret_s = _ret_sample_call(proj, state_all, gng_all, gnb_all, j, ret_s, n_prompt_rows=n_prompt_rows,
                                          n_seq=n_seq, dec_seq=dec_seq, dk=dk, dv=dv)
            ret_p.append(st_p)
            x = _linear_call(a_p, ret_w_out, j, h_sample=a_s, n_out=d, tn=512, out_dtype=F32, resid=resid1,
                             name="ret_out", **geom)
        elif kind == 1:
            z = _linear_call(h, conv_w_pw1, j, n_out=d, tn=1024, out_dtype=BF16, bias=conv_b_pw1,
                             glu=True, name="conv_pw1", **geom)
            cargs = (conv_w_dw[j], conv_b_dw[j][None], conv_ln_gain[j][None], conv_ln_bias[j][None])
            a_p = _conv_prompt_call(z, *cargs, n_batch=n_batch, seq=seq)
            a_s = _conv_sample_call(z, cache_conv[j].astype(F32), *cargs, n_prompt_rows=n_prompt_rows, n_seq=n_seq)
            z_tail = z[:n_prompt_rows].reshape(n_batch, seq, d)[:, seq - CONV_STATE:].astype(F32)
            conv_p.append(z_tail)
            z_new = z[n_prompt_rows:].reshape(n_seq, s, d)[:, :dec_seq].astype(F32)
            conv_s.append(jnp.concatenate([cache_conv[j].astype(F32), z_new], axis=1)[:, -CONV_STATE:])
            x = _linear_call(a_p, conv_w_pw2, j, h_sample=a_s, n_out=d, tn=1024, out_dtype=F32,
                             bias=conv_b_pw2, resid=resid1, name="conv_pw2", **geom)
        else:
            nkv = ATT_KV_HEADS * hd
            proj = _linear_call(h, att_w_qkv, j, n_out=att_w_qkv.shape[2], tn=1280, out_dtype=BF16,
                                name="att_qkv", **geom)
            gq, gk = att_q_gain[j][None].astype(F32), att_k_gain[j][None].astype(F32)
            sinks = att_sinks[j].astype(F32)
            a_p, kn_p = _swa_prompt_call(proj, gq, gk, sinks, rel_bias, n_batch=n_batch, seq=seq, hd=hd)
            win = cache_swa_k.shape[2]
            ck = cache_swa_k[j].astype(F32).reshape(n_seq, win, nkv)
            cv = cache_swa_v[j].astype(F32).reshape(n_seq, win, nkv)
            a_s, kn_s = _swa_sample_call(proj, ck, cv, gq, gk, sinks, rel_bias, n_prompt_rows=n_prompt_rows,
                                         n_seq=n_seq, dec_seq=dec_seq, hd=hd)
            vcol = ATT_Q_HEADS * hd + nkv
            v_tail = proj[:n_prompt_rows, vcol:].reshape(n_batch, seq, nkv)[:, seq - WINDOW:].astype(F32)
            kp_l.append(kn_p.reshape(n_batch, WINDOW, ATT_KV_HEADS, hd))
            vp_l.append(v_tail.reshape(n_batch, WINDOW, ATT_KV_HEADS, hd))
            k_new = kn_s.reshape(n_seq, s, nkv)[:, :dec_seq]
            v_new = proj[n_prompt_rows:, vcol:].reshape(n_seq, s, nkv)[:, :dec_seq].astype(F32)
            ks_l.append(jnp.concatenate([ck, k_new], axis=1)[:, -win:].reshape(n_seq, win, ATT_KV_HEADS, hd))
            vs_l.append(jnp.concatenate([cv, v_new], axis=1)[:, -win:].reshape(n_seq, win, ATT_KV_HEADS, hd))
            x = _linear_call(a_p, att_w_o, j, h_sample=a_s, n_out=d, tn=1024, out_dtype=F32, resid=resid1,
                             name="att_out", **geom)

        lane_pad = ROUTER_LANES - MOE_GROUPS - MOE_GROUPS * MOE_EPG
        router_w = jnp.pad(jnp.concatenate([moe_wg[l], moe_we[l]], axis=1).astype(F32), ((0, 0), (0, lane_pad)))
        router_hi = router_w.astype(BF16)
        router_w = jnp.stack([router_hi, (router_w - router_hi.astype(F32)).astype(BF16)])
        router_b = jnp.pad(jnp.concatenate([moe_bg[l], moe_be[l]]).astype(F32), (0, lane_pad))[None]
        h2, route, counts = _norm_call(x, mod4, modtok, norm=(gains, 2 * l + 1, l, 3, 4),
                                       router=(router_w, router_b), **geom)
        ypair = _moe_layer(h2, route, counts, moe_w_in, moe_w_out, l)
        if l + 1 < depth:
            x, h = _norm_call(x, mod4, modtok, resid=(ypair, route, l, 5),
                              norm=(gains, 2 * l + 2, l + 1, 0, 1), **geom)
        else:
            x_p, x_s = _norm_call(x, mod4, modtok, resid=(ypair, route, l, 5), split_out=True, **geom)

    y_prompt = x_p.reshape(n_batch, seq, d)
    y_sample = x_s.reshape(n_seq, s, d)[:, :dec_seq]
    return (y_prompt, y_sample, jnp.stack(ret_p), ret_s, jnp.stack(conv_p), jnp.stack(conv_s),
            jnp.stack(kp_l), jnp.stack(vp_l), jnp.stack(ks_l), jnp.stack(vs_l))
```

```python
import functools
import math

import jax
import jax.numpy as jnp
from jax import lax
from jax.experimental import pallas as pl
from jax.experimental.pallas import tpu as pltpu

F32 = jnp.float32
BF16 = jnp.bfloat16

NORM_EPS = 1e-6
NEG_INF = -1e30
ROPE_BASE = 10000.0
PAST_LEN = 16384

RET_HEADS = 8
RET_CHUNK = 128
CONV_WIDTH = 31
CONV_STATE = CONV_WIDTH - 1
ATT_Q_HEADS = 32
ATT_KV_HEADS = 4
ATT_GROUP = ATT_Q_HEADS // ATT_KV_HEADS
WINDOW = 128
REL_BUCKETS = 32
REL_MAX_DIST = 128
MOE_GROUPS = 4
MOE_EPG = 4
MOE_TOPK = 2

ROW_TILE = 512
SAMPLE_ROWS = 16
ROUTER_LANES = 128
VMEM_LIMIT_BYTES = 56 * 1024 * 1024


def _cparams(sem):
    return pltpu.CompilerParams(dimension_semantics=sem, vmem_limit_bytes=VMEM_LIMIT_BYTES)


def _pack_bf16_pairs(y):
    half = y.shape[1] // 2
    hi = lax.bitcast_convert_type(y[:, :half].astype(BF16).astype(F32), jnp.uint32)
    lo = lax.bitcast_convert_type(y[:, half:].astype(BF16).astype(F32), jnp.uint32)
    return hi | lax.shift_right_logical(lo, jnp.uint32(16))


def _unpack_bf16_pairs(w):
    hi = lax.bitcast_convert_type(w & jnp.uint32(0xFFFF0000), F32)
    lo = lax.bitcast_convert_type(lax.shift_left(w, jnp.uint32(16)), F32)
    return jnp.concatenate([hi, lo], axis=1)


def _ada_body(c_ref, w_ref, b_ref, o_ref):
    c = c_ref[...]
    s = (c * jax.nn.sigmoid(c)).astype(BF16)
    o_ref[...] = jnp.dot(s, w_ref[...].astype(BF16), preferred_element_type=F32) + b_ref[...]


def _ada_call(c_all, ada_w, ada_b):
    depth, d, n = ada_w.shape
    rows = c_all.shape[0]
    tn = 1024
    return pl.pallas_call(
        _ada_body,
        out_shape=jax.ShapeDtypeStruct((depth, rows, n), F32),
        grid=(depth, n // tn),
        in_specs=[
            pl.BlockSpec((rows, d), lambda l, j: (0, 0)),
            pl.BlockSpec((None, d, tn), lambda l, j: (l, 0, j)),
            pl.BlockSpec((None, 1, tn), lambda l, j: (l, 0, j)),
        ],
        out_specs=pl.BlockSpec((None, rows, tn), lambda l, j: (l, 0, j)),
        compiler_params=_cparams(("arbitrary", "arbitrary")),
        name="ada_mod",
    )(c_all, ada_w, ada_b.reshape(depth, 1, n))


ROUTE_E1, ROUTE_E2, ROUTE_W1, ROUTE_W2, ROUTE_R1, ROUTE_R2 = range(6)


def _route_tile(logits, rb, carry):
    tm = logits.shape[0]
    g, epg = MOE_GROUPS, MOE_EPG
    lg = logits + rb
    lane = lax.broadcasted_iota(jnp.int32, lg.shape, 1)
    lane_f = lane.astype(F32)

    def first_lane(mask):
        return jnp.min(jnp.where(mask, lane_f, float(ROUTER_LANES)), axis=-1, keepdims=True).astype(jnp.int32)

    is_g = lane < g
    mg = jnp.max(jnp.where(is_g, lg, NEG_INF), axis=-1, keepdims=True)
    eg = jnp.where(is_g, jnp.exp(lg - mg), 0.0)
    p_grp = 1.0 / jnp.sum(eg, axis=-1, keepdims=True)
    grp = first_lane(jnp.logical_and(is_g, lg == mg))
    lo = g + epg * grp
    is_e = jnp.logical_and(lane >= lo, lane < lo + epg)
    me = jnp.max(jnp.where(is_e, lg, NEG_INF), axis=-1, keepdims=True)
    ee = jnp.where(is_e, jnp.exp(lg - me), 0.0)
    se = jnp.sum(ee, axis=-1, keepdims=True)
    i1 = first_lane(jnp.logical_and(is_e, lg == me))
    rest = jnp.logical_and(is_e, lane != i1)
    m2 = jnp.max(jnp.where(rest, ee, -1.0), axis=-1, keepdims=True)
    i2 = first_lane(jnp.logical_and(rest, ee == m2))
    p1 = 1.0 / se
    p2 = m2 / se
    w1 = p_grp * p1 / (p1 + p2)
    w2 = p_grp * p2 / (p1 + p2)
    sel1 = lane == i1
    sel2 = lane == i2
    onehot = jnp.where(jnp.logical_or(sel1, sel2), 1.0, 0.0)
    row = lax.broadcasted_iota(jnp.int32, (tm, tm), 0)
    col = lax.broadcasted_iota(jnp.int32, (tm, tm), 1)
    tri = jnp.where(col <= row, 1.0, 0.0).astype(BF16)
    incl = jnp.dot(tri, onehot.astype(BF16), preferred_element_type=F32)
    rank = incl - 1.0 + carry
    r1 = jnp.sum(jnp.where(sel1, rank, 0.0), axis=-1, keepdims=True)
    r2 = jnp.sum(jnp.where(sel2, rank, 0.0), axis=-1, keepdims=True)
    rec = jnp.zeros_like(lg)
    for ln, val in ((ROUTE_E1, (i1 - g).astype(F32)), (ROUTE_E2, (i2 - g).astype(F32)), (ROUTE_W1, w1),
                    (ROUTE_W2, w2), (ROUTE_R1, r1), (ROUTE_R2, r2)):
        rec = jnp.where(lane == ln, val, rec)
    return rec, carry + jnp.sum(onehot, axis=0, keepdims=True)


def _norm_body(*refs, n_prompt_tiles, has_resid, has_norm, has_router, split_out):
    it = iter(refs)
    x_ref = next(it)
    if has_resid:
        y0_ref, y1_ref, rt_ref, grow_ref, gtok_ref = next(it), next(it), next(it), next(it), next(it)
    if has_norm:
        gain_ref, shrow_ref, scrow_ref, shtok_ref, sctok_ref = next(it), next(it), next(it), next(it), next(it)
    if has_router:
        wr_ref, rb_ref = next(it), next(it)
    if has_resid:
        xo_ref = next(it)
        if split_out:
            xs_ref = next(it)
    if has_norm:
        h_ref = next(it)
    if has_router:
        rec_ref, cnt_ref, carry = next(it), next(it), next(it)

    is_sample = pl.program_id(0) >= n_prompt_tiles

    if has_router:
        @pl.when(pl.program_id(0) == 0)
        def _():
            carry[...] = jnp.zeros_like(carry)

    def tile(sample):
        x = x_ref[...]
        if has_resid:
            gate = gtok_ref[...] if sample else grow_ref[...]
            rt = rt_ref[...]
            y = (rt[:, ROUTE_W1:ROUTE_W1 + 1] * _unpack_bf16_pairs(y0_ref[...])
                 + rt[:, ROUTE_W2:ROUTE_W2 + 1] * _unpack_bf16_pairs(y1_ref[...]))
            x = x + gate * y
            if split_out and sample:
                xs_ref[...] = x
            else:
                xo_ref[...] = x
        if has_norm:
            scale = sctok_ref[...] if sample else scrow_ref[...]
            shift = shtok_ref[...] if sample else shrow_ref[...]
            ms = jnp.mean(x * x, axis=-1, keepdims=True)
            h = (x * lax.rsqrt(ms + NORM_EPS)) * (gain_ref[...] * (1.0 + scale)) + shift
            h_ref[...] = h.astype(h_ref.dtype)
            if has_router:
                h_hi = h.astype(BF16)
                h_lo = (h - h_hi.astype(F32)).astype(BF16)
                logits = (jnp.dot(h_hi, wr_ref[0], preferred_element_type=F32)
                          + jnp.dot(h_lo, wr_ref[0], preferred_element_type=F32)
                          + jnp.dot(h_hi, wr_ref[1], preferred_element_type=F32))
                rec, new_carry = _route_tile(logits, rb_ref[...], carry[0:1, :])
                rec_ref[...] = rec
                carry[...] = jnp.broadcast_to(new_carry, carry.shape)
                cnt_ref[...] = carry[...]

    pl.when(jnp.logical_not(is_sample))(functools.partial(tile, False))
    pl.when(is_sample)(functools.partial(tile, True))


def _norm_call(x, mod4, modtok, *, seq_tiles, n_batch, resid=None, norm=None, router=None, split_out=False):
    r, d = x.shape
    n_tiles = r // ROW_TILE
    n_prompt_tiles = n_tiles - 1

    def row_spec(layer, col):
        return pl.BlockSpec((None, None, 1, d),
                            lambda i: (layer, jnp.minimum(i // seq_tiles, n_batch - 1), 0, col))

    def tok_spec(layer, col):
        return pl.BlockSpec((None, ROW_TILE, d), lambda i: (layer, 0, col))

    tile = pl.BlockSpec((ROW_TILE, d), lambda i: (i, 0))
    args, in_specs, out_shape, out_specs = [x], [tile], [], []
    scratch = []
    if resid is not None:
        ypair, route, layer, gcol = resid
        args += [ypair, ypair, route, mod4, modtok]
        in_specs += [pl.BlockSpec((ROW_TILE, d // 2), lambda i: (i, 0)),
                     pl.BlockSpec((ROW_TILE, d // 2), lambda i: (n_tiles + i, 0)),
                     pl.BlockSpec((ROW_TILE, ROUTER_LANES), lambda i: (i, 0)),
                     row_spec(layer, gcol), tok_spec(layer, gcol)]
        if split_out:
            out_shape += [jax.ShapeDtypeStruct((r - ROW_TILE, d), F32), jax.ShapeDtypeStruct((ROW_TILE, d), F32)]
            out_specs += [pl.BlockSpec((ROW_TILE, d), lambda i: (jnp.minimum(i, n_prompt_tiles - 1), 0)),
                          pl.BlockSpec((ROW_TILE, d), lambda i: (0, 0))]
        else:
            out_shape.append(jax.ShapeDtypeStruct((r, d), F32))
            out_specs.append(tile)
    if norm is not None:
        gains, gidx, layer, shcol, sccol = norm
        args += [gains, mod4, mod4, modtok, modtok]
        in_specs += [pl.BlockSpec((None, 1, d), lambda i: (gidx, 0, 0)), row_spec(layer, shcol),
                     row_spec(layer, sccol), tok_spec(layer, shcol), tok_spec(layer, sccol)]
        out_shape.append(jax.ShapeDtypeStruct((r, d), F32 if router is not None else BF16))
        out_specs.append(tile)
        if router is not None:
            args += list(router)
            in_specs += [pl.BlockSpec((2, d, ROUTER_LANES), lambda i: (0, 0, 0)),
                         pl.BlockSpec((1, ROUTER_LANES), lambda i: (0, 0))]
            out_shape += [jax.ShapeDtypeStruct((r, ROUTER_LANES), F32), jax.ShapeDtypeStruct((8, ROUTER_LANES), F32)]
            out_specs += [pl.BlockSpec((ROW_TILE, ROUTER_LANES), lambda i: (i, 0)),
                          pl.BlockSpec((8, ROUTER_LANES), lambda i: (0, 0))]
            scratch.append(pltpu.VMEM((8, ROUTER_LANES), F32))
    body = functools.partial(_norm_body, n_prompt_tiles=n_prompt_tiles, has_resid=resid is not None,
                             has_norm=norm is not None, has_router=router is not None, split_out=split_out)
    return pl.pallas_call(
        body, out_shape=out_shape, grid=(n_tiles,), in_specs=in_specs, out_specs=out_specs,
        scratch_shapes=scratch, compiler_params=_cparams(("arbitrary",)), name="mod_norm",
    )(*args)


def _head_rms_norm(x, gain_row, seg_ref, exp_ref, hd):
    ssq = jnp.dot((x * x).astype(BF16), seg_ref[...], preferred_element_type=F32)
    r = lax.rsqrt(ssq * (1.0 / hd) + NORM_EPS)
    r_hi = r.astype(BF16)
    r_lo = (r - r_hi.astype(F32)).astype(BF16)
    e = exp_ref[...]
    scale = jnp.dot(r_hi, e, preferred_element_type=F32) + jnp.dot(r_lo, e, preferred_element_type=F32)
    return x * scale * gain_row


def _linear_body(*refs, n_prompt_tiles, pair, glu, has_bias, has_resid, head_dim):
    it = iter(refs)
    h_ref = next(it)
    hs_ref = next(it) if pair else None
    w_ref = next(it)
    w2_ref = next(it) if glu else None
    b_ref = next(it) if has_bias else None
    b2_ref = next(it) if glu else None
    if has_resid:
        x_ref, grow_ref, gtok_ref = next(it), next(it), next(it)
    if head_dim:
        hgain_ref, hmask_ref, seg_ref, exp_ref = next(it), next(it), next(it), next(it)
    o_ref = next(it)
    wb = next(it)
    wb2 = next(it) if glu else None

    i = pl.program_id(1)
    is_sample = i >= n_prompt_tiles

    @pl.when(i == 0)
    def _():
        wb[...] = w_ref[...].astype(BF16)
        if glu:
            wb2[...] = w2_ref[...].astype(BF16)

    def compute(hv):
        acc = jnp.dot(hv, wb[...], preferred_element_type=F32)
        if has_bias:
            acc = acc + b_ref[...]
        if glu:
            acc2 = jnp.dot(hv, wb2[...], preferred_element_type=F32) + b2_ref[...]
            acc = acc * jax.nn.sigmoid(acc2)
        if has_resid:
            gate = jnp.where(is_sample, gtok_ref[...], grow_ref[...])
            acc = x_ref[...] + gate * acc
        if head_dim:
            normed = _head_rms_norm(acc, hgain_ref[...], seg_ref, exp_ref, head_dim)
            acc = jnp.where(hmask_ref[...] > 0.0, normed, acc)
        o_ref[...] = acc.astype(o_ref.dtype)

    if pair:
        @pl.when(jnp.logical_not(is_sample))
        def _():
            compute(h_ref[...])

        @pl.when(is_sample)
        def _():
            compute(hs_ref[...])
    else:
        compute(h_ref[...])


def _linear_call(h, w, wl, *, n_out, tn, out_dtype, seq_tiles, n_batch, h_sample=None, bias=None, glu=False,
                 resid=None, head_norm=None, name="linear"):
    k = w.shape[1]
    pair = h_sample is not None
    n_prompt_tiles = h.shape[0] // ROW_TILE - (0 if pair else 1)
    n_tiles = n_prompt_tiles + 1
    r = n_tiles * ROW_TILE
    nblk = n_out // tn

    args = [h]
    in_specs = [pl.BlockSpec((ROW_TILE, k), lambda j, i: (jnp.minimum(i, n_prompt_tiles - 1) if pair else i, 0))]
    if pair:
        args.append(h_sample)
        in_specs.append(pl.BlockSpec((ROW_TILE, k), lambda j, i: (0, 0)))
    args.append(w)
    in_specs.append(pl.BlockSpec((None, k, tn), lambda j, i: (wl, 0, j)))
    if glu:
        args.append(w)
        in_specs.append(pl.BlockSpec((None, k, tn), lambda j, i: (wl, 0, nblk + j)))
    if bias is not None:
        bias = bias.reshape(bias.shape[0], 1, bias.shape[1])
        args.append(bias)
        in_specs.append(pl.BlockSpec((None, 1, tn), lambda j, i: (wl, 0, j)))
        if glu:
            args.append(bias)
            in_specs.append(pl.BlockSpec((None, 1, tn), lambda j, i: (wl, 0, nblk + j)))
    if resid is not None:
        x, mod4, modtok, layer, gcol = resid
        cb = gcol * nblk
        args += [x, mod4, modtok]
        in_specs += [
            pl.BlockSpec((ROW_TILE, tn), lambda j, i: (i, j)),
            pl.BlockSpec((None, None, 1, tn),
                         lambda j, i: (layer, jnp.minimum(i // seq_tiles, n_batch - 1), 0, cb + j)),
            pl.BlockSpec((None, ROW_TILE, tn), lambda j, i: (layer, 0, cb + j)),
        ]
    head_dim = 0
    if head_norm is not None:
        hgain, hmask, head_dim = head_norm
        seg = (jnp.arange(tn, dtype=jnp.int32)[:, None] // head_dim
               == jnp.arange(128, dtype=jnp.int32)[None, :]).astype(BF16)
        args += [hgain, hmask, seg, seg.T]
        in_specs += [pl.BlockSpec((1, tn), lambda j, i: (0, j)), pl.BlockSpec((1, tn), lambda j, i: (0, j)),
                     pl.BlockSpec((tn, 128), lambda j, i: (0, 0)), pl.BlockSpec((128, tn), lambda j, i: (0, 0))]
    scratch = [pltpu.VMEM((k, tn), BF16)] + ([pltpu.VMEM((k, tn), BF16)] if glu else [])
    body = functools.partial(_linear_body, n_prompt_tiles=n_prompt_tiles, pair=pair, glu=glu,
                             has_bias=bias is not None, has_resid=resid is not None, head_dim=head_dim)
    return pl.pallas_call(
        body,
        out_shape=jax.ShapeDtypeStruct((r, n_out), out_dtype),
        grid=(nblk, n_tiles),
        in_specs=in_specs,
        out_specs=pl.BlockSpec((ROW_TILE, tn), lambda j, i: (i, j)),
        scratch_shapes=scratch,
        compiler_params=_cparams(("arbitrary", "arbitrary")),
        name=name,
    )(*args)


def _rotate(x, cos, sin):
    half = x.shape[-1] // 2
    x1, x2 = x[:, :half], x[:, half:]
    return jnp.concatenate([x1 * cos - x2 * sin, x1 * sin + x2 * cos], axis=-1)


def _group_norm_gate(o, g, gain, bias):
    mu = jnp.mean(o, axis=-1, keepdims=True)
    var = jnp.mean(jnp.square(o - mu), axis=-1, keepdims=True)
    on = (o - mu) * lax.rsqrt(var + NORM_EPS) * gain + bias
    g = g.astype(F32)
    return (g * jax.nn.sigmoid(g) * on).astype(BF16)


def _ret_prompt_body(q_ref, k_ref, v_ref, g_ref, cos_ref, sin_ref, dm_ref, qd_ref, kd_ref, gl_ref,
                     gng_ref, gnb_ref, o_ref, st_ref, s_acc, *, n_chunks):
    s_acc[...] = jnp.zeros_like(s_acc)
    dmask = dm_ref[...]
    qdec = qd_ref[...]
    kdec = kd_ref[...]
    gl = gl_ref[0:1, 0:1]
    gng = gng_ref[...]
    gnb = gnb_ref[...]

    def chunk(c, carry):
        r0 = pl.multiple_of(c * RET_CHUNK, RET_CHUNK)
        rows = pl.ds(r0, RET_CHUNK)
        cos = cos_ref[rows, :]
        sin = sin_ref[rows, :]
        qr = _rotate(q_ref[rows, :].astype(F32), cos, sin)
        kr = _rotate(k_ref[rows, :].astype(F32), cos, sin)
        v = v_ref[rows, :]
        state = s_acc[...]
        scores = lax.dot_general(qr.astype(BF16), kr.astype(BF16), (((1,), (1,)), ((), ())),
                                 preferred_element_type=F32)
        scores = scores * dmask
        out = jnp.dot(scores.astype(BF16), v, preferred_element_type=F32)
        out = out + jnp.dot((qr * qdec).astype(BF16), state.astype(BF16), preferred_element_type=F32)
        kv = lax.dot_general((kr * kdec).astype(BF16), v, (((0,), (0,)), ((), ())), preferred_element_type=F32)
        s_acc[...] = gl * state + kv
        o_ref[rows, :] = _group_norm_gate(out, g_ref[rows, :], gng, gnb)
        return carry

    lax.fori_loop(0, n_chunks, chunk, 0, unroll=4)
    st_ref[...] = s_acc[...]


def _ret_tables(chunk, dk, n_valid=None):
    n_valid = chunk if n_valid is None else n_valid
    lg = jnp.log1p(-jnp.exp2(-5.0 - jnp.arange(RET_HEADS, dtype=F32)))
    idx = jnp.arange(chunk, dtype=F32)
    diff = idx[:, None] - idx[None, :]
    inside = (idx[:, None] < n_valid) & (idx[None, :] < n_valid)
    dmask = jnp.where((diff[None] >= 0) & inside[None],
                      jnp.exp(jnp.maximum(diff, 0.0)[None] * lg[:, None, None]), 0.0) * (dk ** -0.5)
    qdec = jnp.exp((idx[None, :] + 1.0) * lg[:, None])
    kdec = jnp.where(idx[None, :] < n_valid, jnp.exp((n_valid - 1.0 - idx)[None, :] * lg[:, None]), 0.0) * (dk ** -0.5)
    gl = jnp.exp(n_valid * lg)
    qdec = jnp.broadcast_to(qdec[:, :, None], (RET_HEADS, chunk, dk))
    kdec = jnp.broadcast_to(kdec[:, :, None], (RET_HEADS, chunk, dk))
    gl = jnp.broadcast_to(gl[:, None, None], (RET_HEADS, 8, 128))
    return dmask, qdec, kdec, gl


def _rope_tables(pos, half):
    inv = ROPE_BASE ** (-jnp.arange(half, dtype=F32) / half)
    ang = pos.astype(F32)[:, None] * inv[None, :]
    return jnp.cos(ang), jnp.sin(ang)


def _ret_prompt_call(proj, gn_gain, gn_bias, j, *, n_batch, seq, dk, dv):
    h = RET_HEADS
    n_chunks = seq // RET_CHUNK
    cos, sin = _rope_tables(jnp.arange(seq, dtype=jnp.int32), dk // 2)
    dmask, qdec, kdec, gl = _ret_tables(RET_CHUNK, dk)
    kcol, vcol, gcol = h, (2 * h * dk) // dv, (2 * h * dk) // dv + h
    body = functools.partial(_ret_prompt_body, n_chunks=n_chunks)
    return pl.pallas_call(
        body,
        out_shape=[jax.ShapeDtypeStruct((n_batch * seq, h * dv), BF16),
                   jax.ShapeDtypeStruct((n_batch, h, dk, dv), F32)],
        grid=(n_batch, h),
        in_specs=[
            pl.BlockSpec((seq, dk), lambda b, hh: (b, hh)),
            pl.BlockSpec((seq, dk), lambda b, hh: (b, kcol + hh)),
            pl.BlockSpec((seq, dv), lambda b, hh: (b, vcol + hh)),
            pl.BlockSpec((seq, dv), lambda b, hh: (b, gcol + hh)),
            pl.BlockSpec((seq, dk // 2), lambda b, hh: (0, 0)),
            pl.BlockSpec((seq, dk // 2), lambda b, hh: (0, 0)),
            pl.BlockSpec((None, RET_CHUNK, RET_CHUNK), lambda b, hh: (hh, 0, 0)),
            pl.BlockSpec((None, RET_CHUNK, dk), lambda b, hh: (hh, 0, 0)),
            pl.BlockSpec((None, RET_CHUNK, dk), lambda b, hh: (hh, 0, 0)),
            pl.BlockSpec((None, 8, 128), lambda b, hh: (hh, 0, 0)),
            pl.BlockSpec((None, 1, dv), lambda b, hh: (j, 0, hh)),
            pl.BlockSpec((None, 1, dv), lambda b, hh: (j, 0, hh)),
        ],
        out_specs=[pl.BlockSpec((seq, dv), lambda b, hh: (b, hh)),
                   pl.BlockSpec((None, None, dk, dv), lambda b, hh: (b, hh, 0, 0))],
        scratch_shapes=[pltpu.VMEM((dk, dv), F32)],
        compiler_params=_cparams(("arbitrary", "arbitrary")),
        name="retention_prompt",
    )(proj, proj, proj, proj, cos, sin, dmask, qdec, kdec, gl, gn_gain, gn_bias)


def _ret_sample_body(p_ref, st_ref, cos_ref, sin_ref, dm_ref, qd_ref, kd_ref, gl_ref, gng_ref, gnb_ref,
                     *rest, dk, dv):
    o_ref, so_ref = rest[-2], rest[-1]
    h = RET_HEADS
    cos = cos_ref[...]
    sin = sin_ref[...]
    pad = 128 - SAMPLE_ROWS
    for hh in range(h):
        q = p_ref[:, hh * dk:(hh + 1) * dk].astype(F32)
        k = p_ref[:, h * dk + hh * dk:h * dk + (hh + 1) * dk].astype(F32)
        v = p_ref[:, 2 * h * dk + hh * dv:2 * h * dk + (hh + 1) * dv]
        g = p_ref[:, 2 * h * dk + h * dv + hh * dv:2 * h * dk + h * dv + (hh + 1) * dv]
        qr = _rotate(q, cos, sin)
        kr = _rotate(k, cos, sin)
        k_pad = jnp.concatenate([kr.astype(BF16), jnp.zeros((pad, dk), BF16)], axis=0)
        kd_pad = jnp.concatenate([(kr * kd_ref[hh]).astype(BF16), jnp.zeros((pad, dk), BF16)], axis=0)
        v_pad = jnp.concatenate([v, jnp.zeros((pad, dv), BF16)], axis=0)
        state = st_ref[hh]
        scores = lax.dot_general(qr.astype(BF16), k_pad, (((1,), (1,)), ((), ())), preferred_element_type=F32)
        scores = scores * dm_ref[hh]
        out = jnp.dot(scores.astype(BF16), v_pad, preferred_element_type=F32)
        out = out + jnp.dot((qr * qd_ref[hh]).astype(BF16), state.astype(BF16), preferred_element_type=F32)
        kv = lax.dot_general(kd_pad, v_pad, (((0,), (0,)), ((), ())), preferred_element_type=F32)
        so_ref[hh] = gl_ref[hh, 0:1, 0:1] * state + kv
        o_ref[:, hh * dv:(hh + 1) * dv] = _group_norm_gate(out, g, gng_ref[:, hh * dv:(hh + 1) * dv],
                                                           gnb_ref[:, hh * dv:(hh + 1) * dv])


def _ret_sample_call(proj, state, gn_gain, gn_bias, j, new_state, *, n_prompt_rows, n_seq, dec_seq, dk, dv):
    h = RET_HEADS
    s = SAMPLE_ROWS
    cos, sin = _rope_tables(PAST_LEN + jnp.arange(s, dtype=jnp.int32), dk // 2)
    dmask, qdec, kdec, gl = _ret_tables(s, dk, n_valid=dec_seq)
    dmask = jnp.pad(dmask, ((0, 0), (0, 0), (0, 128 - s)))
    base = n_prompt_rows // s
    body = functools.partial(_ret_sample_body, dk=dk, dv=dv)
    width = proj.shape[1]
    args = [proj, state, cos, sin, dmask, qdec, kdec, gl, gn_gain, gn_bias]
    in_specs = [
        pl.BlockSpec((s, width), lambda b: (base + b, 0)),
        pl.BlockSpec((None, None, h, dk, dv), lambda b: (j, b, 0, 0, 0)),
        pl.BlockSpec((s, dk // 2), lambda b: (0, 0)),
        pl.BlockSpec((s, dk // 2), lambda b: (0, 0)),
        pl.BlockSpec((h, s, 128), lambda b: (0, 0, 0)),
        pl.BlockSpec((h, s, dk), lambda b: (0, 0, 0)),
        pl.BlockSpec((h, s, dk), lambda b: (0, 0, 0)),
        pl.BlockSpec((h, 8, 128), lambda b: (0, 0, 0)),
        pl.BlockSpec((None, 1, h * dv), lambda b: (j, 0, 0)),
        pl.BlockSpec((None, 1, h * dv), lambda b: (j, 0, 0)),
    ]
    aliases = {}
    if new_state is not None:
        aliases = {len(args): 1}
        args.append(new_state)
        in_specs.append(pl.BlockSpec(memory_space=pl.ANY))
    return pl.pallas_call(
        body,
        out_shape=[jax.ShapeDtypeStruct((n_seq * s, h * dv), BF16),
                   jax.ShapeDtypeStruct(state.shape, F32)],
        grid=(n_seq,),
        in_specs=in_specs,
        out_specs=[pl.BlockSpec((s, h * dv), lambda b: (b, 0)),
                   pl.BlockSpec((None, None, h, dk, dv), lambda b: (j, b, 0, 0, 0))],
        input_output_aliases=aliases,
        compiler_params=_cparams(("arbitrary",)),
        name="retention_sample",
    )(*args)


CONV_HALO = 32
CONV_ROW_CHUNK = 64
CONV_LANES = 128


def _layer_norm_swish(u, gain, bias):
    mu = jnp.mean(u, axis=-1, keepdims=True)
    var = jnp.mean(jnp.square(u - mu), axis=-1, keepdims=True)
    un = (u - mu) * lax.rsqrt(var + NORM_EPS) * gain + bias
    return (un * jax.nn.sigmoid(un)).astype(BF16)


def _conv_prompt_body(z_ref, w_ref, bdw_ref, lng_ref, lnb_ref, o_ref, zbuf, ubuf, *, tt, d):
    t = pl.program_id(1)

    @pl.when(t == 0)
    def _():
        zbuf[0:CONV_HALO, :] = jnp.zeros((CONV_HALO, d), F32)

    @pl.when(t > 0)
    def _():
        zbuf[0:CONV_HALO, :] = zbuf[tt:tt + CONV_HALO, :]

    zbuf[CONV_HALO:CONV_HALO + tt, :] = z_ref[...].astype(F32)

    span = CONV_ROW_CHUNK + CONV_HALO
    n_row_chunks = tt // CONV_ROW_CHUNK
    n_strips = d // CONV_LANES

    def strip(n, carry):
        r0 = pl.multiple_of((n % n_row_chunks) * CONV_ROW_CHUNK, CONV_ROW_CHUNK)
        c0 = pl.multiple_of((n // n_row_chunks) * CONV_LANES, CONV_LANES)
        cols = pl.ds(c0, CONV_LANES)
        blk = zbuf[pl.ds(r0, span), cols]
        acc = jnp.zeros((CONV_ROW_CHUNK, CONV_LANES), F32)
        for b in range(8):
            rb = blk if b == 0 else pltpu.roll(blk, span - b, axis=0)
            for a in range(5):
                o = 8 * a + b
                if 2 <= o <= CONV_HALO:
                    acc = acc + rb[8 * a:8 * a + CONV_ROW_CHUNK, :] * w_ref[pl.ds(o - 2, 1), cols]
        ubuf[pl.ds(r0, CONV_ROW_CHUNK), cols] = acc
        return carry

    lax.fori_loop(0, n_row_chunks * n_strips, strip, 0)

    ln_rows = 128

    def ln_chunk(c, carry):
        rows = pl.ds(pl.multiple_of(c * ln_rows, ln_rows), ln_rows)
        o_ref[rows, :] = _layer_norm_swish(ubuf[rows, :] + bdw_ref[...], lng_ref[...], lnb_ref[...])
        return carry

    lax.fori_loop(0, tt // ln_rows, ln_chunk, 0)


def _conv_prompt_call(z, w_dw, b_dw, ln_gain, ln_bias, *, n_batch, seq):
    d = z.shape[1]
    tt = ROW_TILE
    nt = seq // tt
    w_pad = jnp.pad(w_dw, ((0, 32 - CONV_WIDTH), (0, 0)))
    body = functools.partial(_conv_prompt_body, tt=tt, d=d)
    vec = pl.BlockSpec((1, d), lambda b, t: (0, 0))
    return pl.pallas_call(
        body,
        out_shape=jax.ShapeDtypeStruct((n_batch * seq, d), BF16),
        grid=(n_batch, nt),
        in_specs=[pl.BlockSpec((tt, d), lambda b, t: (b * nt + t, 0)),
                  pl.BlockSpec((32, d), lambda b, t: (0, 0)), vec, vec, vec],
        out_specs=pl.BlockSpec((tt, d), lambda b, t: (b * nt + t, 0)),
        scratch_shapes=[pltpu.VMEM((tt + CONV_HALO, d), F32), pltpu.VMEM((tt, d), F32)],
        compiler_params=_cparams(("arbitrary", "arbitrary")),
        name="conv_prompt",
    )(z, w_pad, b_dw, ln_gain, ln_bias)


def _conv_sample_body(z_ref, c_ref, w_ref, bdw_ref, lng_ref, lnb_ref, o_ref, zbuf, *, d):
    s = SAMPLE_ROWS
    zbuf[0:CONV_STATE, :] = c_ref[...]
    zbuf[CONV_STATE:CONV_STATE + s, :] = z_ref[...].astype(F32)
    acc = jnp.zeros((s, d), F32)
    for j in range(CONV_WIDTH):
        acc = acc + zbuf[j:j + s, :] * w_ref[j:j + 1, :]
    o_ref[...] = _layer_norm_swish(acc + bdw_ref[...], lng_ref[...], lnb_ref[...])


def _conv_sample_call(z, cache, w_dw, b_dw, ln_gain, ln_bias, *, n_prompt_rows, n_seq):
    d = z.shape[1]
    s = SAMPLE_ROWS
    base = n_prompt_rows // s
    body = functools.partial(_conv_sample_body, d=d)
    vec = pl.BlockSpec((1, d), lambda b: (0, 0))
    return pl.pallas_call(
        body,
        out_shape=jax.ShapeDtypeStruct((n_seq * s, d), BF16),
        grid=(n_seq,),
        in_specs=[pl.BlockSpec((s, d), lambda b: (base + b, 0)),
                  pl.BlockSpec((None, CONV_STATE, d), lambda b: (b, 0, 0)),
                  pl.BlockSpec((CONV_WIDTH, d), lambda b: (0, 0)), vec, vec, vec],
        out_specs=pl.BlockSpec((s, d), lambda b: (b, 0)),
        scratch_shapes=[pltpu.VMEM((CONV_STATE + s, d), F32)],
        compiler_params=_cparams(("arbitrary",)),
        name="conv_sample",
    )(z, cache, w_dw, b_dw, ln_gain, ln_bias)


def _t5_bucket(dist):
    n = jnp.maximum(dist, 0)
    max_exact = REL_BUCKETS // 2
    nf = jnp.maximum(n, 1).astype(F32)
    large = max_exact + (jnp.log(nf / max_exact) / math.log(REL_MAX_DIST / max_exact)
                         * (REL_BUCKETS - max_exact)).astype(jnp.int32)
    large = jnp.minimum(large, REL_BUCKETS - 1)
    return jnp.where(n < max_exact, n, large)


def _bias_table(rel_bias, dist, valid):
    onehot = (_t5_bucket(dist).reshape(-1)[None, :] == jnp.arange(REL_BUCKETS, dtype=jnp.int32)[:, None]).astype(F32)
    tbl = jnp.dot(rel_bias.astype(F32).T, onehot, precision=lax.Precision.HIGHEST)
    tbl = jnp.where(valid.reshape(-1)[None, :], tbl, NEG_INF)
    return tbl.reshape(-1, dist.shape[-1])


def _softmax_sink_pv(s, sink, vals):
    m = jnp.maximum(jnp.max(s, axis=-1, keepdims=True), sink)
    p = jnp.exp(s - m).astype(BF16)
    den = jnp.dot(p, jnp.ones(vals.shape, BF16), preferred_element_type=F32) + jnp.exp(sink - m)
    o = jnp.dot(p, vals, preferred_element_type=F32)
    return o / den


def _swa_prompt_body(sink_ref, q_ref, kvp_ref, kvc_ref, bias_ref, o_ref, *, hd):
    i = pl.program_id(1)
    blk = WINDOW
    g = ATT_GROUP
    nkv = ATT_KV_HEADS
    nk = nkv * hd
    col = lax.broadcasted_iota(jnp.int32, (1, 2 * blk), 1)
    first_mask = jnp.where(jnp.logical_and(i == 0, col < blk), NEG_INF, 0.0)
    for hk in range(nkv):
        keys = jnp.concatenate([kvp_ref[:, hk * hd:(hk + 1) * hd], kvc_ref[:, hk * hd:(hk + 1) * hd]], axis=0)
        vals = jnp.concatenate([kvp_ref[:, nk + hk * hd:nk + (hk + 1) * hd],
                                kvc_ref[:, nk + hk * hd:nk + (hk + 1) * hd]], axis=0)
        qs = jnp.concatenate([q_ref[:, (hk * g + gg) * hd:(hk * g + gg + 1) * hd] for gg in range(g)], axis=0)
        s = lax.dot_general(qs, keys, (((1,), (1,)), ((), ())), preferred_element_type=F32)
        s = s + bias_ref[hk * g * blk:(hk + 1) * g * blk, :] + first_mask
        sink = jnp.concatenate([jnp.full((blk, 1), sink_ref[hk * g + gg], F32) for gg in range(g)], axis=0)
        o = _softmax_sink_pv(s, sink, vals)
        for gg in range(g):
            o_ref[:, (hk * g + gg) * hd:(hk * g + gg + 1) * hd] = o[gg * blk:(gg + 1) * blk, :].astype(BF16)


def _swa_prompt_call(proj, sinks, rel_bias, *, n_batch, seq, hd):
    blk = WINDOW
    nb = seq // blk
    nq = ATT_Q_HEADS * hd
    nkv2 = 2 * ATT_KV_HEADS * hd
    kvcol = nq // nkv2
    i_idx = jnp.arange(blk, dtype=jnp.int32)[:, None]
    j_idx = jnp.arange(2 * blk, dtype=jnp.int32)[None, :]
    dist = blk + i_idx - j_idx
    bias = _bias_table(rel_bias, dist, (dist >= 0) & (dist < WINDOW))
    body = functools.partial(_swa_prompt_body, hd=hd)
    return pl.pallas_call(
        body,
        out_shape=jax.ShapeDtypeStruct((n_batch * seq, nq), BF16),
        grid=(n_batch, nb),
        in_specs=[
            pl.BlockSpec(memory_space=pltpu.SMEM),
            pl.BlockSpec((blk, nq), lambda b, i: (b * nb + i, 0)),
            pl.BlockSpec((blk, nkv2), lambda b, i: (b * nb + jnp.maximum(i - 1, 0), kvcol)),
            pl.BlockSpec((blk, nkv2), lambda b, i: (b * nb + i, kvcol)),
            pl.BlockSpec((ATT_Q_HEADS * blk, 2 * blk), lambda b, i: (0, 0)),
        ],
        out_specs=pl.BlockSpec((blk, nq), lambda b, i: (b * nb + i, 0)),
        compiler_params=_cparams(("arbitrary", "arbitrary")),
        name="swa_prompt",
    )(sinks, proj, proj, proj, bias)


def _swa_sample_body(sink_ref, p_ref, ck_ref, cv_ref, bias_ref, o_ref, *, hd):
    s = SAMPLE_ROWS
    g = ATT_GROUP
    nkv = ATT_KV_HEADS
    nq = ATT_Q_HEADS * hd
    for hk in range(nkv):
        keys = jnp.concatenate([ck_ref[:, hk * hd:(hk + 1) * hd].astype(BF16),
                                p_ref[:, nq + hk * hd:nq + (hk + 1) * hd]], axis=0)
        vals = jnp.concatenate([cv_ref[:, hk * hd:(hk + 1) * hd].astype(BF16),
                                p_ref[:, nq + (nkv + hk) * hd:nq + (nkv + hk + 1) * hd]], axis=0)
        qs = jnp.concatenate([p_ref[:, (hk * g + gg) * hd:(hk * g + gg + 1) * hd] for gg in range(g)], axis=0)
        sc = lax.dot_general(qs, keys, (((1,), (1,)), ((), ())), preferred_element_type=F32)
        sc = sc + bias_ref[hk * g * s:(hk + 1) * g * s, :]
        sink = jnp.concatenate([jnp.full((s, 1), sink_ref[hk * g + gg], F32) for gg in range(g)], axis=0)
        o = _softmax_sink_pv(sc, sink, vals)
        for gg in range(g):
            o_ref[:, (hk * g + gg) * hd:(hk * g + gg + 1) * hd] = o[gg * s:(gg + 1) * s, :].astype(BF16)


def _swa_sample_call(proj, cache_k, cache_v, sinks, rel_bias, *, n_prompt_rows, n_seq, dec_seq, hd):
    s = SAMPLE_ROWS
    nq = ATT_Q_HEADS * hd
    nkv = ATT_KV_HEADS * hd
    sc = cache_k.shape[1]
    i_idx = jnp.arange(s, dtype=jnp.int32)[:, None]
    j_idx = jnp.arange(sc + s, dtype=jnp.int32)[None, :]
    dist = sc + i_idx - j_idx
    valid = (dist >= 0) & (dist < WINDOW) & (j_idx < sc + dec_seq)
    bias = _bias_table(rel_bias, dist, valid)
    base = n_prompt_rows // s
    body = functools.partial(_swa_sample_body, hd=hd)
    return pl.pallas_call(
        body,
        out_shape=jax.ShapeDtypeStruct((n_seq * s, nq), BF16),
        grid=(n_seq,),
        in_specs=[
            pl.BlockSpec(memory_space=pltpu.SMEM),
            pl.BlockSpec((s, proj.shape[1]), lambda b: (base + b, 0)),
            pl.BlockSpec((None, sc, nkv), lambda b: (b, 0, 0)),
            pl.BlockSpec((None, sc, nkv), lambda b: (b, 0, 0)),
            pl.BlockSpec((ATT_Q_HEADS * s, sc + s), lambda b: (0, 0)),
        ],
        out_specs=pl.BlockSpec((s, nq), lambda b: (b, 0)),
        compiler_params=_cparams(("arbitrary",)),
        name="swa_sample",
    )(sinks, proj, cache_k, cache_v, bias)


MOE_TILE = 256


def _moe_body(src_ref, dst_ref, te_ref, nu_ref, h_hbm, win_ref, wout_ref, y_hbm, xg, yb, wib, wob, gsem, ssem,
              *, ff):
    t = pl.program_id(0)
    nt = pl.num_programs(0)
    n_used = nu_ref[0]
    tm = MOE_TILE
    slot = lax.rem(t, 2)

    def gather_start(tile, sl):
        base = tile * tm
        for r in range(tm):
            pltpu.make_async_copy(h_hbm.at[pl.ds(src_ref[base + r], 1), :], xg.at[sl, pl.ds(r, 1), :],
                                  gsem.at[sl]).start()

    def gather_wait(sl):
        pltpu.make_async_copy(h_hbm.at[pl.ds(0, tm), :], xg.at[sl], gsem.at[sl]).wait()

    def scatter_start(tile, sl):
        base = tile * tm
        for r in range(tm):
            pltpu.make_async_copy(yb.at[sl, pl.ds(r, 1), :], y_hbm.at[pl.ds(dst_ref[base + r], 1), :],
                                  ssem.at[sl]).start()

    def scatter_wait(sl):
        pltpu.make_async_copy(yb.at[sl], y_hbm.at[pl.ds(0, tm), :], ssem.at[sl]).wait()

    valid = t < n_used

    @pl.when(t == 0)
    def _():
        gather_start(0, 0)

    @pl.when(jnp.logical_and(valid, jnp.logical_or(t == 0, te_ref[t] != te_ref[jnp.maximum(t - 1, 0)])))
    def _():
        wib[...] = win_ref[...].astype(BF16)
        wob[...] = wout_ref[...].astype(BF16)

    def step(sl):
        @pl.when(t >= 2)
        def _():
            scatter_wait(sl)

        gather_wait(sl)

        @pl.when(t + 1 < n_used)
        def _():
            gather_start(t + 1, 1 - sl)

        gu = jnp.dot(xg[sl].astype(BF16), wib[...], preferred_element_type=F32)
        gate, up = gu[:, :ff], gu[:, ff:]
        act = (gate * jax.nn.sigmoid(gate) * up).astype(BF16)
        yb[sl] = _pack_bf16_pairs(jnp.dot(act, wob[...], preferred_element_type=F32))
        scatter_start(t, sl)

    for sl in range(2):
        pl.when(jnp.logical_and(valid, slot == sl))(functools.partial(step, sl))

    @pl.when(t == nt - 1)
    def _():
        scatter_wait(lax.rem(n_used - 1, 2))
        scatter_wait(lax.rem(n_used, 2))


def _moe_layer(h2, route, counts, w_in, w_out, layer):
    r, d = h2.shape
    depth, n_exp, _, ff2 = w_in.shape
    ff = ff2 // 2
    tm = MOE_TILE
    n_tiles = 2 * r // tm + n_exp
    n_slots = n_tiles * tm

    cnt = counts[0, MOE_GROUPS:MOE_GROUPS + n_exp].astype(jnp.int32)
    padded = ((cnt + tm - 1) // tm) * tm
    ends = jnp.cumsum(padded)
    starts = ends - padded
    rt = route[:, :ROUTE_R2 + 1].astype(jnp.int32)
    pos = jnp.concatenate([starts[rt[:, ROUTE_E1]] + rt[:, ROUTE_R1], starts[rt[:, ROUTE_E2]] + rt[:, ROUTE_R2]])
    tok = jnp.arange(r, dtype=jnp.int32)
    dst = (2 * r + jnp.arange(n_slots, dtype=jnp.int32)).at[pos].set(
        jnp.concatenate([tok, r + tok]), unique_indices=True, indices_are_sorted=False)
    src = jnp.where(dst < r, dst, jnp.where(dst < 2 * r, dst - r, 0))
    tile_start = jnp.arange(n_tiles, dtype=jnp.int32) * tm
    tile_expert = jnp.minimum(jnp.sum(tile_start[:, None] >= ends[None, :], axis=1), n_exp - 1).astype(jnp.int32)
    n_used = (ends[-1:] // tm).astype(jnp.int32)

    return pl.pallas_call(
        functools.partial(_moe_body, ff=ff),
        out_shape=jax.ShapeDtypeStruct((2 * r + n_slots, d // 2), jnp.uint32),
        grid_spec=pltpu.PrefetchScalarGridSpec(
            num_scalar_prefetch=4,
            grid=(n_tiles,),
            in_specs=[
                pl.BlockSpec(memory_space=pl.ANY),
                pl.BlockSpec((None, None, d, ff2), lambda t, src, dst, te, nu: (layer, te[t], 0, 0)),
                pl.BlockSpec((None, None, ff, d), lambda t, src, dst, te, nu: (layer, te[t], 0, 0)),
            ],
            out_specs=pl.BlockSpec(memory_space=pl.ANY),
            scratch_shapes=[pltpu.VMEM((2, tm, d), F32), pltpu.VMEM((2, tm, d // 2), jnp.uint32),
                            pltpu.VMEM((d, ff2), BF16), pltpu.VMEM((ff, d), BF16),
                            pltpu.SemaphoreType.DMA((2,)), pltpu.SemaphoreType.DMA((2,))],
        ),
        compiler_params=_cparams(("arbitrary",)),
        name="moe_experts",
    )(src, dst, tile_expert, n_used, h2, w_in, w_out)


def kernel(x_prompt, x_sample, c_prompt, c_sample, state_ret, cache_conv, cache_swa_k, cache_swa_v, ada_w, ada_b, norm_gain, ret_w_in, ret_gn_gain, ret_gn_bias, ret_w_out, conv_w_pw1, conv_b_pw1, conv_w_dw, conv_b_dw, conv_ln_gain, conv_ln_bias, conv_w_pw2, conv_b_pw2, att_w_qkv, att_q_gain, att_k_gain, att_sinks, att_w_o, rel_bias, moe_wg, moe_bg, moe_we, moe_be, moe_w_in, moe_w_out):
    n_batch, seq, d = x_prompt.shape
    n_seq, dec_seq, _ = x_sample.shape
    depth = ada_w.shape[0]
    s = SAMPLE_ROWS
    assert n_seq * s == ROW_TILE and seq % ROW_TILE == 0 and dec_seq <= s
    n_prompt_rows = n_batch * seq
    seq_tiles = seq // ROW_TILE
    dk = ret_w_in.shape[2] // (6 * RET_HEADS)
    dv = 2 * dk
    hd = d // ATT_Q_HEADS
    geom = dict(seq_tiles=seq_tiles, n_batch=n_batch)

    xs_pad = jnp.pad(x_sample, ((0, 0), (0, s - dec_seq), (0, 0))).reshape(n_seq * s, d)
    x = jnp.concatenate([x_prompt.reshape(n_prompt_rows, d), xs_pad], axis=0)

    n_c = n_batch + n_seq
    c_rows = ((n_c + 7) // 8) * 8
    c_all = jnp.pad(jnp.concatenate([c_prompt, c_sample], axis=0), ((0, c_rows - n_c), (0, 0)))
    mod = _ada_call(c_all, ada_w, ada_b)
    mod4 = mod.reshape(depth, c_rows, 1, 6 * d)
    modtok = jnp.repeat(mod[:, n_batch:n_c], s, axis=1)

    gains = norm_gain.astype(F32).reshape(2 * depth, 1, d)
    gng_all = ret_gn_gain.astype(F32)[:, None, :]
    gnb_all = ret_gn_bias.astype(F32)[:, None, :]
    state_all = state_ret.astype(F32)

    ret_p, conv_p, conv_s, kp_l, vp_l, ks_l, vs_l = [], [], [], [], [], [], []
    ret_s = None
    (h,) = _norm_call(x, mod4, modtok, norm=(gains, 0, 0, 0, 1), **geom)
    for l in range(depth):
        kind, j = l % 3, l // 3
        resid1 = (x, mod4, modtok, l, 2)
        if kind == 0:
            proj = _linear_call(h, ret_w_in, j, n_out=ret_w_in.shape[2], tn=2048, out_dtype=BF16,
                                name="ret_in", **geom)
            a_p, st_p = _ret_prompt_call(proj, gng_all, gnb_all, j, n_batch=n_batch, seq=seq, dk=dk, dv=dv)
            a_s, ret_s = _ret_sample_call(proj, state_all, gng_all, gnb_all, j, ret_s, n_prompt_rows=n_prompt_rows,
                                          n_seq=n_seq, dec_seq=dec_seq, dk=dk, dv=dv)
            ret_p.append(st_p)
            x = _linear_call(a_p, ret_w_out, j, h_sample=a_s, n_out=d, tn=512, out_dtype=F32, resid=resid1,
                             name="ret_out", **geom)
        elif kind == 1:
            z = _linear_call(h, conv_w_pw1, j, n_out=d, tn=1024, out_dtype=BF16, bias=conv_b_pw1,
                             glu=True, name="conv_pw1", **geom)
            cargs = (conv_w_dw[j], conv_b_dw[j][None], conv_ln_gain[j][None], conv_ln_bias[j][None])
            a_p = _conv_prompt_call(z, *cargs, n_batch=n_batch, seq=seq)
            a_s = _conv_sample_call(z, cache_conv[j].astype(F32), *cargs, n_prompt_rows=n_prompt_rows, n_seq=n_seq)
            z_tail = z[:n_prompt_rows].reshape(n_batch, seq, d)[:, seq - CONV_STATE:].astype(F32)
            conv_p.append(z_tail)
            z_new = z[n_prompt_rows:].reshape(n_seq, s, d)[:, :dec_seq].astype(F32)
            conv_s.append(jnp.concatenate([cache_conv[j].astype(F32), z_new], axis=1)[:, -CONV_STATE:])
            x = _linear_call(a_p, conv_w_pw2, j, h_sample=a_s, n_out=d, tn=1024, out_dtype=F32,
                             bias=conv_b_pw2, resid=resid1, name="conv_pw2", **geom)
        else:
            nkv = ATT_KV_HEADS * hd
            nq = ATT_Q_HEADS * hd
            hgain = jnp.concatenate([jnp.tile(att_q_gain[j].astype(F32) * (hd ** -0.5), ATT_Q_HEADS),
                                     jnp.tile(att_k_gain[j].astype(F32), ATT_KV_HEADS), jnp.ones((nkv,), F32)])[None]
            hmask = jnp.concatenate([jnp.ones((nq + nkv,), F32), jnp.zeros((nkv,), F32)])[None]
            proj = _linear_call(h, att_w_qkv, j, n_out=att_w_qkv.shape[2], tn=1280, out_dtype=BF16,
                                head_norm=(hgain, hmask, hd), name="att_qkv", **geom)
            sinks = att_sinks[j].astype(F32)
            a_p = _swa_prompt_call(proj, sinks, rel_bias, n_batch=n_batch, seq=seq, hd=hd)
            win = cache_swa_k.shape[2]
            ck = cache_swa_k[j].astype(F32).reshape(n_seq, win, nkv)
            cv = cache_swa_v[j].astype(F32).reshape(n_seq, win, nkv)
            a_s = _swa_sample_call(proj, ck, cv, sinks, rel_bias, n_prompt_rows=n_prompt_rows,
                                   n_seq=n_seq, dec_seq=dec_seq, hd=hd)
            kv_tail = proj[:n_prompt_rows, nq:].reshape(n_batch, seq, 2 * nkv)[:, seq - WINDOW:].astype(F32)
            kp_l.append(kv_tail[:, :, :nkv].reshape(n_batch, WINDOW, ATT_KV_HEADS, hd))
            vp_l.append(kv_tail[:, :, nkv:].reshape(n_batch, WINDOW, ATT_KV_HEADS, hd))
            kv_new = proj[n_prompt_rows:, nq:].reshape(n_seq, s, 2 * nkv)[:, :dec_seq].astype(F32)
            k_new, v_new = kv_new[:, :, :nkv], kv_new[:, :, nkv:]
            ks_l.append(jnp.concatenate([ck, k_new], axis=1)[:, -win:].reshape(n_seq, win, ATT_KV_HEADS, hd))
            vs_l.append(jnp.concatenate([cv, v_new], axis=1)[:, -win:].reshape(n_seq, win, ATT_KV_HEADS, hd))
            x = _linear_call(a_p, att_w_o, j, h_sample=a_s, n_out=d, tn=1024, out_dtype=F32, resid=resid1,
                             name="att_out", **geom)

        lane_pad = ROUTER_LANES - MOE_GROUPS - MOE_GROUPS * MOE_EPG
        router_w = jnp.pad(jnp.concatenate([moe_wg[l], moe_we[l]], axis=1).astype(F32), ((0, 0), (0, lane_pad)))
        router_hi = router_w.astype(BF16)
        router_w = jnp.stack([router_hi, (router_w - router_hi.astype(F32)).astype(BF16)])
        router_b = jnp.pad(jnp.concatenate([moe_bg[l], moe_be[l]]).astype(F32), (0, lane_pad))[None]
        h2, route, counts = _norm_call(x, mod4, modtok, norm=(gains, 2 * l + 1, l, 3, 4),
                                       router=(router_w, router_b), **geom)
        ypair = _moe_layer(h2, route, counts, moe_w_in, moe_w_out, l)
        if l + 1 < depth:
            x, h = _norm_call(x, mod4, modtok, resid=(ypair, route, l, 5),
                              norm=(gains, 2 * l + 2, l + 1, 0, 1), **geom)
        else:
            x_p, x_s = _norm_call(x, mod4, modtok, resid=(ypair, route, l, 5), split_out=True, **geom)

    y_prompt = x_p.reshape(n_batch, seq, d)
    y_sample = x_s.reshape(n_seq, s, d)[:, :dec_seq]
    return (y_prompt, y_sample, jnp.stack(ret_p), ret_s, jnp.stack(conv_p), jnp.stack(conv_s),
            jnp.stack(kp_l), jnp.stack(vp_l), jnp.stack(ks_l), jnp.stack(vs_l))
```

```python
import functools
import math

import jax
import jax.numpy as jnp
from jax import lax
from jax.experimental import pallas as pl
from jax.experimental.pallas import tpu as pltpu

F32 = jnp.float32
BF16 = jnp.bfloat16

NORM_EPS = 1e-6
NEG_INF = -1e30
ROPE_BASE = 10000.0
PAST_LEN = 16384

RET_HEADS = 8
RET_CHUNK = 128
CONV_WIDTH = 31
CONV_STATE = CONV_WIDTH - 1
ATT_Q_HEADS = 32
ATT_KV_HEADS = 4
ATT_GROUP = ATT_Q_HEADS // ATT_KV_HEADS
WINDOW = 128
REL_BUCKETS = 32
REL_MAX_DIST = 128
MOE_GROUPS = 4
MOE_EPG = 4
MOE_TOPK = 2

ROW_TILE = 512
SAMPLE_ROWS = 16
ROUTER_LANES = 128
VMEM_LIMIT_BYTES = 56 * 1024 * 1024


def _cparams(sem):
    return pltpu.CompilerParams(dimension_semantics=sem, vmem_limit_bytes=VMEM_LIMIT_BYTES)


def _pack_bf16_pairs(y):
    half = y.shape[1] // 2
    hi = lax.bitcast_convert_type(y[:, :half].astype(BF16).astype(F32), jnp.uint32)
    lo = lax.bitcast_convert_type(y[:, half:].astype(BF16).astype(F32), jnp.uint32)
    return hi | lax.shift_right_logical(lo, jnp.uint32(16))


def _unpack_bf16_pairs(w):
    hi = lax.bitcast_convert_type(w & jnp.uint32(0xFFFF0000), F32)
    lo = lax.bitcast_convert_type(lax.shift_left(w, jnp.uint32(16)), F32)
    return jnp.concatenate([hi, lo], axis=1)


def _ada_body(c_ref, w_ref, b_ref, o_ref):
    c = c_ref[...]
    s = (c * jax.nn.sigmoid(c)).astype(BF16)
    o_ref[...] = jnp.dot(s, w_ref[...].astype(BF16), preferred_element_type=F32) + b_ref[...]


def _ada_call(c_all, ada_w, ada_b):
    depth, d, n = ada_w.shape
    rows = c_all.shape[0]
    tn = 1024
    return pl.pallas_call(
        _ada_body,
        out_shape=jax.ShapeDtypeStruct((depth, rows, n), F32),
        grid=(depth, n // tn),
        in_specs=[
            pl.BlockSpec((rows, d), lambda l, j: (0, 0)),
            pl.BlockSpec((None, d, tn), lambda l, j: (l, 0, j)),
            pl.BlockSpec((None, 1, tn), lambda l, j: (l, 0, j)),
        ],
        out_specs=pl.BlockSpec((None, rows, tn), lambda l, j: (l, 0, j)),
        compiler_params=_cparams(("arbitrary", "arbitrary")),
        name="ada_mod",
    )(c_all, ada_w, ada_b.reshape(depth, 1, n))


ROUTE_E1, ROUTE_E2, ROUTE_W1, ROUTE_W2, ROUTE_R1, ROUTE_R2 = range(6)


def _route_tile(logits, rb, carry):
    tm = logits.shape[0]
    g, epg = MOE_GROUPS, MOE_EPG
    lg = logits + rb
    lane = lax.broadcasted_iota(jnp.int32, lg.shape, 1)
    lane_f = lane.astype(F32)

    def first_lane(mask):
        return jnp.min(jnp.where(mask, lane_f, float(ROUTER_LANES)), axis=-1, keepdims=True).astype(jnp.int32)

    is_g = lane < g
    mg = jnp.max(jnp.where(is_g, lg, NEG_INF), axis=-1, keepdims=True)
    eg = jnp.where(is_g, jnp.exp(lg - mg), 0.0)
    p_grp = 1.0 / jnp.sum(eg, axis=-1, keepdims=True)
    grp = first_lane(jnp.logical_and(is_g, lg == mg))
    lo = g + epg * grp
    is_e = jnp.logical_and(lane >= lo, lane < lo + epg)
    me = jnp.max(jnp.where(is_e, lg, NEG_INF), axis=-1, keepdims=True)
    ee = jnp.where(is_e, jnp.exp(lg - me), 0.0)
    se = jnp.sum(ee, axis=-1, keepdims=True)
    i1 = first_lane(jnp.logical_and(is_e, lg == me))
    rest = jnp.logical_and(is_e, lane != i1)
    m2 = jnp.max(jnp.where(rest, ee, -1.0), axis=-1, keepdims=True)
    i2 = first_lane(jnp.logical_and(rest, ee == m2))
    p1 = 1.0 / se
    p2 = m2 / se
    w1 = p_grp * p1 / (p1 + p2)
    w2 = p_grp * p2 / (p1 + p2)
    sel1 = lane == i1
    sel2 = lane == i2
    onehot = jnp.where(jnp.logical_or(sel1, sel2), 1.0, 0.0)
    row = lax.broadcasted_iota(jnp.int32, (tm, tm), 0)
    col = lax.broadcasted_iota(jnp.int32, (tm, tm), 1)
    tri = jnp.where(col <= row, 1.0, 0.0).astype(BF16)
    incl = jnp.dot(tri, onehot.astype(BF16), preferred_element_type=F32)
    rank = incl - 1.0 + carry
    r1 = jnp.sum(jnp.where(sel1, rank, 0.0), axis=-1, keepdims=True)
    r2 = jnp.sum(jnp.where(sel2, rank, 0.0), axis=-1, keepdims=True)
    rec = jnp.zeros_like(lg)
    for ln, val in ((ROUTE_E1, (i1 - g).astype(F32)), (ROUTE_E2, (i2 - g).astype(F32)), (ROUTE_W1, w1),
                    (ROUTE_W2, w2), (ROUTE_R1, r1), (ROUTE_R2, r2)):
        rec = jnp.where(lane == ln, val, rec)
    return rec, carry + jnp.sum(onehot, axis=0, keepdims=True)


def _norm_body(*refs, n_prompt_tiles, has_resid, has_norm, has_router, split_out):
    it = iter(refs)
    x_ref = next(it)
    if has_resid:
        y0_ref, y1_ref, rt_ref, grow_ref, gtok_ref = next(it), next(it), next(it), next(it), next(it)
    if has_norm:
        gain_ref, shrow_ref, scrow_ref, shtok_ref, sctok_ref = next(it), next(it), next(it), next(it), next(it)
    if has_router:
        wr_ref, rb_ref = next(it), next(it)
    if has_resid:
        xo_ref = next(it)
        if split_out:
            xs_ref = next(it)
    if has_norm:
        h_ref = next(it)
    if has_router:
        rec_ref, cnt_ref, carry = next(it), next(it), next(it)

    is_sample = pl.program_id(0) >= n_prompt_tiles

    if has_router:
        @pl.when(pl.program_id(0) == 0)
        def _():
            carry[...] = jnp.zeros_like(carry)

    def tile(sample):
        x = x_ref[...]
        if has_resid:
            gate = gtok_ref[...] if sample else grow_ref[...]
            rt = rt_ref[...]
            y = (rt[:, ROUTE_W1:ROUTE_W1 + 1] * _unpack_bf16_pairs(y0_ref[...])
                 + rt[:, ROUTE_W2:ROUTE_W2 + 1] * _unpack_bf16_pairs(y1_ref[...]))
            x = x + gate * y
            if split_out and sample:
                xs_ref[...] = x
            else:
                xo_ref[...] = x
        if has_norm:
            scale = sctok_ref[...] if sample else scrow_ref[...]
            shift = shtok_ref[...] if sample else shrow_ref[...]
            ms = jnp.mean(x * x, axis=-1, keepdims=True)
            h = (x * lax.rsqrt(ms + NORM_EPS)) * (gain_ref[...] * (1.0 + scale)) + shift
            if has_router:
                h_ref[...] = _pack_bf16_pairs(h)
            else:
                h_ref[...] = h.astype(h_ref.dtype)
            if has_router:
                h_hi = h.astype(BF16)
                h_lo = (h - h_hi.astype(F32)).astype(BF16)
                logits = (jnp.dot(h_hi, wr_ref[0], preferred_element_type=F32)
                          + jnp.dot(h_lo, wr_ref[0], preferred_element_type=F32)
                          + jnp.dot(h_hi, wr_ref[1], preferred_element_type=F32))
                rec, new_carry = _route_tile(logits, rb_ref[...], carry[0:1, :])
                rec_ref[...] = rec
                carry[...] = jnp.broadcast_to(new_carry, carry.shape)
                cnt_ref[...] = carry[...]

    pl.when(jnp.logical_not(is_sample))(functools.partial(tile, False))
    pl.when(is_sample)(functools.partial(tile, True))


def _norm_call(x, mod4, modtok, *, seq_tiles, n_batch, resid=None, norm=None, router=None, split_out=False):
    r, d = x.shape
    n_tiles = r // ROW_TILE
    n_prompt_tiles = n_tiles - 1

    def row_spec(layer, col):
        return pl.BlockSpec((None, None, 1, d),
                            lambda i: (layer, jnp.minimum(i // seq_tiles, n_batch - 1), 0, col))

    def tok_spec(layer, col):
        return pl.BlockSpec((None, ROW_TILE, d), lambda i: (layer, 0, col))

    tile = pl.BlockSpec((ROW_TILE, d), lambda i: (i, 0))
    args, in_specs, out_shape, out_specs = [x], [tile], [], []
    scratch = []
    if resid is not None:
        ypair, route, layer, gcol = resid
        args += [ypair, ypair, route, mod4, modtok]
        in_specs += [pl.BlockSpec((ROW_TILE, d // 2), lambda i: (i, 0)),
                     pl.BlockSpec((ROW_TILE, d // 2), lambda i: (n_tiles + i, 0)),
                     pl.BlockSpec((ROW_TILE, ROUTER_LANES), lambda i: (i, 0)),
                     row_spec(layer, gcol), tok_spec(layer, gcol)]
        if split_out:
            out_shape += [jax.ShapeDtypeStruct((r - ROW_TILE, d), F32), jax.ShapeDtypeStruct((ROW_TILE, d), F32)]
            out_specs += [pl.BlockSpec((ROW_TILE, d), lambda i: (jnp.minimum(i, n_prompt_tiles - 1), 0)),
                          pl.BlockSpec((ROW_TILE, d), lambda i: (0, 0))]
        else:
            out_shape.append(jax.ShapeDtypeStruct((r, d), F32))
            out_specs.append(tile)
    if norm is not None:
        gains, gidx, layer, shcol, sccol = norm
        args += [gains, mod4, mod4, modtok, modtok]
        in_specs += [pl.BlockSpec((None, 1, d), lambda i: (gidx, 0, 0)), row_spec(layer, shcol),
                     row_spec(layer, sccol), tok_spec(layer, shcol), tok_spec(layer, sccol)]
        if router is not None:
            out_shape.append(jax.ShapeDtypeStruct((r, d // 2), jnp.uint32))
            out_specs.append(pl.BlockSpec((ROW_TILE, d // 2), lambda i: (i, 0)))
        else:
            out_shape.append(jax.ShapeDtypeStruct((r, d), BF16))
            out_specs.append(tile)
        if router is not None:
            args += list(router)
            in_specs += [pl.BlockSpec((2, d, ROUTER_LANES), lambda i: (0, 0, 0)),
                         pl.BlockSpec((1, ROUTER_LANES), lambda i: (0, 0))]
            out_shape += [jax.ShapeDtypeStruct((r, ROUTER_LANES), F32), jax.ShapeDtypeStruct((8, ROUTER_LANES), F32)]
            out_specs += [pl.BlockSpec((ROW_TILE, ROUTER_LANES), lambda i: (i, 0)),
                          pl.BlockSpec((8, ROUTER_LANES), lambda i: (0, 0))]
            scratch.append(pltpu.VMEM((8, ROUTER_LANES), F32))
    body = functools.partial(_norm_body, n_prompt_tiles=n_prompt_tiles, has_resid=resid is not None,
                             has_norm=norm is not None, has_router=router is not None, split_out=split_out)
    return pl.pallas_call(
        body, out_shape=out_shape, grid=(n_tiles,), in_specs=in_specs, out_specs=out_specs,
        scratch_shapes=scratch, compiler_params=_cparams(("arbitrary",)), name="mod_norm",
    )(*args)


def _head_rms_norm(x, gain_row, seg_ref, exp_ref, hd):
    ssq = jnp.dot((x * x).astype(BF16), seg_ref[...], preferred_element_type=F32)
    r = lax.rsqrt(ssq * (1.0 / hd) + NORM_EPS)
    r_hi = r.astype(BF16)
    r_lo = (r - r_hi.astype(F32)).astype(BF16)
    scale = jnp.dot(jnp.concatenate([r_hi, r_lo], axis=1), exp_ref[...], preferred_element_type=F32)
    return x * scale * gain_row


def _linear_body(*refs, n_prompt_tiles, pair, glu, has_bias, has_resid, head_dim):
    it = iter(refs)
    h_ref = next(it)
    hs_ref = next(it) if pair else None
    w_ref = next(it)
    w2_ref = next(it) if glu else None
    b_ref = next(it) if has_bias else None
    b2_ref = next(it) if glu else None
    if has_resid:
        x_ref, grow_ref, gtok_ref = next(it), next(it), next(it)
    if head_dim:
        hgain_ref, hmask_ref, seg_ref, exp_ref = next(it), next(it), next(it), next(it)
    o_ref = next(it)
    wb = next(it)
    wb2 = next(it) if glu else None

    i = pl.program_id(1)
    is_sample = i >= n_prompt_tiles

    @pl.when(i == 0)
    def _():
        wb[...] = w_ref[...].astype(BF16)
        if glu:
            wb2[...] = w2_ref[...].astype(BF16)

    def compute(hv):
        acc = jnp.dot(hv, wb[...], preferred_element_type=F32)
        if has_bias:
            acc = acc + b_ref[...]
        if glu:
            acc2 = jnp.dot(hv, wb2[...], preferred_element_type=F32) + b2_ref[...]
            acc = acc * jax.nn.sigmoid(acc2)
        if has_resid:
            gate = jnp.where(is_sample, gtok_ref[...], grow_ref[...])
            acc = x_ref[...] + gate * acc
        if head_dim:
            normed = _head_rms_norm(acc, hgain_ref[...], seg_ref, exp_ref, head_dim)
            acc = jnp.where(hmask_ref[...] > 0.0, normed, acc)
        o_ref[...] = acc.astype(o_ref.dtype)

    if pair:
        @pl.when(jnp.logical_not(is_sample))
        def _():
            compute(h_ref[...])

        @pl.when(is_sample)
        def _():
            compute(hs_ref[...])
    else:
        compute(h_ref[...])


def _linear_call(h, w, wl, *, n_out, tn, out_dtype, seq_tiles, n_batch, h_sample=None, bias=None, glu=False,
                 resid=None, head_norm=None, name="linear"):
    k = w.shape[1]
    pair = h_sample is not None
    n_prompt_tiles = h.shape[0] // ROW_TILE - (0 if pair else 1)
    n_tiles = n_prompt_tiles + 1
    r = n_tiles * ROW_TILE
    nblk = n_out // tn

    args = [h]
    in_specs = [pl.BlockSpec((ROW_TILE, k), lambda j, i: (jnp.minimum(i, n_prompt_tiles - 1) if pair else i, 0))]
    if pair:
        args.append(h_sample)
        in_specs.append(pl.BlockSpec((ROW_TILE, k), lambda j, i: (0, 0)))
    args.append(w)
    in_specs.append(pl.BlockSpec((None, k, tn), lambda j, i: (wl, 0, j)))
    if glu:
        args.append(w)
        in_specs.append(pl.BlockSpec((None, k, tn), lambda j, i: (wl, 0, nblk + j)))
    if bias is not None:
        bias = bias.reshape(bias.shape[0], 1, bias.shape[1])
        args.append(bias)
        in_specs.append(pl.BlockSpec((None, 1, tn), lambda j, i: (wl, 0, j)))
        if glu:
            args.append(bias)
            in_specs.append(pl.BlockSpec((None, 1, tn), lambda j, i: (wl, 0, nblk + j)))
    if resid is not None:
        x, mod4, modtok, layer, gcol = resid
        cb = gcol * nblk
        args += [x, mod4, modtok]
        in_specs += [
            pl.BlockSpec((ROW_TILE, tn), lambda j, i: (i, j)),
            pl.BlockSpec((None, None, 1, tn),
                         lambda j, i: (layer, jnp.minimum(i // seq_tiles, n_batch - 1), 0, cb + j)),
            pl.BlockSpec((None, ROW_TILE, tn), lambda j, i: (layer, 0, cb + j)),
        ]
    head_dim = 0
    if head_norm is not None:
        hgain, hmask, head_dim = head_norm
        seg = (jnp.arange(tn, dtype=jnp.int32)[:, None] // head_dim
               == jnp.arange(128, dtype=jnp.int32)[None, :]).astype(BF16)
        args += [hgain, hmask, seg, jnp.concatenate([seg.T, seg.T], axis=0)]
        in_specs += [pl.BlockSpec((1, tn), lambda j, i: (0, j)), pl.BlockSpec((1, tn), lambda j, i: (0, j)),
                     pl.BlockSpec((tn, 128), lambda j, i: (0, 0)), pl.BlockSpec((256, tn), lambda j, i: (0, 0))]
    scratch = [pltpu.VMEM((k, tn), BF16)] + ([pltpu.VMEM((k, tn), BF16)] if glu else [])
    body = functools.partial(_linear_body, n_prompt_tiles=n_prompt_tiles, pair=pair, glu=glu,
                             has_bias=bias is not None, has_resid=resid is not None, head_dim=head_dim)
    return pl.pallas_call(
        body,
        out_shape=jax.ShapeDtypeStruct((r, n_out), out_dtype),
        grid=(nblk, n_tiles),
        in_specs=in_specs,
        out_specs=pl.BlockSpec((ROW_TILE, tn), lambda j, i: (i, j)),
        scratch_shapes=scratch,
        compiler_params=_cparams(("arbitrary", "arbitrary")),
        name=name,
    )(*args)


def _rotate(x, cos, sin):
    half = x.shape[-1] // 2
    x1, x2 = x[:, :half], x[:, half:]
    return jnp.concatenate([x1 * cos - x2 * sin, x1 * sin + x2 * cos], axis=-1)


def _group_norm_gate(o, g, gain, bias):
    mu = jnp.mean(o, axis=-1, keepdims=True)
    var = jnp.mean(jnp.square(o - mu), axis=-1, keepdims=True)
    on = (o - mu) * lax.rsqrt(var + NORM_EPS) * gain + bias
    g = g.astype(F32)
    return (g * jax.nn.sigmoid(g) * on).astype(BF16)


def _ret_prompt_body(q_ref, k_ref, v_ref, g_ref, cos_ref, sin_ref, dm_ref, qd_ref, kd_ref, gl_ref,
                     gng_ref, gnb_ref, o_ref, st_ref, s_acc, *, n_chunks):
    s_acc[...] = jnp.zeros_like(s_acc)
    dmask = dm_ref[...]
    qdec = qd_ref[...]
    kdec = kd_ref[...]
    gl = gl_ref[0:1, 0:1]
    gng = gng_ref[...]
    gnb = gnb_ref[...]

    def chunk(c, carry):
        r0 = pl.multiple_of(c * RET_CHUNK, RET_CHUNK)
        rows = pl.ds(r0, RET_CHUNK)
        cos = cos_ref[rows, :]
        sin = sin_ref[rows, :]
        qr = _rotate(q_ref[rows, :].astype(F32), cos, sin)
        kr = _rotate(k_ref[rows, :].astype(F32), cos, sin)
        v = v_ref[rows, :]
        state = s_acc[...]
        scores = lax.dot_general(qr.astype(BF16), kr.astype(BF16), (((1,), (1,)), ((), ())),
                                 preferred_element_type=F32)
        scores = scores * dmask
        out = jnp.dot(scores.astype(BF16), v, preferred_element_type=F32)
        out = out + jnp.dot((qr * qdec).astype(BF16), state.astype(BF16), preferred_element_type=F32)
        kv = lax.dot_general((kr * kdec).astype(BF16), v, (((0,), (0,)), ((), ())), preferred_element_type=F32)
        s_acc[...] = gl * state + kv
        o_ref[rows, :] = _group_norm_gate(out, g_ref[rows, :], gng, gnb)
        return carry

    lax.fori_loop(0, n_chunks, chunk, 0, unroll=4)
    st_ref[...] = s_acc[...]


def _ret_tables(chunk, dk, n_valid=None):
    n_valid = chunk if n_valid is None else n_valid
    lg = jnp.log1p(-jnp.exp2(-5.0 - jnp.arange(RET_HEADS, dtype=F32)))
    idx = jnp.arange(chunk, dtype=F32)
    diff = idx[:, None] - idx[None, :]
    inside = (idx[:, None] < n_valid) & (idx[None, :] < n_valid)
    dmask = jnp.where((diff[None] >= 0) & inside[None],
                      jnp.exp(jnp.maximum(diff, 0.0)[None] * lg[:, None, None]), 0.0) * (dk ** -0.5)
    qdec = jnp.exp((idx[None, :] + 1.0) * lg[:, None])
    kdec = jnp.where(idx[None, :] < n_valid, jnp.exp((n_valid - 1.0 - idx)[None, :] * lg[:, None]), 0.0) * (dk ** -0.5)
    gl = jnp.exp(n_valid * lg)
    qdec = jnp.broadcast_to(qdec[:, :, None], (RET_HEADS, chunk, dk))
    kdec = jnp.broadcast_to(kdec[:, :, None], (RET_HEADS, chunk, dk))
    gl = jnp.broadcast_to(gl[:, None, None], (RET_HEADS, 8, 128))
    return dmask, qdec, kdec, gl


def _rope_tables(pos, half):
    inv = ROPE_BASE ** (-jnp.arange(half, dtype=F32) / half)
    ang = pos.astype(F32)[:, None] * inv[None, :]
    return jnp.cos(ang), jnp.sin(ang)


def _ret_prompt_call(proj, gn_gain, gn_bias, j, *, n_batch, seq, dk, dv):
    h = RET_HEADS
    n_chunks = seq // RET_CHUNK
    cos, sin = _rope_tables(jnp.arange(seq, dtype=jnp.int32), dk // 2)
    dmask, qdec, kdec, gl = _ret_tables(RET_CHUNK, dk)
    kcol, vcol, gcol = h, (2 * h * dk) // dv, (2 * h * dk) // dv + h
    body = functools.partial(_ret_prompt_body, n_chunks=n_chunks)
    return pl.pallas_call(
        body,
        out_shape=[jax.ShapeDtypeStruct((n_batch * seq, h * dv), BF16),
                   jax.ShapeDtypeStruct((n_batch, h, dk, dv), F32)],
        grid=(n_batch, h),
        in_specs=[
            pl.BlockSpec((seq, dk), lambda b, hh: (b, hh)),
            pl.BlockSpec((seq, dk), lambda b, hh: (b, kcol + hh)),
            pl.BlockSpec((seq, dv), lambda b, hh: (b, vcol + hh)),
            pl.BlockSpec((seq, dv), lambda b, hh: (b, gcol + hh)),
            pl.BlockSpec((seq, dk // 2), lambda b, hh: (0, 0)),
            pl.BlockSpec((seq, dk // 2), lambda b, hh: (0, 0)),
            pl.BlockSpec((None, RET_CHUNK, RET_CHUNK), lambda b, hh: (hh, 0, 0)),
            pl.BlockSpec((None, RET_CHUNK, dk), lambda b, hh: (hh, 0, 0)),
            pl.BlockSpec((None, RET_CHUNK, dk), lambda b, hh: (hh, 0, 0)),
            pl.BlockSpec((None, 8, 128), lambda b, hh: (hh, 0, 0)),
            pl.BlockSpec((None, 1, dv), lambda b, hh: (j, 0, hh)),
            pl.BlockSpec((None, 1, dv), lambda b, hh: (j, 0, hh)),
        ],
        out_specs=[pl.BlockSpec((seq, dv), lambda b, hh: (b, hh)),
                   pl.BlockSpec((None, None, dk, dv), lambda b, hh: (b, hh, 0, 0))],
        scratch_shapes=[pltpu.VMEM((dk, dv), F32)],
        compiler_params=_cparams(("arbitrary", "arbitrary")),
        name="retention_prompt",
    )(proj, proj, proj, proj, cos, sin, dmask, qdec, kdec, gl, gn_gain, gn_bias)


def _ret_sample_body(p_ref, st_ref, cos_ref, sin_ref, dm_ref, qd_ref, kd_ref, gl_ref, gng_ref, gnb_ref,
                     *rest, dk, dv):
    o_ref, so_ref = rest[-2], rest[-1]
    h = RET_HEADS
    cos = cos_ref[...]
    sin = sin_ref[...]
    pad = 128 - SAMPLE_ROWS
    for hh in range(h):
        q = p_ref[:, hh * dk:(hh + 1) * dk].astype(F32)
        k = p_ref[:, h * dk + hh * dk:h * dk + (hh + 1) * dk].astype(F32)
        v = p_ref[:, 2 * h * dk + hh * dv:2 * h * dk + (hh + 1) * dv]
        g = p_ref[:, 2 * h * dk + h * dv + hh * dv:2 * h * dk + h * dv + (hh + 1) * dv]
        qr = _rotate(q, cos, sin)
        kr = _rotate(k, cos, sin)
        k_pad = jnp.concatenate([kr.astype(BF16), jnp.zeros((pad, dk), BF16)], axis=0)
        kd_pad = jnp.concatenate([(kr * kd_ref[hh]).astype(BF16), jnp.zeros((pad, dk), BF16)], axis=0)
        v_pad = jnp.concatenate([v, jnp.zeros((pad, dv), BF16)], axis=0)
        state = st_ref[hh]
        scores = lax.dot_general(qr.astype(BF16), k_pad, (((1,), (1,)), ((), ())), preferred_element_type=F32)
        scores = scores * dm_ref[hh]
        out = jnp.dot(scores.astype(BF16), v_pad, preferred_element_type=F32)
        out = out + jnp.dot((qr * qd_ref[hh]).astype(BF16), state.astype(BF16), preferred_element_type=F32)
        kv = lax.dot_general(kd_pad, v_pad, (((0,), (0,)), ((), ())), preferred_element_type=F32)
        so_ref[hh] = gl_ref[hh, 0:1, 0:1] * state + kv
        o_ref[:, hh * dv:(hh + 1) * dv] = _group_norm_gate(out, g, gng_ref[:, hh * dv:(hh + 1) * dv],
                                                           gnb_ref[:, hh * dv:(hh + 1) * dv])


def _ret_sample_call(proj, state, gn_gain, gn_bias, j, new_state, *, n_prompt_rows, n_seq, dec_seq, dk, dv):
    h = RET_HEADS
    s = SAMPLE_ROWS
    cos, sin = _rope_tables(PAST_LEN + jnp.arange(s, dtype=jnp.int32), dk // 2)
    dmask, qdec, kdec, gl = _ret_tables(s, dk, n_valid=dec_seq)
    dmask = jnp.pad(dmask, ((0, 0), (0, 0), (0, 128 - s)))
    base = n_prompt_rows // s
    body = functools.partial(_ret_sample_body, dk=dk, dv=dv)
    width = proj.shape[1]
    args = [proj, state, cos, sin, dmask, qdec, kdec, gl, gn_gain, gn_bias]
    in_specs = [
        pl.BlockSpec((s, width), lambda b: (base + b, 0)),
        pl.BlockSpec((None, None, h, dk, dv), lambda b: (j, b, 0, 0, 0)),
        pl.BlockSpec((s, dk // 2), lambda b: (0, 0)),
        pl.BlockSpec((s, dk // 2), lambda b: (0, 0)),
        pl.BlockSpec((h, s, 128), lambda b: (0, 0, 0)),
        pl.BlockSpec((h, s, dk), lambda b: (0, 0, 0)),
        pl.BlockSpec((h, s, dk), lambda b: (0, 0, 0)),
        pl.BlockSpec((h, 8, 128), lambda b: (0, 0, 0)),
        pl.BlockSpec((None, 1, h * dv), lambda b: (j, 0, 0)),
        pl.BlockSpec((None, 1, h * dv), lambda b: (j, 0, 0)),
    ]
    aliases = {}
    if new_state is not None:
        aliases = {len(args): 1}
        args.append(new_state)
        in_specs.append(pl.BlockSpec(memory_space=pl.ANY))
    return pl.pallas_call(
        body,
        out_shape=[jax.ShapeDtypeStruct((n_seq * s, h * dv), BF16),
                   jax.ShapeDtypeStruct(state.shape, F32)],
        grid=(n_seq,),
        in_specs=in_specs,
        out_specs=[pl.BlockSpec((s, h * dv), lambda b: (b, 0)),
                   pl.BlockSpec((None, None, h, dk, dv), lambda b: (j, b, 0, 0, 0))],
        input_output_aliases=aliases,
        compiler_params=_cparams(("arbitrary",)),
        name="retention_sample",
    )(*args)


CONV_HALO = 32
CONV_ROW_CHUNK = 64
CONV_LANES = 128


def _layer_norm_swish(u, gain, bias):
    mu = jnp.mean(u, axis=-1, keepdims=True)
    var = jnp.mean(jnp.square(u - mu), axis=-1, keepdims=True)
    un = (u - mu) * lax.rsqrt(var + NORM_EPS) * gain + bias
    return (un * jax.nn.sigmoid(un)).astype(BF16)


def _conv_prompt_body(z_ref, w_ref, bdw_ref, lng_ref, lnb_ref, o_ref, zbuf, ubuf, *, tt, d):
    t = pl.program_id(1)

    @pl.when(t == 0)
    def _():
        zbuf[0:CONV_HALO, :] = jnp.zeros((CONV_HALO, d), F32)

    @pl.when(t > 0)
    def _():
        zbuf[0:CONV_HALO, :] = zbuf[tt:tt + CONV_HALO, :]

    zbuf[CONV_HALO:CONV_HALO + tt, :] = z_ref[...].astype(F32)

    span = CONV_ROW_CHUNK + CONV_HALO
    n_row_chunks = tt // CONV_ROW_CHUNK
    n_strips = d // CONV_LANES

    def strip(n, carry):
        r0 = pl.multiple_of((n % n_row_chunks) * CONV_ROW_CHUNK, CONV_ROW_CHUNK)
        c0 = pl.multiple_of((n // n_row_chunks) * CONV_LANES, CONV_LANES)
        cols = pl.ds(c0, CONV_LANES)
        blk = zbuf[pl.ds(r0, span), cols]
        acc = jnp.zeros((CONV_ROW_CHUNK, CONV_LANES), F32)
        for b in range(8):
            rb = blk if b == 0 else pltpu.roll(blk, span - b, axis=0)
            for a in range(5):
                o = 8 * a + b
                if 2 <= o <= CONV_HALO:
                    acc = acc + rb[8 * a:8 * a + CONV_ROW_CHUNK, :] * w_ref[pl.ds(o - 2, 1), cols]
        ubuf[pl.ds(r0, CONV_ROW_CHUNK), cols] = acc
        return carry

    lax.fori_loop(0, n_row_chunks * n_strips, strip, 0)

    ln_rows = 128

    def ln_chunk(c, carry):
        rows = pl.ds(pl.multiple_of(c * ln_rows, ln_rows), ln_rows)
        o_ref[rows, :] = _layer_norm_swish(ubuf[rows, :] + bdw_ref[...], lng_ref[...], lnb_ref[...])
        return carry

    lax.fori_loop(0, tt // ln_rows, ln_chunk, 0)


def _conv_prompt_call(z, w_dw, b_dw, ln_gain, ln_bias, *, n_batch, seq):
    d = z.shape[1]
    tt = ROW_TILE
    nt = seq // tt
    w_pad = jnp.pad(w_dw, ((0, 32 - CONV_WIDTH), (0, 0)))
    body = functools.partial(_conv_prompt_body, tt=tt, d=d)
    vec = pl.BlockSpec((1, d), lambda b, t: (0, 0))
    return pl.pallas_call(
        body,
        out_shape=jax.ShapeDtypeStruct((n_batch * seq, d), BF16),
        grid=(n_batch, nt),
        in_specs=[pl.BlockSpec((tt, d), lambda b, t: (b * nt + t, 0)),
                  pl.BlockSpec((32, d), lambda b, t: (0, 0)), vec, vec, vec],
        out_specs=pl.BlockSpec((tt, d), lambda b, t: (b * nt + t, 0)),
        scratch_shapes=[pltpu.VMEM((tt + CONV_HALO, d), F32), pltpu.VMEM((tt, d), F32)],
        compiler_params=_cparams(("arbitrary", "arbitrary")),
        name="conv_prompt",
    )(z, w_pad, b_dw, ln_gain, ln_bias)


def _conv_sample_body(z_ref, c_ref, w_ref, bdw_ref, lng_ref, lnb_ref, o_ref, zbuf, *, d):
    s = SAMPLE_ROWS
    zbuf[0:CONV_STATE, :] = c_ref[...]
    zbuf[CONV_STATE:CONV_STATE + s, :] = z_ref[...].astype(F32)
    acc = jnp.zeros((s, d), F32)
    for j in range(CONV_WIDTH):
        acc = acc + zbuf[j:j + s, :] * w_ref[j:j + 1, :]
    o_ref[...] = _layer_norm_swish(acc + bdw_ref[...], lng_ref[...], lnb_ref[...])


def _conv_sample_call(z, cache, w_dw, b_dw, ln_gain, ln_bias, *, n_prompt_rows, n_seq):
    d = z.shape[1]
    s = SAMPLE_ROWS
    base = n_prompt_rows // s
    body = functools.partial(_conv_sample_body, d=d)
    vec = pl.BlockSpec((1, d), lambda b: (0, 0))
    return pl.pallas_call(
        body,
        out_shape=jax.ShapeDtypeStruct((n_seq * s, d), BF16),
        grid=(n_seq,),
        in_specs=[pl.BlockSpec((s, d), lambda b: (base + b, 0)),
                  pl.BlockSpec((None, CONV_STATE, d), lambda b: (b, 0, 0)),
                  pl.BlockSpec((CONV_WIDTH, d), lambda b: (0, 0)), vec, vec, vec],
        out_specs=pl.BlockSpec((s, d), lambda b: (b, 0)),
        scratch_shapes=[pltpu.VMEM((CONV_STATE + s, d), F32)],
        compiler_params=_cparams(("arbitrary",)),
        name="conv_sample",
    )(z, cache, w_dw, b_dw, ln_gain, ln_bias)


def _t5_bucket(dist):
    n = jnp.maximum(dist, 0)
    max_exact = REL_BUCKETS // 2
    nf = jnp.maximum(n, 1).astype(F32)
    large = max_exact + (jnp.log(nf / max_exact) / math.log(REL_MAX_DIST / max_exact)
                         * (REL_BUCKETS - max_exact)).astype(jnp.int32)
    large = jnp.minimum(large, REL_BUCKETS - 1)
    return jnp.where(n < max_exact, n, large)


def _bias_table(rel_bias, dist, valid):
    onehot = (_t5_bucket(dist).reshape(-1)[None, :] == jnp.arange(REL_BUCKETS, dtype=jnp.int32)[:, None]).astype(F32)
    tbl = jnp.dot(rel_bias.astype(F32).T, onehot, precision=lax.Precision.HIGHEST)
    tbl = jnp.where(valid.reshape(-1)[None, :], tbl, NEG_INF)
    return tbl.reshape(-1, dist.shape[-1])


def _softmax_sink_pv(s, sink, vals):
    m = jnp.maximum(jnp.max(s, axis=-1, keepdims=True), sink)
    p = jnp.exp(s - m).astype(BF16)
    den = jnp.dot(p, jnp.ones(vals.shape, BF16), preferred_element_type=F32) + jnp.exp(sink - m)
    o = jnp.dot(p, vals, preferred_element_type=F32)
    return o / den


def _swa_prompt_body(sink_ref, q_ref, kvp_ref, kvc_ref, bias_ref, o_ref, *, hd):
    i = pl.program_id(1)
    blk = WINDOW
    g = ATT_GROUP
    nkv = ATT_KV_HEADS
    nk = nkv * hd
    col = lax.broadcasted_iota(jnp.int32, (1, 2 * blk), 1)
    first_mask = jnp.where(jnp.logical_and(i == 0, col < blk), NEG_INF, 0.0)
    for hk in range(nkv):
        keys = jnp.concatenate([kvp_ref[:, hk * hd:(hk + 1) * hd], kvc_ref[:, hk * hd:(hk + 1) * hd]], axis=0)
        vals = jnp.concatenate([kvp_ref[:, nk + hk * hd:nk + (hk + 1) * hd],
                                kvc_ref[:, nk + hk * hd:nk + (hk + 1) * hd]], axis=0)
        qs = jnp.concatenate([q_ref[:, (hk * g + gg) * hd:(hk * g + gg + 1) * hd] for gg in range(g)], axis=0)
        s = lax.dot_general(qs, keys, (((1,), (1,)), ((), ())), preferred_element_type=F32)
        s = s + bias_ref[hk * g * blk:(hk + 1) * g * blk, :] + first_mask
        sink = jnp.concatenate([jnp.full((blk, 1), sink_ref[hk * g + gg], F32) for gg in range(g)], axis=0)
        o = _softmax_sink_pv(s, sink, vals)
        for gg in range(g):
            o_ref[:, (hk * g + gg) * hd:(hk * g + gg + 1) * hd] = o[gg * blk:(gg + 1) * blk, :].astype(BF16)


def _swa_prompt_call(proj, sinks, rel_bias, *, n_batch, seq, hd):
    blk = WINDOW
    nb = seq // blk
    nq = ATT_Q_HEADS * hd
    nkv2 = 2 * ATT_KV_HEADS * hd
    kvcol = nq // nkv2
    i_idx = jnp.arange(blk, dtype=jnp.int32)[:, None]
    j_idx = jnp.arange(2 * blk, dtype=jnp.int32)[None, :]
    dist = blk + i_idx - j_idx
    bias = _bias_table(rel_bias, dist, (dist >= 0) & (dist < WINDOW))
    body = functools.partial(_swa_prompt_body, hd=hd)
    return pl.pallas_call(
        body,
        out_shape=jax.ShapeDtypeStruct((n_batch * seq, nq), BF16),
        grid=(n_batch, nb),
        in_specs=[
            pl.BlockSpec(memory_space=pltpu.SMEM),
            pl.BlockSpec((blk, nq), lambda b, i: (b * nb + i, 0)),
            pl.BlockSpec((blk, nkv2), lambda b, i: (b * nb + jnp.maximum(i - 1, 0), kvcol)),
            pl.BlockSpec((blk, nkv2), lambda b, i: (b * nb + i, kvcol)),
            pl.BlockSpec((ATT_Q_HEADS * blk, 2 * blk), lambda b, i: (0, 0)),
        ],
        out_specs=pl.BlockSpec((blk, nq), lambda b, i: (b * nb + i, 0)),
        compiler_params=_cparams(("arbitrary", "arbitrary")),
        name="swa_prompt",
    )(sinks, proj, proj, proj, bias)


def _swa_sample_body(sink_ref, p_ref, ck_ref, cv_ref, bias_ref, o_ref, *, hd):
    s = SAMPLE_ROWS
    g = ATT_GROUP
    nkv = ATT_KV_HEADS
    nq = ATT_Q_HEADS * hd
    for hk in range(nkv):
        keys = jnp.concatenate([ck_ref[:, hk * hd:(hk + 1) * hd].astype(BF16),
                                p_ref[:, nq + hk * hd:nq + (hk + 1) * hd]], axis=0)
        vals = jnp.concatenate([cv_ref[:, hk * hd:(hk + 1) * hd].astype(BF16),
                                p_ref[:, nq + (nkv + hk) * hd:nq + (nkv + hk + 1) * hd]], axis=0)
        qs = jnp.concatenate([p_ref[:, (hk * g + gg) * hd:(hk * g + gg + 1) * hd] for gg in range(g)], axis=0)
        sc = lax.dot_general(qs, keys, (((1,), (1,)), ((), ())), preferred_element_type=F32)
        sc = sc + bias_ref[hk * g * s:(hk + 1) * g * s, :]
        sink = jnp.concatenate([jnp.full((s, 1), sink_ref[hk * g + gg], F32) for gg in range(g)], axis=0)
        o = _softmax_sink_pv(sc, sink, vals)
        for gg in range(g):
            o_ref[:, (hk * g + gg) * hd:(hk * g + gg + 1) * hd] = o[gg * s:(gg + 1) * s, :].astype(BF16)


def _swa_sample_call(proj, cache_k, cache_v, sinks, rel_bias, *, n_prompt_rows, n_seq, dec_seq, hd):
    s = SAMPLE_ROWS
    nq = ATT_Q_HEADS * hd
    nkv = ATT_KV_HEADS * hd
    sc = cache_k.shape[1]
    i_idx = jnp.arange(s, dtype=jnp.int32)[:, None]
    j_idx = jnp.arange(sc + s, dtype=jnp.int32)[None, :]
    dist = sc + i_idx - j_idx
    valid = (dist >= 0) & (dist < WINDOW) & (j_idx < sc + dec_seq)
    bias = _bias_table(rel_bias, dist, valid)
    base = n_prompt_rows // s
    body = functools.partial(_swa_sample_body, hd=hd)
    return pl.pallas_call(
        body,
        out_shape=jax.ShapeDtypeStruct((n_seq * s, nq), BF16),
        grid=(n_seq,),
        in_specs=[
            pl.BlockSpec(memory_space=pltpu.SMEM),
            pl.BlockSpec((s, proj.shape[1]), lambda b: (base + b, 0)),
            pl.BlockSpec((None, sc, nkv), lambda b: (b, 0, 0)),
            pl.BlockSpec((None, sc, nkv), lambda b: (b, 0, 0)),
            pl.BlockSpec((ATT_Q_HEADS * s, sc + s), lambda b: (0, 0)),
        ],
        out_specs=pl.BlockSpec((s, nq), lambda b: (b, 0)),
        compiler_params=_cparams(("arbitrary",)),
        name="swa_sample",
    )(sinks, proj, cache_k, cache_v, bias)


MOE_TILE = 256


def _moe_body(src_ref, dst_ref, te_ref, nu_ref, h_hbm, win_ref, wout_ref, y_hbm, xg, yb, wib, wob, gsem, ssem,
              *, ff):
    t = pl.program_id(0)
    nt = pl.num_programs(0)
    n_used = nu_ref[0]
    tm = MOE_TILE
    slot = lax.rem(t, 2)

    def gather_start(tile, sl):
        base = tile * tm
        for r in range(tm):
            pltpu.make_async_copy(h_hbm.at[pl.ds(src_ref[base + r], 1), :], xg.at[sl, pl.ds(r, 1), :],
                                  gsem.at[sl]).start()

    def gather_wait(sl):
        pltpu.make_async_copy(h_hbm.at[pl.ds(0, tm), :], xg.at[sl], gsem.at[sl]).wait()

    def scatter_start(tile, sl):
        base = tile * tm
        for r in range(tm):
            pltpu.make_async_copy(yb.at[sl, pl.ds(r, 1), :], y_hbm.at[pl.ds(dst_ref[base + r], 1), :],
                                  ssem.at[sl]).start()

    def scatter_wait(sl):
        pltpu.make_async_copy(yb.at[sl], y_hbm.at[pl.ds(0, tm), :], ssem.at[sl]).wait()

    valid = t < n_used

    @pl.when(t == 0)
    def _():
        gather_start(0, 0)

    @pl.when(jnp.logical_and(valid, jnp.logical_or(t == 0, te_ref[t] != te_ref[jnp.maximum(t - 1, 0)])))
    def _():
        wib[...] = win_ref[...].astype(BF16)
        wob[...] = wout_ref[...].astype(BF16)

    def step(sl):
        @pl.when(t >= 2)
        def _():
            scatter_wait(sl)

        gather_wait(sl)

        @pl.when(t + 1 < n_used)
        def _():
            gather_start(t + 1, 1 - sl)

        gu = jnp.dot(_unpack_bf16_pairs(xg[sl]).astype(BF16), wib[...], preferred_element_type=F32)
        gate, up = gu[:, :ff], gu[:, ff:]
        act = (gate * jax.nn.sigmoid(gate) * up).astype(BF16)
        yb[sl] = _pack_bf16_pairs(jnp.dot(act, wob[...], preferred_element_type=F32))
        scatter_start(t, sl)

    for sl in range(2):
        pl.when(jnp.logical_and(valid, slot == sl))(functools.partial(step, sl))

    @pl.when(t == nt - 1)
    def _():
        scatter_wait(lax.rem(n_used - 1, 2))
        scatter_wait(lax.rem(n_used, 2))


def _moe_layer(h2, route, counts, w_in, w_out, layer):
    r = h2.shape[0]
    depth, n_exp, d, ff2 = w_in.shape
    ff = ff2 // 2
    tm = MOE_TILE
    n_tiles = 2 * r // tm + n_exp
    n_slots = n_tiles * tm

    cnt = counts[0, MOE_GROUPS:MOE_GROUPS + n_exp].astype(jnp.int32)
    padded = ((cnt + tm - 1) // tm) * tm
    ends = jnp.cumsum(padded)
    starts = ends - padded
    rt = route[:, :ROUTE_R2 + 1].astype(jnp.int32)
    pos = jnp.concatenate([starts[rt[:, ROUTE_E1]] + rt[:, ROUTE_R1], starts[rt[:, ROUTE_E2]] + rt[:, ROUTE_R2]])
    tok = jnp.arange(r, dtype=jnp.int32)
    dst = (2 * r + jnp.arange(n_slots, dtype=jnp.int32)).at[pos].set(
        jnp.concatenate([tok, r + tok]), unique_indices=True, indices_are_sorted=False)
    src = jnp.where(dst < r, dst, jnp.where(dst < 2 * r, dst - r, 0))
    tile_start = jnp.arange(n_tiles, dtype=jnp.int32) * tm
    tile_expert = jnp.minimum(jnp.sum(tile_start[:, None] >= ends[None, :], axis=1), n_exp - 1).astype(jnp.int32)
    n_used = (ends[-1:] // tm).astype(jnp.int32)

    return pl.pallas_call(
        functools.partial(_moe_body, ff=ff),
        out_shape=jax.ShapeDtypeStruct((2 * r + n_slots, d // 2), jnp.uint32),
        grid_spec=pltpu.PrefetchScalarGridSpec(
            num_scalar_prefetch=4,
            grid=(n_tiles,),
            in_specs=[
                pl.BlockSpec(memory_space=pl.ANY),
                pl.BlockSpec((None, None, d, ff2), lambda t, src, dst, te, nu: (layer, te[t], 0, 0)),
                pl.BlockSpec((None, None, ff, d), lambda t, src, dst, te, nu: (layer, te[t], 0, 0)),
            ],
            out_specs=pl.BlockSpec(memory_space=pl.ANY),
            scratch_shapes=[pltpu.VMEM((2, tm, d // 2), jnp.uint32), pltpu.VMEM((2, tm, d // 2), jnp.uint32),
                            pltpu.VMEM((d, ff2), BF16), pltpu.VMEM((ff, d), BF16),
                            pltpu.SemaphoreType.DMA((2,)), pltpu.SemaphoreType.DMA((2,))],
        ),
        compiler_params=_cparams(("arbitrary",)),
        name="moe_experts",
    )(src, dst, tile_expert, n_used, h2, w_in, w_out)


def kernel(x_prompt, x_sample, c_prompt, c_sample, state_ret, cache_conv, cache_swa_k, cache_swa_v, ada_w, ada_b, norm_gain, ret_w_in, ret_gn_gain, ret_gn_bias, ret_w_out, conv_w_pw1, conv_b_pw1, conv_w_dw, conv_b_dw, conv_ln_gain, conv_ln_bias, conv_w_pw2, conv_b_pw2, att_w_qkv, att_q_gain, att_k_gain, att_sinks, att_w_o, rel_bias, moe_wg, moe_bg, moe_we, moe_be, moe_w_in, moe_w_out):
    n_batch, seq, d = x_prompt.shape
    n_seq, dec_seq, _ = x_sample.shape
    depth = ada_w.shape[0]
    s = SAMPLE_ROWS
    assert n_seq * s == ROW_TILE and seq % ROW_TILE == 0 and dec_seq <= s
    n_prompt_rows = n_batch * seq
    seq_tiles = seq // ROW_TILE
    dk = ret_w_in.shape[2] // (6 * RET_HEADS)
    dv = 2 * dk
    hd = d // ATT_Q_HEADS
    geom = dict(seq_tiles=seq_tiles, n_batch=n_batch)

    xs_pad = jnp.pad(x_sample, ((0, 0), (0, s - dec_seq), (0, 0))).reshape(n_seq * s, d)
    x = jnp.concatenate([x_prompt.reshape(n_prompt_rows, d), xs_pad], axis=0)

    n_c = n_batch + n_seq
    c_rows = ((n_c + 7) // 8) * 8
    c_all = jnp.pad(jnp.concatenate([c_prompt, c_sample], axis=0), ((0, c_rows - n_c), (0, 0)))
    mod = _ada_call(c_all, ada_w, ada_b)
    mod4 = mod.reshape(depth, c_rows, 1, 6 * d)
    modtok = jnp.repeat(mod[:, n_batch:n_c], s, axis=1)

    gains = norm_gain.astype(F32).reshape(2 * depth, 1, d)
    gng_all = ret_gn_gain.astype(F32)[:, None, :]
    gnb_all = ret_gn_bias.astype(F32)[:, None, :]
    state_all = state_ret.astype(F32)

    ret_p, conv_p, conv_s, kp_l, vp_l, ks_l, vs_l = [], [], [], [], [], [], []
    ret_s = None
    (h,) = _norm_call(x, mod4, modtok, norm=(gains, 0, 0, 0, 1), **geom)
    for l in range(depth):
        kind, j = l % 3, l // 3
        resid1 = (x, mod4, modtok, l, 2)
        if kind == 0:
            proj = _linear_call(h, ret_w_in, j, n_out=ret_w_in.shape[2], tn=2048, out_dtype=BF16,
                                name="ret_in", **geom)
            a_p, st_p = _ret_prompt_call(proj, gng_all, gnb_all, j, n_batch=n_batch, seq=seq, dk=dk, dv=dv)
            a_s, ret_s = _ret_sample_call(proj, state_all, gng_all, gnb_all, j, ret_s, n_prompt_rows=n_prompt_rows,
                                          n_seq=n_seq, dec_seq=dec_seq, dk=dk, dv=dv)
            ret_p.append(st_p)
            x = _linear_call(a_p, ret_w_out, j, h_sample=a_s, n_out=d, tn=512, out_dtype=F32, resid=resid1,
                             name="ret_out", **geom)
        elif kind == 1:
            z = _linear_call(h, conv_w_pw1, j, n_out=d, tn=1024, out_dtype=BF16, bias=conv_b_pw1,
                             glu=True, name="conv_pw1", **geom)
            cargs = (conv_w_dw[j], conv_b_dw[j][None], conv_ln_gain[j][None], conv_ln_bias[j][None])
            a_p = _conv_prompt_call(z, *cargs, n_batch=n_batch, seq=seq)
            a_s = _conv_sample_call(z, cache_conv[j].astype(F32), *cargs, n_prompt_rows=n_prompt_rows, n_seq=n_seq)
            z_tail = z[:n_prompt_rows].reshape(n_batch, seq, d)[:, seq - CONV_STATE:].astype(F32)
            conv_p.append(z_tail)
            z_new = z[n_prompt_rows:].reshape(n_seq, s, d)[:, :dec_seq].astype(F32)
            conv_s.append(jnp.concatenate([cache_conv[j].astype(F32), z_new], axis=1)[:, -CONV_STATE:])
            x = _linear_call(a_p, conv_w_pw2, j, h_sample=a_s, n_out=d, tn=1024, out_dtype=F32,
                             bias=conv_b_pw2, resid=resid1, name="conv_pw2", **geom)
        else:
            nkv = ATT_KV_HEADS * hd
            nq = ATT_Q_HEADS * hd
            hgain = jnp.concatenate([jnp.tile(att_q_gain[j].astype(F32) * (hd ** -0.5), ATT_Q_HEADS),
                                     jnp.tile(att_k_gain[j].astype(F32), ATT_KV_HEADS), jnp.ones((nkv,), F32)])[None]
            hmask = jnp.concatenate([jnp.ones((nq + nkv,), F32), jnp.zeros((nkv,), F32)])[None]
            proj = _linear_call(h, att_w_qkv, j, n_out=att_w_qkv.shape[2], tn=1280, out_dtype=BF16,
                                head_norm=(hgain, hmask, hd), name="att_qkv", **geom)
            sinks = att_sinks[j].astype(F32)
            a_p = _swa_prompt_call(proj, sinks, rel_bias, n_batch=n_batch, seq=seq, hd=hd)
            win = cache_swa_k.shape[2]
            ck = cache_swa_k[j].astype(F32).reshape(n_seq, win, nkv)
            cv = cache_swa_v[j].astype(F32).reshape(n_seq, win, nkv)
            a_s = _swa_sample_call(proj, ck, cv, sinks, rel_bias, n_prompt_rows=n_prompt_rows,
                                   n_seq=n_seq, dec_seq=dec_seq, hd=hd)
            kv_tail = proj[:n_prompt_rows, nq:].reshape(n_batch, seq, 2 * nkv)[:, seq - WINDOW:].astype(F32)
            kp_l.append(kv_tail[:, :, :nkv].reshape(n_batch, WINDOW, ATT_KV_HEADS, hd))
            vp_l.append(kv_tail[:, :, nkv:].reshape(n_batch, WINDOW, ATT_KV_HEADS, hd))
            kv_new = proj[n_prompt_rows:, nq:].reshape(n_seq, s, 2 * nkv)[:, :dec_seq].astype(F32)
            k_new, v_new = kv_new[:, :, :nkv], kv_new[:, :, nkv:]
            ks_l.append(jnp.concatenate([ck, k_new], axis=1)[:, -win:].reshape(n_seq, win, ATT_KV_HEADS, hd))
            vs_l.append(jnp.concatenate([cv, v_new], axis=1)[:, -win:].reshape(n_seq, win, ATT_KV_HEADS, hd))
            x = _linear_call(a_p, att_w_o, j, h_sample=a_s, n_out=d, tn=1024, out_dtype=F32, resid=resid1,
                             name="att_out", **geom)

        lane_pad = ROUTER_LANES - MOE_GROUPS - MOE_GROUPS * MOE_EPG
        router_w = jnp.pad(jnp.concatenate([moe_wg[l], moe_we[l]], axis=1).astype(F32), ((0, 0), (0, lane_pad)))
        router_hi = router_w.astype(BF16)
        router_w = jnp.stack([router_hi, (router_w - router_hi.astype(F32)).astype(BF16)])
        router_b = jnp.pad(jnp.concatenate([moe_bg[l], moe_be[l]]).astype(F32), (0, lane_pad))[None]
        h2, route, counts = _norm_call(x, mod4, modtok, norm=(gains, 2 * l + 1, l, 3, 4),
                                       router=(router_w, router_b), **geom)
        ypair = _moe_layer(h2, route, counts, moe_w_in, moe_w_out, l)
        if l + 1 < depth:
            x, h = _norm_call(x, mod4, modtok, resid=(ypair, route, l, 5),
                              norm=(gains, 2 * l + 2, l + 1, 0, 1), **geom)
        else:
            x_p, x_s = _norm_call(x, mod4, modtok, resid=(ypair, route, l, 5), split_out=True, **geom)

    y_prompt = x_p.reshape(n_batch, seq, d)
    y_sample = x_s.reshape(n_seq, s, d)[:, :dec_seq]
    return (y_prompt, y_sample, jnp.stack(ret_p), ret_s, jnp.stack(conv_p), jnp.stack(conv_s),
            jnp.stack(kp_l), jnp.stack(vp_l), jnp.stack(ks_l), jnp.stack(vs_l))
```

```python
import functools
import math

import jax
import jax.numpy as jnp
from jax import lax
from jax.experimental import pallas as pl
from jax.experimental.pallas import tpu as pltpu

F32 = jnp.float32
BF16 = jnp.bfloat16

NORM_EPS = 1e-6
NEG_INF = -1e30
ROPE_BASE = 10000.0
PAST_LEN = 16384

RET_HEADS = 8
RET_CHUNK = 128
CONV_WIDTH = 31
CONV_STATE = CONV_WIDTH - 1
ATT_Q_HEADS = 32
ATT_KV_HEADS = 4
ATT_GROUP = ATT_Q_HEADS // ATT_KV_HEADS
WINDOW = 128
REL_BUCKETS = 32
REL_MAX_DIST = 128
MOE_GROUPS = 4
MOE_EPG = 4
MOE_TOPK = 2

ROW_TILE = 512
SAMPLE_ROWS = 16
ROUTER_LANES = 128
VMEM_LIMIT_BYTES = 56 * 1024 * 1024


def _cparams(sem):
    return pltpu.CompilerParams(dimension_semantics=sem, vmem_limit_bytes=VMEM_LIMIT_BYTES)


def _pack_bf16_pairs(y):
    half = y.shape[1] // 2
    hi = lax.bitcast_convert_type(y[:, :half].astype(BF16).astype(F32), jnp.uint32)
    lo = lax.bitcast_convert_type(y[:, half:].astype(BF16).astype(F32), jnp.uint32)
    return hi | lax.shift_right_logical(lo, jnp.uint32(16))


def _unpack_bf16_pairs(w):
    hi = lax.bitcast_convert_type(w & jnp.uint32(0xFFFF0000), F32)
    lo = lax.bitcast_convert_type(lax.shift_left(w, jnp.uint32(16)), F32)
    return jnp.concatenate([hi, lo], axis=1)


def _ada_body(c_ref, w_ref, b_ref, o_ref):
    c = c_ref[...]
    s = (c * jax.nn.sigmoid(c)).astype(BF16)
    o_ref[...] = jnp.dot(s, w_ref[...].astype(BF16), preferred_element_type=F32) + b_ref[...]


def _ada_call(c_all, ada_w, ada_b):
    depth, d, n = ada_w.shape
    rows = c_all.shape[0]
    tn = 1024
    return pl.pallas_call(
        _ada_body,
        out_shape=jax.ShapeDtypeStruct((depth, rows, n), F32),
        grid=(depth, n // tn),
        in_specs=[
            pl.BlockSpec((rows, d), lambda l, j: (0, 0)),
            pl.BlockSpec((None, d, tn), lambda l, j: (l, 0, j)),
            pl.BlockSpec((None, 1, tn), lambda l, j: (l, 0, j)),
        ],
        out_specs=pl.BlockSpec((None, rows, tn), lambda l, j: (l, 0, j)),
        compiler_params=_cparams(("arbitrary", "arbitrary")),
        name="ada_mod",
    )(c_all, ada_w, ada_b.reshape(depth, 1, n))


ROUTE_E1, ROUTE_E2, ROUTE_W1, ROUTE_W2, ROUTE_R1, ROUTE_R2 = range(6)


def _route_tile(logits, rb, carry):
    tm = logits.shape[0]
    g, epg = MOE_GROUPS, MOE_EPG
    lg = logits + rb
    lane = lax.broadcasted_iota(jnp.int32, lg.shape, 1)
    lane_f = lane.astype(F32)

    def first_lane(mask):
        return jnp.min(jnp.where(mask, lane_f, float(ROUTER_LANES)), axis=-1, keepdims=True).astype(jnp.int32)

    is_g = lane < g
    mg = jnp.max(jnp.where(is_g, lg, NEG_INF), axis=-1, keepdims=True)
    eg = jnp.where(is_g, jnp.exp(lg - mg), 0.0)
    p_grp = 1.0 / jnp.sum(eg, axis=-1, keepdims=True)
    grp = first_lane(jnp.logical_and(is_g, lg == mg))
    lo = g + epg * grp
    is_e = jnp.logical_and(lane >= lo, lane < lo + epg)
    me = jnp.max(jnp.where(is_e, lg, NEG_INF), axis=-1, keepdims=True)
    ee = jnp.where(is_e, jnp.exp(lg - me), 0.0)
    se = jnp.sum(ee, axis=-1, keepdims=True)
    i1 = first_lane(jnp.logical_and(is_e, lg == me))
    rest = jnp.logical_and(is_e, lane != i1)
    m2 = jnp.max(jnp.where(rest, ee, -1.0), axis=-1, keepdims=True)
    i2 = first_lane(jnp.logical_and(rest, ee == m2))
    p1 = 1.0 / se
    p2 = m2 / se
    w1 = p_grp * p1 / (p1 + p2)
    w2 = p_grp * p2 / (p1 + p2)
    sel1 = lane == i1
    sel2 = lane == i2
    onehot = jnp.where(jnp.logical_or(sel1, sel2), 1.0, 0.0)
    row = lax.broadcasted_iota(jnp.int32, (tm, tm), 0)
    col = lax.broadcasted_iota(jnp.int32, (tm, tm), 1)
    tri = jnp.where(col <= row, 1.0, 0.0).astype(BF16)
    incl = jnp.dot(tri, onehot.astype(BF16), preferred_element_type=F32)
    rank = incl - 1.0 + carry
    r1 = jnp.sum(jnp.where(sel1, rank, 0.0), axis=-1, keepdims=True)
    r2 = jnp.sum(jnp.where(sel2, rank, 0.0), axis=-1, keepdims=True)
    rec = jnp.zeros_like(lg)
    for ln, val in ((ROUTE_E1, (i1 - g).astype(F32)), (ROUTE_E2, (i2 - g).astype(F32)), (ROUTE_W1, w1),
                    (ROUTE_W2, w2), (ROUTE_R1, r1), (ROUTE_R2, r2)):
        rec = jnp.where(lane == ln, val, rec)
    return rec, carry + jnp.sum(onehot, axis=0, keepdims=True)


def _norm_body(*refs, n_prompt_tiles, has_resid, has_norm, has_router, split_out):
    it = iter(refs)
    x_ref = next(it)
    if has_resid:
        y0_ref, y1_ref, rt_ref, grow_ref, gtok_ref = next(it), next(it), next(it), next(it), next(it)
    if has_norm:
        gain_ref, shrow_ref, scrow_ref, shtok_ref, sctok_ref = next(it), next(it), next(it), next(it), next(it)
    if has_router:
        wr_ref, rb_ref = next(it), next(it)
    if has_resid:
        xo_ref = next(it)
        if split_out:
            xs_ref = next(it)
    if has_norm:
        h_ref = next(it)
    if has_router:
        rec_ref, cnt_ref, carry = next(it), next(it), next(it)

    is_sample = pl.program_id(0) >= n_prompt_tiles

    if has_router:
        @pl.when(pl.program_id(0) == 0)
        def _():
            carry[...] = jnp.zeros_like(carry)

    def tile(sample):
        x = x_ref[...]
        if has_resid:
            gate = gtok_ref[...] if sample else grow_ref[...]
            rt = rt_ref[...]
            y = (rt[:, ROUTE_W1:ROUTE_W1 + 1] * _unpack_bf16_pairs(y0_ref[...])
                 + rt[:, ROUTE_W2:ROUTE_W2 + 1] * _unpack_bf16_pairs(y1_ref[...]))
            x = x + gate * y
            if split_out and sample:
                xs_ref[...] = x
            else:
                xo_ref[...] = x
        if has_norm:
            scale = sctok_ref[...] if sample else scrow_ref[...]
            shift = shtok_ref[...] if sample else shrow_ref[...]
            ms = jnp.mean(x * x, axis=-1, keepdims=True)
            h = (x * lax.rsqrt(ms + NORM_EPS)) * (gain_ref[...] * (1.0 + scale)) + shift
            if has_router:
                h_ref[...] = _pack_bf16_pairs(h)
            else:
                h_ref[...] = h.astype(h_ref.dtype)
            if has_router:
                h_hi = h.astype(BF16)
                h_lo = (h - h_hi.astype(F32)).astype(BF16)
                logits = (jnp.dot(h_hi, wr_ref[0], preferred_element_type=F32)
                          + jnp.dot(h_lo, wr_ref[0], preferred_element_type=F32)
                          + jnp.dot(h_hi, wr_ref[1], preferred_element_type=F32))
                rec, new_carry = _route_tile(logits, rb_ref[...], carry[0:1, :])
                rec_ref[...] = rec
                carry[...] = jnp.broadcast_to(new_carry, carry.shape)
                cnt_ref[...] = carry[...]

    pl.when(jnp.logical_not(is_sample))(functools.partial(tile, False))
    pl.when(is_sample)(functools.partial(tile, True))


def _norm_call(x, mod4, modtok, *, seq_tiles, n_batch, resid=None, norm=None, router=None, split_out=False):
    r, d = x.shape
    n_tiles = r // ROW_TILE
    n_prompt_tiles = n_tiles - 1

    def row_spec(layer, col):
        return pl.BlockSpec((None, None, 1, d),
                            lambda i: (layer, jnp.minimum(i // seq_tiles, n_batch - 1), 0, col))

    def tok_spec(layer, col):
        return pl.BlockSpec((None, ROW_TILE, d), lambda i: (layer, 0, col))

    tile = pl.BlockSpec((ROW_TILE, d), lambda i: (i, 0))
    args, in_specs, out_shape, out_specs = [x], [tile], [], []
    scratch = []
    if resid is not None:
        ypair, route, layer, gcol = resid
        args += [ypair, ypair, route, mod4, modtok]
        in_specs += [pl.BlockSpec((ROW_TILE, d // 2), lambda i: (i, 0)),
                     pl.BlockSpec((ROW_TILE, d // 2), lambda i: (n_tiles + i, 0)),
                     pl.BlockSpec((ROW_TILE, ROUTER_LANES), lambda i: (i, 0)),
                     row_spec(layer, gcol), tok_spec(layer, gcol)]
        if split_out:
            out_shape += [jax.ShapeDtypeStruct((r - ROW_TILE, d), F32), jax.ShapeDtypeStruct((ROW_TILE, d), F32)]
            out_specs += [pl.BlockSpec((ROW_TILE, d), lambda i: (jnp.minimum(i, n_prompt_tiles - 1), 0)),
                          pl.BlockSpec((ROW_TILE, d), lambda i: (0, 0))]
        else:
            out_shape.append(jax.ShapeDtypeStruct((r, d), F32))
            out_specs.append(tile)
    if norm is not None:
        gains, gidx, layer, shcol, sccol = norm
        args += [gains, mod4, mod4, modtok, modtok]
        in_specs += [pl.BlockSpec((None, 1, d), lambda i: (gidx, 0, 0)), row_spec(layer, shcol),
                     row_spec(layer, sccol), tok_spec(layer, shcol), tok_spec(layer, sccol)]
        if router is not None:
            out_shape.append(jax.ShapeDtypeStruct((r, d // 2), jnp.uint32))
            out_specs.append(pl.BlockSpec((ROW_TILE, d // 2), lambda i: (i, 0)))
        else:
            out_shape.append(jax.ShapeDtypeStruct((r, d), BF16))
            out_specs.append(tile)
        if router is not None:
            args += list(router)
            in_specs += [pl.BlockSpec((2, d, ROUTER_LANES), lambda i: (0, 0, 0)),
                         pl.BlockSpec((1, ROUTER_LANES), lambda i: (0, 0))]
            out_shape += [jax.ShapeDtypeStruct((r, ROUTER_LANES), F32), jax.ShapeDtypeStruct((8, ROUTER_LANES), F32)]
            out_specs += [pl.BlockSpec((ROW_TILE, ROUTER_LANES), lambda i: (i, 0)),
                          pl.BlockSpec((8, ROUTER_LANES), lambda i: (0, 0))]
            scratch.append(pltpu.VMEM((8, ROUTER_LANES), F32))
    body = functools.partial(_norm_body, n_prompt_tiles=n_prompt_tiles, has_resid=resid is not None,
                             has_norm=norm is not None, has_router=router is not None, split_out=split_out)
    return pl.pallas_call(
        body, out_shape=out_shape, grid=(n_tiles,), in_specs=in_specs, out_specs=out_specs,
        scratch_shapes=scratch, compiler_params=_cparams(("arbitrary",)), name="mod_norm",
    )(*args)


def _head_rms_norm(x, gain_row, seg_ref, exp_ref, hd):
    ssq = jnp.dot((x * x).astype(BF16), seg_ref[...], preferred_element_type=F32)
    r = lax.rsqrt(ssq * (1.0 / hd) + NORM_EPS)
    r_hi = r.astype(BF16)
    r_lo = (r - r_hi.astype(F32)).astype(BF16)
    scale = jnp.dot(jnp.concatenate([r_hi, r_lo], axis=1), exp_ref[...], preferred_element_type=F32)
    return x * scale * gain_row


def _linear_body(*refs, n_prompt_tiles, pair, glu, has_bias, has_resid, head_dim):
    it = iter(refs)
    h_ref = next(it)
    hs_ref = next(it) if pair else None
    w_ref = next(it)
    w2_ref = next(it) if glu else None
    b_ref = next(it) if has_bias else None
    b2_ref = next(it) if glu else None
    if has_resid:
        x_ref, grow_ref, gtok_ref = next(it), next(it), next(it)
    if head_dim:
        hgain_ref, hmask_ref, seg_ref, exp_ref = next(it), next(it), next(it), next(it)
    o_ref = next(it)
    wb = next(it)
    wb2 = next(it) if glu else None

    i = pl.program_id(1)
    is_sample = i >= n_prompt_tiles

    @pl.when(i == 0)
    def _():
        wb[...] = w_ref[...].astype(BF16)
        if glu:
            wb2[...] = w2_ref[...].astype(BF16)

    def compute(hv):
        acc = jnp.dot(hv, wb[...], preferred_element_type=F32)
        if has_bias:
            acc = acc + b_ref[...]
        if glu:
            acc2 = jnp.dot(hv, wb2[...], preferred_element_type=F32) + b2_ref[...]
            acc = acc * jax.nn.sigmoid(acc2)
        if has_resid:
            gate = jnp.where(is_sample, gtok_ref[...], grow_ref[...])
            acc = x_ref[...] + gate * acc
        if head_dim:
            normed = _head_rms_norm(acc, hgain_ref[...], seg_ref, exp_ref, head_dim)
            acc = jnp.where(hmask_ref[...] > 0.0, normed, acc)
        o_ref[...] = acc.astype(o_ref.dtype)

    if pair:
        @pl.when(jnp.logical_not(is_sample))
        def _():
            compute(h_ref[...])

        @pl.when(is_sample)
        def _():
            compute(hs_ref[...])
    else:
        compute(h_ref[...])


def _linear_call(h, w, wl, *, n_out, tn, out_dtype, seq_tiles, n_batch, h_sample=None, bias=None, glu=False,
                 resid=None, head_norm=None, name="linear"):
    k = w.shape[1]
    pair = h_sample is not None
    n_prompt_tiles = h.shape[0] // ROW_TILE - (0 if pair else 1)
    n_tiles = n_prompt_tiles + 1
    r = n_tiles * ROW_TILE
    nblk = n_out // tn

    args = [h]
    in_specs = [pl.BlockSpec((ROW_TILE, k), lambda j, i: (jnp.minimum(i, n_prompt_tiles - 1) if pair else i, 0))]
    if pair:
        args.append(h_sample)
        in_specs.append(pl.BlockSpec((ROW_TILE, k), lambda j, i: (0, 0)))
    args.append(w)
    in_specs.append(pl.BlockSpec((None, k, tn), lambda j, i: (wl, 0, j)))
    if glu:
        args.append(w)
        in_specs.append(pl.BlockSpec((None, k, tn), lambda j, i: (wl, 0, nblk + j)))
    if bias is not None:
        bias = bias.reshape(bias.shape[0], 1, bias.shape[1])
        args.append(bias)
        in_specs.append(pl.BlockSpec((None, 1, tn), lambda j, i: (wl, 0, j)))
        if glu:
            args.append(bias)
            in_specs.append(pl.BlockSpec((None, 1, tn), lambda j, i: (wl, 0, nblk + j)))
    if resid is not None:
        x, mod4, modtok, layer, gcol = resid
        cb = gcol * nblk
        args += [x, mod4, modtok]
        in_specs += [
            pl.BlockSpec((ROW_TILE, tn), lambda j, i: (i, j)),
            pl.BlockSpec((None, None, 1, tn),
                         lambda j, i: (layer, jnp.minimum(i // seq_tiles, n_batch - 1), 0, cb + j)),
            pl.BlockSpec((None, ROW_TILE, tn), lambda j, i: (layer, 0, cb + j)),
        ]
    head_dim = 0
    if head_norm is not None:
        hgain, hmask, head_dim = head_norm
        seg = (jnp.arange(tn, dtype=jnp.int32)[:, None] // head_dim
               == jnp.arange(128, dtype=jnp.int32)[None, :]).astype(BF16)
        args += [hgain, hmask, seg, jnp.concatenate([seg.T, seg.T], axis=0)]
        in_specs += [pl.BlockSpec((1, tn), lambda j, i: (0, j)), pl.BlockSpec((1, tn), lambda j, i: (0, j)),
                     pl.BlockSpec((tn, 128), lambda j, i: (0, 0)), pl.BlockSpec((256, tn), lambda j, i: (0, 0))]
    scratch = [pltpu.VMEM((k, tn), BF16)] + ([pltpu.VMEM((k, tn), BF16)] if glu else [])
    body = functools.partial(_linear_body, n_prompt_tiles=n_prompt_tiles, pair=pair, glu=glu,
                             has_bias=bias is not None, has_resid=resid is not None, head_dim=head_dim)
    return pl.pallas_call(
        body,
        out_shape=jax.ShapeDtypeStruct((r, n_out), out_dtype),
        grid=(nblk, n_tiles),
        in_specs=in_specs,
        out_specs=pl.BlockSpec((ROW_TILE, tn), lambda j, i: (i, j)),
        scratch_shapes=scratch,
        compiler_params=_cparams(("arbitrary", "arbitrary")),
        name=name,
    )(*args)


def _rotate(x, cos, sin):
    half = x.shape[-1] // 2
    x1, x2 = x[:, :half], x[:, half:]
    return jnp.concatenate([x1 * cos - x2 * sin, x1 * sin + x2 * cos], axis=-1)


def _group_norm_gate(o, g, gain, bias):
    mu = jnp.mean(o, axis=-1, keepdims=True)
    var = jnp.mean(jnp.square(o - mu), axis=-1, keepdims=True)
    on = (o - mu) * lax.rsqrt(var + NORM_EPS) * gain + bias
    g = g.astype(F32)
    return (g * jax.nn.sigmoid(g) * on).astype(BF16)


def _ret_prompt_body(q_ref, k_ref, v_ref, g_ref, cos_ref, sin_ref, dm_ref, qd_ref, kd_ref, gl_ref,
                     gng_ref, gnb_ref, o_ref, st_ref, s_acc, *, n_chunks):
    s_acc[...] = jnp.zeros_like(s_acc)
    dmask = dm_ref[...]
    qdec = qd_ref[...]
    kdec = kd_ref[...]
    gl = gl_ref[0:1, 0:1]
    gng = gng_ref[...]
    gnb = gnb_ref[...]

    def chunk(c, carry):
        r0 = pl.multiple_of(c * RET_CHUNK, RET_CHUNK)
        rows = pl.ds(r0, RET_CHUNK)
        cos = cos_ref[rows, :]
        sin = sin_ref[rows, :]
        qr = _rotate(q_ref[rows, :].astype(F32), cos, sin)
        kr = _rotate(k_ref[rows, :].astype(F32), cos, sin)
        v = v_ref[rows, :]
        state = s_acc[...]
        scores = lax.dot_general(qr.astype(BF16), kr.astype(BF16), (((1,), (1,)), ((), ())),
                                 preferred_element_type=F32)
        scores = scores * dmask
        out = jnp.dot(scores.astype(BF16), v, preferred_element_type=F32)
        out = out + jnp.dot((qr * qdec).astype(BF16), state.astype(BF16), preferred_element_type=F32)
        kv = lax.dot_general((kr * kdec).astype(BF16), v, (((0,), (0,)), ((), ())), preferred_element_type=F32)
        s_acc[...] = gl * state + kv
        o_ref[rows, :] = _group_norm_gate(out, g_ref[rows, :], gng, gnb)
        return carry

    lax.fori_loop(0, n_chunks, chunk, 0, unroll=4)
    st_ref[...] = s_acc[...]


def _ret_tables(chunk, dk, n_valid=None):
    n_valid = chunk if n_valid is None else n_valid
    lg = jnp.log1p(-jnp.exp2(-5.0 - jnp.arange(RET_HEADS, dtype=F32)))
    idx = jnp.arange(chunk, dtype=F32)
    diff = idx[:, None] - idx[None, :]
    inside = (idx[:, None] < n_valid) & (idx[None, :] < n_valid)
    dmask = jnp.where((diff[None] >= 0) & inside[None],
                      jnp.exp(jnp.maximum(diff, 0.0)[None] * lg[:, None, None]), 0.0) * (dk ** -0.5)
    qdec = jnp.exp((idx[None, :] + 1.0) * lg[:, None])
    kdec = jnp.where(idx[None, :] < n_valid, jnp.exp((n_valid - 1.0 - idx)[None, :] * lg[:, None]), 0.0) * (dk ** -0.5)
    gl = jnp.exp(n_valid * lg)
    qdec = jnp.broadcast_to(qdec[:, :, None], (RET_HEADS, chunk, dk))
    kdec = jnp.broadcast_to(kdec[:, :, None], (RET_HEADS, chunk, dk))
    gl = jnp.broadcast_to(gl[:, None, None], (RET_HEADS, 8, 128))
    return dmask, qdec, kdec, gl


def _rope_tables(pos, half):
    inv = ROPE_BASE ** (-jnp.arange(half, dtype=F32) / half)
    ang = pos.astype(F32)[:, None] * inv[None, :]
    return jnp.cos(ang), jnp.sin(ang)


def _ret_prompt_call(proj, gn_gain, gn_bias, j, *, n_batch, seq, dk, dv):
    h = RET_HEADS
    n_chunks = seq // RET_CHUNK
    cos, sin = _rope_tables(jnp.arange(seq, dtype=jnp.int32), dk // 2)
    dmask, qdec, kdec, gl = _ret_tables(RET_CHUNK, dk)
    kcol, vcol, gcol = h, (2 * h * dk) // dv, (2 * h * dk) // dv + h
    body = functools.partial(_ret_prompt_body, n_chunks=n_chunks)
    return pl.pallas_call(
        body,
        out_shape=[jax.ShapeDtypeStruct((n_batch * seq, h * dv), BF16),
                   jax.ShapeDtypeStruct((n_batch, h, dk, dv), F32)],
        grid=(n_batch, h),
        in_specs=[
            pl.BlockSpec((seq, dk), lambda b, hh: (b, hh)),
            pl.BlockSpec((seq, dk), lambda b, hh: (b, kcol + hh)),
            pl.BlockSpec((seq, dv), lambda b, hh: (b, vcol + hh)),
            pl.BlockSpec((seq, dv), lambda b, hh: (b, gcol + hh)),
            pl.BlockSpec((seq, dk // 2), lambda b, hh: (0, 0)),
            pl.BlockSpec((seq, dk // 2), lambda b, hh: (0, 0)),
            pl.BlockSpec((None, RET_CHUNK, RET_CHUNK), lambda b, hh: (hh, 0, 0)),
            pl.BlockSpec((None, RET_CHUNK, dk), lambda b, hh: (hh, 0, 0)),
            pl.BlockSpec((None, RET_CHUNK, dk), lambda b, hh: (hh, 0, 0)),
            pl.BlockSpec((None, 8, 128), lambda b, hh: (hh, 0, 0)),
            pl.BlockSpec((None, 1, dv), lambda b, hh: (j, 0, hh)),
            pl.BlockSpec((None, 1, dv), lambda b, hh: (j, 0, hh)),
        ],
        out_specs=[pl.BlockSpec((seq, dv), lambda b, hh: (b, hh)),
                   pl.BlockSpec((None, None, dk, dv), lambda b, hh: (b, hh, 0, 0))],
        scratch_shapes=[pltpu.VMEM((dk, dv), F32)],
        compiler_params=_cparams(("arbitrary", "arbitrary")),
        name="retention_prompt",
    )(proj, proj, proj, proj, cos, sin, dmask, qdec, kdec, gl, gn_gain, gn_bias)


def _ret_sample_body(p_ref, st_ref, cos_ref, sin_ref, dm_ref, qd_ref, kd_ref, gl_ref, gng_ref, gnb_ref,
                     *rest, dk, dv):
    o_ref, so_ref = rest[-2], rest[-1]
    h = RET_HEADS
    cos = cos_ref[...]
    sin = sin_ref[...]
    pad = 128 - SAMPLE_ROWS
    for hh in range(h):
        q = p_ref[:, hh * dk:(hh + 1) * dk].astype(F32)
        k = p_ref[:, h * dk + hh * dk:h * dk + (hh + 1) * dk].astype(F32)
        v = p_ref[:, 2 * h * dk + hh * dv:2 * h * dk + (hh + 1) * dv]
        g = p_ref[:, 2 * h * dk + h * dv + hh * dv:2 * h * dk + h * dv + (hh + 1) * dv]
        qr = _rotate(q, cos, sin)
        kr = _rotate(k, cos, sin)
        k_pad = jnp.concatenate([kr.astype(BF16), jnp.zeros((pad, dk), BF16)], axis=0)
        kd_pad = jnp.concatenate([(kr * kd_ref[hh]).astype(BF16), jnp.zeros((pad, dk), BF16)], axis=0)
        v_pad = jnp.concatenate([v, jnp.zeros((pad, dv), BF16)], axis=0)
        state = st_ref[hh]
        scores = lax.dot_general(qr.astype(BF16), k_pad, (((1,), (1,)), ((), ())), preferred_element_type=F32)
        scores = scores * dm_ref[hh]
        out = jnp.dot(scores.astype(BF16), v_pad, preferred_element_type=F32)
        out = out + jnp.dot((qr * qd_ref[hh]).astype(BF16), state.astype(BF16), preferred_element_type=F32)
        kv = lax.dot_general(kd_pad, v_pad, (((0,), (0,)), ((), ())), preferred_element_type=F32)
        so_ref[hh] = gl_ref[hh, 0:1, 0:1] * state + kv
        o_ref[:, hh * dv:(hh + 1) * dv] = _group_norm_gate(out, g, gng_ref[:, hh * dv:(hh + 1) * dv],
                                                           gnb_ref[:, hh * dv:(hh + 1) * dv])


def _ret_sample_call(proj, state, gn_gain, gn_bias, j, new_state, *, n_prompt_rows, n_seq, dec_seq, dk, dv):
    h = RET_HEADS
    s = SAMPLE_ROWS
    cos, sin = _rope_tables(PAST_LEN + jnp.arange(s, dtype=jnp.int32), dk // 2)
    dmask, qdec, kdec, gl = _ret_tables(s, dk, n_valid=dec_seq)
    dmask = jnp.pad(dmask, ((0, 0), (0, 0), (0, 128 - s)))
    base = n_prompt_rows // s
    body = functools.partial(_ret_sample_body, dk=dk, dv=dv)
    width = proj.shape[1]
    args = [proj, state, cos, sin, dmask, qdec, kdec, gl, gn_gain, gn_bias]
    in_specs = [
        pl.BlockSpec((s, width), lambda b: (base + b, 0)),
        pl.BlockSpec((None, None, h, dk, dv), lambda b: (j, b, 0, 0, 0)),
        pl.BlockSpec((s, dk // 2), lambda b: (0, 0)),
        pl.BlockSpec((s, dk // 2), lambda b: (0, 0)),
        pl.BlockSpec((h, s, 128), lambda b: (0, 0, 0)),
        pl.BlockSpec((h, s, dk), lambda b: (0, 0, 0)),
        pl.BlockSpec((h, s, dk), lambda b: (0, 0, 0)),
        pl.BlockSpec((h, 8, 128), lambda b: (0, 0, 0)),
        pl.BlockSpec((None, 1, h * dv), lambda b: (j, 0, 0)),
        pl.BlockSpec((None, 1, h * dv), lambda b: (j, 0, 0)),
    ]
    aliases = {}
    if new_state is not None:
        aliases = {len(args): 1}
        args.append(new_state)
        in_specs.append(pl.BlockSpec(memory_space=pl.ANY))
    return pl.pallas_call(
        body,
        out_shape=[jax.ShapeDtypeStruct((n_seq * s, h * dv), BF16),
                   jax.ShapeDtypeStruct(state.shape, F32)],
        grid=(n_seq,),
        in_specs=in_specs,
        out_specs=[pl.BlockSpec((s, h * dv), lambda b: (b, 0)),
                   pl.BlockSpec((None, None, h, dk, dv), lambda b: (j, b, 0, 0, 0))],
        input_output_aliases=aliases,
        compiler_params=_cparams(("arbitrary",)),
        name="retention_sample",
    )(*args)


CONV_HALO = 32
CONV_ROW_CHUNK = 64
CONV_LANES = 128


def _layer_norm_swish(u, gain, bias):
    mu = jnp.mean(u, axis=-1, keepdims=True)
    var = jnp.mean(jnp.square(u - mu), axis=-1, keepdims=True)
    un = (u - mu) * lax.rsqrt(var + NORM_EPS) * gain + bias
    return (un * jax.nn.sigmoid(un)).astype(BF16)


def _conv_prompt_body(z_ref, w_ref, bdw_ref, lng_ref, lnb_ref, o_ref, zbuf, ubuf, *, tt, d):
    t = pl.program_id(1)

    @pl.when(t == 0)
    def _():
        zbuf[0:CONV_HALO, :] = jnp.zeros((CONV_HALO, d), F32)

    @pl.when(t > 0)
    def _():
        zbuf[0:CONV_HALO, :] = zbuf[tt:tt + CONV_HALO, :]

    zbuf[CONV_HALO:CONV_HALO + tt, :] = z_ref[...].astype(F32)

    span = CONV_ROW_CHUNK + CONV_HALO
    n_row_chunks = tt // CONV_ROW_CHUNK
    n_strips = d // CONV_LANES

    def strip(n, carry):
        r0 = pl.multiple_of((n % n_row_chunks) * CONV_ROW_CHUNK, CONV_ROW_CHUNK)
        c0 = pl.multiple_of((n // n_row_chunks) * CONV_LANES, CONV_LANES)
        cols = pl.ds(c0, CONV_LANES)
        blk = zbuf[pl.ds(r0, span), cols]
        acc = jnp.zeros((CONV_ROW_CHUNK, CONV_LANES), F32)
        for b in range(8):
            rb = blk if b == 0 else pltpu.roll(blk, span - b, axis=0)
            for a in range(5):
                o = 8 * a + b
                if 2 <= o <= CONV_HALO:
                    acc = acc + rb[8 * a:8 * a + CONV_ROW_CHUNK, :] * w_ref[pl.ds(o - 2, 1), cols]
        ubuf[pl.ds(r0, CONV_ROW_CHUNK), cols] = acc
        return carry

    lax.fori_loop(0, n_row_chunks * n_strips, strip, 0)

    ln_rows = 128

    def ln_chunk(c, carry):
        rows = pl.ds(pl.multiple_of(c * ln_rows, ln_rows), ln_rows)
        o_ref[rows, :] = _layer_norm_swish(ubuf[rows, :] + bdw_ref[...], lng_ref[...], lnb_ref[...])
        return carry

    lax.fori_loop(0, tt // ln_rows, ln_chunk, 0)


def _conv_prompt_call(z, w_dw, b_dw, ln_gain, ln_bias, *, n_batch, seq):
    d = z.shape[1]
    tt = ROW_TILE
    nt = seq // tt
    w_pad = jnp.pad(w_dw, ((0, 32 - CONV_WIDTH), (0, 0)))
    body = functools.partial(_conv_prompt_body, tt=tt, d=d)
    vec = pl.BlockSpec((1, d), lambda b, t: (0, 0))
    return pl.pallas_call(
        body,
        out_shape=jax.ShapeDtypeStruct((n_batch * seq, d), BF16),
        grid=(n_batch, nt),
        in_specs=[pl.BlockSpec((tt, d), lambda b, t: (b * nt + t, 0)),
                  pl.BlockSpec((32, d), lambda b, t: (0, 0)), vec, vec, vec],
        out_specs=pl.BlockSpec((tt, d), lambda b, t: (b * nt + t, 0)),
        scratch_shapes=[pltpu.VMEM((tt + CONV_HALO, d), F32), pltpu.VMEM((tt, d), F32)],
        compiler_params=_cparams(("arbitrary", "arbitrary")),
        name="conv_prompt",
    )(z, w_pad, b_dw, ln_gain, ln_bias)


def _conv_sample_body(z_ref, c_ref, w_ref, bdw_ref, lng_ref, lnb_ref, o_ref, zbuf, *, d):
    s = SAMPLE_ROWS
    zbuf[0:CONV_STATE, :] = c_ref[...]
    zbuf[CONV_STATE:CONV_STATE + s, :] = z_ref[...].astype(F32)
    acc = jnp.zeros((s, d), F32)
    for j in range(CONV_WIDTH):
        acc = acc + zbuf[j:j + s, :] * w_ref[j:j + 1, :]
    o_ref[...] = _layer_norm_swish(acc + bdw_ref[...], lng_ref[...], lnb_ref[...])


def _conv_sample_call(z, cache, w_dw, b_dw, ln_gain, ln_bias, *, n_prompt_rows, n_seq):
    d = z.shape[1]
    s = SAMPLE_ROWS
    base = n_prompt_rows // s
    body = functools.partial(_conv_sample_body, d=d)
    vec = pl.BlockSpec((1, d), lambda b: (0, 0))
    return pl.pallas_call(
        body,
        out_shape=jax.ShapeDtypeStruct((n_seq * s, d), BF16),
        grid=(n_seq,),
        in_specs=[pl.BlockSpec((s, d), lambda b: (base + b, 0)),
                  pl.BlockSpec((None, CONV_STATE, d), lambda b: (b, 0, 0)),
                  pl.BlockSpec((CONV_WIDTH, d), lambda b: (0, 0)), vec, vec, vec],
        out_specs=pl.BlockSpec((s, d), lambda b: (b, 0)),
        scratch_shapes=[pltpu.VMEM((CONV_STATE + s, d), F32)],
        compiler_params=_cparams(("arbitrary",)),
        name="conv_sample",
    )(z, cache, w_dw, b_dw, ln_gain, ln_bias)


def _t5_bucket(dist):
    n = jnp.maximum(dist, 0)
    max_exact = REL_BUCKETS // 2
    nf = jnp.maximum(n, 1).astype(F32)
    large = max_exact + (jnp.log(nf / max_exact) / math.log(REL_MAX_DIST / max_exact)
                         * (REL_BUCKETS - max_exact)).astype(jnp.int32)
    large = jnp.minimum(large, REL_BUCKETS - 1)
    return jnp.where(n < max_exact, n, large)


def _bias_table(rel_bias, dist, valid):
    onehot = (_t5_bucket(dist).reshape(-1)[None, :] == jnp.arange(REL_BUCKETS, dtype=jnp.int32)[:, None]).astype(F32)
    tbl = jnp.dot(rel_bias.astype(F32).T, onehot, precision=lax.Precision.HIGHEST)
    tbl = jnp.where(valid.reshape(-1)[None, :], tbl, NEG_INF)
    return tbl.reshape(-1, dist.shape[-1])


def _swa_prompt_body(sink_ref, q_ref, kvp_ref, kvc_ref, bias_ref, o_ref, *, hd):
    i = pl.program_id(1)
    blk = WINDOW
    g = ATT_GROUP
    nkv = ATT_KV_HEADS
    nk = nkv * hd
    col = lax.broadcasted_iota(jnp.int32, (1, 1, 2 * blk), 2)
    first_mask = jnp.where(jnp.logical_and(i == 0, col < blk), NEG_INF, 0.0)
    keys = jnp.stack([jnp.concatenate([kvp_ref[:, hk * hd:(hk + 1) * hd], kvc_ref[:, hk * hd:(hk + 1) * hd]], axis=0)
                      for hk in range(nkv)])
    vals = jnp.stack([jnp.concatenate([kvp_ref[:, nk + hk * hd:nk + (hk + 1) * hd],
                                       kvc_ref[:, nk + hk * hd:nk + (hk + 1) * hd]], axis=0)
                      for hk in range(nkv)])
    qs = jnp.stack([jnp.concatenate([q_ref[:, (hk * g + gg) * hd:(hk * g + gg + 1) * hd] for gg in range(g)], axis=0)
                    for hk in range(nkv)])
    s = jnp.einsum('hqd,hkd->hqk', qs, keys, preferred_element_type=F32)
    s = s + bias_ref[...].reshape(nkv, g * blk, 2 * blk) + first_mask
    sink = jnp.stack([jnp.concatenate([jnp.full((blk, 1), sink_ref[hk * g + gg], F32) for gg in range(g)], axis=0)
                      for hk in range(nkv)])
    m = jnp.maximum(jnp.max(s, axis=-1, keepdims=True), sink)
    p = jnp.exp(s - m).astype(BF16)
    den = jnp.einsum('hqk,hkd->hqd', p, jnp.ones(vals.shape, BF16), preferred_element_type=F32) + jnp.exp(sink - m)
    o = jnp.einsum('hqk,hkd->hqd', p, vals, preferred_element_type=F32) / den
    for hk in range(nkv):
        for gg in range(g):
            o_ref[:, (hk * g + gg) * hd:(hk * g + gg + 1) * hd] = o[hk, gg * blk:(gg + 1) * blk, :].astype(BF16)


def _swa_prompt_call(proj, sinks, rel_bias, *, n_batch, seq, hd):
    blk = WINDOW
    nb = seq // blk
    nq = ATT_Q_HEADS * hd
    nkv2 = 2 * ATT_KV_HEADS * hd
    kvcol = nq // nkv2
    i_idx = jnp.arange(blk, dtype=jnp.int32)[:, None]
    j_idx = jnp.arange(2 * blk, dtype=jnp.int32)[None, :]
    dist = blk + i_idx - j_idx
    bias = _bias_table(rel_bias, dist, (dist >= 0) & (dist < WINDOW))
    body = functools.partial(_swa_prompt_body, hd=hd)
    return pl.pallas_call(
        body,
        out_shape=jax.ShapeDtypeStruct((n_batch * seq, nq), BF16),
        grid=(n_batch, nb),
        in_specs=[
            pl.BlockSpec(memory_space=pltpu.SMEM),
            pl.BlockSpec((blk, nq), lambda b, i: (b * nb + i, 0)),
            pl.BlockSpec((blk, nkv2), lambda b, i: (b * nb + jnp.maximum(i - 1, 0), kvcol)),
            pl.BlockSpec((blk, nkv2), lambda b, i: (b * nb + i, kvcol)),
            pl.BlockSpec((ATT_Q_HEADS * blk, 2 * blk), lambda b, i: (0, 0)),
        ],
        out_specs=pl.BlockSpec((blk, nq), lambda b, i: (b * nb + i, 0)),
        compiler_params=_cparams(("arbitrary", "arbitrary")),
        name="swa_prompt",
    )(sinks, proj, proj, proj, bias)


def _swa_sample_body(sink_ref, p_ref, ck_ref, cv_ref, bias_ref, o_ref, *, hd):
    s = SAMPLE_ROWS
    g = ATT_GROUP
    nkv = ATT_KV_HEADS
    nq = ATT_Q_HEADS * hd
    keys = jnp.stack([jnp.concatenate([ck_ref[:, hk * hd:(hk + 1) * hd].astype(BF16),
                                       p_ref[:, nq + hk * hd:nq + (hk + 1) * hd]], axis=0)
                      for hk in range(nkv)])
    vals = jnp.stack([jnp.concatenate([cv_ref[:, hk * hd:(hk + 1) * hd].astype(BF16),
                                       p_ref[:, nq + (nkv + hk) * hd:nq + (nkv + hk + 1) * hd]], axis=0)
                      for hk in range(nkv)])
    qs = jnp.stack([jnp.concatenate([p_ref[:, (hk * g + gg) * hd:(hk * g + gg + 1) * hd] for gg in range(g)], axis=0)
                    for hk in range(nkv)])
    sc = jnp.einsum('hqd,hkd->hqk', qs, keys, preferred_element_type=F32)
    sc = sc + bias_ref[...].reshape(nkv, g * s, keys.shape[1])
    sink = jnp.stack([jnp.concatenate([jnp.full((s, 1), sink_ref[hk * g + gg], F32) for gg in range(g)], axis=0)
                      for hk in range(nkv)])
    m = jnp.maximum(jnp.max(sc, axis=-1, keepdims=True), sink)
    p = jnp.exp(sc - m).astype(BF16)
    den = jnp.einsum('hqk,hkd->hqd', p, jnp.ones(vals.shape, BF16), preferred_element_type=F32) + jnp.exp(sink - m)
    o = jnp.einsum('hqk,hkd->hqd', p, vals, preferred_element_type=F32) / den
    for hk in range(nkv):
        for gg in range(g):
            o_ref[:, (hk * g + gg) * hd:(hk * g + gg + 1) * hd] = o[hk, gg * s:(gg + 1) * s, :].astype(BF16)


def _swa_sample_call(proj, cache_k, cache_v, sinks, rel_bias, *, n_prompt_rows, n_seq, dec_seq, hd):
    s = SAMPLE_ROWS
    nq = ATT_Q_HEADS * hd
    nkv = ATT_KV_HEADS * hd
    sc = cache_k.shape[1]
    i_idx = jnp.arange(s, dtype=jnp.int32)[:, None]
    j_idx = jnp.arange(sc + s, dtype=jnp.int32)[None, :]
    dist = sc + i_idx - j_idx
    valid = (dist >= 0) & (dist < WINDOW) & (j_idx < sc + dec_seq)
    bias = _bias_table(rel_bias, dist, valid)
    base = n_prompt_rows // s
    body = functools.partial(_swa_sample_body, hd=hd)
    return pl.pallas_call(
        body,
        out_shape=jax.ShapeDtypeStruct((n_seq * s, nq), BF16),
        grid=(n_seq,),
        in_specs=[
            pl.BlockSpec(memory_space=pltpu.SMEM),
            pl.BlockSpec((s, proj.shape[1]), lambda b: (base + b, 0)),
            pl.BlockSpec((None, sc, nkv), lambda b: (b, 0, 0)),
            pl.BlockSpec((None, sc, nkv), lambda b: (b, 0, 0)),
            pl.BlockSpec((ATT_Q_HEADS * s, sc + s), lambda b: (0, 0)),
        ],
        out_specs=pl.BlockSpec((s, nq), lambda b: (b, 0)),
        compiler_params=_cparams(("arbitrary",)),
        name="swa_sample",
    )(sinks, proj, cache_k, cache_v, bias)


MOE_TILE = 256


def _moe_body(src_ref, dst_ref, te_ref, nu_ref, h_hbm, win_ref, wout_ref, y_hbm, xg, yb, wib, wob, gsem, ssem,
              *, ff):
    t = pl.program_id(0)
    nt = pl.num_programs(0)
    n_used = nu_ref[0]
    tm = MOE_TILE
    slot = lax.rem(t, 2)

    def gather_start(tile, sl):
        base = tile * tm
        for r in range(tm):
            pltpu.make_async_copy(h_hbm.at[pl.ds(src_ref[base + r], 1), :], xg.at[sl, pl.ds(r, 1), :],
                                  gsem.at[sl]).start()

    def gather_wait(sl):
        pltpu.make_async_copy(h_hbm.at[pl.ds(0, tm), :], xg.at[sl], gsem.at[sl]).wait()

    def scatter_start(tile, sl):
        base = tile * tm
        for r in range(tm):
            pltpu.make_async_copy(yb.at[sl, pl.ds(r, 1), :], y_hbm.at[pl.ds(dst_ref[base + r], 1), :],
                                  ssem.at[sl]).start()

    def scatter_wait(sl):
        pltpu.make_async_copy(yb.at[sl], y_hbm.at[pl.ds(0, tm), :], ssem.at[sl]).wait()

    valid = t < n_used

    @pl.when(t == 0)
    def _():
        gather_start(0, 0)

    @pl.when(jnp.logical_and(valid, jnp.logical_or(t == 0, te_ref[t] != te_ref[jnp.maximum(t - 1, 0)])))
    def _():
        wib[...] = win_ref[...].astype(BF16)
        wob[...] = wout_ref[...].astype(BF16)

    def step(sl):
        @pl.when(t >= 2)
        def _():
            scatter_wait(sl)

        gather_wait(sl)

        @pl.when(t + 1 < n_used)
        def _():
            gather_start(t + 1, 1 - sl)

        gu = jnp.dot(_unpack_bf16_pairs(xg[sl]).astype(BF16), wib[...], preferred_element_type=F32)
        gate, up = gu[:, :ff], gu[:, ff:]
        act = (gate * jax.nn.sigmoid(gate) * up).astype(BF16)
        yb[sl] = _pack_bf16_pairs(jnp.dot(act, wob[...], preferred_element_type=F32))
        scatter_start(t, sl)

    for sl in range(2):
        pl.when(jnp.logical_and(valid, slot == sl))(functools.partial(step, sl))

    @pl.when(t == nt - 1)
    def _():
        scatter_wait(lax.rem(n_used - 1, 2))
        scatter_wait(lax.rem(n_used, 2))


def _moe_layer(h2, route, counts, w_in, w_out, layer):
    r = h2.shape[0]
    depth, n_exp, d, ff2 = w_in.shape
    ff = ff2 // 2
    tm = MOE_TILE
    n_tiles = 2 * r // tm + n_exp
    n_slots = n_tiles * tm

    cnt = counts[0, MOE_GROUPS:MOE_GROUPS + n_exp].astype(jnp.int32)
    padded = ((cnt + tm - 1) // tm) * tm
    ends = jnp.cumsum(padded)
    starts = ends - padded
    rt = route[:, :ROUTE_R2 + 1].astype(jnp.int32)
    pos = jnp.concatenate([starts[rt[:, ROUTE_E1]] + rt[:, ROUTE_R1], starts[rt[:, ROUTE_E2]] + rt[:, ROUTE_R2]])
    tok = jnp.arange(r, dtype=jnp.int32)
    dst = (2 * r + jnp.arange(n_slots, dtype=jnp.int32)).at[pos].set(
        jnp.concatenate([tok, r + tok]), unique_indices=True, indices_are_sorted=False)
    src = jnp.where(dst < r, dst, jnp.where(dst < 2 * r, dst - r, 0))
    tile_start = jnp.arange(n_tiles, dtype=jnp.int32) * tm
    tile_expert = jnp.minimum(jnp.sum(tile_start[:, None] >= ends[None, :], axis=1), n_exp - 1).astype(jnp.int32)
    n_used = (ends[-1:] // tm).astype(jnp.int32)

    return pl.pallas_call(
        functools.partial(_moe_body, ff=ff),
        out_shape=jax.ShapeDtypeStruct((2 * r + n_slots, d // 2), jnp.uint32),
        grid_spec=pltpu.PrefetchScalarGridSpec(
            num_scalar_prefetch=4,
            grid=(n_tiles,),
            in_specs=[
                pl.BlockSpec(memory_space=pl.ANY),
                pl.BlockSpec((None, None, d, ff2), lambda t, src, dst, te, nu: (layer, te[t], 0, 0)),
                pl.BlockSpec((None, None, ff, d), lambda t, src, dst, te, nu: (layer, te[t], 0, 0)),
            ],
            out_specs=pl.BlockSpec(memory_space=pl.ANY),
            scratch_shapes=[pltpu.VMEM((2, tm, d // 2), jnp.uint32), pltpu.VMEM((2, tm, d // 2), jnp.uint32),
                            pltpu.VMEM((d, ff2), BF16), pltpu.VMEM((ff, d), BF16),
                            pltpu.SemaphoreType.DMA((2,)), pltpu.SemaphoreType.DMA((2,))],
        ),
        compiler_params=_cparams(("arbitrary",)),
        name="moe_experts",
    )(src, dst, tile_expert, n_used, h2, w_in, w_out)


def kernel(x_prompt, x_sample, c_prompt, c_sample, state_ret, cache_conv, cache_swa_k, cache_swa_v, ada_w, ada_b, norm_gain, ret_w_in, ret_gn_gain, ret_gn_bias, ret_w_out, conv_w_pw1, conv_b_pw1, conv_w_dw, conv_b_dw, conv_ln_gain, conv_ln_bias, conv_w_pw2, conv_b_pw2, att_w_qkv, att_q_gain, att_k_gain, att_sinks, att_w_o, rel_bias, moe_wg, moe_bg, moe_we, moe_be, moe_w_in, moe_w_out):
    n_batch, seq, d = x_prompt.shape
    n_seq, dec_seq, _ = x_sample.shape
    depth = ada_w.shape[0]
    s = SAMPLE_ROWS
    assert n_seq * s == ROW_TILE and seq % ROW_TILE == 0 and dec_seq <= s
    n_prompt_rows = n_batch * seq
    seq_tiles = seq // ROW_TILE
    dk = ret_w_in.shape[2] // (6 * RET_HEADS)
    dv = 2 * dk
    hd = d // ATT_Q_HEADS
    geom = dict(seq_tiles=seq_tiles, n_batch=n_batch)

    xs_pad = jnp.pad(x_sample, ((0, 0), (0, s - dec_seq), (0, 0))).reshape(n_seq * s, d)
    x = jnp.concatenate([x_prompt.reshape(n_prompt_rows, d), xs_pad], axis=0)

    n_c = n_batch + n_seq
    c_rows = ((n_c + 7) // 8) * 8
    c_all = jnp.pad(jnp.concatenate([c_prompt, c_sample], axis=0), ((0, c_rows - n_c), (0, 0)))
    mod = _ada_call(c_all, ada_w, ada_b)
    mod4 = mod.reshape(depth, c_rows, 1, 6 * d)
    modtok = jnp.repeat(mod[:, n_batch:n_c], s, axis=1)

    gains = norm_gain.astype(F32).reshape(2 * depth, 1, d)
    gng_all = ret_gn_gain.astype(F32)[:, None, :]
    gnb_all = ret_gn_bias.astype(F32)[:, None, :]
    state_all = state_ret.astype(F32)

    ret_p, conv_p, conv_s, kp_l, vp_l, ks_l, vs_l = [], [], [], [], [], [], []
    ret_s = None
    (h,) = _norm_call(x, mod4, modtok, norm=(gains, 0, 0, 0, 1), **geom)
    for l in range(depth):
        kind, j = l % 3, l // 3
        resid1 = (x, mod4, modtok, l, 2)
        if kind == 0:
            proj = _linear_call(h, ret_w_in, j, n_out=ret_w_in.shape[2], tn=2048, out_dtype=BF16,
                                name="ret_in", **geom)
            a_p, st_p = _ret_prompt_call(proj, gng_all, gnb_all, j, n_batch=n_batch, seq=seq, dk=dk, dv=dv)
            a_s, ret_s = _ret_sample_call(proj, state_all, gng_all, gnb_all, j, ret_s, n_prompt_rows=n_prompt_rows,
                                          n_seq=n_seq, dec_seq=dec_seq, dk=dk, dv=dv)
            ret_p.append(st_p)
            x = _linear_call(a_p, ret_w_out, j, h_sample=a_s, n_out=d, tn=512, out_dtype=F32, resid=resid1,
                             name="ret_out", **geom)
        elif kind == 1:
            z = _linear_call(h, conv_w_pw1, j, n_out=d, tn=1024, out_dtype=BF16, bias=conv_b_pw1,
                             glu=True, name="conv_pw1", **geom)
            cargs = (conv_w_dw[j], conv_b_dw[j][None], conv_ln_gain[j][None], conv_ln_bias[j][None])
            a_p = _conv_prompt_call(z, *cargs, n_batch=n_batch, seq=seq)
            a_s = _conv_sample_call(z, cache_conv[j].astype(F32), *cargs, n_prompt_rows=n_prompt_rows, n_seq=n_seq)
            z_tail = z[:n_prompt_rows].reshape(n_batch, seq, d)[:, seq - CONV_STATE:].astype(F32)
            conv_p.append(z_tail)
            z_new = z[n_prompt_rows:].reshape(n_seq, s, d)[:, :dec_seq].astype(F32)
            conv_s.append(jnp.concatenate([cache_conv[j].astype(F32), z_new], axis=1)[:, -CONV_STATE:])
            x = _linear_call(a_p, conv_w_pw2, j, h_sample=a_s, n_out=d, tn=1024, out_dtype=F32,
                             bias=conv_b_pw2, resid=resid1, name="conv_pw2", **geom)
        else:
            nkv = ATT_KV_HEADS * hd
            nq = ATT_Q_HEADS * hd
            hgain = jnp.concatenate([jnp.tile(att_q_gain[j].astype(F32) * (hd ** -0.5), ATT_Q_HEADS),
                                     jnp.tile(att_k_gain[j].astype(F32), ATT_KV_HEADS), jnp.ones((nkv,), F32)])[None]
            hmask = jnp.concatenate([jnp.ones((nq + nkv,), F32), jnp.zeros((nkv,), F32)])[None]
            proj = _linear_call(h, att_w_qkv, j, n_out=att_w_qkv.shape[2], tn=1280, out_dtype=BF16,
                                head_norm=(hgain, hmask, hd), name="att_qkv", **geom)
            sinks = att_sinks[j].astype(F32)
            a_p = _swa_prompt_call(proj, sinks, rel_bias, n_batch=n_batch, seq=seq, hd=hd)
            win = cache_swa_k.shape[2]
            ck = cache_swa_k[j].astype(F32).reshape(n_seq, win, nkv)
            cv = cache_swa_v[j].astype(F32).reshape(n_seq, win, nkv)
            a_s = _swa_sample_call(proj, ck, cv, sinks, rel_bias, n_prompt_rows=n_prompt_rows,
                                   n_seq=n_seq, dec_seq=dec_seq, hd=hd)
            kv_tail = proj[:n_prompt_rows, nq:].reshape(n_batch, seq, 2 * nkv)[:, seq - WINDOW:].astype(F32)
            kp_l.append(kv_tail[:, :, :nkv].reshape(n_batch, WINDOW, ATT_KV_HEADS, hd))
            vp_l.append(kv_tail[:, :, nkv:].reshape(n_batch, WINDOW, ATT_KV_HEADS, hd))
            kv_new = proj[n_prompt_rows:, nq:].reshape(n_seq, s, 2 * nkv)[:, :dec_seq].astype(F32)
            k_new, v_new = kv_new[:, :, :nkv], kv_new[:, :, nkv:]
            ks_l.append(jnp.concatenate([ck, k_new], axis=1)[:, -win:].reshape(n_seq, win, ATT_KV_HEADS, hd))
            vs_l.append(jnp.concatenate([cv, v_new], axis=1)[:, -win:].reshape(n_seq, win, ATT_KV_HEADS, hd))
            x = _linear_call(a_p, att_w_o, j, h_sample=a_s, n_out=d, tn=1024, out_dtype=F32, resid=resid1,
                             name="att_out", **geom)

        lane_pad = ROUTER_LANES - MOE_GROUPS - MOE_GROUPS * MOE_EPG
        router_w = jnp.pad(jnp.concatenate([moe_wg[l], moe_we[l]], axis=1).astype(F32), ((0, 0), (0, lane_pad)))
        router_hi = router_w.astype(BF16)
        router_w = jnp.stack([router_hi, (router_w - router_hi.astype(F32)).astype(BF16)])
        router_b = jnp.pad(jnp.concatenate([moe_bg[l], moe_be[l]]).astype(F32), (0, lane_pad))[None]
        h2, route, counts = _norm_call(x, mod4, modtok, norm=(gains, 2 * l + 1, l, 3, 4),
                                       router=(router_w, router_b), **geom)
        ypair = _moe_layer(h2, route, counts, moe_w_in, moe_w_out, l)
        if l + 1 < depth:
            x, h = _norm_call(x, mod4, modtok, resid=(ypair, route, l, 5),
                              norm=(gains, 2 * l + 2, l + 1, 0, 1), **geom)
        else:
            x_p, x_s = _norm_call(x, mod4, modtok, resid=(ypair, route, l, 5), split_out=True, **geom)

    y_prompt = x_p.reshape(n_batch, seq, d)
    y_sample = x_s.reshape(n_seq, s, d)[:, :dec_seq]
    return (y_prompt, y_sample, jnp.stack(ret_p), ret_s, jnp.stack(conv_p), jnp.stack(conv_s),
            jnp.stack(kp_l), jnp.stack(vp_l), jnp.stack(ks_l), jnp.stack(vs_l))
```

```python
import functools
import math

import jax
import jax.numpy as jnp
from jax import lax
from jax.experimental import pallas as pl
from jax.experimental.pallas import tpu as pltpu

F32 = jnp.float32
BF16 = jnp.bfloat16

NORM_EPS = 1e-6
NEG_INF = -1e30
ROPE_BASE = 10000.0
PAST_LEN = 16384

RET_HEADS = 8
RET_CHUNK = 128
CONV_WIDTH = 31
CONV_STATE = CONV_WIDTH - 1
ATT_Q_HEADS = 32
ATT_KV_HEADS = 4
ATT_GROUP = ATT_Q_HEADS // ATT_KV_HEADS
WINDOW = 128
REL_BUCKETS = 32
REL_MAX_DIST = 128
MOE_GROUPS = 4
MOE_EPG = 4
MOE_TOPK = 2

ROW_TILE = 512
SAMPLE_ROWS = 16
ROUTER_LANES = 128
VMEM_LIMIT_BYTES = 56 * 1024 * 1024


def _cparams(sem):
    return pltpu.CompilerParams(dimension_semantics=sem, vmem_limit_bytes=VMEM_LIMIT_BYTES)


def _pack_bf16_pairs(y):
    half = y.shape[1] // 2
    hi = lax.bitcast_convert_type(y[:, :half].astype(BF16).astype(F32), jnp.uint32)
    lo = lax.bitcast_convert_type(y[:, half:].astype(BF16).astype(F32), jnp.uint32)
    return hi | lax.shift_right_logical(lo, jnp.uint32(16))


def _unpack_bf16_pairs(w):
    hi = lax.bitcast_convert_type(w & jnp.uint32(0xFFFF0000), F32)
    lo = lax.bitcast_convert_type(lax.shift_left(w, jnp.uint32(16)), F32)
    return jnp.concatenate([hi, lo], axis=1)


def _ada_body(c_ref, w_ref, b_ref, o_ref):
    c = c_ref[...]
    s = (c * jax.nn.sigmoid(c)).astype(BF16)
    o_ref[...] = jnp.dot(s, w_ref[...].astype(BF16), preferred_element_type=F32) + b_ref[...]


def _ada_call(c_all, ada_w, ada_b):
    depth, d, n = ada_w.shape
    rows = c_all.shape[0]
    tn = 1024
    return pl.pallas_call(
        _ada_body,
        out_shape=jax.ShapeDtypeStruct((depth, rows, n), F32),
        grid=(depth, n // tn),
        in_specs=[
            pl.BlockSpec((rows, d), lambda l, j: (0, 0)),
            pl.BlockSpec((None, d, tn), lambda l, j: (l, 0, j)),
            pl.BlockSpec((None, 1, tn), lambda l, j: (l, 0, j)),
        ],
        out_specs=pl.BlockSpec((None, rows, tn), lambda l, j: (l, 0, j)),
        compiler_params=_cparams(("arbitrary", "arbitrary")),
        name="ada_mod",
    )(c_all, ada_w, ada_b.reshape(depth, 1, n))


ROUTE_E1, ROUTE_E2, ROUTE_W1, ROUTE_W2, ROUTE_R1, ROUTE_R2 = range(6)


def _route_tile(logits, rb, carry):
    tm = logits.shape[0]
    g, epg = MOE_GROUPS, MOE_EPG
    lg = logits + rb
    lane = lax.broadcasted_iota(jnp.int32, lg.shape, 1)
    lane_f = lane.astype(F32)

    def first_lane(mask):
        return jnp.min(jnp.where(mask, lane_f, float(ROUTER_LANES)), axis=-1, keepdims=True).astype(jnp.int32)

    is_g = lane < g
    mg = jnp.max(jnp.where(is_g, lg, NEG_INF), axis=-1, keepdims=True)
    eg = jnp.where(is_g, jnp.exp(lg - mg), 0.0)
    p_grp = 1.0 / jnp.sum(eg, axis=-1, keepdims=True)
    grp = first_lane(jnp.logical_and(is_g, lg == mg))
    lo = g + epg * grp
    is_e = jnp.logical_and(lane >= lo, lane < lo + epg)
    me = jnp.max(jnp.where(is_e, lg, NEG_INF), axis=-1, keepdims=True)
    ee = jnp.where(is_e, jnp.exp(lg - me), 0.0)
    se = jnp.sum(ee, axis=-1, keepdims=True)
    i1 = first_lane(jnp.logical_and(is_e, lg == me))
    rest = jnp.logical_and(is_e, lane != i1)
    m2 = jnp.max(jnp.where(rest, ee, -1.0), axis=-1, keepdims=True)
    i2 = first_lane(jnp.logical_and(rest, ee == m2))
    p1 = 1.0 / se
    p2 = m2 / se
    w1 = p_grp * p1 / (p1 + p2)
    w2 = p_grp * p2 / (p1 + p2)
    sel1 = lane == i1
    sel2 = lane == i2
    onehot = jnp.where(jnp.logical_or(sel1, sel2), 1.0, 0.0)
    row = lax.broadcasted_iota(jnp.int32, (tm, tm), 0)
    col = lax.broadcasted_iota(jnp.int32, (tm, tm), 1)
    tri = jnp.where(col <= row, 1.0, 0.0).astype(BF16)
    incl = jnp.dot(tri, onehot.astype(BF16), preferred_element_type=F32)
    rank = incl - 1.0 + carry
    r1 = jnp.sum(jnp.where(sel1, rank, 0.0), axis=-1, keepdims=True)
    r2 = jnp.sum(jnp.where(sel2, rank, 0.0), axis=-1, keepdims=True)
    rec = jnp.zeros_like(lg)
    for ln, val in ((ROUTE_E1, (i1 - g).astype(F32)), (ROUTE_E2, (i2 - g).astype(F32)), (ROUTE_W1, w1),
                    (ROUTE_W2, w2), (ROUTE_R1, r1), (ROUTE_R2, r2)):
        rec = jnp.where(lane == ln, val, rec)
    return rec, carry + jnp.sum(onehot, axis=0, keepdims=True)


def _norm_body(*refs, n_prompt_tiles, has_resid, has_norm, has_router, split_out):
    it = iter(refs)
    x_ref = next(it)
    if has_resid:
        y0_ref, y1_ref, rt_ref, grow_ref, gtok_ref = next(it), next(it), next(it), next(it), next(it)
    if has_norm:
        gain_ref, shrow_ref, scrow_ref, shtok_ref, sctok_ref = next(it), next(it), next(it), next(it), next(it)
    if has_router:
        wr_ref, rb_ref = next(it), next(it)
    if has_resid:
        xo_ref = next(it)
        if split_out:
            xs_ref = next(it)
    if has_norm:
        h_ref = next(it)
    if has_router:
        rec_ref, cnt_ref, carry = next(it), next(it), next(it)

    is_sample = pl.program_id(0) >= n_prompt_tiles

    if has_router:
        @pl.when(pl.program_id(0) == 0)
        def _():
            carry[...] = jnp.zeros_like(carry)

    def tile(sample):
        x = x_ref[...]
        if has_resid:
            gate = gtok_ref[...] if sample else grow_ref[...]
            rt = rt_ref[...]
            y = (rt[:, ROUTE_W1:ROUTE_W1 + 1] * _unpack_bf16_pairs(y0_ref[...])
                 + rt[:, ROUTE_W2:ROUTE_W2 + 1] * _unpack_bf16_pairs(y1_ref[...]))
            x = x + gate * y
            if split_out and sample:
                xs_ref[...] = x
            else:
                xo_ref[...] = x
        if has_norm:
            scale = sctok_ref[...] if sample else scrow_ref[...]
            shift = shtok_ref[...] if sample else shrow_ref[...]
            ms = jnp.mean(x * x, axis=-1, keepdims=True)
            h = (x * lax.rsqrt(ms + NORM_EPS)) * (gain_ref[...] * (1.0 + scale)) + shift
            if has_router:
                h_ref[...] = _pack_bf16_pairs(h)
            else:
                h_ref[...] = h.astype(h_ref.dtype)
            if has_router:
                h_hi = h.astype(BF16)
                h_lo = (h - h_hi.astype(F32)).astype(BF16)
                logits = (jnp.dot(h_hi, wr_ref[0], preferred_element_type=F32)
                          + jnp.dot(h_lo, wr_ref[0], preferred_element_type=F32)
                          + jnp.dot(h_hi, wr_ref[1], preferred_element_type=F32))
                rec, new_carry = _route_tile(logits, rb_ref[...], carry[0:1, :])
                rec_ref[...] = rec
                carry[...] = jnp.broadcast_to(new_carry, carry.shape)
                cnt_ref[...] = carry[...]

    pl.when(jnp.logical_not(is_sample))(functools.partial(tile, False))
    pl.when(is_sample)(functools.partial(tile, True))


def _norm_call(x, mod4, modtok, *, seq_tiles, n_batch, resid=None, norm=None, router=None, split_out=False):
    r, d = x.shape
    n_tiles = r // ROW_TILE
    n_prompt_tiles = n_tiles - 1

    def row_spec(layer, col):
        return pl.BlockSpec((None, None, 1, d),
                            lambda i: (layer, jnp.minimum(i // seq_tiles, n_batch - 1), 0, col))

    def tok_spec(layer, col):
        return pl.BlockSpec((None, ROW_TILE, d), lambda i: (layer, 0, col))

    tile = pl.BlockSpec((ROW_TILE, d), lambda i: (i, 0))
    args, in_specs, out_shape, out_specs = [x], [tile], [], []
    scratch = []
    if resid is not None:
        ypair, route, layer, gcol = resid
        args += [ypair, ypair, route, mod4, modtok]
        in_specs += [pl.BlockSpec((ROW_TILE, d // 2), lambda i: (i, 0)),
                     pl.BlockSpec((ROW_TILE, d // 2), lambda i: (n_tiles + i, 0)),
                     pl.BlockSpec((ROW_TILE, ROUTER_LANES), lambda i: (i, 0)),
                     row_spec(layer, gcol), tok_spec(layer, gcol)]
        if split_out:
            out_shape += [jax.ShapeDtypeStruct((r - ROW_TILE, d), F32), jax.ShapeDtypeStruct((ROW_TILE, d), F32)]
            out_specs += [pl.BlockSpec((ROW_TILE, d), lambda i: (jnp.minimum(i, n_prompt_tiles - 1), 0)),
                          pl.BlockSpec((ROW_TILE, d), lambda i: (0, 0))]
        else:
            out_shape.append(jax.ShapeDtypeStruct((r, d), F32))
            out_specs.append(tile)
    if norm is not None:
        gains, gidx, layer, shcol, sccol = norm
        args += [gains, mod4, mod4, modtok, modtok]
        in_specs += [pl.BlockSpec((None, 1, d), lambda i: (gidx, 0, 0)), row_spec(layer, shcol),
                     row_spec(layer, sccol), tok_spec(layer, shcol), tok_spec(layer, sccol)]
        if router is not None:
            out_shape.append(jax.ShapeDtypeStruct((r, d // 2), jnp.uint32))
            out_specs.append(pl.BlockSpec((ROW_TILE, d // 2), lambda i: (i, 0)))
        else:
            out_shape.append(jax.ShapeDtypeStruct((r, d), BF16))
            out_specs.append(tile)
        if router is not None:
            args += list(router)
            in_specs += [pl.BlockSpec((2, d, ROUTER_LANES), lambda i: (0, 0, 0)),
                         pl.BlockSpec((1, ROUTER_LANES), lambda i: (0, 0))]
            out_shape += [jax.ShapeDtypeStruct((r, ROUTER_LANES), F32), jax.ShapeDtypeStruct((8, ROUTER_LANES), F32)]
            out_specs += [pl.BlockSpec((ROW_TILE, ROUTER_LANES), lambda i: (i, 0)),
                          pl.BlockSpec((8, ROUTER_LANES), lambda i: (0, 0))]
            scratch.append(pltpu.VMEM((8, ROUTER_LANES), F32))
    body = functools.partial(_norm_body, n_prompt_tiles=n_prompt_tiles, has_resid=resid is not None,
                             has_norm=norm is not None, has_router=router is not None, split_out=split_out)
    return pl.pallas_call(
        body, out_shape=out_shape, grid=(n_tiles,), in_specs=in_specs, out_specs=out_specs,
        scratch_shapes=scratch, compiler_params=_cparams(("arbitrary",)), name="mod_norm",
    )(*args)


def _head_rms_norm(x, gain_row, seg_ref, exp_ref, hd):
    ssq = jnp.dot((x * x).astype(BF16), seg_ref[...], preferred_element_type=F32)
    r = lax.rsqrt(ssq * (1.0 / hd) + NORM_EPS)
    r_hi = r.astype(BF16)
    r_lo = (r - r_hi.astype(F32)).astype(BF16)
    scale = jnp.dot(jnp.concatenate([r_hi, r_lo], axis=1), exp_ref[...], preferred_element_type=F32)
    return x * scale * gain_row


def _linear_body(*refs, n_prompt_tiles, pair, glu, has_bias, has_resid, head_dim):
    it = iter(refs)
    h_ref = next(it)
    hs_ref = next(it) if pair else None
    w_ref = next(it)
    w2_ref = next(it) if glu else None
    b_ref = next(it) if has_bias else None
    b2_ref = next(it) if glu else None
    if has_resid:
        x_ref, grow_ref, gtok_ref = next(it), next(it), next(it)
    if head_dim:
        hgain_ref, hmask_ref, seg_ref, exp_ref = next(it), next(it), next(it), next(it)
    o_ref = next(it)
    wb = next(it)
    wb2 = next(it) if glu else None

    i = pl.program_id(1)
    is_sample = i >= n_prompt_tiles

    @pl.when(i == 0)
    def _():
        wb[...] = w_ref[...].astype(BF16)
        if glu:
            wb2[...] = w2_ref[...].astype(BF16)

    def compute(hv):
        acc = jnp.dot(hv, wb[...], preferred_element_type=F32)
        if has_bias:
            acc = acc + b_ref[...]
        if glu:
            acc2 = jnp.dot(hv, wb2[...], preferred_element_type=F32) + b2_ref[...]
            acc = acc * jax.nn.sigmoid(acc2)
        if has_resid:
            gate = jnp.where(is_sample, gtok_ref[...], grow_ref[...])
            acc = x_ref[...] + gate * acc
        if head_dim:
            normed = _head_rms_norm(acc, hgain_ref[...], seg_ref, exp_ref, head_dim)
            acc = jnp.where(hmask_ref[...] > 0.0, normed, acc)
        o_ref[...] = acc.astype(o_ref.dtype)

    if pair:
        @pl.when(jnp.logical_not(is_sample))
        def _():
            compute(h_ref[...])

        @pl.when(is_sample)
        def _():
            compute(hs_ref[...])
    else:
        compute(h_ref[...])


def _linear_call(h, w, wl, *, n_out, tn, out_dtype, seq_tiles, n_batch, h_sample=None, bias=None, glu=False,
                 resid=None, head_norm=None, name="linear"):
    k = w.shape[1]
    pair = h_sample is not None
    n_prompt_tiles = h.shape[0] // ROW_TILE - (0 if pair else 1)
    n_tiles = n_prompt_tiles + 1
    r = n_tiles * ROW_TILE
    nblk = n_out // tn

    args = [h]
    in_specs = [pl.BlockSpec((ROW_TILE, k), lambda j, i: (jnp.minimum(i, n_prompt_tiles - 1) if pair else i, 0))]
    if pair:
        args.append(h_sample)
        in_specs.append(pl.BlockSpec((ROW_TILE, k), lambda j, i: (0, 0)))
    args.append(w)
    in_specs.append(pl.BlockSpec((None, k, tn), lambda j, i: (wl, 0, j)))
    if glu:
        args.append(w)
        in_specs.append(pl.BlockSpec((None, k, tn), lambda j, i: (wl, 0, nblk + j)))
    if bias is not None:
        bias = bias.reshape(bias.shape[0], 1, bias.shape[1])
        args.append(bias)
        in_specs.append(pl.BlockSpec((None, 1, tn), lambda j, i: (wl, 0, j)))
        if glu:
            args.append(bias)
            in_specs.append(pl.BlockSpec((None, 1, tn), lambda j, i: (wl, 0, nblk + j)))
    if resid is not None:
        x, mod4, modtok, layer, gcol = resid
        cb = gcol * nblk
        args += [x, mod4, modtok]
        in_specs += [
            pl.BlockSpec((ROW_TILE, tn), lambda j, i: (i, j)),
            pl.BlockSpec((None, None, 1, tn),
                         lambda j, i: (layer, jnp.minimum(i // seq_tiles, n_batch - 1), 0, cb + j)),
            pl.BlockSpec((None, ROW_TILE, tn), lambda j, i: (layer, 0, cb + j)),
        ]
    head_dim = 0
    if head_norm is not None:
        hgain, hmask, head_dim = head_norm
        seg = (jnp.arange(tn, dtype=jnp.int32)[:, None] // head_dim
               == jnp.arange(128, dtype=jnp.int32)[None, :]).astype(BF16)
        args += [hgain, hmask, seg, jnp.concatenate([seg.T, seg.T], axis=0)]
        in_specs += [pl.BlockSpec((1, tn), lambda j, i: (0, j)), pl.BlockSpec((1, tn), lambda j, i: (0, j)),
                     pl.BlockSpec((tn, 128), lambda j, i: (0, 0)), pl.BlockSpec((256, tn), lambda j, i: (0, 0))]
    scratch = [pltpu.VMEM((k, tn), BF16)] + ([pltpu.VMEM((k, tn), BF16)] if glu else [])
    body = functools.partial(_linear_body, n_prompt_tiles=n_prompt_tiles, pair=pair, glu=glu,
                             has_bias=bias is not None, has_resid=resid is not None, head_dim=head_dim)
    return pl.pallas_call(
        body,
        out_shape=jax.ShapeDtypeStruct((r, n_out), out_dtype),
        grid=(nblk, n_tiles),
        in_specs=in_specs,
        out_specs=pl.BlockSpec((ROW_TILE, tn), lambda j, i: (i, j)),
        scratch_shapes=scratch,
        compiler_params=_cparams(("arbitrary", "arbitrary")),
        name=name,
    )(*args)


def _rotate(x, cos, sin):
    half = x.shape[-1] // 2
    x1, x2 = x[:, :half], x[:, half:]
    return jnp.concatenate([x1 * cos - x2 * sin, x1 * sin + x2 * cos], axis=-1)


def _group_norm_gate(o, g, gain, bias):
    mu = jnp.mean(o, axis=-1, keepdims=True)
    var = jnp.mean(jnp.square(o - mu), axis=-1, keepdims=True)
    on = (o - mu) * lax.rsqrt(var + NORM_EPS) * gain + bias
    g = g.astype(F32)
    return (g * jax.nn.sigmoid(g) * on).astype(BF16)


def _ret_prompt_body(q_ref, k_ref, v_ref, g_ref, cos_ref, sin_ref, dm_ref, qd_ref, kd_ref, gl_ref,
                     gng_ref, gnb_ref, o_ref, st_ref, s_acc, *, n_chunks):
    s_acc[...] = jnp.zeros_like(s_acc)
    dmask = dm_ref[...]
    qdec = qd_ref[...]
    kdec = kd_ref[...]
    gl = gl_ref[0:1, 0:1]
    gng = gng_ref[...]
    gnb = gnb_ref[...]

    def chunk(c, carry):
        r0 = pl.multiple_of(c * RET_CHUNK, RET_CHUNK)
        rows = pl.ds(r0, RET_CHUNK)
        cos = cos_ref[rows, :]
        sin = sin_ref[rows, :]
        qr = _rotate(q_ref[rows, :].astype(F32), cos, sin)
        kr = _rotate(k_ref[rows, :].astype(F32), cos, sin)
        v = v_ref[rows, :]
        state = s_acc[...]
        scores = lax.dot_general(qr.astype(BF16), kr.astype(BF16), (((1,), (1,)), ((), ())),
                                 preferred_element_type=F32)
        scores = scores * dmask
        out = jnp.dot(scores.astype(BF16), v, preferred_element_type=F32)
        out = out + jnp.dot((qr * qdec).astype(BF16), state.astype(BF16), preferred_element_type=F32)
        kv = lax.dot_general((kr * kdec).astype(BF16), v, (((0,), (0,)), ((), ())), preferred_element_type=F32)
        s_acc[...] = gl * state + kv
        o_ref[rows, :] = _group_norm_gate(out, g_ref[rows, :], gng, gnb)
        return carry

    lax.fori_loop(0, n_chunks, chunk, 0, unroll=8)
    st_ref[...] = s_acc[...]


def _ret_tables(chunk, dk, n_valid=None):
    n_valid = chunk if n_valid is None else n_valid
    lg = jnp.log1p(-jnp.exp2(-5.0 - jnp.arange(RET_HEADS, dtype=F32)))
    idx = jnp.arange(chunk, dtype=F32)
    diff = idx[:, None] - idx[None, :]
    inside = (idx[:, None] < n_valid) & (idx[None, :] < n_valid)
    dmask = jnp.where((diff[None] >= 0) & inside[None],
                      jnp.exp(jnp.maximum(diff, 0.0)[None] * lg[:, None, None]), 0.0) * (dk ** -0.5)
    qdec = jnp.exp((idx[None, :] + 1.0) * lg[:, None])
    kdec = jnp.where(idx[None, :] < n_valid, jnp.exp((n_valid - 1.0 - idx)[None, :] * lg[:, None]), 0.0) * (dk ** -0.5)
    gl = jnp.exp(n_valid * lg)
    qdec = jnp.broadcast_to(qdec[:, :, None], (RET_HEADS, chunk, dk))
    kdec = jnp.broadcast_to(kdec[:, :, None], (RET_HEADS, chunk, dk))
    gl = jnp.broadcast_to(gl[:, None, None], (RET_HEADS, 8, 128))
    return dmask, qdec, kdec, gl


def _rope_tables(pos, half):
    inv = ROPE_BASE ** (-jnp.arange(half, dtype=F32) / half)
    ang = pos.astype(F32)[:, None] * inv[None, :]
    return jnp.cos(ang), jnp.sin(ang)


def _ret_prompt_call(proj, gn_gain, gn_bias, j, *, n_batch, seq, dk, dv):
    h = RET_HEADS
    n_chunks = seq // RET_CHUNK
    cos, sin = _rope_tables(jnp.arange(seq, dtype=jnp.int32), dk // 2)
    dmask, qdec, kdec, gl = _ret_tables(RET_CHUNK, dk)
    kcol, vcol, gcol = h, (2 * h * dk) // dv, (2 * h * dk) // dv + h
    body = functools.partial(_ret_prompt_body, n_chunks=n_chunks)
    return pl.pallas_call(
        body,
        out_shape=[jax.ShapeDtypeStruct((n_batch * seq, h * dv), BF16),
                   jax.ShapeDtypeStruct((n_batch, h, dk, dv), F32)],
        grid=(n_batch, h),
        in_specs=[
            pl.BlockSpec((seq, dk), lambda b, hh: (b, hh)),
            pl.BlockSpec((seq, dk), lambda b, hh: (b, kcol + hh)),
            pl.BlockSpec((seq, dv), lambda b, hh: (b, vcol + hh)),
            pl.BlockSpec((seq, dv), lambda b, hh: (b, gcol + hh)),
            pl.BlockSpec((seq, dk // 2), lambda b, hh: (0, 0)),
            pl.BlockSpec((seq, dk // 2), lambda b, hh: (0, 0)),
            pl.BlockSpec((None, RET_CHUNK, RET_CHUNK), lambda b, hh: (hh, 0, 0)),
            pl.BlockSpec((None, RET_CHUNK, dk), lambda b, hh: (hh, 0, 0)),
            pl.BlockSpec((None, RET_CHUNK, dk), lambda b, hh: (hh, 0, 0)),
            pl.BlockSpec((None, 8, 128), lambda b, hh: (hh, 0, 0)),
            pl.BlockSpec((None, 1, dv), lambda b, hh: (j, 0, hh)),
            pl.BlockSpec((None, 1, dv), lambda b, hh: (j, 0, hh)),
        ],
        out_specs=[pl.BlockSpec((seq, dv), lambda b, hh: (b, hh)),
                   pl.BlockSpec((None, None, dk, dv), lambda b, hh: (b, hh, 0, 0))],
        scratch_shapes=[pltpu.VMEM((dk, dv), F32)],
        compiler_params=_cparams(("arbitrary", "arbitrary")),
        name="retention_prompt",
    )(proj, proj, proj, proj, cos, sin, dmask, qdec, kdec, gl, gn_gain, gn_bias)


def _ret_sample_body(p_ref, st_ref, cos_ref, sin_ref, dm_ref, qd_ref, kd_ref, gl_ref, gng_ref, gnb_ref,
                     *rest, dk, dv):
    o_ref, so_ref = rest[-2], rest[-1]
    h = RET_HEADS
    cos = cos_ref[...]
    sin = sin_ref[...]
    pad = 128 - SAMPLE_ROWS
    for hh in range(h):
        q = p_ref[:, hh * dk:(hh + 1) * dk].astype(F32)
        k = p_ref[:, h * dk + hh * dk:h * dk + (hh + 1) * dk].astype(F32)
        v = p_ref[:, 2 * h * dk + hh * dv:2 * h * dk + (hh + 1) * dv]
        g = p_ref[:, 2 * h * dk + h * dv + hh * dv:2 * h * dk + h * dv + (hh + 1) * dv]
        qr = _rotate(q, cos, sin)
        kr = _rotate(k, cos, sin)
        k_pad = jnp.concatenate([kr.astype(BF16), jnp.zeros((pad, dk), BF16)], axis=0)
        kd_pad = jnp.concatenate([(kr * kd_ref[hh]).astype(BF16), jnp.zeros((pad, dk), BF16)], axis=0)
        v_pad = jnp.concatenate([v, jnp.zeros((pad, dv), BF16)], axis=0)
        state = st_ref[hh]
        scores = lax.dot_general(qr.astype(BF16), k_pad, (((1,), (1,)), ((), ())), preferred_element_type=F32)
        scores = scores * dm_ref[hh]
        out = jnp.dot(scores.astype(BF16), v_pad, preferred_element_type=F32)
        out = out + jnp.dot((qr * qd_ref[hh]).astype(BF16), state.astype(BF16), preferred_element_type=F32)
        kv = lax.dot_general(kd_pad, v_pad, (((0,), (0,)), ((), ())), preferred_element_type=F32)
        so_ref[hh] = gl_ref[hh, 0:1, 0:1] * state + kv
        o_ref[:, hh * dv:(hh + 1) * dv] = _group_norm_gate(out, g, gng_ref[:, hh * dv:(hh + 1) * dv],
                                                           gnb_ref[:, hh * dv:(hh + 1) * dv])


def _ret_sample_call(proj, state, gn_gain, gn_bias, j, new_state, *, n_prompt_rows, n_seq, dec_seq, dk, dv):
    h = RET_HEADS
    s = SAMPLE_ROWS
    cos, sin = _rope_tables(PAST_LEN + jnp.arange(s, dtype=jnp.int32), dk // 2)
    dmask, qdec, kdec, gl = _ret_tables(s, dk, n_valid=dec_seq)
    dmask = jnp.pad(dmask, ((0, 0), (0, 0), (0, 128 - s)))
    base = n_prompt_rows // s
    body = functools.partial(_ret_sample_body, dk=dk, dv=dv)
    width = proj.shape[1]
    args = [proj, state, cos, sin, dmask, qdec, kdec, gl, gn_gain, gn_bias]
    in_specs = [
        pl.BlockSpec((s, width), lambda b: (base + b, 0)),
        pl.BlockSpec((None, None, h, dk, dv), lambda b: (j, b, 0, 0, 0)),
        pl.BlockSpec((s, dk // 2), lambda b: (0, 0)),
        pl.BlockSpec((s, dk // 2), lambda b: (0, 0)),
        pl.BlockSpec((h, s, 128), lambda b: (0, 0, 0)),
        pl.BlockSpec((h, s, dk), lambda b: (0, 0, 0)),
        pl.BlockSpec((h, s, dk), lambda b: (0, 0, 0)),
        pl.BlockSpec((h, 8, 128), lambda b: (0, 0, 0)),
        pl.BlockSpec((None, 1, h * dv), lambda b: (j, 0, 0)),
        pl.BlockSpec((None, 1, h * dv), lambda b: (j, 0, 0)),
    ]
    aliases = {}
    if new_state is not None:
        aliases = {len(args): 1}
        args.append(new_state)
        in_specs.append(pl.BlockSpec(memory_space=pl.ANY))
    return pl.pallas_call(
        body,
        out_shape=[jax.ShapeDtypeStruct((n_seq * s, h * dv), BF16),
                   jax.ShapeDtypeStruct(state.shape, F32)],
        grid=(n_seq,),
        in_specs=in_specs,
        out_specs=[pl.BlockSpec((s, h * dv), lambda b: (b, 0)),
                   pl.BlockSpec((None, None, h, dk, dv), lambda b: (j, b, 0, 0, 0))],
        input_output_aliases=aliases,
        compiler_params=_cparams(("arbitrary",)),
        name="retention_sample",
    )(*args)


CONV_HALO = 32
CONV_ROW_CHUNK = 64
CONV_LANES = 128


def _layer_norm_swish(u, gain, bias):
    mu = jnp.mean(u, axis=-1, keepdims=True)
    var = jnp.mean(jnp.square(u - mu), axis=-1, keepdims=True)
    un = (u - mu) * lax.rsqrt(var + NORM_EPS) * gain + bias
    return (un * jax.nn.sigmoid(un)).astype(BF16)


def _conv_prompt_body(z_ref, w_ref, bdw_ref, lng_ref, lnb_ref, o_ref, zbuf, ubuf, *, tt, d):
    t = pl.program_id(1)

    @pl.when(t == 0)
    def _():
        zbuf[0:CONV_HALO, :] = jnp.zeros((CONV_HALO, d), F32)

    @pl.when(t > 0)
    def _():
        zbuf[0:CONV_HALO, :] = zbuf[tt:tt + CONV_HALO, :]

    zbuf[CONV_HALO:CONV_HALO + tt, :] = z_ref[...].astype(F32)

    span = CONV_ROW_CHUNK + CONV_HALO
    n_row_chunks = tt // CONV_ROW_CHUNK
    n_strips = d // CONV_LANES

    def strip(n, carry):
        r0 = pl.multiple_of((n % n_row_chunks) * CONV_ROW_CHUNK, CONV_ROW_CHUNK)
        c0 = pl.multiple_of((n // n_row_chunks) * CONV_LANES, CONV_LANES)
        cols = pl.ds(c0, CONV_LANES)
        blk = zbuf[pl.ds(r0, span), cols]
        acc = jnp.zeros((CONV_ROW_CHUNK, CONV_LANES), F32)
        for b in range(8):
            rb = blk if b == 0 else pltpu.roll(blk, span - b, axis=0)
            for a in range(5):
                o = 8 * a + b
                if 2 <= o <= CONV_HALO:
                    acc = acc + rb[8 * a:8 * a + CONV_ROW_CHUNK, :] * w_ref[pl.ds(o - 2, 1), cols]
        ubuf[pl.ds(r0, CONV_ROW_CHUNK), cols] = acc
        return carry

    lax.fori_loop(0, n_row_chunks * n_strips, strip, 0, unroll=2)

    ln_rows = 128

    def ln_chunk(c, carry):
        rows = pl.ds(pl.multiple_of(c * ln_rows, ln_rows), ln_rows)
        o_ref[rows, :] = _layer_norm_swish(ubuf[rows, :] + bdw_ref[...], lng_ref[...], lnb_ref[...])
        return carry

    lax.fori_loop(0, tt // ln_rows, ln_chunk, 0)


def _conv_prompt_call(z, w_dw, b_dw, ln_gain, ln_bias, *, n_batch, seq):
    d = z.shape[1]
    tt = ROW_TILE
    nt = seq // tt
    w_pad = jnp.pad(w_dw, ((0, 32 - CONV_WIDTH), (0, 0)))
    body = functools.partial(_conv_prompt_body, tt=tt, d=d)
    vec = pl.BlockSpec((1, d), lambda b, t: (0, 0))
    return pl.pallas_call(
        body,
        out_shape=jax.ShapeDtypeStruct((n_batch * seq, d), BF16),
        grid=(n_batch, nt),
        in_specs=[pl.BlockSpec((tt, d), lambda b, t: (b * nt + t, 0)),
                  pl.BlockSpec((32, d), lambda b, t: (0, 0)), vec, vec, vec],
        out_specs=pl.BlockSpec((tt, d), lambda b, t: (b * nt + t, 0)),
        scratch_shapes=[pltpu.VMEM((tt + CONV_HALO, d), F32), pltpu.VMEM((tt, d), F32)],
        compiler_params=_cparams(("arbitrary", "arbitrary")),
        name="conv_prompt",
    )(z, w_pad, b_dw, ln_gain, ln_bias)


def _conv_sample_body(z_ref, c_ref, w_ref, bdw_ref, lng_ref, lnb_ref, o_ref, zbuf, *, d):
    s = SAMPLE_ROWS
    zbuf[0:CONV_STATE, :] = c_ref[...]
    zbuf[CONV_STATE:CONV_STATE + s, :] = z_ref[...].astype(F32)
    acc = jnp.zeros((s, d), F32)
    for j in range(CONV_WIDTH):
        acc = acc + zbuf[j:j + s, :] * w_ref[j:j + 1, :]
    o_ref[...] = _layer_norm_swish(acc + bdw_ref[...], lng_ref[...], lnb_ref[...])


def _conv_sample_call(z, cache, w_dw, b_dw, ln_gain, ln_bias, *, n_prompt_rows, n_seq):
    d = z.shape[1]
    s = SAMPLE_ROWS
    base = n_prompt_rows // s
    body = functools.partial(_conv_sample_body, d=d)
    vec = pl.BlockSpec((1, d), lambda b: (0, 0))
    return pl.pallas_call(
        body,
        out_shape=jax.ShapeDtypeStruct((n_seq * s, d), BF16),
        grid=(n_seq,),
        in_specs=[pl.BlockSpec((s, d), lambda b: (base + b, 0)),
                  pl.BlockSpec((None, CONV_STATE, d), lambda b: (b, 0, 0)),
                  pl.BlockSpec((CONV_WIDTH, d), lambda b: (0, 0)), vec, vec, vec],
        out_specs=pl.BlockSpec((s, d), lambda b: (b, 0)),
        scratch_shapes=[pltpu.VMEM((CONV_STATE + s, d), F32)],
        compiler_params=_cparams(("arbitrary",)),
        name="conv_sample",
    )(z, cache, w_dw, b_dw, ln_gain, ln_bias)


def _t5_bucket(dist):
    n = jnp.maximum(dist, 0)
    max_exact = REL_BUCKETS // 2
    nf = jnp.maximum(n, 1).astype(F32)
    large = max_exact + (jnp.log(nf / max_exact) / math.log(REL_MAX_DIST / max_exact)
                         * (REL_BUCKETS - max_exact)).astype(jnp.int32)
    large = jnp.minimum(large, REL_BUCKETS - 1)
    return jnp.where(n < max_exact, n, large)


def _bias_table(rel_bias, dist, valid):
    onehot = (_t5_bucket(dist).reshape(-1)[None, :] == jnp.arange(REL_BUCKETS, dtype=jnp.int32)[:, None]).astype(F32)
    tbl = jnp.dot(rel_bias.astype(F32).T, onehot, precision=lax.Precision.HIGHEST)
    tbl = jnp.where(valid.reshape(-1)[None, :], tbl, NEG_INF)
    return tbl.reshape(-1, dist.shape[-1])


def _swa_prompt_body(sink_ref, q_ref, kvp_ref, kvc_ref, bias_ref, o_ref, *, hd):
    i = pl.program_id(1)
    blk = WINDOW
    g = ATT_GROUP
    nkv = ATT_KV_HEADS
    nk = nkv * hd
    col = lax.broadcasted_iota(jnp.int32, (1, 1, 2 * blk), 2)
    first_mask = jnp.where(jnp.logical_and(i == 0, col < blk), NEG_INF, 0.0)
    keys = jnp.stack([jnp.concatenate([kvp_ref[:, hk * hd:(hk + 1) * hd], kvc_ref[:, hk * hd:(hk + 1) * hd]], axis=0)
                      for hk in range(nkv)])
    vals = jnp.stack([jnp.concatenate([kvp_ref[:, nk + hk * hd:nk + (hk + 1) * hd],
                                       kvc_ref[:, nk + hk * hd:nk + (hk + 1) * hd]], axis=0)
                      for hk in range(nkv)])
    qs = jnp.stack([jnp.concatenate([q_ref[:, (hk * g + gg) * hd:(hk * g + gg + 1) * hd] for gg in range(g)], axis=0)
                    for hk in range(nkv)])
    s = jnp.einsum('hqd,hkd->hqk', qs, keys, preferred_element_type=F32)
    s = s + bias_ref[...].reshape(nkv, g * blk, 2 * blk) + first_mask
    sink = jnp.stack([jnp.concatenate([jnp.full((blk, 1), sink_ref[hk * g + gg], F32) for gg in range(g)], axis=0)
                      for hk in range(nkv)])
    m = jnp.maximum(jnp.max(s, axis=-1, keepdims=True), sink)
    p = jnp.exp(s - m).astype(BF16)
    den = jnp.einsum('hqk,hkd->hqd', p, jnp.ones(vals.shape, BF16), preferred_element_type=F32) + jnp.exp(sink - m)
    o = jnp.einsum('hqk,hkd->hqd', p, vals, preferred_element_type=F32) / den
    for hk in range(nkv):
        for gg in range(g):
            o_ref[:, (hk * g + gg) * hd:(hk * g + gg + 1) * hd] = o[hk, gg * blk:(gg + 1) * blk, :].astype(BF16)


def _swa_prompt_call(proj, sinks, rel_bias, *, n_batch, seq, hd):
    blk = WINDOW
    nb = seq // blk
    nq = ATT_Q_HEADS * hd
    nkv2 = 2 * ATT_KV_HEADS * hd
    kvcol = nq // nkv2
    i_idx = jnp.arange(blk, dtype=jnp.int32)[:, None]
    j_idx = jnp.arange(2 * blk, dtype=jnp.int32)[None, :]
    dist = blk + i_idx - j_idx
    bias = _bias_table(rel_bias, dist, (dist >= 0) & (dist < WINDOW))
    body = functools.partial(_swa_prompt_body, hd=hd)
    return pl.pallas_call(
        body,
        out_shape=jax.ShapeDtypeStruct((n_batch * seq, nq), BF16),
        grid=(n_batch, nb),
        in_specs=[
            pl.BlockSpec(memory_space=pltpu.SMEM),
            pl.BlockSpec((blk, nq), lambda b, i: (b * nb + i, 0)),
            pl.BlockSpec((blk, nkv2), lambda b, i: (b * nb + jnp.maximum(i - 1, 0), kvcol)),
            pl.BlockSpec((blk, nkv2), lambda b, i: (b * nb + i, kvcol)),
            pl.BlockSpec((ATT_Q_HEADS * blk, 2 * blk), lambda b, i: (0, 0)),
        ],
        out_specs=pl.BlockSpec((blk, nq), lambda b, i: (b * nb + i, 0)),
        compiler_params=_cparams(("arbitrary", "arbitrary")),
        name="swa_prompt",
    )(sinks, proj, proj, proj, bias)


def _swa_sample_body(sink_ref, p_ref, ck_ref, cv_ref, bias_ref, o_ref, *, hd):
    s = SAMPLE_ROWS
    g = ATT_GROUP
    nkv = ATT_KV_HEADS
    nq = ATT_Q_HEADS * hd
    keys = jnp.stack([jnp.concatenate([ck_ref[:, hk * hd:(hk + 1) * hd].astype(BF16),
                                       p_ref[:, nq + hk * hd:nq + (hk + 1) * hd]], axis=0)
                      for hk in range(nkv)])
    vals = jnp.stack([jnp.concatenate([cv_ref[:, hk * hd:(hk + 1) * hd].astype(BF16),
                                       p_ref[:, nq + (nkv + hk) * hd:nq + (nkv + hk + 1) * hd]], axis=0)
                      for hk in range(nkv)])
    qs = jnp.stack([jnp.concatenate([p_ref[:, (hk * g + gg) * hd:(hk * g + gg + 1) * hd] for gg in range(g)], axis=0)
                    for hk in range(nkv)])
    sc = jnp.einsum('hqd,hkd->hqk', qs, keys, preferred_element_type=F32)
    sc = sc + bias_ref[...].reshape(nkv, g * s, keys.shape[1])
    sink = jnp.stack([jnp.concatenate([jnp.full((s, 1), sink_ref[hk * g + gg], F32) for gg in range(g)], axis=0)
                      for hk in range(nkv)])
    m = jnp.maximum(jnp.max(sc, axis=-1, keepdims=True), sink)
    p = jnp.exp(sc - m).astype(BF16)
    den = jnp.einsum('hqk,hkd->hqd', p, jnp.ones(vals.shape, BF16), preferred_element_type=F32) + jnp.exp(sink - m)
    o = jnp.einsum('hqk,hkd->hqd', p, vals, preferred_element_type=F32) / den
    for hk in range(nkv):
        for gg in range(g):
            o_ref[:, (hk * g + gg) * hd:(hk * g + gg + 1) * hd] = o[hk, gg * s:(gg + 1) * s, :].astype(BF16)


def _swa_sample_call(proj, cache_k, cache_v, sinks, rel_bias, *, n_prompt_rows, n_seq, dec_seq, hd):
    s = SAMPLE_ROWS
    nq = ATT_Q_HEADS * hd
    nkv = ATT_KV_HEADS * hd
    sc = cache_k.shape[1]
    i_idx = jnp.arange(s, dtype=jnp.int32)[:, None]
    j_idx = jnp.arange(sc + s, dtype=jnp.int32)[None, :]
    dist = sc + i_idx - j_idx
    valid = (dist >= 0) & (dist < WINDOW) & (j_idx < sc + dec_seq)
    bias = _bias_table(rel_bias, dist, valid)
    base = n_prompt_rows // s
    body = functools.partial(_swa_sample_body, hd=hd)
    return pl.pallas_call(
        body,
        out_shape=jax.ShapeDtypeStruct((n_seq * s, nq), BF16),
        grid=(n_seq,),
        in_specs=[
            pl.BlockSpec(memory_space=pltpu.SMEM),
            pl.BlockSpec((s, proj.shape[1]), lambda b: (base + b, 0)),
            pl.BlockSpec((None, sc, nkv), lambda b: (b, 0, 0)),
            pl.BlockSpec((None, sc, nkv), lambda b: (b, 0, 0)),
            pl.BlockSpec((ATT_Q_HEADS * s, sc + s), lambda b: (0, 0)),
        ],
        out_specs=pl.BlockSpec((s, nq), lambda b: (b, 0)),
        compiler_params=_cparams(("arbitrary",)),
        name="swa_sample",
    )(sinks, proj, cache_k, cache_v, bias)


MOE_TILE = 256


def _moe_body(src_ref, dst_ref, te_ref, nu_ref, h_hbm, win_ref, wout_ref, y_hbm, xg, yb, wib, wob, gsem, ssem,
              *, ff):
    t = pl.program_id(0)
    nt = pl.num_programs(0)
    n_used = nu_ref[0]
    tm = MOE_TILE
    slot = lax.rem(t, 2)

    def gather_start(tile, sl):
        base = tile * tm
        for r in range(tm):
            pltpu.make_async_copy(h_hbm.at[pl.ds(src_ref[base + r], 1), :], xg.at[sl, pl.ds(r, 1), :],
                                  gsem.at[sl]).start()

    def gather_wait(sl):
        pltpu.make_async_copy(h_hbm.at[pl.ds(0, tm), :], xg.at[sl], gsem.at[sl]).wait()

    def scatter_start(tile, sl):
        base = tile * tm
        for r in range(tm):
            pltpu.make_async_copy(yb.at[sl, pl.ds(r, 1), :], y_hbm.at[pl.ds(dst_ref[base + r], 1), :],
                                  ssem.at[sl]).start()

    def scatter_wait(sl):
        pltpu.make_async_copy(yb.at[sl], y_hbm.at[pl.ds(0, tm), :], ssem.at[sl]).wait()

    valid = t < n_used

    @pl.when(t == 0)
    def _():
        gather_start(0, 0)

    @pl.when(jnp.logical_and(valid, jnp.logical_or(t == 0, te_ref[t] != te_ref[jnp.maximum(t - 1, 0)])))
    def _():
        wib[...] = win_ref[...].astype(BF16)
        wob[...] = wout_ref[...].astype(BF16)

    def step(sl):
        @pl.when(t >= 2)
        def _():
            scatter_wait(sl)

        gather_wait(sl)

        @pl.when(t + 1 < n_used)
        def _():
            gather_start(t + 1, 1 - sl)

        gu = jnp.dot(_unpack_bf16_pairs(xg[sl]).astype(BF16), wib[...], preferred_element_type=F32)
        gate, up = gu[:, :ff], gu[:, ff:]
        act = (gate * jax.nn.sigmoid(gate) * up).astype(BF16)
        yb[sl] = _pack_bf16_pairs(jnp.dot(act, wob[...], preferred_element_type=F32))
        scatter_start(t, sl)

    for sl in range(2):
        pl.when(jnp.logical_and(valid, slot == sl))(functools.partial(step, sl))

    @pl.when(t == nt - 1)
    def _():
        scatter_wait(lax.rem(n_used - 1, 2))
        scatter_wait(lax.rem(n_used, 2))


def _moe_layer(h2, route, counts, w_in, w_out, layer):
    r = h2.shape[0]
    depth, n_exp, d, ff2 = w_in.shape
    ff = ff2 // 2
    tm = MOE_TILE
    n_tiles = 2 * r // tm + n_exp
    n_slots = n_tiles * tm

    cnt = counts[0, MOE_GROUPS:MOE_GROUPS + n_exp].astype(jnp.int32)
    padded = ((cnt + tm - 1) // tm) * tm
    ends = jnp.cumsum(padded)
    starts = ends - padded
    rt = route[:, :8].T.astype(jnp.int32)
    pos = jnp.concatenate([starts[rt[ROUTE_E1]] + rt[ROUTE_R1], starts[rt[ROUTE_E2]] + rt[ROUTE_R2]])
    tok = jnp.arange(r, dtype=jnp.int32)
    dst = (2 * r + jnp.arange(n_slots, dtype=jnp.int32)).at[pos].set(
        jnp.concatenate([tok, r + tok]), unique_indices=True, indices_are_sorted=False)
    src = jnp.where(dst < r, dst, jnp.where(dst < 2 * r, dst - r, 0))
    tile_start = jnp.arange(n_tiles, dtype=jnp.int32) * tm
    tile_expert = jnp.minimum(jnp.sum(tile_start[:, None] >= ends[None, :], axis=1), n_exp - 1).astype(jnp.int32)
    n_used = (ends[-1:] // tm).astype(jnp.int32)

    return pl.pallas_call(
        functools.partial(_moe_body, ff=ff),
        out_shape=jax.ShapeDtypeStruct((2 * r + n_slots, d // 2), jnp.uint32),
        grid_spec=pltpu.PrefetchScalarGridSpec(
            num_scalar_prefetch=4,
            grid=(n_tiles,),
            in_specs=[
                pl.BlockSpec(memory_space=pl.ANY),
                pl.BlockSpec((None, None, d, ff2), lambda t, src, dst, te, nu: (layer, te[t], 0, 0)),
                pl.BlockSpec((None, None, ff, d), lambda t, src, dst, te, nu: (layer, te[t], 0, 0)),
            ],
            out_specs=pl.BlockSpec(memory_space=pl.ANY),
            scratch_shapes=[pltpu.VMEM((2, tm, d // 2), jnp.uint32), pltpu.VMEM((2, tm, d // 2), jnp.uint32),
                            pltpu.VMEM((d, ff2), BF16), pltpu.VMEM((ff, d), BF16),
                            pltpu.SemaphoreType.DMA((2,)), pltpu.SemaphoreType.DMA((2,))],
        ),
        compiler_params=_cparams(("arbitrary",)),
        name="moe_experts",
    )(src, dst, tile_expert, n_used, h2, w_in, w_out)


def kernel(x_prompt, x_sample, c_prompt, c_sample, state_ret, cache_conv, cache_swa_k, cache_swa_v, ada_w, ada_b, norm_gain, ret_w_in, ret_gn_gain, ret_gn_bias, ret_w_out, conv_w_pw1, conv_b_pw1, conv_w_dw, conv_b_dw, conv_ln_gain, conv_ln_bias, conv_w_pw2, conv_b_pw2, att_w_qkv, att_q_gain, att_k_gain, att_sinks, att_w_o, rel_bias, moe_wg, moe_bg, moe_we, moe_be, moe_w_in, moe_w_out):
    n_batch, seq, d = x_prompt.shape
    n_seq, dec_seq, _ = x_sample.shape
    depth = ada_w.shape[0]
    s = SAMPLE_ROWS
    assert n_seq * s == ROW_TILE and seq % ROW_TILE == 0 and dec_seq <= s
    n_prompt_rows = n_batch * seq
    seq_tiles = seq // ROW_TILE
    dk = ret_w_in.shape[2] // (6 * RET_HEADS)
    dv = 2 * dk
    hd = d // ATT_Q_HEADS
    geom = dict(seq_tiles=seq_tiles, n_batch=n_batch)

    xs_pad = jnp.pad(x_sample, ((0, 0), (0, s - dec_seq), (0, 0))).reshape(n_seq * s, d)
    x = jnp.concatenate([x_prompt.reshape(n_prompt_rows, d), xs_pad], axis=0)

    n_c = n_batch + n_seq
    c_rows = ((n_c + 7) // 8) * 8
    c_all = jnp.pad(jnp.concatenate([c_prompt, c_sample], axis=0), ((0, c_rows - n_c), (0, 0)))
    mod = _ada_call(c_all, ada_w, ada_b)
    mod4 = mod.reshape(depth, c_rows, 1, 6 * d)
    modtok = jnp.repeat(mod[:, n_batch:n_c], s, axis=1)

    gains = norm_gain.astype(F32).reshape(2 * depth, 1, d)
    gng_all = ret_gn_gain.astype(F32)[:, None, :]
    gnb_all = ret_gn_bias.astype(F32)[:, None, :]
    state_all = state_ret.astype(F32)

    ret_p, conv_p, conv_s, kp_l, vp_l, ks_l, vs_l = [], [], [], [], [], [], []
    ret_s = None
    (h,) = _norm_call(x, mod4, modtok, norm=(gains, 0, 0, 0, 1), **geom)
    for l in range(depth):
        kind, j = l % 3, l // 3
        resid1 = (x, mod4, modtok, l, 2)
        if kind == 0:
            proj = _linear_call(h, ret_w_in, j, n_out=ret_w_in.shape[2], tn=2048, out_dtype=BF16,
                                name="ret_in", **geom)
            a_p, st_p = _ret_prompt_call(proj, gng_all, gnb_all, j, n_batch=n_batch, seq=seq, dk=dk, dv=dv)
            a_s, ret_s = _ret_sample_call(proj, state_all, gng_all, gnb_all, j, ret_s, n_prompt_rows=n_prompt_rows,
                                          n_seq=n_seq, dec_seq=dec_seq, dk=dk, dv=dv)
            ret_p.append(st_p)
            x = _linear_call(a_p, ret_w_out, j, h_sample=a_s, n_out=d, tn=512, out_dtype=F32, resid=resid1,
                             name="ret_out", **geom)
        elif kind == 1:
            z = _linear_call(h, conv_w_pw1, j, n_out=d, tn=1024, out_dtype=BF16, bias=conv_b_pw1,
                             glu=True, name="conv_pw1", **geom)
            cargs = (conv_w_dw[j], conv_b_dw[j][None], conv_ln_gain[j][None], conv_ln_bias[j][None])
            a_p = _conv_prompt_call(z, *cargs, n_batch=n_batch, seq=seq)
            a_s = _conv_sample_call(z, cache_conv[j].astype(F32), *cargs, n_prompt_rows=n_prompt_rows, n_seq=n_seq)
            z_tail = z[:n_prompt_rows].reshape(n_batch, seq, d)[:, seq - CONV_STATE:].astype(F32)
            conv_p.append(z_tail)
            z_new = z[n_prompt_rows:].reshape(n_seq, s, d)[:, :dec_seq].astype(F32)
            conv_s.append(jnp.concatenate([cache_conv[j].astype(F32), z_new], axis=1)[:, -CONV_STATE:])
            x = _linear_call(a_p, conv_w_pw2, j, h_sample=a_s, n_out=d, tn=1024, out_dtype=F32,
                             bias=conv_b_pw2, resid=resid1, name="conv_pw2", **geom)
        else:
            nkv = ATT_KV_HEADS * hd
            nq = ATT_Q_HEADS * hd
            hgain = jnp.concatenate([jnp.tile(att_q_gain[j].astype(F32) * (hd ** -0.5), ATT_Q_HEADS),
                                     jnp.tile(att_k_gain[j].astype(F32), ATT_KV_HEADS), jnp.ones((nkv,), F32)])[None]
            hmask = jnp.concatenate([jnp.ones((nq + nkv,), F32), jnp.zeros((nkv,), F32)])[None]
            proj = _linear_call(h, att_w_qkv, j, n_out=att_w_qkv.shape[2], tn=1280, out_dtype=BF16,
                                head_norm=(hgain, hmask, hd), name="att_qkv", **geom)
            sinks = att_sinks[j].astype(F32)
            a_p = _swa_prompt_call(proj, sinks, rel_bias, n_batch=n_batch, seq=seq, hd=hd)
            win = cache_swa_k.shape[2]
            ck = cache_swa_k[j].astype(F32).reshape(n_seq, win, nkv)
            cv = cache_swa_v[j].astype(F32).reshape(n_seq, win, nkv)
            a_s = _swa_sample_call(proj, ck, cv, sinks, rel_bias, n_prompt_rows=n_prompt_rows,
                                   n_seq=n_seq, dec_seq=dec_seq, hd=hd)
            kv_tail = proj[:n_prompt_rows, nq:].reshape(n_batch, seq, 2 * nkv)[:, seq - WINDOW:].astype(F32)
            kp_l.append(kv_tail[:, :, :nkv].reshape(n_batch, WINDOW, ATT_KV_HEADS, hd))
            vp_l.append(kv_tail[:, :, nkv:].reshape(n_batch, WINDOW, ATT_KV_HEADS, hd))
            kv_new = proj[n_prompt_rows:, nq:].reshape(n_seq, s, 2 * nkv)[:, :dec_seq].astype(F32)
            k_new, v_new = kv_new[:, :, :nkv], kv_new[:, :, nkv:]
            ks_l.append(jnp.concatenate([ck, k_new], axis=1)[:, -win:].reshape(n_seq, win, ATT_KV_HEADS, hd))
            vs_l.append(jnp.concatenate([cv, v_new], axis=1)[:, -win:].reshape(n_seq, win, ATT_KV_HEADS, hd))
            x = _linear_call(a_p, att_w_o, j, h_sample=a_s, n_out=d, tn=1024, out_dtype=F32, resid=resid1,
                             name="att_out", **geom)

        lane_pad = ROUTER_LANES - MOE_GROUPS - MOE_GROUPS * MOE_EPG
        router_w = jnp.pad(jnp.concatenate([moe_wg[l], moe_we[l]], axis=1).astype(F32), ((0, 0), (0, lane_pad)))
        router_hi = router_w.astype(BF16)
        router_w = jnp.stack([router_hi, (router_w - router_hi.astype(F32)).astype(BF16)])
        router_b = jnp.pad(jnp.concatenate([moe_bg[l], moe_be[l]]).astype(F32), (0, lane_pad))[None]
        h2, route, counts = _norm_call(x, mod4, modtok, norm=(gains, 2 * l + 1, l, 3, 4),
                                       router=(router_w, router_b), **geom)
        ypair = _moe_layer(h2, route, counts, moe_w_in, moe_w_out, l)
        if l + 1 < depth:
            x, h = _norm_call(x, mod4, modtok, resid=(ypair, route, l, 5),
                              norm=(gains, 2 * l + 2, l + 1, 0, 1), **geom)
        else:
            x_p, x_s = _norm_call(x, mod4, modtok, resid=(ypair, route, l, 5), split_out=True, **geom)

    y_prompt = x_p.reshape(n_batch, seq, d)
    y_sample = x_s.reshape(n_seq, s, d)[:, :dec_seq]
    return (y_prompt, y_sample, jnp.stack(ret_p), ret_s, jnp.stack(conv_p), jnp.stack(conv_s),
            jnp.stack(kp_l), jnp.stack(vp_l), jnp.stack(ks_l), jnp.stack(vs_l))
```

```python
import functools
import math

import jax
import jax.numpy as jnp
from jax import lax
from jax.experimental import pallas as pl
from jax.experimental.pallas import tpu as pltpu

F32 = jnp.float32
BF16 = jnp.bfloat16

NORM_EPS = 1e-6
NEG_INF = -1e30
ROPE_BASE = 10000.0
PAST_LEN = 16384

RET_HEADS = 8
RET_CHUNK = 128
CONV_WIDTH = 31
CONV_STATE = CONV_WIDTH - 1
ATT_Q_HEADS = 32
ATT_KV_HEADS = 4
ATT_GROUP = ATT_Q_HEADS // ATT_KV_HEADS
WINDOW = 128
REL_BUCKETS = 32
REL_MAX_DIST = 128
MOE_GROUPS = 4
MOE_EPG = 4
MOE_TOPK = 2

ROW_TILE = 512
SAMPLE_ROWS = 16
ROUTER_LANES = 128
VMEM_LIMIT_BYTES = 56 * 1024 * 1024


def _cparams(sem):
    return pltpu.CompilerParams(dimension_semantics=sem, vmem_limit_bytes=VMEM_LIMIT_BYTES)


def _pack_bf16_pairs(y):
    half = y.shape[1] // 2
    hi = lax.bitcast_convert_type(y[:, :half].astype(BF16).astype(F32), jnp.uint32)
    lo = lax.bitcast_convert_type(y[:, half:].astype(BF16).astype(F32), jnp.uint32)
    return hi | lax.shift_right_logical(lo, jnp.uint32(16))


def _unpack_bf16_pairs(w):
    hi = lax.bitcast_convert_type(w & jnp.uint32(0xFFFF0000), F32)
    lo = lax.bitcast_convert_type(lax.shift_left(w, jnp.uint32(16)), F32)
    return jnp.concatenate([hi, lo], axis=1)


def _ada_body(c_ref, w_ref, b_ref, o_ref):
    c = c_ref[...]
    s = (c * jax.nn.sigmoid(c)).astype(BF16)
    o_ref[...] = jnp.dot(s, w_ref[...].astype(BF16), preferred_element_type=F32) + b_ref[...]


def _ada_call(c_all, ada_w, ada_b):
    depth, d, n = ada_w.shape
    rows = c_all.shape[0]
    tn = 1024
    return pl.pallas_call(
        _ada_body,
        out_shape=jax.ShapeDtypeStruct((depth, rows, n), F32),
        grid=(depth, n // tn),
        in_specs=[
            pl.BlockSpec((rows, d), lambda l, j: (0, 0)),
            pl.BlockSpec((None, d, tn), lambda l, j: (l, 0, j)),
            pl.BlockSpec((None, 1, tn), lambda l, j: (l, 0, j)),
        ],
        out_specs=pl.BlockSpec((None, rows, tn), lambda l, j: (l, 0, j)),
        compiler_params=_cparams(("arbitrary", "arbitrary")),
        name="ada_mod",
    )(c_all, ada_w, ada_b.reshape(depth, 1, n))


ROUTE_E1, ROUTE_E2, ROUTE_W1, ROUTE_W2, ROUTE_R1, ROUTE_R2 = range(6)


def _route_tile(logits, rb, carry):
    tm = logits.shape[0]
    g, epg = MOE_GROUPS, MOE_EPG
    lg = logits + rb
    lane = lax.broadcasted_iota(jnp.int32, lg.shape, 1)
    lane_f = lane.astype(F32)

    def first_lane(mask):
        return jnp.min(jnp.where(mask, lane_f, float(ROUTER_LANES)), axis=-1, keepdims=True).astype(jnp.int32)

    is_g = lane < g
    mg = jnp.max(jnp.where(is_g, lg, NEG_INF), axis=-1, keepdims=True)
    eg = jnp.where(is_g, jnp.exp(lg - mg), 0.0)
    p_grp = 1.0 / jnp.sum(eg, axis=-1, keepdims=True)
    grp = first_lane(jnp.logical_and(is_g, lg == mg))
    lo = g + epg * grp
    is_e = jnp.logical_and(lane >= lo, lane < lo + epg)
    me = jnp.max(jnp.where(is_e, lg, NEG_INF), axis=-1, keepdims=True)
    ee = jnp.where(is_e, jnp.exp(lg - me), 0.0)
    se = jnp.sum(ee, axis=-1, keepdims=True)
    i1 = first_lane(jnp.logical_and(is_e, lg == me))
    rest = jnp.logical_and(is_e, lane != i1)
    m2 = jnp.max(jnp.where(rest, ee, -1.0), axis=-1, keepdims=True)
    i2 = first_lane(jnp.logical_and(rest, ee == m2))
    p1 = 1.0 / se
    p2 = m2 / se
    w1 = p_grp * p1 / (p1 + p2)
    w2 = p_grp * p2 / (p1 + p2)
    sel1 = lane == i1
    sel2 = lane == i2
    onehot = jnp.where(jnp.logical_or(sel1, sel2), 1.0, 0.0)
    row = lax.broadcasted_iota(jnp.int32, (tm, tm), 0)
    col = lax.broadcasted_iota(jnp.int32, (tm, tm), 1)
    tri = jnp.where(col <= row, 1.0, 0.0).astype(BF16)
    incl = jnp.dot(tri, onehot.astype(BF16), preferred_element_type=F32)
    rank = incl - 1.0 + carry
    r1 = jnp.sum(jnp.where(sel1, rank, 0.0), axis=-1, keepdims=True)
    r2 = jnp.sum(jnp.where(sel2, rank, 0.0), axis=-1, keepdims=True)
    rec = jnp.zeros_like(lg)
    for ln, val in ((ROUTE_E1, (i1 - g).astype(F32)), (ROUTE_E2, (i2 - g).astype(F32)), (ROUTE_W1, w1),
                    (ROUTE_W2, w2), (ROUTE_R1, r1), (ROUTE_R2, r2)):
        rec = jnp.where(lane == ln, val, rec)
    return rec, carry + jnp.sum(onehot, axis=0, keepdims=True)


def _norm_body(*refs, n_prompt_tiles, has_resid, has_norm, has_router, split_out):
    it = iter(refs)
    x_ref = next(it)
    if has_resid:
        y0_ref, y1_ref, rt_ref, grow_ref, gtok_ref = next(it), next(it), next(it), next(it), next(it)
    if has_norm:
        gain_ref, shrow_ref, scrow_ref, shtok_ref, sctok_ref = next(it), next(it), next(it), next(it), next(it)
    if has_router:
        wr_ref, rb_ref = next(it), next(it)
    if has_resid:
        xo_ref = next(it)
        if split_out:
            xs_ref = next(it)
    if has_norm:
        h_ref = next(it)
    if has_router:
        rec_ref, cnt_ref, carry = next(it), next(it), next(it)

    is_sample = pl.program_id(0) >= n_prompt_tiles

    if has_router:
        @pl.when(pl.program_id(0) == 0)
        def _():
            carry[...] = jnp.zeros_like(carry)

    def tile(sample):
        x = x_ref[...]
        if has_resid:
            gate = gtok_ref[...] if sample else grow_ref[...]
            rt = rt_ref[...]
            y = (rt[:, ROUTE_W1:ROUTE_W1 + 1] * _unpack_bf16_pairs(y0_ref[...])
                 + rt[:, ROUTE_W2:ROUTE_W2 + 1] * _unpack_bf16_pairs(y1_ref[...]))
            x = x + gate * y
            if split_out and sample:
                xs_ref[...] = x
            else:
                xo_ref[...] = x
        if has_norm:
            scale = sctok_ref[...] if sample else scrow_ref[...]
            shift = shtok_ref[...] if sample else shrow_ref[...]
            ms = jnp.mean(x * x, axis=-1, keepdims=True)
            h = (x * lax.rsqrt(ms + NORM_EPS)) * (gain_ref[...] * (1.0 + scale)) + shift
            if has_router:
                h_ref[...] = _pack_bf16_pairs(h)
            else:
                h_ref[...] = h.astype(h_ref.dtype)
            if has_router:
                h_hi = h.astype(BF16)
                h_lo = (h - h_hi.astype(F32)).astype(BF16)
                logits = (jnp.dot(h_hi, wr_ref[0], preferred_element_type=F32)
                          + jnp.dot(h_lo, wr_ref[0], preferred_element_type=F32)
                          + jnp.dot(h_hi, wr_ref[1], preferred_element_type=F32))
                rec, new_carry = _route_tile(logits, rb_ref[...], carry[0:1, :])
                rec_ref[...] = rec
                carry[...] = jnp.broadcast_to(new_carry, carry.shape)
                cnt_ref[...] = carry[...]

    pl.when(jnp.logical_not(is_sample))(functools.partial(tile, False))
    pl.when(is_sample)(functools.partial(tile, True))


def _norm_call(x, mod4, modtok, *, seq_tiles, n_batch, resid=None, norm=None, router=None, split_out=False):
    r, d = x.shape
    n_tiles = r // ROW_TILE
    n_prompt_tiles = n_tiles - 1

    def row_spec(layer, col):
        return pl.BlockSpec((None, None, 1, d),
                            lambda i: (layer, jnp.minimum(i // seq_tiles, n_batch - 1), 0, col))

    def tok_spec(layer, col):
        return pl.BlockSpec((None, ROW_TILE, d), lambda i: (layer, 0, col))

    tile = pl.BlockSpec((ROW_TILE, d), lambda i: (i, 0))
    args, in_specs, out_shape, out_specs = [x], [tile], [], []
    scratch = []
    if resid is not None:
        ypair, route, layer, gcol = resid
        args += [ypair, ypair, route, mod4, modtok]
        in_specs += [pl.BlockSpec((ROW_TILE, d // 2), lambda i: (i, 0)),
                     pl.BlockSpec((ROW_TILE, d // 2), lambda i: (n_tiles + i, 0)),
                     pl.BlockSpec((ROW_TILE, ROUTER_LANES), lambda i: (i, 0)),
                     row_spec(layer, gcol), tok_spec(layer, gcol)]
        if split_out:
            out_shape += [jax.ShapeDtypeStruct((r - ROW_TILE, d), F32), jax.ShapeDtypeStruct((ROW_TILE, d), F32)]
            out_specs += [pl.BlockSpec((ROW_TILE, d), lambda i: (jnp.minimum(i, n_prompt_tiles - 1), 0)),
                          pl.BlockSpec((ROW_TILE, d), lambda i: (0, 0))]
        else:
            out_shape.append(jax.ShapeDtypeStruct((r, d), F32))
            out_specs.append(tile)
    if norm is not None:
        gains, gidx, layer, shcol, sccol = norm
        args += [gains, mod4, mod4, modtok, modtok]
        in_specs += [pl.BlockSpec((None, 1, d), lambda i: (gidx, 0, 0)), row_spec(layer, shcol),
                     row_spec(layer, sccol), tok_spec(layer, shcol), tok_spec(layer, sccol)]
        if router is not None:
            out_shape.append(jax.ShapeDtypeStruct((r, d // 2), jnp.uint32))
            out_specs.append(pl.BlockSpec((ROW_TILE, d // 2), lambda i: (i, 0)))
        else:
            out_shape.append(jax.ShapeDtypeStruct((r, d), BF16))
            out_specs.append(tile)
        if router is not None:
            args += list(router)
            in_specs += [pl.BlockSpec((2, d, ROUTER_LANES), lambda i: (0, 0, 0)),
                         pl.BlockSpec((1, ROUTER_LANES), lambda i: (0, 0))]
            out_shape += [jax.ShapeDtypeStruct((r, ROUTER_LANES), F32), jax.ShapeDtypeStruct((8, ROUTER_LANES), F32)]
            out_specs += [pl.BlockSpec((ROW_TILE, ROUTER_LANES), lambda i: (i, 0)),
                          pl.BlockSpec((8, ROUTER_LANES), lambda i: (0, 0))]
            scratch.append(pltpu.VMEM((8, ROUTER_LANES), F32))
    body = functools.partial(_norm_body, n_prompt_tiles=n_prompt_tiles, has_resid=resid is not None,
                             has_norm=norm is not None, has_router=router is not None, split_out=split_out)
    return pl.pallas_call(
        body, out_shape=out_shape, grid=(n_tiles,), in_specs=in_specs, out_specs=out_specs,
        scratch_shapes=scratch, compiler_params=_cparams(("arbitrary",)), name="mod_norm",
    )(*args)


def _head_rms_norm(x, gain_row, seg_ref, exp_ref, hd):
    ssq = jnp.dot((x * x).astype(BF16), seg_ref[...], preferred_element_type=F32)
    r = lax.rsqrt(ssq * (1.0 / hd) + NORM_EPS)
    r_hi = r.astype(BF16)
    r_lo = (r - r_hi.astype(F32)).astype(BF16)
    scale = jnp.dot(jnp.concatenate([r_hi, r_lo], axis=1), exp_ref[...], preferred_element_type=F32)
    return x * scale * gain_row


def _linear_body(*refs, n_prompt_tiles, pair, glu, has_bias, has_resid, head_dim):
    it = iter(refs)
    h_ref = next(it)
    hs_ref = next(it) if pair else None
    w_ref = next(it)
    w2_ref = next(it) if glu else None
    b_ref = next(it) if has_bias else None
    b2_ref = next(it) if glu else None
    if has_resid:
        x_ref, grow_ref, gtok_ref = next(it), next(it), next(it)
    if head_dim:
        hgain_ref, hmask_ref, seg_ref, exp_ref = next(it), next(it), next(it), next(it)
    o_ref = next(it)
    wb = next(it)
    wb2 = next(it) if glu else None

    i = pl.program_id(1)
    is_sample = i >= n_prompt_tiles

    @pl.when(i == 0)
    def _():
        wb[...] = w_ref[...].astype(BF16)
        if glu:
            wb2[...] = w2_ref[...].astype(BF16)

    def compute(hv):
        acc = jnp.dot(hv, wb[...], preferred_element_type=F32)
        if has_bias:
            acc = acc + b_ref[...]
        if glu:
            acc2 = jnp.dot(hv, wb2[...], preferred_element_type=F32) + b2_ref[...]
            acc = acc * jax.nn.sigmoid(acc2)
        if has_resid:
            gate = jnp.where(is_sample, gtok_ref[...], grow_ref[...])
            acc = x_ref[...] + gate * acc
        if head_dim:
            normed = _head_rms_norm(acc, hgain_ref[...], seg_ref, exp_ref, head_dim)
            acc = jnp.where(hmask_ref[...] > 0.0, normed, acc)
        o_ref[...] = acc.astype(o_ref.dtype)

    if pair:
        @pl.when(jnp.logical_not(is_sample))
        def _():
            compute(h_ref[...])

        @pl.when(is_sample)
        def _():
            compute(hs_ref[...])
    else:
        compute(h_ref[...])


def _linear_call(h, w, wl, *, n_out, tn, out_dtype, seq_tiles, n_batch, h_sample=None, bias=None, glu=False,
                 resid=None, head_norm=None, single_buffer_w=False, name="linear"):
    k = w.shape[1]
    pair = h_sample is not None
    n_prompt_tiles = h.shape[0] // ROW_TILE - (0 if pair else 1)
    n_tiles = n_prompt_tiles + 1
    r = n_tiles * ROW_TILE
    nblk = n_out // tn

    args = [h]
    in_specs = [pl.BlockSpec((ROW_TILE, k), lambda j, i: (jnp.minimum(i, n_prompt_tiles - 1) if pair else i, 0))]
    if pair:
        args.append(h_sample)
        in_specs.append(pl.BlockSpec((ROW_TILE, k), lambda j, i: (0, 0), pipeline_mode=pl.Buffered(1)))
    args.append(w)
    w_mode = dict(pipeline_mode=pl.Buffered(1)) if single_buffer_w else {}
    in_specs.append(pl.BlockSpec((None, k, tn), lambda j, i: (wl, 0, j), **w_mode))
    if glu:
        args.append(w)
        in_specs.append(pl.BlockSpec((None, k, tn), lambda j, i: (wl, 0, nblk + j)))
    if bias is not None:
        bias = bias.reshape(bias.shape[0], 1, bias.shape[1])
        args.append(bias)
        in_specs.append(pl.BlockSpec((None, 1, tn), lambda j, i: (wl, 0, j)))
        if glu:
            args.append(bias)
            in_specs.append(pl.BlockSpec((None, 1, tn), lambda j, i: (wl, 0, nblk + j)))
    if resid is not None:
        x, mod4, modtok, layer, gcol = resid
        cb = gcol * nblk
        args += [x, mod4, modtok]
        in_specs += [
            pl.BlockSpec((ROW_TILE, tn), lambda j, i: (i, j)),
            pl.BlockSpec((None, None, 1, tn),
                         lambda j, i: (layer, jnp.minimum(i // seq_tiles, n_batch - 1), 0, cb + j)),
            pl.BlockSpec((None, ROW_TILE, tn), lambda j, i: (layer, 0, cb + j)),
        ]
    head_dim = 0
    if head_norm is not None:
        hgain, hmask, head_dim = head_norm
        seg = (jnp.arange(tn, dtype=jnp.int32)[:, None] // head_dim
               == jnp.arange(128, dtype=jnp.int32)[None, :]).astype(BF16)
        args += [hgain, hmask, seg, jnp.concatenate([seg.T, seg.T], axis=0)]
        in_specs += [pl.BlockSpec((1, tn), lambda j, i: (0, j)), pl.BlockSpec((1, tn), lambda j, i: (0, j)),
                     pl.BlockSpec((tn, 128), lambda j, i: (0, 0)), pl.BlockSpec((256, tn), lambda j, i: (0, 0))]
    scratch = [pltpu.VMEM((k, tn), BF16)] + ([pltpu.VMEM((k, tn), BF16)] if glu else [])
    body = functools.partial(_linear_body, n_prompt_tiles=n_prompt_tiles, pair=pair, glu=glu,
                             has_bias=bias is not None, has_resid=resid is not None, head_dim=head_dim)
    return pl.pallas_call(
        body,
        out_shape=jax.ShapeDtypeStruct((r, n_out), out_dtype),
        grid=(nblk, n_tiles),
        in_specs=in_specs,
        out_specs=pl.BlockSpec((ROW_TILE, tn), lambda j, i: (i, j)),
        scratch_shapes=scratch,
        compiler_params=_cparams(("arbitrary", "arbitrary")),
        name=name,
    )(*args)


def _rotate(x, cos, sin):
    half = x.shape[-1] // 2
    x1, x2 = x[:, :half], x[:, half:]
    return jnp.concatenate([x1 * cos - x2 * sin, x1 * sin + x2 * cos], axis=-1)


def _group_norm_gate(o, g, gain, bias):
    mu = jnp.mean(o, axis=-1, keepdims=True)
    var = jnp.mean(jnp.square(o - mu), axis=-1, keepdims=True)
    on = (o - mu) * lax.rsqrt(var + NORM_EPS) * gain + bias
    g = g.astype(F32)
    return (g * jax.nn.sigmoid(g) * on).astype(BF16)


def _ret_prompt_body(q_ref, k_ref, v_ref, g_ref, cos_ref, sin_ref, dm_ref, qd_ref, kd_ref, gl_ref,
                     gng_ref, gnb_ref, o_ref, st_ref, s_acc, *, n_chunks):
    s_acc[...] = jnp.zeros_like(s_acc)
    dmask = dm_ref[...]
    qdec = qd_ref[...]
    kdec = kd_ref[...]
    gl = gl_ref[0:1, 0:1]
    gng = gng_ref[...]
    gnb = gnb_ref[...]

    def chunk(c, carry):
        r0 = pl.multiple_of(c * RET_CHUNK, RET_CHUNK)
        rows = pl.ds(r0, RET_CHUNK)
        cos = cos_ref[rows, :]
        sin = sin_ref[rows, :]
        qr = _rotate(q_ref[rows, :].astype(F32), cos, sin)
        kr = _rotate(k_ref[rows, :].astype(F32), cos, sin)
        v = v_ref[rows, :]
        state = s_acc[...]
        scores = lax.dot_general(qr.astype(BF16), kr.astype(BF16), (((1,), (1,)), ((), ())),
                                 preferred_element_type=F32)
        scores = scores * dmask
        out = jnp.dot(scores.astype(BF16), v, preferred_element_type=F32)
        out = out + jnp.dot((qr * qdec).astype(BF16), state.astype(BF16), preferred_element_type=F32)
        kv = lax.dot_general((kr * kdec).astype(BF16), v, (((0,), (0,)), ((), ())), preferred_element_type=F32)
        s_acc[...] = gl * state + kv
        o_ref[rows, :] = _group_norm_gate(out, g_ref[rows, :], gng, gnb)
        return carry

    lax.fori_loop(0, n_chunks, chunk, 0, unroll=8)
    st_ref[...] = s_acc[...]


def _ret_tables(chunk, dk, n_valid=None):
    n_valid = chunk if n_valid is None else n_valid
    lg = jnp.log1p(-jnp.exp2(-5.0 - jnp.arange(RET_HEADS, dtype=F32)))
    idx = jnp.arange(chunk, dtype=F32)
    diff = idx[:, None] - idx[None, :]
    inside = (idx[:, None] < n_valid) & (idx[None, :] < n_valid)
    dmask = jnp.where((diff[None] >= 0) & inside[None],
                      jnp.exp(jnp.maximum(diff, 0.0)[None] * lg[:, None, None]), 0.0) * (dk ** -0.5)
    qdec = jnp.exp((idx[None, :] + 1.0) * lg[:, None])
    kdec = jnp.where(idx[None, :] < n_valid, jnp.exp((n_valid - 1.0 - idx)[None, :] * lg[:, None]), 0.0) * (dk ** -0.5)
    gl = jnp.exp(n_valid * lg)
    qdec = jnp.broadcast_to(qdec[:, :, None], (RET_HEADS, chunk, dk))
    kdec = jnp.broadcast_to(kdec[:, :, None], (RET_HEADS, chunk, dk))
    gl = jnp.broadcast_to(gl[:, None, None], (RET_HEADS, 8, 128))
    return dmask, qdec, kdec, gl


def _rope_tables(pos, half):
    inv = ROPE_BASE ** (-jnp.arange(half, dtype=F32) / half)
    ang = pos.astype(F32)[:, None] * inv[None, :]
    return jnp.cos(ang), jnp.sin(ang)


def _ret_prompt_call(proj, gn_gain, gn_bias, j, *, n_batch, seq, dk, dv):
    h = RET_HEADS
    n_chunks = seq // RET_CHUNK
    cos, sin = _rope_tables(jnp.arange(seq, dtype=jnp.int32), dk // 2)
    dmask, qdec, kdec, gl = _ret_tables(RET_CHUNK, dk)
    kcol, vcol, gcol = h, (2 * h * dk) // dv, (2 * h * dk) // dv + h
    body = functools.partial(_ret_prompt_body, n_chunks=n_chunks)
    return pl.pallas_call(
        body,
        out_shape=[jax.ShapeDtypeStruct((n_batch * seq, h * dv), BF16),
                   jax.ShapeDtypeStruct((n_batch, h, dk, dv), F32)],
        grid=(n_batch, h),
        in_specs=[
            pl.BlockSpec((seq, dk), lambda b, hh: (b, hh)),
            pl.BlockSpec((seq, dk), lambda b, hh: (b, kcol + hh)),
            pl.BlockSpec((seq, dv), lambda b, hh: (b, vcol + hh)),
            pl.BlockSpec((seq, dv), lambda b, hh: (b, gcol + hh)),
            pl.BlockSpec((seq, dk // 2), lambda b, hh: (0, 0)),
            pl.BlockSpec((seq, dk // 2), lambda b, hh: (0, 0)),
            pl.BlockSpec((None, RET_CHUNK, RET_CHUNK), lambda b, hh: (hh, 0, 0)),
            pl.BlockSpec((None, RET_CHUNK, dk), lambda b, hh: (hh, 0, 0)),
            pl.BlockSpec((None, RET_CHUNK, dk), lambda b, hh: (hh, 0, 0)),
            pl.BlockSpec((None, 8, 128), lambda b, hh: (hh, 0, 0)),
            pl.BlockSpec((None, 1, dv), lambda b, hh: (j, 0, hh)),
            pl.BlockSpec((None, 1, dv), lambda b, hh: (j, 0, hh)),
        ],
        out_specs=[pl.BlockSpec((seq, dv), lambda b, hh: (b, hh)),
                   pl.BlockSpec((None, None, dk, dv), lambda b, hh: (b, hh, 0, 0))],
        scratch_shapes=[pltpu.VMEM((dk, dv), F32)],
        compiler_params=_cparams(("arbitrary", "arbitrary")),
        name="retention_prompt",
    )(proj, proj, proj, proj, cos, sin, dmask, qdec, kdec, gl, gn_gain, gn_bias)


def _ret_sample_body(p_ref, st_ref, cos_ref, sin_ref, dm_ref, qd_ref, kd_ref, gl_ref, gng_ref, gnb_ref,
                     *rest, dk, dv):
    o_ref, so_ref = rest[-2], rest[-1]
    h = RET_HEADS
    cos = cos_ref[...]
    sin = sin_ref[...]
    pad = 128 - SAMPLE_ROWS
    for hh in range(h):
        q = p_ref[:, hh * dk:(hh + 1) * dk].astype(F32)
        k = p_ref[:, h * dk + hh * dk:h * dk + (hh + 1) * dk].astype(F32)
        v = p_ref[:, 2 * h * dk + hh * dv:2 * h * dk + (hh + 1) * dv]
        g = p_ref[:, 2 * h * dk + h * dv + hh * dv:2 * h * dk + h * dv + (hh + 1) * dv]
        qr = _rotate(q, cos, sin)
        kr = _rotate(k, cos, sin)
        k_pad = jnp.concatenate([kr.astype(BF16), jnp.zeros((pad, dk), BF16)], axis=0)
        kd_pad = jnp.concatenate([(kr * kd_ref[hh]).astype(BF16), jnp.zeros((pad, dk), BF16)], axis=0)
        v_pad = jnp.concatenate([v, jnp.zeros((pad, dv), BF16)], axis=0)
        state = st_ref[hh]
        scores = lax.dot_general(qr.astype(BF16), k_pad, (((1,), (1,)), ((), ())), preferred_element_type=F32)
        scores = scores * dm_ref[hh]
        out = jnp.dot(scores.astype(BF16), v_pad, preferred_element_type=F32)
        out = out + jnp.dot((qr * qd_ref[hh]).astype(BF16), state.astype(BF16), preferred_element_type=F32)
        kv = lax.dot_general(kd_pad, v_pad, (((0,), (0,)), ((), ())), preferred_element_type=F32)
        so_ref[hh] = gl_ref[hh, 0:1, 0:1] * state + kv
        o_ref[:, hh * dv:(hh + 1) * dv] = _group_norm_gate(out, g, gng_ref[:, hh * dv:(hh + 1) * dv],
                                                           gnb_ref[:, hh * dv:(hh + 1) * dv])


def _ret_sample_call(proj, state, gn_gain, gn_bias, j, new_state, *, n_prompt_rows, n_seq, dec_seq, dk, dv):
    h = RET_HEADS
    s = SAMPLE_ROWS
    cos, sin = _rope_tables(PAST_LEN + jnp.arange(s, dtype=jnp.int32), dk // 2)
    dmask, qdec, kdec, gl = _ret_tables(s, dk, n_valid=dec_seq)
    dmask = jnp.pad(dmask, ((0, 0), (0, 0), (0, 128 - s)))
    base = n_prompt_rows // s
    body = functools.partial(_ret_sample_body, dk=dk, dv=dv)
    width = proj.shape[1]
    args = [proj, state, cos, sin, dmask, qdec, kdec, gl, gn_gain, gn_bias]
    in_specs = [
        pl.BlockSpec((s, width), lambda b: (base + b, 0)),
        pl.BlockSpec((None, None, h, dk, dv), lambda b: (j, b, 0, 0, 0)),
        pl.BlockSpec((s, dk // 2), lambda b: (0, 0)),
        pl.BlockSpec((s, dk // 2), lambda b: (0, 0)),
        pl.BlockSpec((h, s, 128), lambda b: (0, 0, 0)),
        pl.BlockSpec((h, s, dk), lambda b: (0, 0, 0)),
        pl.BlockSpec((h, s, dk), lambda b: (0, 0, 0)),
        pl.BlockSpec((h, 8, 128), lambda b: (0, 0, 0)),
        pl.BlockSpec((None, 1, h * dv), lambda b: (j, 0, 0)),
        pl.BlockSpec((None, 1, h * dv), lambda b: (j, 0, 0)),
    ]
    aliases = {}
    if new_state is not None:
        aliases = {len(args): 1}
        args.append(new_state)
        in_specs.append(pl.BlockSpec(memory_space=pl.ANY))
    return pl.pallas_call(
        body,
        out_shape=[jax.ShapeDtypeStruct((n_seq * s, h * dv), BF16),
                   jax.ShapeDtypeStruct(state.shape, F32)],
        grid=(n_seq,),
        in_specs=in_specs,
        out_specs=[pl.BlockSpec((s, h * dv), lambda b: (b, 0)),
                   pl.BlockSpec((None, None, h, dk, dv), lambda b: (j, b, 0, 0, 0))],
        input_output_aliases=aliases,
        compiler_params=_cparams(("arbitrary",)),
        name="retention_sample",
    )(*args)


CONV_HALO = 32
CONV_ROW_CHUNK = 64
CONV_LANES = 128


def _layer_norm_swish(u, gain, bias):
    mu = jnp.mean(u, axis=-1, keepdims=True)
    var = jnp.mean(jnp.square(u - mu), axis=-1, keepdims=True)
    un = (u - mu) * lax.rsqrt(var + NORM_EPS) * gain + bias
    return (un * jax.nn.sigmoid(un)).astype(BF16)


def _conv_prompt_body(z_ref, w_ref, bdw_ref, lng_ref, lnb_ref, o_ref, zbuf, ubuf, *, tt, d):
    t = pl.program_id(1)

    @pl.when(t == 0)
    def _():
        zbuf[0:CONV_HALO, :] = jnp.zeros((CONV_HALO, d), F32)

    @pl.when(t > 0)
    def _():
        zbuf[0:CONV_HALO, :] = zbuf[tt:tt + CONV_HALO, :]

    zbuf[CONV_HALO:CONV_HALO + tt, :] = z_ref[...].astype(F32)

    span = CONV_ROW_CHUNK + CONV_HALO
    n_row_chunks = tt // CONV_ROW_CHUNK
    n_strips = d // CONV_LANES

    def strip(n, carry):
        r0 = pl.multiple_of((n % n_row_chunks) * CONV_ROW_CHUNK, CONV_ROW_CHUNK)
        c0 = pl.multiple_of((n // n_row_chunks) * CONV_LANES, CONV_LANES)
        cols = pl.ds(c0, CONV_LANES)
        blk = zbuf[pl.ds(r0, span), cols]
        acc = jnp.zeros((CONV_ROW_CHUNK, CONV_LANES), F32)
        for b in range(8):
            rb = blk if b == 0 else pltpu.roll(blk, span - b, axis=0)
            for a in range(5):
                o = 8 * a + b
                if 2 <= o <= CONV_HALO:
                    acc = acc + rb[8 * a:8 * a + CONV_ROW_CHUNK, :] * w_ref[pl.ds(o - 2, 1), cols]
        ubuf[pl.ds(r0, CONV_ROW_CHUNK), cols] = acc
        return carry

    lax.fori_loop(0, n_row_chunks * n_strips, strip, 0, unroll=2)

    ln_rows = 128

    def ln_chunk(c, carry):
        rows = pl.ds(pl.multiple_of(c * ln_rows, ln_rows), ln_rows)
        o_ref[rows, :] = _layer_norm_swish(ubuf[rows, :] + bdw_ref[...], lng_ref[...], lnb_ref[...])
        return carry

    lax.fori_loop(0, tt // ln_rows, ln_chunk, 0)


def _conv_prompt_call(z, w_dw, b_dw, ln_gain, ln_bias, *, n_batch, seq):
    d = z.shape[1]
    tt = ROW_TILE
    nt = seq // tt
    w_pad = jnp.pad(w_dw, ((0, 32 - CONV_WIDTH), (0, 0)))
    body = functools.partial(_conv_prompt_body, tt=tt, d=d)
    vec = pl.BlockSpec((1, d), lambda b, t: (0, 0))
    return pl.pallas_call(
        body,
        out_shape=jax.ShapeDtypeStruct((n_batch * seq, d), BF16),
        grid=(n_batch, nt),
        in_specs=[pl.BlockSpec((tt, d), lambda b, t: (b * nt + t, 0)),
                  pl.BlockSpec((32, d), lambda b, t: (0, 0)), vec, vec, vec],
        out_specs=pl.BlockSpec((tt, d), lambda b, t: (b * nt + t, 0)),
        scratch_shapes=[pltpu.VMEM((tt + CONV_HALO, d), F32), pltpu.VMEM((tt, d), F32)],
        compiler_params=_cparams(("arbitrary", "arbitrary")),
        name="conv_prompt",
    )(z, w_pad, b_dw, ln_gain, ln_bias)


def _conv_sample_body(z_ref, c_ref, w_ref, bdw_ref, lng_ref, lnb_ref, o_ref, zbuf, *, d):
    s = SAMPLE_ROWS
    zbuf[0:CONV_STATE, :] = c_ref[...]
    zbuf[CONV_STATE:CONV_STATE + s, :] = z_ref[...].astype(F32)
    acc = jnp.zeros((s, d), F32)
    for j in range(CONV_WIDTH):
        acc = acc + zbuf[j:j + s, :] * w_ref[j:j + 1, :]
    o_ref[...] = _layer_norm_swish(acc + bdw_ref[...], lng_ref[...], lnb_ref[...])


def _conv_sample_call(z, cache, w_dw, b_dw, ln_gain, ln_bias, *, n_prompt_rows, n_seq):
    d = z.shape[1]
    s = SAMPLE_ROWS
    base = n_prompt_rows // s
    body = functools.partial(_conv_sample_body, d=d)
    vec = pl.BlockSpec((1, d), lambda b: (0, 0))
    return pl.pallas_call(
        body,
        out_shape=jax.ShapeDtypeStruct((n_seq * s, d), BF16),
        grid=(n_seq,),
        in_specs=[pl.BlockSpec((s, d), lambda b: (base + b, 0)),
                  pl.BlockSpec((None, CONV_STATE, d), lambda b: (b, 0, 0)),
                  pl.BlockSpec((CONV_WIDTH, d), lambda b: (0, 0)), vec, vec, vec],
        out_specs=pl.BlockSpec((s, d), lambda b: (b, 0)),
        scratch_shapes=[pltpu.VMEM((CONV_STATE + s, d), F32)],
        compiler_params=_cparams(("arbitrary",)),
        name="conv_sample",
    )(z, cache, w_dw, b_dw, ln_gain, ln_bias)


def _t5_bucket(dist):
    n = jnp.maximum(dist, 0)
    max_exact = REL_BUCKETS // 2
    nf = jnp.maximum(n, 1).astype(F32)
    large = max_exact + (jnp.log(nf / max_exact) / math.log(REL_MAX_DIST / max_exact)
                         * (REL_BUCKETS - max_exact)).astype(jnp.int32)
    large = jnp.minimum(large, REL_BUCKETS - 1)
    return jnp.where(n < max_exact, n, large)


def _bias_table(rel_bias, dist, valid):
    onehot = (_t5_bucket(dist).reshape(-1)[None, :] == jnp.arange(REL_BUCKETS, dtype=jnp.int32)[:, None]).astype(F32)
    tbl = jnp.dot(rel_bias.astype(F32).T, onehot, precision=lax.Precision.HIGHEST)
    tbl = jnp.where(valid.reshape(-1)[None, :], tbl, NEG_INF)
    return tbl.reshape(-1, dist.shape[-1])


def _swa_prompt_body(sink_ref, q_ref, kvp_ref, kvc_ref, bias_ref, o_ref, *, hd):
    i = pl.program_id(1)
    blk = WINDOW
    g = ATT_GROUP
    nkv = ATT_KV_HEADS
    nk = nkv * hd
    col = lax.broadcasted_iota(jnp.int32, (1, 1, 2 * blk), 2)
    first_mask = jnp.where(jnp.logical_and(i == 0, col < blk), NEG_INF, 0.0)
    keys = jnp.stack([jnp.concatenate([kvp_ref[:, hk * hd:(hk + 1) * hd], kvc_ref[:, hk * hd:(hk + 1) * hd]], axis=0)
                      for hk in range(nkv)])
    vals = jnp.stack([jnp.concatenate([kvp_ref[:, nk + hk * hd:nk + (hk + 1) * hd],
                                       kvc_ref[:, nk + hk * hd:nk + (hk + 1) * hd]], axis=0)
                      for hk in range(nkv)])
    qs = jnp.stack([jnp.concatenate([q_ref[:, (hk * g + gg) * hd:(hk * g + gg + 1) * hd] for gg in range(g)], axis=0)
                    for hk in range(nkv)])
    s = jnp.einsum('hqd,hkd->hqk', qs, keys, preferred_element_type=F32)
    s = s + bias_ref[...].reshape(nkv, g * blk, 2 * blk) + first_mask
    sink = jnp.stack([jnp.concatenate([jnp.full((blk, 1), sink_ref[hk * g + gg], F32) for gg in range(g)], axis=0)
                      for hk in range(nkv)])
    m = jnp.maximum(jnp.max(s, axis=-1, keepdims=True), sink)
    p = jnp.exp(s - m).astype(BF16)
    den = jnp.einsum('hqk,hkd->hqd', p, jnp.ones(vals.shape, BF16), preferred_element_type=F32) + jnp.exp(sink - m)
    o = jnp.einsum('hqk,hkd->hqd', p, vals, preferred_element_type=F32) / den
    for hk in range(nkv):
        for gg in range(g):
            o_ref[:, (hk * g + gg) * hd:(hk * g + gg + 1) * hd] = o[hk, gg * blk:(gg + 1) * blk, :].astype(BF16)


def _swa_prompt_call(proj, sinks, rel_bias, *, n_batch, seq, hd):
    blk = WINDOW
    nb = seq // blk
    nq = ATT_Q_HEADS * hd
    nkv2 = 2 * ATT_KV_HEADS * hd
    kvcol = nq // nkv2
    i_idx = jnp.arange(blk, dtype=jnp.int32)[:, None]
    j_idx = jnp.arange(2 * blk, dtype=jnp.int32)[None, :]
    dist = blk + i_idx - j_idx
    bias = _bias_table(rel_bias, dist, (dist >= 0) & (dist < WINDOW))
    body = functools.partial(_swa_prompt_body, hd=hd)
    return pl.pallas_call(
        body,
        out_shape=jax.ShapeDtypeStruct((n_batch * seq, nq), BF16),
        grid=(n_batch, nb),
        in_specs=[
            pl.BlockSpec(memory_space=pltpu.SMEM),
            pl.BlockSpec((blk, nq), lambda b, i: (b * nb + i, 0)),
            pl.BlockSpec((blk, nkv2), lambda b, i: (b * nb + jnp.maximum(i - 1, 0), kvcol)),
            pl.BlockSpec((blk, nkv2), lambda b, i: (b * nb + i, kvcol)),
            pl.BlockSpec(bias.shape, lambda b, i: (0, 0)),
        ],
        out_specs=pl.BlockSpec((blk, nq), lambda b, i: (b * nb + i, 0)),
        compiler_params=_cparams(("arbitrary", "arbitrary")),
        name="swa_prompt",
    )(sinks, proj, proj, proj, bias)


def _swa_sample_body(sink_ref, p_ref, ck_ref, cv_ref, bias_ref, o_ref, *, hd):
    s = SAMPLE_ROWS
    g = ATT_GROUP
    nkv = ATT_KV_HEADS
    nq = ATT_Q_HEADS * hd
    keys = jnp.stack([jnp.concatenate([ck_ref[:, hk * hd:(hk + 1) * hd].astype(BF16),
                                       p_ref[:, nq + hk * hd:nq + (hk + 1) * hd]], axis=0)
                      for hk in range(nkv)])
    vals = jnp.stack([jnp.concatenate([cv_ref[:, hk * hd:(hk + 1) * hd].astype(BF16),
                                       p_ref[:, nq + (nkv + hk) * hd:nq + (nkv + hk + 1) * hd]], axis=0)
                      for hk in range(nkv)])
    qs = jnp.stack([jnp.concatenate([p_ref[:, (hk * g + gg) * hd:(hk * g + gg + 1) * hd] for gg in range(g)], axis=0)
                    for hk in range(nkv)])
    sc = jnp.einsum('hqd,hkd->hqk', qs, keys, preferred_element_type=F32)
    sc = sc + bias_ref[...].reshape(nkv, g * s, keys.shape[1])
    sink = jnp.stack([jnp.concatenate([jnp.full((s, 1), sink_ref[hk * g + gg], F32) for gg in range(g)], axis=0)
                      for hk in range(nkv)])
    m = jnp.maximum(jnp.max(sc, axis=-1, keepdims=True), sink)
    p = jnp.exp(sc - m).astype(BF16)
    den = jnp.einsum('hqk,hkd->hqd', p, jnp.ones(vals.shape, BF16), preferred_element_type=F32) + jnp.exp(sink - m)
    o = jnp.einsum('hqk,hkd->hqd', p, vals, preferred_element_type=F32) / den
    for hk in range(nkv):
        for gg in range(g):
            o_ref[:, (hk * g + gg) * hd:(hk * g + gg + 1) * hd] = o[hk, gg * s:(gg + 1) * s, :].astype(BF16)


def _swa_sample_call(proj, cache_k, cache_v, sinks, rel_bias, *, n_prompt_rows, n_seq, dec_seq, hd):
    s = SAMPLE_ROWS
    nq = ATT_Q_HEADS * hd
    nkv = ATT_KV_HEADS * hd
    sc = cache_k.shape[1]
    i_idx = jnp.arange(s, dtype=jnp.int32)[:, None]
    j_idx = jnp.arange(sc + s, dtype=jnp.int32)[None, :]
    dist = sc + i_idx - j_idx
    valid = (dist >= 0) & (dist < WINDOW) & (j_idx < sc + dec_seq)
    bias = _bias_table(rel_bias, dist, valid)
    base = n_prompt_rows // s
    body = functools.partial(_swa_sample_body, hd=hd)
    return pl.pallas_call(
        body,
        out_shape=jax.ShapeDtypeStruct((n_seq * s, nq), BF16),
        grid=(n_seq,),
        in_specs=[
            pl.BlockSpec(memory_space=pltpu.SMEM),
            pl.BlockSpec((s, proj.shape[1]), lambda b: (base + b, 0)),
            pl.BlockSpec((None, sc, nkv), lambda b: (b, 0, 0)),
            pl.BlockSpec((None, sc, nkv), lambda b: (b, 0, 0)),
            pl.BlockSpec((ATT_Q_HEADS * s, sc + s), lambda b: (0, 0)),
        ],
        out_specs=pl.BlockSpec((s, nq), lambda b: (b, 0)),
        compiler_params=_cparams(("arbitrary",)),
        name="swa_sample",
    )(sinks, proj, cache_k, cache_v, bias)


MOE_TILE = 256


def _moe_body(src_ref, dst_ref, te_ref, nu_ref, h_hbm, win_ref, wout_ref, y_hbm, xg, yb, wib, wob, gsem, ssem,
              *, ff):
    t = pl.program_id(0)
    nt = pl.num_programs(0)
    n_used = nu_ref[0]
    tm = MOE_TILE
    slot = lax.rem(t, 2)

    def gather_start(tile, sl):
        base = tile * tm
        for r in range(tm):
            pltpu.make_async_copy(h_hbm.at[pl.ds(src_ref[base + r], 1), :], xg.at[sl, pl.ds(r, 1), :],
                                  gsem.at[sl]).start()

    def gather_wait(sl):
        pltpu.make_async_copy(h_hbm.at[pl.ds(0, tm), :], xg.at[sl], gsem.at[sl]).wait()

    def scatter_start(tile, sl):
        base = tile * tm
        for r in range(tm):
            pltpu.make_async_copy(yb.at[sl, pl.ds(r, 1), :], y_hbm.at[pl.ds(dst_ref[base + r], 1), :],
                                  ssem.at[sl]).start()

    def scatter_wait(sl):
        pltpu.make_async_copy(yb.at[sl], y_hbm.at[pl.ds(0, tm), :], ssem.at[sl]).wait()

    valid = t < n_used

    @pl.when(t == 0)
    def _():
        gather_start(0, 0)

    @pl.when(jnp.logical_and(valid, jnp.logical_or(t == 0, te_ref[t] != te_ref[jnp.maximum(t - 1, 0)])))
    def _():
        wib[...] = win_ref[...].astype(BF16)
        wob[...] = wout_ref[...].astype(BF16)

    def step(sl):
        @pl.when(t >= 2)
        def _():
            scatter_wait(sl)

        gather_wait(sl)

        @pl.when(t + 1 < n_used)
        def _():
            gather_start(t + 1, 1 - sl)

        gu = jnp.dot(_unpack_bf16_pairs(xg[sl]).astype(BF16), wib[...], preferred_element_type=F32)
        gate, up = gu[:, :ff], gu[:, ff:]
        act = (gate * jax.nn.sigmoid(gate) * up).astype(BF16)
        yb[sl] = _pack_bf16_pairs(jnp.dot(act, wob[...], preferred_element_type=F32))
        scatter_start(t, sl)

    for sl in range(2):
        pl.when(jnp.logical_and(valid, slot == sl))(functools.partial(step, sl))

    @pl.when(t == nt - 1)
    def _():
        scatter_wait(lax.rem(n_used - 1, 2))
        scatter_wait(lax.rem(n_used, 2))


def _moe_layer(h2, route, counts, w_in, w_out, layer):
    r = h2.shape[0]
    depth, n_exp, d, ff2 = w_in.shape
    ff = ff2 // 2
    tm = MOE_TILE
    n_tiles = 2 * r // tm + n_exp
    n_slots = n_tiles * tm

    cnt = counts[0, MOE_GROUPS:MOE_GROUPS + n_exp].astype(jnp.int32)
    padded = ((cnt + tm - 1) // tm) * tm
    ends = jnp.cumsum(padded)
    starts = ends - padded
    rt = route[:, :8].T.astype(jnp.int32)
    pos = jnp.concatenate([starts[rt[ROUTE_E1]] + rt[ROUTE_R1], starts[rt[ROUTE_E2]] + rt[ROUTE_R2]])
    tok = jnp.arange(r, dtype=jnp.int32)
    dst = (2 * r + jnp.arange(n_slots, dtype=jnp.int32)).at[pos].set(
        jnp.concatenate([tok, r + tok]), unique_indices=True, indices_are_sorted=False)
    src = jnp.where(dst < r, dst, jnp.where(dst < 2 * r, dst - r, 0))
    tile_start = jnp.arange(n_tiles, dtype=jnp.int32) * tm
    tile_expert = jnp.minimum(jnp.sum(tile_start[:, None] >= ends[None, :], axis=1), n_exp - 1).astype(jnp.int32)
    n_used = (ends[-1:] // tm).astype(jnp.int32)

    return pl.pallas_call(
        functools.partial(_moe_body, ff=ff),
        out_shape=jax.ShapeDtypeStruct((2 * r + n_slots, d // 2), jnp.uint32),
        grid_spec=pltpu.PrefetchScalarGridSpec(
            num_scalar_prefetch=4,
            grid=(n_tiles,),
            in_specs=[
                pl.BlockSpec(memory_space=pl.ANY),
                pl.BlockSpec((None, None, d, ff2), lambda t, src, dst, te, nu: (layer, te[t], 0, 0)),
                pl.BlockSpec((None, None, ff, d), lambda t, src, dst, te, nu: (layer, te[t], 0, 0)),
            ],
            out_specs=pl.BlockSpec(memory_space=pl.ANY),
            scratch_shapes=[pltpu.VMEM((2, tm, d // 2), jnp.uint32), pltpu.VMEM((2, tm, d // 2), jnp.uint32),
                            pltpu.VMEM((d, ff2), BF16), pltpu.VMEM((ff, d), BF16),
                            pltpu.SemaphoreType.DMA((2,)), pltpu.SemaphoreType.DMA((2,))],
        ),
        compiler_params=_cparams(("arbitrary",)),
        name="moe_experts",
    )(src, dst, tile_expert, n_used, h2, w_in, w_out)


def kernel(x_prompt, x_sample, c_prompt, c_sample, state_ret, cache_conv, cache_swa_k, cache_swa_v, ada_w, ada_b, norm_gain, ret_w_in, ret_gn_gain, ret_gn_bias, ret_w_out, conv_w_pw1, conv_b_pw1, conv_w_dw, conv_b_dw, conv_ln_gain, conv_ln_bias, conv_w_pw2, conv_b_pw2, att_w_qkv, att_q_gain, att_k_gain, att_sinks, att_w_o, rel_bias, moe_wg, moe_bg, moe_we, moe_be, moe_w_in, moe_w_out):
    n_batch, seq, d = x_prompt.shape
    n_seq, dec_seq, _ = x_sample.shape
    depth = ada_w.shape[0]
    s = SAMPLE_ROWS
    assert n_seq * s == ROW_TILE and seq % ROW_TILE == 0 and dec_seq <= s
    n_prompt_rows = n_batch * seq
    seq_tiles = seq // ROW_TILE
    dk = ret_w_in.shape[2] // (6 * RET_HEADS)
    dv = 2 * dk
    hd = d // ATT_Q_HEADS
    geom = dict(seq_tiles=seq_tiles, n_batch=n_batch)

    xs_pad = jnp.pad(x_sample, ((0, 0), (0, s - dec_seq), (0, 0))).reshape(n_seq * s, d)
    x = jnp.concatenate([x_prompt.reshape(n_prompt_rows, d), xs_pad], axis=0)

    n_c = n_batch + n_seq
    c_rows = ((n_c + 7) // 8) * 8
    c_all = jnp.pad(jnp.concatenate([c_prompt, c_sample], axis=0), ((0, c_rows - n_c), (0, 0)))
    mod = _ada_call(c_all, ada_w, ada_b)
    mod4 = mod.reshape(depth, c_rows, 1, 6 * d)
    modtok = jnp.repeat(mod[:, n_batch:n_c], s, axis=1)

    gains = norm_gain.astype(F32).reshape(2 * depth, 1, d)
    gng_all = ret_gn_gain.astype(F32)[:, None, :]
    gnb_all = ret_gn_bias.astype(F32)[:, None, :]
    state_all = state_ret.astype(F32)

    ret_p, conv_p, conv_s, kp_l, vp_l, ks_l, vs_l = [], [], [], [], [], [], []
    ret_s = None
    (h,) = _norm_call(x, mod4, modtok, norm=(gains, 0, 0, 0, 1), **geom)
    for l in range(depth):
        kind, j = l % 3, l // 3
        resid1 = (x, mod4, modtok, l, 2)
        if kind == 0:
            proj = _linear_call(h, ret_w_in, j, n_out=ret_w_in.shape[2], tn=2048, out_dtype=BF16,
                                name="ret_in", **geom)
            a_p, st_p = _ret_prompt_call(proj, gng_all, gnb_all, j, n_batch=n_batch, seq=seq, dk=dk, dv=dv)
            a_s, ret_s = _ret_sample_call(proj, state_all, gng_all, gnb_all, j, ret_s, n_prompt_rows=n_prompt_rows,
                                          n_seq=n_seq, dec_seq=dec_seq, dk=dk, dv=dv)
            ret_p.append(st_p)
            x = _linear_call(a_p, ret_w_out, j, h_sample=a_s, n_out=d, tn=1024, out_dtype=F32, resid=resid1,
                             single_buffer_w=True, name="ret_out", **geom)
        elif kind == 1:
            z = _linear_call(h, conv_w_pw1, j, n_out=d, tn=1024, out_dtype=BF16, bias=conv_b_pw1,
                             glu=True, name="conv_pw1", **geom)
            cargs = (conv_w_dw[j], conv_b_dw[j][None], conv_ln_gain[j][None], conv_ln_bias[j][None])
            a_p = _conv_prompt_call(z, *cargs, n_batch=n_batch, seq=seq)
            a_s = _conv_sample_call(z, cache_conv[j].astype(F32), *cargs, n_prompt_rows=n_prompt_rows, n_seq=n_seq)
            z_tail = jnp.stack([z[(b + 1) * seq - CONV_STATE:(b + 1) * seq] for b in range(n_batch)])
            conv_p.append(z_tail.astype(F32))
            z_new = z[n_prompt_rows:].reshape(n_seq, s, d)[:, :dec_seq].astype(F32)
            conv_s.append(jnp.concatenate([cache_conv[j].astype(F32), z_new], axis=1)[:, -CONV_STATE:])
            x = _linear_call(a_p, conv_w_pw2, j, h_sample=a_s, n_out=d, tn=1024, out_dtype=F32,
                             bias=conv_b_pw2, resid=resid1, name="conv_pw2", **geom)
        else:
            nkv = ATT_KV_HEADS * hd
            nq = ATT_Q_HEADS * hd
            hgain = jnp.concatenate([jnp.tile(att_q_gain[j].astype(F32) * (hd ** -0.5), ATT_Q_HEADS),
                                     jnp.tile(att_k_gain[j].astype(F32), ATT_KV_HEADS), jnp.ones((nkv,), F32)])[None]
            hmask = jnp.concatenate([jnp.ones((nq + nkv,), F32), jnp.zeros((nkv,), F32)])[None]
            proj = _linear_call(h, att_w_qkv, j, n_out=att_w_qkv.shape[2], tn=1280, out_dtype=BF16,
                                head_norm=(hgain, hmask, hd), name="att_qkv", **geom)
            sinks = att_sinks[j].astype(F32)
            a_p = _swa_prompt_call(proj, sinks, rel_bias, n_batch=n_batch, seq=seq, hd=hd)
            win = cache_swa_k.shape[2]
            ck = cache_swa_k[j].astype(F32).reshape(n_seq, win, nkv)
            cv = cache_swa_v[j].astype(F32).reshape(n_seq, win, nkv)
            a_s = _swa_sample_call(proj, ck, cv, sinks, rel_bias, n_prompt_rows=n_prompt_rows,
                                   n_seq=n_seq, dec_seq=dec_seq, hd=hd)
            kv_tail = jnp.stack([proj[(b + 1) * seq - WINDOW:(b + 1) * seq, nq:] for b in range(n_batch)]).astype(F32)
            kp_l.append(kv_tail[:, :, :nkv].reshape(n_batch, WINDOW, ATT_KV_HEADS, hd))
            vp_l.append(kv_tail[:, :, nkv:].reshape(n_batch, WINDOW, ATT_KV_HEADS, hd))
            kv_new = proj[n_prompt_rows:, nq:].reshape(n_seq, s, 2 * nkv)[:, :dec_seq].astype(F32)
            k_new, v_new = kv_new[:, :, :nkv], kv_new[:, :, nkv:]
            ks_l.append(jnp.concatenate([ck, k_new], axis=1)[:, -win:].reshape(n_seq, win, ATT_KV_HEADS, hd))
            vs_l.append(jnp.concatenate([cv, v_new], axis=1)[:, -win:].reshape(n_seq, win, ATT_KV_HEADS, hd))
            x = _linear_call(a_p, att_w_o, j, h_sample=a_s, n_out=d, tn=1024, out_dtype=F32, resid=resid1,
                             name="att_out", **geom)

        lane_pad = ROUTER_LANES - MOE_GROUPS - MOE_GROUPS * MOE_EPG
        router_w = jnp.pad(jnp.concatenate([moe_wg[l], moe_we[l]], axis=1).astype(F32), ((0, 0), (0, lane_pad)))
        router_hi = router_w.astype(BF16)
        router_w = jnp.stack([router_hi, (router_w - router_hi.astype(F32)).astype(BF16)])
        router_b = jnp.pad(jnp.concatenate([moe_bg[l], moe_be[l]]).astype(F32), (0, lane_pad))[None]
        h2, route, counts = _norm_call(x, mod4, modtok, norm=(gains, 2 * l + 1, l, 3, 4),
                                       router=(router_w, router_b), **geom)
        ypair = _moe_layer(h2, route, counts, moe_w_in, moe_w_out, l)
        if l + 1 < depth:
            x, h = _norm_call(x, mod4, modtok, resid=(ypair, route, l, 5),
                              norm=(gains, 2 * l + 2, l + 1, 0, 1), **geom)
        else:
            x_p, x_s = _norm_call(x, mod4, modtok, resid=(ypair, route, l, 5), split_out=True, **geom)

    y_prompt = x_p.reshape(n_batch, seq, d)
    y_sample = x_s.reshape(n_seq, s, d)[:, :dec_seq]
    return (y_prompt, y_sample, jnp.stack(ret_p), ret_s, jnp.stack(conv_p), jnp.stack(conv_s),
            jnp.stack(kp_l), jnp.stack(vp_l), jnp.stack(ks_l), jnp.stack(vs_l))
```

```python
import functools
import math

import jax
import jax.numpy as jnp
from jax import lax
from jax.experimental import pallas as pl
from jax.experimental.pallas import tpu as pltpu

F32 = jnp.float32
BF16 = jnp.bfloat16

NORM_EPS = 1e-6
NEG_INF = -1e30
ROPE_BASE = 10000.0
PAST_LEN = 16384

RET_HEADS = 8
RET_CHUNK = 128
CONV_WIDTH = 31
CONV_STATE = CONV_WIDTH - 1
ATT_Q_HEADS = 32
ATT_KV_HEADS = 4
ATT_GROUP = ATT_Q_HEADS // ATT_KV_HEADS
WINDOW = 128
REL_BUCKETS = 32
REL_MAX_DIST = 128
MOE_GROUPS = 4
MOE_EPG = 4
MOE_TOPK = 2

ROW_TILE = 512
SAMPLE_ROWS = 16
ROUTER_LANES = 128
VMEM_LIMIT_BYTES = 56 * 1024 * 1024


def _cparams(sem):
    return pltpu.CompilerParams(dimension_semantics=sem, vmem_limit_bytes=VMEM_LIMIT_BYTES)


def _pack_bf16_pairs(y):
    half = y.shape[1] // 2
    hi = lax.bitcast_convert_type(y[:, :half].astype(BF16).astype(F32), jnp.uint32)
    lo = lax.bitcast_convert_type(y[:, half:].astype(BF16).astype(F32), jnp.uint32)
    return hi | lax.shift_right_logical(lo, jnp.uint32(16))


def _unpack_bf16_pairs(w):
    hi = lax.bitcast_convert_type(w & jnp.uint32(0xFFFF0000), F32)
    lo = lax.bitcast_convert_type(lax.shift_left(w, jnp.uint32(16)), F32)
    return jnp.concatenate([hi, lo], axis=1)


def _ada_body(c_ref, w_ref, b_ref, o_ref):
    c = c_ref[...]
    s = (c * jax.nn.sigmoid(c)).astype(BF16)
    o_ref[...] = jnp.dot(s, w_ref[...].astype(BF16), preferred_element_type=F32) + b_ref[...]


def _ada_call(c_all, ada_w, ada_b):
    depth, d, n = ada_w.shape
    rows = c_all.shape[0]
    tn = 1024
    return pl.pallas_call(
        _ada_body,
        out_shape=jax.ShapeDtypeStruct((depth, rows, n), F32),
        grid=(depth, n // tn),
        in_specs=[
            pl.BlockSpec((rows, d), lambda l, j: (0, 0)),
            pl.BlockSpec((None, d, tn), lambda l, j: (l, 0, j)),
            pl.BlockSpec((None, 1, tn), lambda l, j: (l, 0, j)),
        ],
        out_specs=pl.BlockSpec((None, rows, tn), lambda l, j: (l, 0, j)),
        compiler_params=_cparams(("arbitrary", "arbitrary")),
        name="ada_mod",
    )(c_all, ada_w, ada_b.reshape(depth, 1, n))


ROUTE_E1, ROUTE_E2, ROUTE_W1, ROUTE_W2, ROUTE_R1, ROUTE_R2 = range(6)


def _route_tile(logits, rb, carry):
    tm = logits.shape[0]
    g, epg = MOE_GROUPS, MOE_EPG
    lg = logits + rb
    lane = lax.broadcasted_iota(jnp.int32, lg.shape, 1)
    lane_f = lane.astype(F32)

    def first_lane(mask):
        return jnp.min(jnp.where(mask, lane_f, float(ROUTER_LANES)), axis=-1, keepdims=True).astype(jnp.int32)

    is_g = lane < g
    mg = jnp.max(jnp.where(is_g, lg, NEG_INF), axis=-1, keepdims=True)
    eg = jnp.where(is_g, jnp.exp(lg - mg), 0.0)
    p_grp = 1.0 / jnp.sum(eg, axis=-1, keepdims=True)
    grp = first_lane(jnp.logical_and(is_g, lg == mg))
    lo = g + epg * grp
    is_e = jnp.logical_and(lane >= lo, lane < lo + epg)
    me = jnp.max(jnp.where(is_e, lg, NEG_INF), axis=-1, keepdims=True)
    ee = jnp.where(is_e, jnp.exp(lg - me), 0.0)
    se = jnp.sum(ee, axis=-1, keepdims=True)
    i1 = first_lane(jnp.logical_and(is_e, lg == me))
    rest = jnp.logical_and(is_e, lane != i1)
    m2 = jnp.max(jnp.where(rest, ee, -1.0), axis=-1, keepdims=True)
    i2 = first_lane(jnp.logical_and(rest, ee == m2))
    p1 = 1.0 / se
    p2 = m2 / se
    w1 = p_grp * p1 / (p1 + p2)
    w2 = p_grp * p2 / (p1 + p2)
    sel1 = lane == i1
    sel2 = lane == i2
    onehot = jnp.where(jnp.logical_or(sel1, sel2), 1.0, 0.0)
    row = lax.broadcasted_iota(jnp.int32, (tm, tm), 0)
    col = lax.broadcasted_iota(jnp.int32, (tm, tm), 1)
    tri = jnp.where(col <= row, 1.0, 0.0).astype(BF16)
    incl = jnp.dot(tri, onehot.astype(BF16), preferred_element_type=F32)
    rank = incl - 1.0 + carry
    r1 = jnp.sum(jnp.where(sel1, rank, 0.0), axis=-1, keepdims=True)
    r2 = jnp.sum(jnp.where(sel2, rank, 0.0), axis=-1, keepdims=True)
    rec = jnp.zeros_like(lg)
    for ln, val in ((ROUTE_E1, (i1 - g).astype(F32)), (ROUTE_E2, (i2 - g).astype(F32)), (ROUTE_W1, w1),
                    (ROUTE_W2, w2), (ROUTE_R1, r1), (ROUTE_R2, r2)):
        rec = jnp.where(lane == ln, val, rec)
    return rec, carry + jnp.sum(onehot, axis=0, keepdims=True)


def _norm_body(*refs, n_prompt_tiles, has_resid, has_norm, has_router, split_out):
    it = iter(refs)
    x_ref = next(it)
    if has_resid:
        y0_ref, y1_ref, rt_ref, grow_ref, gtok_ref = next(it), next(it), next(it), next(it), next(it)
    if has_norm:
        gain_ref, shrow_ref, scrow_ref, shtok_ref, sctok_ref = next(it), next(it), next(it), next(it), next(it)
    if has_router:
        wr_ref, rb_ref = next(it), next(it)
    if has_resid:
        xo_ref = next(it)
        if split_out:
            xs_ref = next(it)
    if has_norm:
        h_ref = next(it)
    if has_router:
        rec_ref, cnt_ref, carry = next(it), next(it), next(it)

    is_sample = pl.program_id(0) >= n_prompt_tiles

    if has_router:
        @pl.when(pl.program_id(0) == 0)
        def _():
            carry[...] = jnp.zeros_like(carry)

    def tile(sample):
        x = x_ref[...]
        if has_resid:
            gate = gtok_ref[...] if sample else grow_ref[...]
            rt = rt_ref[...]
            y = (rt[:, ROUTE_W1:ROUTE_W1 + 1] * _unpack_bf16_pairs(y0_ref[...])
                 + rt[:, ROUTE_W2:ROUTE_W2 + 1] * _unpack_bf16_pairs(y1_ref[...]))
            x = x + gate * y
            if split_out and sample:
                xs_ref[...] = x
            else:
                xo_ref[...] = x
        if has_norm:
            scale = sctok_ref[...] if sample else scrow_ref[...]
            shift = shtok_ref[...] if sample else shrow_ref[...]
            ms = jnp.mean(x * x, axis=-1, keepdims=True)
            h = (x * lax.rsqrt(ms + NORM_EPS)) * (gain_ref[...] * (1.0 + scale)) + shift
            if has_router:
                h_ref[...] = _pack_bf16_pairs(h)
            else:
                h_ref[...] = h.astype(h_ref.dtype)
            if has_router:
                h_hi = h.astype(BF16)
                h_lo = (h - h_hi.astype(F32)).astype(BF16)
                logits = (jnp.dot(h_hi, wr_ref[0], preferred_element_type=F32)
                          + jnp.dot(h_lo, wr_ref[0], preferred_element_type=F32)
                          + jnp.dot(h_hi, wr_ref[1], preferred_element_type=F32))
                rec, new_carry = _route_tile(logits, rb_ref[...], carry[0:1, :])
                rec_ref[...] = rec
                carry[...] = jnp.broadcast_to(new_carry, carry.shape)
                cnt_ref[...] = carry[...]

    pl.when(jnp.logical_not(is_sample))(functools.partial(tile, False))
    pl.when(is_sample)(functools.partial(tile, True))


def _norm_call(x, mod4, modtok, *, seq_tiles, n_batch, resid=None, norm=None, router=None, split_out=False):
    r, d = x.shape
    n_tiles = r // ROW_TILE
    n_prompt_tiles = n_tiles - 1

    def row_spec(layer, col):
        return pl.BlockSpec((None, None, 1, d),
                            lambda i: (layer, jnp.minimum(i // seq_tiles, n_batch - 1), 0, col))

    def tok_spec(layer, col):
        return pl.BlockSpec((None, ROW_TILE, d), lambda i: (layer, 0, col))

    tile = pl.BlockSpec((ROW_TILE, d), lambda i: (i, 0))
    args, in_specs, out_shape, out_specs = [x], [tile], [], []
    scratch = []
    if resid is not None:
        ypair, route, layer, gcol = resid
        args += [ypair, ypair, route, mod4, modtok]
        in_specs += [pl.BlockSpec((ROW_TILE, d // 2), lambda i: (i, 0)),
                     pl.BlockSpec((ROW_TILE, d // 2), lambda i: (n_tiles + i, 0)),
                     pl.BlockSpec((ROW_TILE, ROUTER_LANES), lambda i: (i, 0)),
                     row_spec(layer, gcol), tok_spec(layer, gcol)]
        if split_out:
            out_shape += [jax.ShapeDtypeStruct((r - ROW_TILE, d), F32), jax.ShapeDtypeStruct((ROW_TILE, d), F32)]
            out_specs += [pl.BlockSpec((ROW_TILE, d), lambda i: (jnp.minimum(i, n_prompt_tiles - 1), 0)),
                          pl.BlockSpec((ROW_TILE, d), lambda i: (0, 0))]
        else:
            out_shape.append(jax.ShapeDtypeStruct((r, d), F32))
            out_specs.append(tile)
    if norm is not None:
        gains, gidx, layer, shcol, sccol = norm
        args += [gains, mod4, mod4, modtok, modtok]
        in_specs += [pl.BlockSpec((None, 1, d), lambda i: (gidx, 0, 0)), row_spec(layer, shcol),
                     row_spec(layer, sccol), tok_spec(layer, shcol), tok_spec(layer, sccol)]
        if router is not None:
            out_shape.append(jax.ShapeDtypeStruct((r, d // 2), jnp.uint32))
            out_specs.append(pl.BlockSpec((ROW_TILE, d // 2), lambda i: (i, 0)))
        else:
            out_shape.append(jax.ShapeDtypeStruct((r, d), BF16))
            out_specs.append(tile)
        if router is not None:
            args += list(router)
            in_specs += [pl.BlockSpec((2, d, ROUTER_LANES), lambda i: (0, 0, 0)),
                         pl.BlockSpec((1, ROUTER_LANES), lambda i: (0, 0))]
            out_shape += [jax.ShapeDtypeStruct((r, ROUTER_LANES), F32), jax.ShapeDtypeStruct((8, ROUTER_LANES), F32)]
            out_specs += [pl.BlockSpec((ROW_TILE, ROUTER_LANES), lambda i: (i, 0)),
                          pl.BlockSpec((8, ROUTER_LANES), lambda i: (0, 0))]
            scratch.append(pltpu.VMEM((8, ROUTER_LANES), F32))
    body = functools.partial(_norm_body, n_prompt_tiles=n_prompt_tiles, has_resid=resid is not None,
                             has_norm=norm is not None, has_router=router is not None, split_out=split_out)
    return pl.pallas_call(
        body, out_shape=out_shape, grid=(n_tiles,), in_specs=in_specs, out_specs=out_specs,
        scratch_shapes=scratch, compiler_params=_cparams(("arbitrary",)), name="mod_norm",
    )(*args)


def _head_rms_norm(x, gain_row, seg_ref, exp_ref, hd):
    ssq = jnp.dot((x * x).astype(BF16), seg_ref[...], preferred_element_type=F32)
    r = lax.rsqrt(ssq * (1.0 / hd) + NORM_EPS)
    r_hi = r.astype(BF16)
    r_lo = (r - r_hi.astype(F32)).astype(BF16)
    scale = jnp.dot(jnp.concatenate([r_hi, r_lo], axis=1), exp_ref[...], preferred_element_type=F32)
    return x * scale * gain_row


def _linear_body(*refs, n_prompt_tiles, pair, glu, has_bias, has_resid, head_dim):
    it = iter(refs)
    h_ref = next(it)
    hs_ref = next(it) if pair else None
    w_ref = next(it)
    w2_ref = next(it) if glu else None
    b_ref = next(it) if has_bias else None
    b2_ref = next(it) if glu else None
    if has_resid:
        x_ref, grow_ref, gtok_ref = next(it), next(it), next(it)
    if head_dim:
        hgain_ref, hmask_ref, seg_ref, exp_ref = next(it), next(it), next(it), next(it)
    o_ref = next(it)
    wb = next(it)
    wb2 = next(it) if glu else None

    i = pl.program_id(1)
    is_sample = i >= n_prompt_tiles

    @pl.when(i == 0)
    def _():
        wb[...] = w_ref[...].astype(BF16)
        if glu:
            wb2[...] = w2_ref[...].astype(BF16)

    def compute(hv):
        acc = jnp.dot(hv, wb[...], preferred_element_type=F32)
        if has_bias:
            acc = acc + b_ref[...]
        if glu:
            acc2 = jnp.dot(hv, wb2[...], preferred_element_type=F32) + b2_ref[...]
            acc = acc * jax.nn.sigmoid(acc2)
        if has_resid:
            gate = jnp.where(is_sample, gtok_ref[...], grow_ref[...])
            acc = x_ref[...] + gate * acc
        if head_dim:
            normed = _head_rms_norm(acc, hgain_ref[...], seg_ref, exp_ref, head_dim)
            acc = jnp.where(hmask_ref[...] > 0.0, normed, acc)
        o_ref[...] = acc.astype(o_ref.dtype)

    if pair:
        @pl.when(jnp.logical_not(is_sample))
        def _():
            compute(h_ref[...])

        @pl.when(is_sample)
        def _():
            compute(hs_ref[...])
    else:
        compute(h_ref[...])


def _linear_call(h, w, wl, *, n_out, tn, out_dtype, seq_tiles, n_batch, h_sample=None, bias=None, glu=False,
                 resid=None, head_norm=None, single_buffer_w=False, name="linear"):
    k = w.shape[1]
    pair = h_sample is not None
    n_prompt_tiles = h.shape[0] // ROW_TILE - (0 if pair else 1)
    n_tiles = n_prompt_tiles + 1
    r = n_tiles * ROW_TILE
    nblk = n_out // tn

    args = [h]
    in_specs = [pl.BlockSpec((ROW_TILE, k), lambda j, i: (jnp.minimum(i, n_prompt_tiles - 1) if pair else i, 0))]
    if pair:
        args.append(h_sample)
        in_specs.append(pl.BlockSpec((ROW_TILE, k), lambda j, i: (0, 0), pipeline_mode=pl.Buffered(1)))
    args.append(w)
    w_mode = dict(pipeline_mode=pl.Buffered(1)) if single_buffer_w else {}
    in_specs.append(pl.BlockSpec((None, k, tn), lambda j, i: (wl, 0, j), **w_mode))
    if glu:
        args.append(w)
        in_specs.append(pl.BlockSpec((None, k, tn), lambda j, i: (wl, 0, nblk + j)))
    if bias is not None:
        bias = bias.reshape(bias.shape[0], 1, bias.shape[1])
        args.append(bias)
        in_specs.append(pl.BlockSpec((None, 1, tn), lambda j, i: (wl, 0, j)))
        if glu:
            args.append(bias)
            in_specs.append(pl.BlockSpec((None, 1, tn), lambda j, i: (wl, 0, nblk + j)))
    if resid is not None:
        x, mod4, modtok, layer, gcol = resid
        cb = gcol * nblk
        args += [x, mod4, modtok]
        in_specs += [
            pl.BlockSpec((ROW_TILE, tn), lambda j, i: (i, j)),
            pl.BlockSpec((None, None, 1, tn),
                         lambda j, i: (layer, jnp.minimum(i // seq_tiles, n_batch - 1), 0, cb + j)),
            pl.BlockSpec((None, ROW_TILE, tn), lambda j, i: (layer, 0, cb + j)),
        ]
    head_dim = 0
    if head_norm is not None:
        hgain, hmask, head_dim = head_norm
        seg = (jnp.arange(tn, dtype=jnp.int32)[:, None] // head_dim
               == jnp.arange(128, dtype=jnp.int32)[None, :]).astype(BF16)
        args += [hgain, hmask, seg, jnp.concatenate([seg.T, seg.T], axis=0)]
        in_specs += [pl.BlockSpec((1, tn), lambda j, i: (0, j)), pl.BlockSpec((1, tn), lambda j, i: (0, j)),
                     pl.BlockSpec((tn, 128), lambda j, i: (0, 0)), pl.BlockSpec((256, tn), lambda j, i: (0, 0))]
    scratch = [pltpu.VMEM((k, tn), BF16)] + ([pltpu.VMEM((k, tn), BF16)] if glu else [])
    body = functools.partial(_linear_body, n_prompt_tiles=n_prompt_tiles, pair=pair, glu=glu,
                             has_bias=bias is not None, has_resid=resid is not None, head_dim=head_dim)
    return pl.pallas_call(
        body,
        out_shape=jax.ShapeDtypeStruct((r, n_out), out_dtype),
        grid=(nblk, n_tiles),
        in_specs=in_specs,
        out_specs=pl.BlockSpec((ROW_TILE, tn), lambda j, i: (i, j)),
        scratch_shapes=scratch,
        compiler_params=_cparams(("arbitrary", "arbitrary")),
        name=name,
    )(*args)


def _rotate(x, cos, sin):
    half = x.shape[-1] // 2
    x1, x2 = x[:, :half], x[:, half:]
    return jnp.concatenate([x1 * cos - x2 * sin, x1 * sin + x2 * cos], axis=-1)


def _group_norm_gate(o, g, gain, bias):
    mu = jnp.mean(o, axis=-1, keepdims=True)
    var = jnp.mean(jnp.square(o - mu), axis=-1, keepdims=True)
    on = (o - mu) * lax.rsqrt(var + NORM_EPS) * gain + bias
    g = g.astype(F32)
    return (g * jax.nn.sigmoid(g) * on).astype(BF16)


def _ret_prompt_body(q_ref, k_ref, v_ref, g_ref, cos_ref, sin_ref, dm_ref, qd_ref, kd_ref, gl_ref,
                     gng_ref, gnb_ref, o_ref, st_ref, s_acc, *, n_chunks):
    s_acc[...] = jnp.zeros_like(s_acc)
    dmask = dm_ref[...]
    qdec = qd_ref[...]
    kdec = kd_ref[...]
    gl = gl_ref[0:1, 0:1]
    gng = gng_ref[...]
    gnb = gnb_ref[...]

    def chunk(c, carry):
        r0 = pl.multiple_of(c * RET_CHUNK, RET_CHUNK)
        rows = pl.ds(r0, RET_CHUNK)
        cos = cos_ref[rows, :]
        sin = sin_ref[rows, :]
        qr = _rotate(q_ref[rows, :].astype(F32), cos, sin)
        kr = _rotate(k_ref[rows, :].astype(F32), cos, sin)
        v = v_ref[rows, :]
        state = s_acc[...]
        scores = lax.dot_general(qr.astype(BF16), kr.astype(BF16), (((1,), (1,)), ((), ())),
                                 preferred_element_type=F32)
        scores = scores * dmask
        out = jnp.dot(scores.astype(BF16), v, preferred_element_type=F32)
        out = out + jnp.dot((qr * qdec).astype(BF16), state.astype(BF16), preferred_element_type=F32)
        kv = lax.dot_general((kr * kdec).astype(BF16), v, (((0,), (0,)), ((), ())), preferred_element_type=F32)
        s_acc[...] = gl * state + kv
        o_ref[rows, :] = _group_norm_gate(out, g_ref[rows, :], gng, gnb)
        return carry

    lax.fori_loop(0, n_chunks, chunk, 0, unroll=8)
    st_ref[...] = s_acc[...]


def _ret_tables(chunk, dk, n_valid=None):
    n_valid = chunk if n_valid is None else n_valid
    lg = jnp.log1p(-jnp.exp2(-5.0 - jnp.arange(RET_HEADS, dtype=F32)))
    idx = jnp.arange(chunk, dtype=F32)
    diff = idx[:, None] - idx[None, :]
    inside = (idx[:, None] < n_valid) & (idx[None, :] < n_valid)
    dmask = jnp.where((diff[None] >= 0) & inside[None],
                      jnp.exp(jnp.maximum(diff, 0.0)[None] * lg[:, None, None]), 0.0) * (dk ** -0.5)
    qdec = jnp.exp((idx[None, :] + 1.0) * lg[:, None])
    kdec = jnp.where(idx[None, :] < n_valid, jnp.exp((n_valid - 1.0 - idx)[None, :] * lg[:, None]), 0.0) * (dk ** -0.5)
    gl = jnp.exp(n_valid * lg)
    qdec = jnp.broadcast_to(qdec[:, :, None], (RET_HEADS, chunk, dk))
    kdec = jnp.broadcast_to(kdec[:, :, None], (RET_HEADS, chunk, dk))
    gl = jnp.broadcast_to(gl[:, None, None], (RET_HEADS, 8, 128))
    return dmask, qdec, kdec, gl


def _rope_tables(pos, half):
    inv = ROPE_BASE ** (-jnp.arange(half, dtype=F32) / half)
    ang = pos.astype(F32)[:, None] * inv[None, :]
    return jnp.cos(ang), jnp.sin(ang)


def _ret_prompt_call(proj, gn_gain, gn_bias, j, *, n_batch, seq, dk, dv):
    h = RET_HEADS
    n_chunks = seq // RET_CHUNK
    cos, sin = _rope_tables(jnp.arange(seq, dtype=jnp.int32), dk // 2)
    dmask, qdec, kdec, gl = _ret_tables(RET_CHUNK, dk)
    kcol, vcol, gcol = h, (2 * h * dk) // dv, (2 * h * dk) // dv + h
    body = functools.partial(_ret_prompt_body, n_chunks=n_chunks)
    return pl.pallas_call(
        body,
        out_shape=[jax.ShapeDtypeStruct((n_batch * seq, h * dv), BF16),
                   jax.ShapeDtypeStruct((n_batch, h, dk, dv), F32)],
        grid=(n_batch, h),
        in_specs=[
            pl.BlockSpec((seq, dk), lambda b, hh: (b, hh)),
            pl.BlockSpec((seq, dk), lambda b, hh: (b, kcol + hh)),
            pl.BlockSpec((seq, dv), lambda b, hh: (b, vcol + hh)),
            pl.BlockSpec((seq, dv), lambda b, hh: (b, gcol + hh)),
            pl.BlockSpec((seq, dk // 2), lambda b, hh: (0, 0)),
            pl.BlockSpec((seq, dk // 2), lambda b, hh: (0, 0)),
            pl.BlockSpec((None, RET_CHUNK, RET_CHUNK), lambda b, hh: (hh, 0, 0)),
            pl.BlockSpec((None, RET_CHUNK, dk), lambda b, hh: (hh, 0, 0)),
            pl.BlockSpec((None, RET_CHUNK, dk), lambda b, hh: (hh, 0, 0)),
            pl.BlockSpec((None, 8, 128), lambda b, hh: (hh, 0, 0)),
            pl.BlockSpec((None, 1, dv), lambda b, hh: (j, 0, hh)),
            pl.BlockSpec((None, 1, dv), lambda b, hh: (j, 0, hh)),
        ],
        out_specs=[pl.BlockSpec((seq, dv), lambda b, hh: (b, hh)),
                   pl.BlockSpec((None, None, dk, dv), lambda b, hh: (b, hh, 0, 0))],
        scratch_shapes=[pltpu.VMEM((dk, dv), F32)],
        compiler_params=_cparams(("arbitrary", "arbitrary")),
        name="retention_prompt",
    )(proj, proj, proj, proj, cos, sin, dmask, qdec, kdec, gl, gn_gain, gn_bias)


def _ret_sample_body(p_ref, st_ref, cos_ref, sin_ref, dm_ref, qd_ref, kd_ref, gl_ref, gng_ref, gnb_ref,
                     *rest, dk, dv):
    o_ref, so_ref = rest[-2], rest[-1]
    h = RET_HEADS
    cos = cos_ref[...]
    sin = sin_ref[...]
    pad = 128 - SAMPLE_ROWS
    for hh in range(h):
        q = p_ref[:, hh * dk:(hh + 1) * dk].astype(F32)
        k = p_ref[:, h * dk + hh * dk:h * dk + (hh + 1) * dk].astype(F32)
        v = p_ref[:, 2 * h * dk + hh * dv:2 * h * dk + (hh + 1) * dv]
        g = p_ref[:, 2 * h * dk + h * dv + hh * dv:2 * h * dk + h * dv + (hh + 1) * dv]
        qr = _rotate(q, cos, sin)
        kr = _rotate(k, cos, sin)
        k_pad = jnp.concatenate([kr.astype(BF16), jnp.zeros((pad, dk), BF16)], axis=0)
        kd_pad = jnp.concatenate([(kr * kd_ref[hh]).astype(BF16), jnp.zeros((pad, dk), BF16)], axis=0)
        v_pad = jnp.concatenate([v, jnp.zeros((pad, dv), BF16)], axis=0)
        state = st_ref[hh]
        scores = lax.dot_general(qr.astype(BF16), k_pad, (((1,), (1,)), ((), ())), preferred_element_type=F32)
        scores = scores * dm_ref[hh]
        out = jnp.dot(scores.astype(BF16), v_pad, preferred_element_type=F32)
        out = out + jnp.dot((qr * qd_ref[hh]).astype(BF16), state.astype(BF16), preferred_element_type=F32)
        kv = lax.dot_general(kd_pad, v_pad, (((0,), (0,)), ((), ())), preferred_element_type=F32)
        so_ref[hh] = gl_ref[hh, 0:1, 0:1] * state + kv
        o_ref[:, hh * dv:(hh + 1) * dv] = _group_norm_gate(out, g, gng_ref[:, hh * dv:(hh + 1) * dv],
                                                           gnb_ref[:, hh * dv:(hh + 1) * dv])


def _ret_sample_call(proj, state, gn_gain, gn_bias, j, new_state, *, n_prompt_rows, n_seq, dec_seq, dk, dv):
    h = RET_HEADS
    s = SAMPLE_ROWS
    cos, sin = _rope_tables(PAST_LEN + jnp.arange(s, dtype=jnp.int32), dk // 2)
    dmask, qdec, kdec, gl = _ret_tables(s, dk, n_valid=dec_seq)
    dmask = jnp.pad(dmask, ((0, 0), (0, 0), (0, 128 - s)))
    base = n_prompt_rows // s
    body = functools.partial(_ret_sample_body, dk=dk, dv=dv)
    width = proj.shape[1]
    args = [proj, state, cos, sin, dmask, qdec, kdec, gl, gn_gain, gn_bias]
    in_specs = [
        pl.BlockSpec((s, width), lambda b: (base + b, 0)),
        pl.BlockSpec((None, None, h, dk, dv), lambda b: (j, b, 0, 0, 0)),
        pl.BlockSpec((s, dk // 2), lambda b: (0, 0)),
        pl.BlockSpec((s, dk // 2), lambda b: (0, 0)),
        pl.BlockSpec((h, s, 128), lambda b: (0, 0, 0)),
        pl.BlockSpec((h, s, dk), lambda b: (0, 0, 0)),
        pl.BlockSpec((h, s, dk), lambda b: (0, 0, 0)),
        pl.BlockSpec((h, 8, 128), lambda b: (0, 0, 0)),
        pl.BlockSpec((None, 1, h * dv), lambda b: (j, 0, 0)),
        pl.BlockSpec((None, 1, h * dv), lambda b: (j, 0, 0)),
    ]
    aliases = {}
    if new_state is not None:
        aliases = {len(args): 1}
        args.append(new_state)
        in_specs.append(pl.BlockSpec(memory_space=pl.ANY))
    return pl.pallas_call(
        body,
        out_shape=[jax.ShapeDtypeStruct((n_seq * s, h * dv), BF16),
                   jax.ShapeDtypeStruct(state.shape, F32)],
        grid=(n_seq,),
        in_specs=in_specs,
        out_specs=[pl.BlockSpec((s, h * dv), lambda b: (b, 0)),
                   pl.BlockSpec((None, None, h, dk, dv), lambda b: (j, b, 0, 0, 0))],
        input_output_aliases=aliases,
        compiler_params=_cparams(("arbitrary",)),
        name="retention_sample",
    )(*args)


CONV_HALO = 32
CONV_ROW_CHUNK = 64
CONV_LANES = 128


def _layer_norm_swish(u, gain, bias):
    mu = jnp.mean(u, axis=-1, keepdims=True)
    var = jnp.mean(jnp.square(u - mu), axis=-1, keepdims=True)
    un = (u - mu) * lax.rsqrt(var + NORM_EPS) * gain + bias
    return (un * jax.nn.sigmoid(un)).astype(BF16)


def _conv_prompt_body(z_ref, w_ref, bdw_ref, lng_ref, lnb_ref, o_ref, zbuf, ubuf, *, tt, d):
    t = pl.program_id(1)

    @pl.when(t == 0)
    def _():
        zbuf[0:CONV_HALO, :] = jnp.zeros((CONV_HALO, d), F32)

    @pl.when(t > 0)
    def _():
        zbuf[0:CONV_HALO, :] = zbuf[tt:tt + CONV_HALO, :]

    zbuf[CONV_HALO:CONV_HALO + tt, :] = z_ref[...].astype(F32)

    span = CONV_ROW_CHUNK + CONV_HALO
    n_row_chunks = tt // CONV_ROW_CHUNK
    n_strips = d // CONV_LANES

    def strip(n, carry):
        r0 = pl.multiple_of((n % n_row_chunks) * CONV_ROW_CHUNK, CONV_ROW_CHUNK)
        c0 = pl.multiple_of((n // n_row_chunks) * CONV_LANES, CONV_LANES)
        cols = pl.ds(c0, CONV_LANES)
        blk = zbuf[pl.ds(r0, span), cols]
        acc = jnp.zeros((CONV_ROW_CHUNK, CONV_LANES), F32)
        for b in range(8):
            rb = blk if b == 0 else pltpu.roll(blk, span - b, axis=0)
            for a in range(5):
                o = 8 * a + b
                if 2 <= o <= CONV_HALO:
                    acc = acc + rb[8 * a:8 * a + CONV_ROW_CHUNK, :] * w_ref[pl.ds(o - 2, 1), cols]
        ubuf[pl.ds(r0, CONV_ROW_CHUNK), cols] = acc
        return carry

    lax.fori_loop(0, n_row_chunks * n_strips, strip, 0, unroll=4)

    ln_rows = 128

    def ln_chunk(c, carry):
        rows = pl.ds(pl.multiple_of(c * ln_rows, ln_rows), ln_rows)
        o_ref[rows, :] = _layer_norm_swish(ubuf[rows, :] + bdw_ref[...], lng_ref[...], lnb_ref[...])
        return carry

    lax.fori_loop(0, tt // ln_rows, ln_chunk, 0)


def _conv_prompt_call(z, w_dw, b_dw, ln_gain, ln_bias, *, n_batch, seq):
    d = z.shape[1]
    tt = ROW_TILE
    nt = seq // tt
    w_pad = jnp.pad(w_dw, ((0, 32 - CONV_WIDTH), (0, 0)))
    body = functools.partial(_conv_prompt_body, tt=tt, d=d)
    vec = pl.BlockSpec((1, d), lambda b, t: (0, 0))
    return pl.pallas_call(
        body,
        out_shape=jax.ShapeDtypeStruct((n_batch * seq, d), BF16),
        grid=(n_batch, nt),
        in_specs=[pl.BlockSpec((tt, d), lambda b, t: (b * nt + t, 0)),
                  pl.BlockSpec((32, d), lambda b, t: (0, 0)), vec, vec, vec],
        out_specs=pl.BlockSpec((tt, d), lambda b, t: (b * nt + t, 0)),
        scratch_shapes=[pltpu.VMEM((tt + CONV_HALO, d), F32), pltpu.VMEM((tt, d), F32)],
        compiler_params=_cparams(("arbitrary", "arbitrary")),
        name="conv_prompt",
    )(z, w_pad, b_dw, ln_gain, ln_bias)


def _conv_sample_body(z_ref, c_ref, w_ref, bdw_ref, lng_ref, lnb_ref, o_ref, zbuf, *, d):
    s = SAMPLE_ROWS
    zbuf[0:CONV_STATE, :] = c_ref[...]
    zbuf[CONV_STATE:CONV_STATE + s, :] = z_ref[...].astype(F32)
    acc = jnp.zeros((s, d), F32)
    for j in range(CONV_WIDTH):
        acc = acc + zbuf[j:j + s, :] * w_ref[j:j + 1, :]
    o_ref[...] = _layer_norm_swish(acc + bdw_ref[...], lng_ref[...], lnb_ref[...])


def _conv_sample_call(z, cache, w_dw, b_dw, ln_gain, ln_bias, *, n_prompt_rows, n_seq):
    d = z.shape[1]
    s = SAMPLE_ROWS
    base = n_prompt_rows // s
    body = functools.partial(_conv_sample_body, d=d)
    vec = pl.BlockSpec((1, d), lambda b: (0, 0))
    return pl.pallas_call(
        body,
        out_shape=jax.ShapeDtypeStruct((n_seq * s, d), BF16),
        grid=(n_seq,),
        in_specs=[pl.BlockSpec((s, d), lambda b: (base + b, 0)),
                  pl.BlockSpec((None, CONV_STATE, d), lambda b: (b, 0, 0)),
                  pl.BlockSpec((CONV_WIDTH, d), lambda b: (0, 0)), vec, vec, vec],
        out_specs=pl.BlockSpec((s, d), lambda b: (b, 0)),
        scratch_shapes=[pltpu.VMEM((CONV_STATE + s, d), F32)],
        compiler_params=_cparams(("arbitrary",)),
        name="conv_sample",
    )(z, cache, w_dw, b_dw, ln_gain, ln_bias)


def _t5_bucket(dist):
    n = jnp.maximum(dist, 0)
    max_exact = REL_BUCKETS // 2
    nf = jnp.maximum(n, 1).astype(F32)
    large = max_exact + (jnp.log(nf / max_exact) / math.log(REL_MAX_DIST / max_exact)
                         * (REL_BUCKETS - max_exact)).astype(jnp.int32)
    large = jnp.minimum(large, REL_BUCKETS - 1)
    return jnp.where(n < max_exact, n, large)


def _bias_table(rel_bias, dist, valid):
    onehot = (_t5_bucket(dist).reshape(-1)[None, :] == jnp.arange(REL_BUCKETS, dtype=jnp.int32)[:, None]).astype(F32)
    tbl = jnp.dot(rel_bias.astype(F32).T, onehot, precision=lax.Precision.HIGHEST)
    tbl = jnp.where(valid.reshape(-1)[None, :], tbl, NEG_INF)
    return tbl.reshape(-1, dist.shape[-1])


def _swa_prompt_body(sink_ref, q_ref, kvp_ref, kvc_ref, bias_ref, o_ref, *, hd):
    i = pl.program_id(1)
    blk = WINDOW
    g = ATT_GROUP
    nkv = ATT_KV_HEADS
    nk = nkv * hd
    col = lax.broadcasted_iota(jnp.int32, (1, 1, 2 * blk), 2)
    first_mask = jnp.where(jnp.logical_and(i == 0, col < blk), NEG_INF, 0.0)
    keys = jnp.stack([jnp.concatenate([kvp_ref[:, hk * hd:(hk + 1) * hd], kvc_ref[:, hk * hd:(hk + 1) * hd]], axis=0)
                      for hk in range(nkv)])
    vals = jnp.stack([jnp.concatenate([kvp_ref[:, nk + hk * hd:nk + (hk + 1) * hd],
                                       kvc_ref[:, nk + hk * hd:nk + (hk + 1) * hd]], axis=0)
                      for hk in range(nkv)])
    qs = jnp.stack([jnp.concatenate([q_ref[:, (hk * g + gg) * hd:(hk * g + gg + 1) * hd] for gg in range(g)], axis=0)
                    for hk in range(nkv)])
    s = jnp.einsum('hqd,hkd->hqk', qs, keys, preferred_element_type=F32)
    s = s + bias_ref[...].reshape(nkv, g * blk, 2 * blk) + first_mask
    sink = jnp.stack([jnp.concatenate([jnp.full((blk, 1), sink_ref[hk * g + gg], F32) for gg in range(g)], axis=0)
                      for hk in range(nkv)])
    m = jnp.maximum(jnp.max(s, axis=-1, keepdims=True), sink)
    p = jnp.exp(s - m).astype(BF16)
    den = jnp.einsum('hqk,hkd->hqd', p, jnp.ones(vals.shape, BF16), preferred_element_type=F32) + jnp.exp(sink - m)
    o = jnp.einsum('hqk,hkd->hqd', p, vals, preferred_element_type=F32) / den
    for hk in range(nkv):
        for gg in range(g):
            o_ref[:, (hk * g + gg) * hd:(hk * g + gg + 1) * hd] = o[hk, gg * blk:(gg + 1) * blk, :].astype(BF16)


def _swa_prompt_call(proj, sinks, rel_bias, *, n_batch, seq, hd):
    blk = WINDOW
    nb = seq // blk
    nq = ATT_Q_HEADS * hd
    nkv2 = 2 * ATT_KV_HEADS * hd
    kvcol = nq // nkv2
    i_idx = jnp.arange(blk, dtype=jnp.int32)[:, None]
    j_idx = jnp.arange(2 * blk, dtype=jnp.int32)[None, :]
    dist = blk + i_idx - j_idx
    bias = _bias_table(rel_bias, dist, (dist >= 0) & (dist < WINDOW))
    body = functools.partial(_swa_prompt_body, hd=hd)
    return pl.pallas_call(
        body,
        out_shape=jax.ShapeDtypeStruct((n_batch * seq, nq), BF16),
        grid=(n_batch, nb),
        in_specs=[
            pl.BlockSpec(memory_space=pltpu.SMEM),
            pl.BlockSpec((blk, nq), lambda b, i: (b * nb + i, 0)),
            pl.BlockSpec((blk, nkv2), lambda b, i: (b * nb + jnp.maximum(i - 1, 0), kvcol)),
            pl.BlockSpec((blk, nkv2), lambda b, i: (b * nb + i, kvcol)),
            pl.BlockSpec(bias.shape, lambda b, i: (0, 0)),
        ],
        out_specs=pl.BlockSpec((blk, nq), lambda b, i: (b * nb + i, 0)),
        compiler_params=_cparams(("arbitrary", "arbitrary")),
        name="swa_prompt",
    )(sinks, proj, proj, proj, bias)


def _swa_sample_body(sink_ref, p_ref, ck_ref, cv_ref, bias_ref, o_ref, *, hd):
    s = SAMPLE_ROWS
    g = ATT_GROUP
    nkv = ATT_KV_HEADS
    nq = ATT_Q_HEADS * hd
    keys = jnp.stack([jnp.concatenate([ck_ref[:, hk * hd:(hk + 1) * hd].astype(BF16),
                                       p_ref[:, nq + hk * hd:nq + (hk + 1) * hd]], axis=0)
                      for hk in range(nkv)])
    vals = jnp.stack([jnp.concatenate([cv_ref[:, hk * hd:(hk + 1) * hd].astype(BF16),
                                       p_ref[:, nq + (nkv + hk) * hd:nq + (nkv + hk + 1) * hd]], axis=0)
                      for hk in range(nkv)])
    qs = jnp.stack([jnp.concatenate([p_ref[:, (hk * g + gg) * hd:(hk * g + gg + 1) * hd] for gg in range(g)], axis=0)
                    for hk in range(nkv)])
    sc = jnp.einsum('hqd,hkd->hqk', qs, keys, preferred_element_type=F32)
    sc = sc + bias_ref[...].reshape(nkv, g * s, keys.shape[1])
    sink = jnp.stack([jnp.concatenate([jnp.full((s, 1), sink_ref[hk * g + gg], F32) for gg in range(g)], axis=0)
                      for hk in range(nkv)])
    m = jnp.maximum(jnp.max(sc, axis=-1, keepdims=True), sink)
    p = jnp.exp(sc - m).astype(BF16)
    den = jnp.einsum('hqk,hkd->hqd', p, jnp.ones(vals.shape, BF16), preferred_element_type=F32) + jnp.exp(sink - m)
    o = jnp.einsum('hqk,hkd->hqd', p, vals, preferred_element_type=F32) / den
    for hk in range(nkv):
        for gg in range(g):
            o_ref[:, (hk * g + gg) * hd:(hk * g + gg + 1) * hd] = o[hk, gg * s:(gg + 1) * s, :].astype(BF16)


def _swa_sample_call(proj, cache_k, cache_v, sinks, rel_bias, *, n_prompt_rows, n_seq, dec_seq, hd):
    s = SAMPLE_ROWS
    nq = ATT_Q_HEADS * hd
    nkv = ATT_KV_HEADS * hd
    sc = cache_k.shape[1]
    i_idx = jnp.arange(s, dtype=jnp.int32)[:, None]
    j_idx = jnp.arange(sc + s, dtype=jnp.int32)[None, :]
    dist = sc + i_idx - j_idx
    valid = (dist >= 0) & (dist < WINDOW) & (j_idx < sc + dec_seq)
    bias = _bias_table(rel_bias, dist, valid)
    base = n_prompt_rows // s
    body = functools.partial(_swa_sample_body, hd=hd)
    return pl.pallas_call(
        body,
        out_shape=jax.ShapeDtypeStruct((n_seq * s, nq), BF16),
        grid=(n_seq,),
        in_specs=[
            pl.BlockSpec(memory_space=pltpu.SMEM),
            pl.BlockSpec((s, proj.shape[1]), lambda b: (base + b, 0)),
            pl.BlockSpec((None, sc, nkv), lambda b: (b, 0, 0)),
            pl.BlockSpec((None, sc, nkv), lambda b: (b, 0, 0)),
            pl.BlockSpec((ATT_Q_HEADS * s, sc + s), lambda b: (0, 0)),
        ],
        out_specs=pl.BlockSpec((s, nq), lambda b: (b, 0)),
        compiler_params=_cparams(("arbitrary",)),
        name="swa_sample",
    )(sinks, proj, cache_k, cache_v, bias)


MOE_TILE = 256


def _moe_body(src_ref, dst_ref, te_ref, nu_ref, h_hbm, win_ref, wout_ref, y_hbm, xg, yb, wib, wob, gsem, ssem,
              *, ff):
    t = pl.program_id(0)
    nt = pl.num_programs(0)
    n_used = nu_ref[0]
    tm = MOE_TILE
    slot = lax.rem(t, 2)

    def gather_start(tile, sl):
        base = tile * tm
        for r in range(tm):
            pltpu.make_async_copy(h_hbm.at[pl.ds(src_ref[base + r], 1), :], xg.at[sl, pl.ds(r, 1), :],
                                  gsem.at[sl]).start()

    def gather_wait(sl):
        pltpu.make_async_copy(h_hbm.at[pl.ds(0, tm), :], xg.at[sl], gsem.at[sl]).wait()

    def scatter_start(tile, sl):
        base = tile * tm
        for r in range(tm):
            pltpu.make_async_copy(yb.at[sl, pl.ds(r, 1), :], y_hbm.at[pl.ds(dst_ref[base + r], 1), :],
                                  ssem.at[sl]).start()

    def scatter_wait(sl):
        pltpu.make_async_copy(yb.at[sl], y_hbm.at[pl.ds(0, tm), :], ssem.at[sl]).wait()

    valid = t < n_used

    @pl.when(t == 0)
    def _():
        gather_start(0, 0)

    @pl.when(jnp.logical_and(valid, jnp.logical_or(t == 0, te_ref[t] != te_ref[jnp.maximum(t - 1, 0)])))
    def _():
        wib[...] = win_ref[...].astype(BF16)
        wob[...] = wout_ref[...].astype(BF16)

    def step(sl):
        @pl.when(t >= 2)
        def _():
            scatter_wait(sl)

        gather_wait(sl)

        @pl.when(t + 1 < n_used)
        def _():
            gather_start(t + 1, 1 - sl)

        gu = jnp.dot(_unpack_bf16_pairs(xg[sl]).astype(BF16), wib[...], preferred_element_type=F32)
        gate, up = gu[:, :ff], gu[:, ff:]
        act = (gate * jax.nn.sigmoid(gate) * up).astype(BF16)
        yb[sl] = _pack_bf16_pairs(jnp.dot(act, wob[...], preferred_element_type=F32))
        scatter_start(t, sl)

    for sl in range(2):
        pl.when(jnp.logical_and(valid, slot == sl))(functools.partial(step, sl))

    @pl.when(t == nt - 1)
    def _():
        scatter_wait(lax.rem(n_used - 1, 2))
        scatter_wait(lax.rem(n_used, 2))


def _moe_layer(h2, route, counts, w_in, w_out, layer):
    r = h2.shape[0]
    depth, n_exp, d, ff2 = w_in.shape
    ff = ff2 // 2
    tm = MOE_TILE
    n_tiles = 2 * r // tm + n_exp
    n_slots = n_tiles * tm

    cnt = counts[0, MOE_GROUPS:MOE_GROUPS + n_exp].astype(jnp.int32)
    padded = ((cnt + tm - 1) // tm) * tm
    ends = jnp.cumsum(padded)
    starts = ends - padded
    rt = route[:, :8].T.astype(jnp.int32)
    pos = jnp.concatenate([starts[rt[ROUTE_E1]] + rt[ROUTE_R1], starts[rt[ROUTE_E2]] + rt[ROUTE_R2]])
    tok = jnp.arange(r, dtype=jnp.int32)
    dst = (2 * r + jnp.arange(n_slots, dtype=jnp.int32)).at[pos].set(
        jnp.concatenate([tok, r + tok]), unique_indices=True, indices_are_sorted=False)
    src = jnp.where(dst < r, dst, jnp.where(dst < 2 * r, dst - r, 0))
    tile_start = jnp.arange(n_tiles, dtype=jnp.int32) * tm
    tile_expert = jnp.minimum(jnp.sum(tile_start[:, None] >= ends[None, :], axis=1), n_exp - 1).astype(jnp.int32)
    n_used = (ends[-1:] // tm).astype(jnp.int32)

    return pl.pallas_call(
        functools.partial(_moe_body, ff=ff),
        out_shape=jax.ShapeDtypeStruct((2 * r + n_slots, d // 2), jnp.uint32),
        grid_spec=pltpu.PrefetchScalarGridSpec(
            num_scalar_prefetch=4,
            grid=(n_tiles,),
            in_specs=[
                pl.BlockSpec(memory_space=pl.ANY),
                pl.BlockSpec((None, None, d, ff2), lambda t, src, dst, te, nu: (layer, te[t], 0, 0)),
                pl.BlockSpec((None, None, ff, d), lambda t, src, dst, te, nu: (layer, te[t], 0, 0)),
            ],
            out_specs=pl.BlockSpec(memory_space=pl.ANY),
            scratch_shapes=[pltpu.VMEM((2, tm, d // 2), jnp.uint32), pltpu.VMEM((2, tm, d // 2), jnp.uint32),
                            pltpu.VMEM((d, ff2), BF16), pltpu.VMEM((ff, d), BF16),
                            pltpu.SemaphoreType.DMA((2,)), pltpu.SemaphoreType.DMA((2,))],
        ),
        compiler_params=_cparams(("arbitrary",)),
        name="moe_experts",
    )(src, dst, tile_expert, n_used, h2, w_in, w_out)


def kernel(x_prompt, x_sample, c_prompt, c_sample, state_ret, cache_conv, cache_swa_k, cache_swa_v, ada_w, ada_b, norm_gain, ret_w_in, ret_gn_gain, ret_gn_bias, ret_w_out, conv_w_pw1, conv_b_pw1, conv_w_dw, conv_b_dw, conv_ln_gain, conv_ln_bias, conv_w_pw2, conv_b_pw2, att_w_qkv, att_q_gain, att_k_gain, att_sinks, att_w_o, rel_bias, moe_wg, moe_bg, moe_we, moe_be, moe_w_in, moe_w_out):
    n_batch, seq, d = x_prompt.shape
    n_seq, dec_seq, _ = x_sample.shape
    depth = ada_w.shape[0]
    s = SAMPLE_ROWS
    assert n_seq * s == ROW_TILE and seq % ROW_TILE == 0 and dec_seq <= s
    n_prompt_rows = n_batch * seq
    seq_tiles = seq // ROW_TILE
    dk = ret_w_in.shape[2] // (6 * RET_HEADS)
    dv = 2 * dk
    hd = d // ATT_Q_HEADS
    geom = dict(seq_tiles=seq_tiles, n_batch=n_batch)

    xs_pad = jnp.pad(x_sample, ((0, 0), (0, s - dec_seq), (0, 0))).reshape(n_seq * s, d)
    x = jnp.concatenate([x_prompt.reshape(n_prompt_rows, d), xs_pad], axis=0)

    n_c = n_batch + n_seq
    c_rows = ((n_c + 7) // 8) * 8
    c_all = jnp.pad(jnp.concatenate([c_prompt, c_sample], axis=0), ((0, c_rows - n_c), (0, 0)))
    mod = _ada_call(c_all, ada_w, ada_b)
    mod4 = mod.reshape(depth, c_rows, 1, 6 * d)
    modtok = jnp.repeat(mod[:, n_batch:n_c], s, axis=1)

    gains = norm_gain.astype(F32).reshape(2 * depth, 1, d)
    gng_all = ret_gn_gain.astype(F32)[:, None, :]
    gnb_all = ret_gn_bias.astype(F32)[:, None, :]
    state_all = state_ret.astype(F32)

    ret_p, conv_p, conv_s, kp_l, vp_l, ks_l, vs_l = [], [], [], [], [], [], []
    ret_s = None
    (h,) = _norm_call(x, mod4, modtok, norm=(gains, 0, 0, 0, 1), **geom)
    for l in range(depth):
        kind, j = l % 3, l // 3
        resid1 = (x, mod4, modtok, l, 2)
        if kind == 0:
            proj = _linear_call(h, ret_w_in, j, n_out=ret_w_in.shape[2], tn=2048, out_dtype=BF16,
                                name="ret_in", **geom)
            a_p, st_p = _ret_prompt_call(proj, gng_all, gnb_all, j, n_batch=n_batch, seq=seq, dk=dk, dv=dv)
            a_s, ret_s = _ret_sample_call(proj, state_all, gng_all, gnb_all, j, ret_s, n_prompt_rows=n_prompt_rows,
                                          n_seq=n_seq, dec_seq=dec_seq, dk=dk, dv=dv)
            ret_p.append(st_p)
            x = _linear_call(a_p, ret_w_out, j, h_sample=a_s, n_out=d, tn=1024, out_dtype=F32, resid=resid1,
                             single_buffer_w=True, name="ret_out", **geom)
        elif kind == 1:
            z = _linear_call(h, conv_w_pw1, j, n_out=d, tn=1024, out_dtype=BF16, bias=conv_b_pw1,
                             glu=True, name="conv_pw1", **geom)
            cargs = (conv_w_dw[j], conv_b_dw[j][None], conv_ln_gain[j][None], conv_ln_bias[j][None])
            a_p = _conv_prompt_call(z, *cargs, n_batch=n_batch, seq=seq)
            a_s = _conv_sample_call(z, cache_conv[j].astype(F32), *cargs, n_prompt_rows=n_prompt_rows, n_seq=n_seq)
            z_tail = jnp.stack([z[(b + 1) * seq - CONV_STATE:(b + 1) * seq] for b in range(n_batch)])
            conv_p.append(z_tail.astype(F32))
            z_new = z[n_prompt_rows:].reshape(n_seq, s, d)[:, :dec_seq].astype(F32)
            conv_s.append(jnp.concatenate([cache_conv[j].astype(F32), z_new], axis=1)[:, -CONV_STATE:])
            x = _linear_call(a_p, conv_w_pw2, j, h_sample=a_s, n_out=d, tn=1024, out_dtype=F32,
                             bias=conv_b_pw2, resid=resid1, name="conv_pw2", **geom)
        else:
            nkv = ATT_KV_HEADS * hd
            nq = ATT_Q_HEADS * hd
            hgain = jnp.concatenate([jnp.tile(att_q_gain[j].astype(F32) * (hd ** -0.5), ATT_Q_HEADS),
                                     jnp.tile(att_k_gain[j].astype(F32), ATT_KV_HEADS), jnp.ones((nkv,), F32)])[None]
            hmask = jnp.concatenate([jnp.ones((nq + nkv,), F32), jnp.zeros((nkv,), F32)])[None]
            proj = _linear_call(h, att_w_qkv, j, n_out=att_w_qkv.shape[2], tn=att_w_qkv.shape[2], out_dtype=BF16,
                                head_norm=(hgain, hmask, hd), single_buffer_w=True, name="att_qkv", **geom)
            sinks = att_sinks[j].astype(F32)
            a_p = _swa_prompt_call(proj, sinks, rel_bias, n_batch=n_batch, seq=seq, hd=hd)
            win = cache_swa_k.shape[2]
            ck = cache_swa_k[j].astype(F32).reshape(n_seq, win, nkv)
            cv = cache_swa_v[j].astype(F32).reshape(n_seq, win, nkv)
            a_s = _swa_sample_call(proj, ck, cv, sinks, rel_bias, n_prompt_rows=n_prompt_rows,
                                   n_seq=n_seq, dec_seq=dec_seq, hd=hd)
            kv_tail = jnp.stack([proj[(b + 1) * seq - WINDOW:(b + 1) * seq, nq:] for b in range(n_batch)]).astype(F32)
            kp_l.append(kv_tail[:, :, :nkv].reshape(n_batch, WINDOW, ATT_KV_HEADS, hd))
            vp_l.append(kv_tail[:, :, nkv:].reshape(n_batch, WINDOW, ATT_KV_HEADS, hd))
            kv_new = proj[n_prompt_rows:, nq:].reshape(n_seq, s, 2 * nkv)[:, :dec_seq].astype(F32)
            k_new, v_new = kv_new[:, :, :nkv], kv_new[:, :, nkv:]
            ks_l.append(jnp.concatenate([ck, k_new], axis=1)[:, -win:].reshape(n_seq, win, ATT_KV_HEADS, hd))
            vs_l.append(jnp.concatenate([cv, v_new], axis=1)[:, -win:].reshape(n_seq, win, ATT_KV_HEADS, hd))
            x = _linear_call(a_p, att_w_o, j, h_sample=a_s, n_out=d, tn=1024, out_dtype=F32, resid=resid1,
                             name="att_out", **geom)

        lane_pad = ROUTER_LANES - MOE_GROUPS - MOE_GROUPS * MOE_EPG
        router_w = jnp.pad(jnp.concatenate([moe_wg[l], moe_we[l]], axis=1).astype(F32), ((0, 0), (0, lane_pad)))
        router_hi = router_w.astype(BF16)
        router_w = jnp.stack([router_hi, (router_w - router_hi.astype(F32)).astype(BF16)])
        router_b = jnp.pad(jnp.concatenate([moe_bg[l], moe_be[l]]).astype(F32), (0, lane_pad))[None]
        h2, route, counts = _norm_call(x, mod4, modtok, norm=(gains, 2 * l + 1, l, 3, 4),
                                       router=(router_w, router_b), **geom)
        ypair = _moe_layer(h2, route, counts, moe_w_in, moe_w_out, l)
        if l + 1 < depth:
            x, h = _norm_call(x, mod4, modtok, resid=(ypair, route, l, 5),
                              norm=(gains, 2 * l + 2, l + 1, 0, 1), **geom)
        else:
            x_p, x_s = _norm_call(x, mod4, modtok, resid=(ypair, route, l, 5), split_out=True, **geom)

    y_prompt = x_p.reshape(n_batch, seq, d)
    y_sample = x_s.reshape(n_seq, s, d)[:, :dec_seq]
    return (y_prompt, y_sample, jnp.stack(ret_p), ret_s, jnp.stack(conv_p), jnp.stack(conv_s),
            jnp.stack(kp_l), jnp.stack(vp_l), jnp.stack(ks_l), jnp.stack(vs_l))
```

```python
import functools
import math

import jax
import jax.numpy as jnp
from jax import lax
from jax.experimental import pallas as pl
from jax.experimental.pallas import tpu as pltpu

F32 = jnp.float32
BF16 = jnp.bfloat16

NORM_EPS = 1e-6
NEG_INF = -1e30
ROPE_BASE = 10000.0
PAST_LEN = 16384

RET_HEADS = 8
RET_CHUNK = 128
CONV_WIDTH = 31
CONV_STATE = CONV_WIDTH - 1
ATT_Q_HEADS = 32
ATT_KV_HEADS = 4
ATT_GROUP = ATT_Q_HEADS // ATT_KV_HEADS
WINDOW = 128
REL_BUCKETS = 32
REL_MAX_DIST = 128
MOE_GROUPS = 4
MOE_EPG = 4
MOE_TOPK = 2

ROW_TILE = 512
SAMPLE_ROWS = 16
ROUTER_LANES = 128
VMEM_LIMIT_BYTES = 56 * 1024 * 1024


def _cparams(sem):
    return pltpu.CompilerParams(dimension_semantics=sem, vmem_limit_bytes=VMEM_LIMIT_BYTES)


def _pack_bf16_pairs(y):
    half = y.shape[1] // 2
    hi = lax.bitcast_convert_type(y[:, :half].astype(BF16).astype(F32), jnp.uint32)
    lo = lax.bitcast_convert_type(y[:, half:].astype(BF16).astype(F32), jnp.uint32)
    return hi | lax.shift_right_logical(lo, jnp.uint32(16))


def _unpack_bf16_pairs(w):
    hi = lax.bitcast_convert_type(w & jnp.uint32(0xFFFF0000), F32)
    lo = lax.bitcast_convert_type(lax.shift_left(w, jnp.uint32(16)), F32)
    return jnp.concatenate([hi, lo], axis=1)


def _ada_body(c_ref, w_ref, b_ref, o_ref):
    c = c_ref[...]
    s = (c * jax.nn.sigmoid(c)).astype(BF16)
    o_ref[...] = jnp.dot(s, w_ref[...].astype(BF16), preferred_element_type=F32) + b_ref[...]


def _ada_call(c_all, ada_w, ada_b):
    depth, d, n = ada_w.shape
    rows = c_all.shape[0]
    tn = 1024
    return pl.pallas_call(
        _ada_body,
        out_shape=jax.ShapeDtypeStruct((depth, rows, n), F32),
        grid=(depth, n // tn),
        in_specs=[
            pl.BlockSpec((rows, d), lambda l, j: (0, 0)),
            pl.BlockSpec((None, d, tn), lambda l, j: (l, 0, j)),
            pl.BlockSpec((None, 1, tn), lambda l, j: (l, 0, j)),
        ],
        out_specs=pl.BlockSpec((None, rows, tn), lambda l, j: (l, 0, j)),
        compiler_params=_cparams(("arbitrary", "arbitrary")),
        name="ada_mod",
    )(c_all, ada_w, ada_b.reshape(depth, 1, n))


ROUTE_E1, ROUTE_E2, ROUTE_W1, ROUTE_W2, ROUTE_R1, ROUTE_R2 = range(6)


def _route_tile(logits, rb, carry):
    tm = logits.shape[0]
    g, epg = MOE_GROUPS, MOE_EPG
    lg = logits + rb
    lane = lax.broadcasted_iota(jnp.int32, lg.shape, 1)
    lane_f = lane.astype(F32)

    def first_lane(mask):
        return jnp.min(jnp.where(mask, lane_f, float(ROUTER_LANES)), axis=-1, keepdims=True).astype(jnp.int32)

    is_g = lane < g
    mg = jnp.max(jnp.where(is_g, lg, NEG_INF), axis=-1, keepdims=True)
    eg = jnp.where(is_g, jnp.exp(lg - mg), 0.0)
    p_grp = 1.0 / jnp.sum(eg, axis=-1, keepdims=True)
    grp = first_lane(jnp.logical_and(is_g, lg == mg))
    lo = g + epg * grp
    is_e = jnp.logical_and(lane >= lo, lane < lo + epg)
    me = jnp.max(jnp.where(is_e, lg, NEG_INF), axis=-1, keepdims=True)
    ee = jnp.where(is_e, jnp.exp(lg - me), 0.0)
    se = jnp.sum(ee, axis=-1, keepdims=True)
    i1 = first_lane(jnp.logical_and(is_e, lg == me))
    rest = jnp.logical_and(is_e, lane != i1)
    m2 = jnp.max(jnp.where(rest, ee, -1.0), axis=-1, keepdims=True)
    i2 = first_lane(jnp.logical_and(rest, ee == m2))
    p1 = 1.0 / se
    p2 = m2 / se
    w1 = p_grp * p1 / (p1 + p2)
    w2 = p_grp * p2 / (p1 + p2)
    sel1 = lane == i1
    sel2 = lane == i2
    onehot = jnp.where(jnp.logical_or(sel1, sel2), 1.0, 0.0)
    row = lax.broadcasted_iota(jnp.int32, (tm, tm), 0)
    col = lax.broadcasted_iota(jnp.int32, (tm, tm), 1)
    tri = jnp.where(col <= row, 1.0, 0.0).astype(BF16)
    incl = jnp.dot(tri, onehot.astype(BF16), preferred_element_type=F32)
    rank = incl - 1.0 + carry
    r1 = jnp.sum(jnp.where(sel1, rank, 0.0), axis=-1, keepdims=True)
    r2 = jnp.sum(jnp.where(sel2, rank, 0.0), axis=-1, keepdims=True)
    rec = jnp.zeros_like(lg)
    for ln, val in ((ROUTE_E1, (i1 - g).astype(F32)), (ROUTE_E2, (i2 - g).astype(F32)), (ROUTE_W1, w1),
                    (ROUTE_W2, w2), (ROUTE_R1, r1), (ROUTE_R2, r2)):
        rec = jnp.where(lane == ln, val, rec)
    return rec, carry + jnp.sum(onehot, axis=0, keepdims=True)


def _norm_body(*refs, n_prompt_tiles, has_resid, has_norm, has_router, split_out):
    it = iter(refs)
    x_ref = next(it)
    if has_resid:
        y0_ref, y1_ref, rt_ref, grow_ref, gtok_ref = next(it), next(it), next(it), next(it), next(it)
    if has_norm:
        gain_ref, shrow_ref, scrow_ref, shtok_ref, sctok_ref = next(it), next(it), next(it), next(it), next(it)
    if has_router:
        wr_ref, rb_ref = next(it), next(it)
    if has_resid:
        xo_ref = next(it)
        if split_out:
            xs_ref = next(it)
    if has_norm:
        h_ref = next(it)
    if has_router:
        rec_ref, cnt_ref, carry = next(it), next(it), next(it)

    is_sample = pl.program_id(0) >= n_prompt_tiles

    if has_router:
        @pl.when(pl.program_id(0) == 0)
        def _():
            carry[...] = jnp.zeros_like(carry)

    def tile(sample):
        x = x_ref[...]
        if has_resid:
            gate = gtok_ref[...] if sample else grow_ref[...]
            rt = rt_ref[...]
            y = (rt[:, ROUTE_W1:ROUTE_W1 + 1] * _unpack_bf16_pairs(y0_ref[...])
                 + rt[:, ROUTE_W2:ROUTE_W2 + 1] * _unpack_bf16_pairs(y1_ref[...]))
            x = x + gate * y
            if split_out and sample:
                xs_ref[...] = x
            else:
                xo_ref[...] = x
        if has_norm:
            scale = sctok_ref[...] if sample else scrow_ref[...]
            shift = shtok_ref[...] if sample else shrow_ref[...]
            ms = jnp.mean(x * x, axis=-1, keepdims=True)
            h = (x * lax.rsqrt(ms + NORM_EPS)) * (gain_ref[...] * (1.0 + scale)) + shift
            if has_router:
                h_ref[...] = _pack_bf16_pairs(h)
            else:
                h_ref[...] = h.astype(h_ref.dtype)
            if has_router:
                h_hi = h.astype(BF16)
                h_lo = (h - h_hi.astype(F32)).astype(BF16)
                logits = (jnp.dot(h_hi, wr_ref[0], preferred_element_type=F32)
                          + jnp.dot(h_lo, wr_ref[0], preferred_element_type=F32)
                          + jnp.dot(h_hi, wr_ref[1], preferred_element_type=F32))
                rec, new_carry = _route_tile(logits, rb_ref[...], carry[0:1, :])
                rec_ref[...] = rec
                carry[...] = jnp.broadcast_to(new_carry, carry.shape)
                cnt_ref[...] = carry[...]

    pl.when(jnp.logical_not(is_sample))(functools.partial(tile, False))
    pl.when(is_sample)(functools.partial(tile, True))


def _norm_call(x, mod4, modtok, *, seq_tiles, n_batch, resid=None, norm=None, router=None, split_out=False):
    r, d = x.shape
    n_tiles = r // ROW_TILE
    n_prompt_tiles = n_tiles - 1

    def row_spec(layer, col):
        return pl.BlockSpec((None, None, 1, d),
                            lambda i: (layer, jnp.minimum(i // seq_tiles, n_batch - 1), 0, col))

    def tok_spec(layer, col):
        return pl.BlockSpec((None, ROW_TILE, d), lambda i: (layer, 0, col))

    tile = pl.BlockSpec((ROW_TILE, d), lambda i: (i, 0))
    args, in_specs, out_shape, out_specs = [x], [tile], [], []
    scratch = []
    if resid is not None:
        ypair, route, layer, gcol = resid
        args += [ypair, ypair, route, mod4, modtok]
        in_specs += [pl.BlockSpec((ROW_TILE, d // 2), lambda i: (i, 0)),
                     pl.BlockSpec((ROW_TILE, d // 2), lambda i: (n_tiles + i, 0)),
                     pl.BlockSpec((ROW_TILE, ROUTER_LANES), lambda i: (i, 0)),
                     row_spec(layer, gcol), tok_spec(layer, gcol)]
        if split_out:
            out_shape += [jax.ShapeDtypeStruct((r - ROW_TILE, d), F32), jax.ShapeDtypeStruct((ROW_TILE, d), F32)]
            out_specs += [pl.BlockSpec((ROW_TILE, d), lambda i: (jnp.minimum(i, n_prompt_tiles - 1), 0)),
                          pl.BlockSpec((ROW_TILE, d), lambda i: (0, 0))]
        else:
            out_shape.append(jax.ShapeDtypeStruct((r, d), F32))
            out_specs.append(tile)
    if norm is not None:
        gains, gidx, layer, shcol, sccol = norm
        args += [gains, mod4, mod4, modtok, modtok]
        in_specs += [pl.BlockSpec((None, 1, d), lambda i: (gidx, 0, 0)), row_spec(layer, shcol),
                     row_spec(layer, sccol), tok_spec(layer, shcol), tok_spec(layer, sccol)]
        if router is not None:
            out_shape.append(jax.ShapeDtypeStruct((r, d // 2), jnp.uint32))
            out_specs.append(pl.BlockSpec((ROW_TILE, d // 2), lambda i: (i, 0)))
        else:
            out_shape.append(jax.ShapeDtypeStruct((r, d), BF16))
            out_specs.append(tile)
        if router is not None:
            args += list(router)
            in_specs += [pl.BlockSpec((2, d, ROUTER_LANES), lambda i: (0, 0, 0)),
                         pl.BlockSpec((1, ROUTER_LANES), lambda i: (0, 0))]
            out_shape += [jax.ShapeDtypeStruct((r, ROUTER_LANES), F32), jax.ShapeDtypeStruct((8, ROUTER_LANES), F32)]
            out_specs += [pl.BlockSpec((ROW_TILE, ROUTER_LANES), lambda i: (i, 0)),
                          pl.BlockSpec((8, ROUTER_LANES), lambda i: (0, 0))]
            scratch.append(pltpu.VMEM((8, ROUTER_LANES), F32))
    body = functools.partial(_norm_body, n_prompt_tiles=n_prompt_tiles, has_resid=resid is not None,
                             has_norm=norm is not None, has_router=router is not None, split_out=split_out)
    return pl.pallas_call(
        body, out_shape=out_shape, grid=(n_tiles,), in_specs=in_specs, out_specs=out_specs,
        scratch_shapes=scratch, compiler_params=_cparams(("arbitrary",)), name="mod_norm",
    )(*args)


def _head_rms_norm(x, gain_row, seg_ref, exp_ref, hd):
    ssq = jnp.dot((x * x).astype(BF16), seg_ref[...], preferred_element_type=F32)
    r = lax.rsqrt(ssq * (1.0 / hd) + NORM_EPS)
    r_hi = r.astype(BF16)
    r_lo = (r - r_hi.astype(F32)).astype(BF16)
    scale = jnp.dot(jnp.concatenate([r_hi, r_lo], axis=1), exp_ref[...], preferred_element_type=F32)
    return x * scale * gain_row


def _linear_body(*refs, n_prompt_tiles, pair, glu, has_bias, has_resid, head_dim):
    it = iter(refs)
    h_ref = next(it)
    hs_ref = next(it) if pair else None
    w_ref = next(it)
    w2_ref = next(it) if glu else None
    b_ref = next(it) if has_bias else None
    b2_ref = next(it) if glu else None
    if has_resid:
        x_ref, grow_ref, gtok_ref = next(it), next(it), next(it)
    if head_dim:
        hgain_ref, hmask_ref, seg_ref, exp_ref = next(it), next(it), next(it), next(it)
    o_ref = next(it)
    wb = next(it)
    wb2 = next(it) if glu else None

    i = pl.program_id(1)
    is_sample = i >= n_prompt_tiles

    @pl.when(i == 0)
    def _():
        wb[...] = w_ref[...].astype(BF16)
        if glu:
            wb2[...] = w2_ref[...].astype(BF16)

    def compute(hv):
        acc = jnp.dot(hv, wb[...], preferred_element_type=F32)
        if has_bias:
            acc = acc + b_ref[...]
        if glu:
            acc2 = jnp.dot(hv, wb2[...], preferred_element_type=F32) + b2_ref[...]
            acc = acc * jax.nn.sigmoid(acc2)
        if has_resid:
            gate = jnp.where(is_sample, gtok_ref[...], grow_ref[...])
            acc = x_ref[...] + gate * acc
        if head_dim:
            normed = _head_rms_norm(acc, hgain_ref[...], seg_ref, exp_ref, head_dim)
            acc = jnp.where(hmask_ref[...] > 0.0, normed, acc)
        o_ref[...] = acc.astype(o_ref.dtype)

    if pair:
        @pl.when(jnp.logical_not(is_sample))
        def _():
            compute(h_ref[...])

        @pl.when(is_sample)
        def _():
            compute(hs_ref[...])
    else:
        compute(h_ref[...])


def _linear_call(h, w, wl, *, n_out, tn, out_dtype, seq_tiles, n_batch, h_sample=None, bias=None, glu=False,
                 resid=None, head_norm=None, single_buffer_w=False, name="linear"):
    k = w.shape[1]
    pair = h_sample is not None
    n_prompt_tiles = h.shape[0] // ROW_TILE - (0 if pair else 1)
    n_tiles = n_prompt_tiles + 1
    r = n_tiles * ROW_TILE
    nblk = n_out // tn

    args = [h]
    in_specs = [pl.BlockSpec((ROW_TILE, k), lambda j, i: (jnp.minimum(i, n_prompt_tiles - 1) if pair else i, 0))]
    if pair:
        args.append(h_sample)
        in_specs.append(pl.BlockSpec((ROW_TILE, k), lambda j, i: (0, 0), pipeline_mode=pl.Buffered(1)))
    args.append(w)
    w_mode = dict(pipeline_mode=pl.Buffered(1)) if single_buffer_w else {}
    in_specs.append(pl.BlockSpec((None, k, tn), lambda j, i: (wl, 0, j), **w_mode))
    if glu:
        args.append(w)
        in_specs.append(pl.BlockSpec((None, k, tn), lambda j, i: (wl, 0, nblk + j)))
    if bias is not None:
        bias = bias.reshape(bias.shape[0], 1, bias.shape[1])
        args.append(bias)
        in_specs.append(pl.BlockSpec((None, 1, tn), lambda j, i: (wl, 0, j)))
        if glu:
            args.append(bias)
            in_specs.append(pl.BlockSpec((None, 1, tn), lambda j, i: (wl, 0, nblk + j)))
    if resid is not None:
        x, mod4, modtok, layer, gcol = resid
        cb = gcol * nblk
        args += [x, mod4, modtok]
        in_specs += [
            pl.BlockSpec((ROW_TILE, tn), lambda j, i: (i, j)),
            pl.BlockSpec((None, None, 1, tn),
                         lambda j, i: (layer, jnp.minimum(i // seq_tiles, n_batch - 1), 0, cb + j)),
            pl.BlockSpec((None, ROW_TILE, tn), lambda j, i: (layer, 0, cb + j)),
        ]
    head_dim = 0
    if head_norm is not None:
        hgain, hmask, head_dim = head_norm
        seg = (jnp.arange(tn, dtype=jnp.int32)[:, None] // head_dim
               == jnp.arange(128, dtype=jnp.int32)[None, :]).astype(BF16)
        args += [hgain, hmask, seg, jnp.concatenate([seg.T, seg.T], axis=0)]
        in_specs += [pl.BlockSpec((1, tn), lambda j, i: (0, j)), pl.BlockSpec((1, tn), lambda j, i: (0, j)),
                     pl.BlockSpec((tn, 128), lambda j, i: (0, 0)), pl.BlockSpec((256, tn), lambda j, i: (0, 0))]
    scratch = [pltpu.VMEM((k, tn), BF16)] + ([pltpu.VMEM((k, tn), BF16)] if glu else [])
    body = functools.partial(_linear_body, n_prompt_tiles=n_prompt_tiles, pair=pair, glu=glu,
                             has_bias=bias is not None, has_resid=resid is not None, head_dim=head_dim)
    return pl.pallas_call(
        body,
        out_shape=jax.ShapeDtypeStruct((r, n_out), out_dtype),
        grid=(nblk, n_tiles),
        in_specs=in_specs,
        out_specs=pl.BlockSpec((ROW_TILE, tn), lambda j, i: (i, j)),
        scratch_shapes=scratch,
        compiler_params=_cparams(("arbitrary", "arbitrary")),
        name=name,
    )(*args)


def _rotate(x, cos, sin):
    half = x.shape[-1] // 2
    x1, x2 = x[:, :half], x[:, half:]
    return jnp.concatenate([x1 * cos - x2 * sin, x1 * sin + x2 * cos], axis=-1)


def _group_norm_gate(o, g, gain, bias):
    mu = jnp.mean(o, axis=-1, keepdims=True)
    var = jnp.mean(jnp.square(o - mu), axis=-1, keepdims=True)
    on = (o - mu) * lax.rsqrt(var + NORM_EPS) * gain + bias
    g = g.astype(F32)
    return (g * jax.nn.sigmoid(g) * on).astype(BF16)


def _ret_prompt_body(q_ref, k_ref, v_ref, g_ref, cos_ref, sin_ref, dm_ref, qd_ref, kd_ref, gl_ref,
                     gng_ref, gnb_ref, o_ref, st_ref, s_acc, *, n_chunks):
    s_acc[...] = jnp.zeros_like(s_acc)
    dmask = dm_ref[...]
    qdec = qd_ref[...]
    kdec = kd_ref[...]
    gl = gl_ref[0:1, 0:1]
    gng = gng_ref[...]
    gnb = gnb_ref[...]

    def chunk(c, carry):
        r0 = pl.multiple_of(c * RET_CHUNK, RET_CHUNK)
        rows = pl.ds(r0, RET_CHUNK)
        cos = cos_ref[rows, :]
        sin = sin_ref[rows, :]
        qr = _rotate(q_ref[rows, :].astype(F32), cos, sin)
        kr = _rotate(k_ref[rows, :].astype(F32), cos, sin)
        v = v_ref[rows, :]
        state = s_acc[...]
        scores = lax.dot_general(qr.astype(BF16), kr.astype(BF16), (((1,), (1,)), ((), ())),
                                 preferred_element_type=F32)
        scores = scores * dmask
        out = jnp.dot(scores.astype(BF16), v, preferred_element_type=F32)
        out = out + jnp.dot((qr * qdec).astype(BF16), state.astype(BF16), preferred_element_type=F32)
        kv = lax.dot_general((kr * kdec).astype(BF16), v, (((0,), (0,)), ((), ())), preferred_element_type=F32)
        s_acc[...] = gl * state + kv
        o_ref[rows, :] = _group_norm_gate(out, g_ref[rows, :], gng, gnb)
        return carry

    lax.fori_loop(0, n_chunks, chunk, 0, unroll=8)
    st_ref[...] = s_acc[...]


def _ret_tables(chunk, dk, n_valid=None):
    n_valid = chunk if n_valid is None else n_valid
    lg = jnp.log1p(-jnp.exp2(-5.0 - jnp.arange(RET_HEADS, dtype=F32)))
    idx = jnp.arange(chunk, dtype=F32)
    diff = idx[:, None] - idx[None, :]
    inside = (idx[:, None] < n_valid) & (idx[None, :] < n_valid)
    dmask = jnp.where((diff[None] >= 0) & inside[None],
                      jnp.exp(jnp.maximum(diff, 0.0)[None] * lg[:, None, None]), 0.0) * (dk ** -0.5)
    qdec = jnp.exp((idx[None, :] + 1.0) * lg[:, None])
    kdec = jnp.where(idx[None, :] < n_valid, jnp.exp((n_valid - 1.0 - idx)[None, :] * lg[:, None]), 0.0) * (dk ** -0.5)
    gl = jnp.exp(n_valid * lg)
    qdec = jnp.broadcast_to(qdec[:, :, None], (RET_HEADS, chunk, dk))
    kdec = jnp.broadcast_to(kdec[:, :, None], (RET_HEADS, chunk, dk))
    gl = jnp.broadcast_to(gl[:, None, None], (RET_HEADS, 8, 128))
    return dmask, qdec, kdec, gl


def _rope_tables(pos, half):
    inv = ROPE_BASE ** (-jnp.arange(half, dtype=F32) / half)
    ang = pos.astype(F32)[:, None] * inv[None, :]
    return jnp.cos(ang), jnp.sin(ang)


def _ret_prompt_call(proj, gn_gain, gn_bias, j, *, n_batch, seq, dk, dv):
    h = RET_HEADS
    n_chunks = seq // RET_CHUNK
    cos, sin = _rope_tables(jnp.arange(seq, dtype=jnp.int32), dk // 2)
    dmask, qdec, kdec, gl = _ret_tables(RET_CHUNK, dk)
    kcol, vcol, gcol = h, (2 * h * dk) // dv, (2 * h * dk) // dv + h
    body = functools.partial(_ret_prompt_body, n_chunks=n_chunks)
    return pl.pallas_call(
        body,
        out_shape=[jax.ShapeDtypeStruct((n_batch * seq, h * dv), BF16),
                   jax.ShapeDtypeStruct((n_batch, h, dk, dv), F32)],
        grid=(n_batch, h),
        in_specs=[
            pl.BlockSpec((seq, dk), lambda b, hh: (b, hh)),
            pl.BlockSpec((seq, dk), lambda b, hh: (b, kcol + hh)),
            pl.BlockSpec((seq, dv), lambda b, hh: (b, vcol + hh)),
            pl.BlockSpec((seq, dv), lambda b, hh: (b, gcol + hh)),
            pl.BlockSpec((seq, dk // 2), lambda b, hh: (0, 0)),
            pl.BlockSpec((seq, dk // 2), lambda b, hh: (0, 0)),
            pl.BlockSpec((None, RET_CHUNK, RET_CHUNK), lambda b, hh: (hh, 0, 0)),
            pl.BlockSpec((None, RET_CHUNK, dk), lambda b, hh: (hh, 0, 0)),
            pl.BlockSpec((None, RET_CHUNK, dk), lambda b, hh: (hh, 0, 0)),
            pl.BlockSpec((None, 8, 128), lambda b, hh: (hh, 0, 0)),
            pl.BlockSpec((None, 1, dv), lambda b, hh: (j, 0, hh)),
            pl.BlockSpec((None, 1, dv), lambda b, hh: (j, 0, hh)),
        ],
        out_specs=[pl.BlockSpec((seq, dv), lambda b, hh: (b, hh)),
                   pl.BlockSpec((None, None, dk, dv), lambda b, hh: (b, hh, 0, 0))],
        scratch_shapes=[pltpu.VMEM((dk, dv), F32)],
        compiler_params=_cparams(("arbitrary", "arbitrary")),
        name="retention_prompt",
    )(proj, proj, proj, proj, cos, sin, dmask, qdec, kdec, gl, gn_gain, gn_bias)


def _ret_sample_body(p_ref, st_ref, cos_ref, sin_ref, dm_ref, qd_ref, kd_ref, gl_ref, gng_ref, gnb_ref,
                     *rest, dk, dv):
    o_ref, so_ref = rest[-2], rest[-1]
    h = RET_HEADS
    cos = cos_ref[...]
    sin = sin_ref[...]
    pad = 128 - SAMPLE_ROWS
    for hh in range(h):
        q = p_ref[:, hh * dk:(hh + 1) * dk].astype(F32)
        k = p_ref[:, h * dk + hh * dk:h * dk + (hh + 1) * dk].astype(F32)
        v = p_ref[:, 2 * h * dk + hh * dv:2 * h * dk + (hh + 1) * dv]
        g = p_ref[:, 2 * h * dk + h * dv + hh * dv:2 * h * dk + h * dv + (hh + 1) * dv]
        qr = _rotate(q, cos, sin)
        kr = _rotate(k, cos, sin)
        k_pad = jnp.concatenate([kr.astype(BF16), jnp.zeros((pad, dk), BF16)], axis=0)
        kd_pad = jnp.concatenate([(kr * kd_ref[hh]).astype(BF16), jnp.zeros((pad, dk), BF16)], axis=0)
        v_pad = jnp.concatenate([v, jnp.zeros((pad, dv), BF16)], axis=0)
        state = st_ref[hh]
        scores = lax.dot_general(qr.astype(BF16), k_pad, (((1,), (1,)), ((), ())), preferred_element_type=F32)
        scores = scores * dm_ref[hh]
        out = jnp.dot(scores.astype(BF16), v_pad, preferred_element_type=F32)
        out = out + jnp.dot((qr * qd_ref[hh]).astype(BF16), state.astype(BF16), preferred_element_type=F32)
        kv = lax.dot_general(kd_pad, v_pad, (((0,), (0,)), ((), ())), preferred_element_type=F32)
        so_ref[hh] = gl_ref[hh, 0:1, 0:1] * state + kv
        o_ref[:, hh * dv:(hh + 1) * dv] = _group_norm_gate(out, g, gng_ref[:, hh * dv:(hh + 1) * dv],
                                                           gnb_ref[:, hh * dv:(hh + 1) * dv])


def _ret_sample_call(proj, state, gn_gain, gn_bias, j, new_state, *, n_prompt_rows, n_seq, dec_seq, dk, dv):
    h = RET_HEADS
    s = SAMPLE_ROWS
    cos, sin = _rope_tables(PAST_LEN + jnp.arange(s, dtype=jnp.int32), dk // 2)
    dmask, qdec, kdec, gl = _ret_tables(s, dk, n_valid=dec_seq)
    dmask = jnp.pad(dmask, ((0, 0), (0, 0), (0, 128 - s)))
    base = n_prompt_rows // s
    body = functools.partial(_ret_sample_body, dk=dk, dv=dv)
    width = proj.shape[1]
    args = [proj, state, cos, sin, dmask, qdec, kdec, gl, gn_gain, gn_bias]
    in_specs = [
        pl.BlockSpec((s, width), lambda b: (base + b, 0)),
        pl.BlockSpec((None, None, h, dk, dv), lambda b: (j, b, 0, 0, 0)),
        pl.BlockSpec((s, dk // 2), lambda b: (0, 0)),
        pl.BlockSpec((s, dk // 2), lambda b: (0, 0)),
        pl.BlockSpec((h, s, 128), lambda b: (0, 0, 0)),
        pl.BlockSpec((h, s, dk), lambda b: (0, 0, 0)),
        pl.BlockSpec((h, s, dk), lambda b: (0, 0, 0)),
        pl.BlockSpec((h, 8, 128), lambda b: (0, 0, 0)),
        pl.BlockSpec((None, 1, h * dv), lambda b: (j, 0, 0)),
        pl.BlockSpec((None, 1, h * dv), lambda b: (j, 0, 0)),
    ]
    aliases = {}
    if new_state is not None:
        aliases = {len(args): 1}
        args.append(new_state)
        in_specs.append(pl.BlockSpec(memory_space=pl.ANY))
    return pl.pallas_call(
        body,
        out_shape=[jax.ShapeDtypeStruct((n_seq * s, h * dv), BF16),
                   jax.ShapeDtypeStruct(state.shape, F32)],
        grid=(n_seq,),
        in_specs=in_specs,
        out_specs=[pl.BlockSpec((s, h * dv), lambda b: (b, 0)),
                   pl.BlockSpec((None, None, h, dk, dv), lambda b: (j, b, 0, 0, 0))],
        input_output_aliases=aliases,
        compiler_params=_cparams(("arbitrary",)),
        name="retention_sample",
    )(*args)


CONV_HALO = 32
CONV_ROW_CHUNK = 64
CONV_LANES = 128


def _layer_norm_swish(u, gain, bias):
    mu = jnp.mean(u, axis=-1, keepdims=True)
    var = jnp.mean(jnp.square(u - mu), axis=-1, keepdims=True)
    un = (u - mu) * lax.rsqrt(var + NORM_EPS) * gain + bias
    return (un * jax.nn.sigmoid(un)).astype(BF16)


def _conv_prompt_body(z_ref, w_ref, bdw_ref, lng_ref, lnb_ref, o_ref, zbuf, ubuf, *, tt, d):
    t = pl.program_id(1)

    @pl.when(t == 0)
    def _():
        zbuf[0:CONV_HALO, :] = jnp.zeros((CONV_HALO, d), F32)

    @pl.when(t > 0)
    def _():
        zbuf[0:CONV_HALO, :] = zbuf[tt:tt + CONV_HALO, :]

    zbuf[CONV_HALO:CONV_HALO + tt, :] = z_ref[...].astype(F32)

    span = CONV_ROW_CHUNK + CONV_HALO
    n_row_chunks = tt // CONV_ROW_CHUNK
    n_strips = d // CONV_LANES

    def strip(n, carry):
        r0 = pl.multiple_of((n % n_row_chunks) * CONV_ROW_CHUNK, CONV_ROW_CHUNK)
        c0 = pl.multiple_of((n // n_row_chunks) * CONV_LANES, CONV_LANES)
        cols = pl.ds(c0, CONV_LANES)
        blk = zbuf[pl.ds(r0, span), cols]
        acc = jnp.zeros((CONV_ROW_CHUNK, CONV_LANES), F32)
        for b in range(8):
            rb = blk if b == 0 else pltpu.roll(blk, span - b, axis=0)
            for a in range(5):
                o = 8 * a + b
                if 2 <= o <= CONV_HALO:
                    acc = acc + rb[8 * a:8 * a + CONV_ROW_CHUNK, :] * w_ref[pl.ds(o - 2, 1), cols]
        ubuf[pl.ds(r0, CONV_ROW_CHUNK), cols] = acc
        return carry

    lax.fori_loop(0, n_row_chunks * n_strips, strip, 0, unroll=4)

    ln_rows = 128

    def ln_chunk(c, carry):
        rows = pl.ds(pl.multiple_of(c * ln_rows, ln_rows), ln_rows)
        o_ref[rows, :] = _layer_norm_swish(ubuf[rows, :] + bdw_ref[...], lng_ref[...], lnb_ref[...])
        return carry

    lax.fori_loop(0, tt // ln_rows, ln_chunk, 0)


def _conv_prompt_call(z, w_dw, b_dw, ln_gain, ln_bias, *, n_batch, seq):
    d = z.shape[1]
    tt = ROW_TILE
    nt = seq // tt
    w_pad = jnp.pad(w_dw, ((0, 32 - CONV_WIDTH), (0, 0)))
    body = functools.partial(_conv_prompt_body, tt=tt, d=d)
    vec = pl.BlockSpec((1, d), lambda b, t: (0, 0))
    return pl.pallas_call(
        body,
        out_shape=jax.ShapeDtypeStruct((n_batch * seq, d), BF16),
        grid=(n_batch, nt),
        in_specs=[pl.BlockSpec((tt, d), lambda b, t: (b * nt + t, 0)),
                  pl.BlockSpec((32, d), lambda b, t: (0, 0)), vec, vec, vec],
        out_specs=pl.BlockSpec((tt, d), lambda b, t: (b * nt + t, 0)),
        scratch_shapes=[pltpu.VMEM((tt + CONV_HALO, d), F32), pltpu.VMEM((tt, d), F32)],
        compiler_params=_cparams(("arbitrary", "arbitrary")),
        name="conv_prompt",
    )(z, w_pad, b_dw, ln_gain, ln_bias)


def _conv_sample_body(z_ref, c_ref, w_ref, bdw_ref, lng_ref, lnb_ref, o_ref, zbuf, *, d):
    s = SAMPLE_ROWS
    zbuf[0:CONV_STATE, :] = c_ref[...]
    zbuf[CONV_STATE:CONV_STATE + s, :] = z_ref[...].astype(F32)
    acc = jnp.zeros((s, d), F32)
    for j in range(CONV_WIDTH):
        acc = acc + zbuf[j:j + s, :] * w_ref[j:j + 1, :]
    o_ref[...] = _layer_norm_swish(acc + bdw_ref[...], lng_ref[...], lnb_ref[...])


def _conv_sample_call(z, cache, w_dw, b_dw, ln_gain, ln_bias, *, n_prompt_rows, n_seq):
    d = z.shape[1]
    s = SAMPLE_ROWS
    base = n_prompt_rows // s
    body = functools.partial(_conv_sample_body, d=d)
    vec = pl.BlockSpec((1, d), lambda b: (0, 0))
    return pl.pallas_call(
        body,
        out_shape=jax.ShapeDtypeStruct((n_seq * s, d), BF16),
        grid=(n_seq,),
        in_specs=[pl.BlockSpec((s, d), lambda b: (base + b, 0)),
                  pl.BlockSpec((None, CONV_STATE, d), lambda b: (b, 0, 0)),
                  pl.BlockSpec((CONV_WIDTH, d), lambda b: (0, 0)), vec, vec, vec],
        out_specs=pl.BlockSpec((s, d), lambda b: (b, 0)),
        scratch_shapes=[pltpu.VMEM((CONV_STATE + s, d), F32)],
        compiler_params=_cparams(("arbitrary",)),
        name="conv_sample",
    )(z, cache, w_dw, b_dw, ln_gain, ln_bias)


def _t5_bucket(dist):
    n = jnp.maximum(dist, 0)
    max_exact = REL_BUCKETS // 2
    nf = jnp.maximum(n, 1).astype(F32)
    large = max_exact + (jnp.log(nf / max_exact) / math.log(REL_MAX_DIST / max_exact)
                         * (REL_BUCKETS - max_exact)).astype(jnp.int32)
    large = jnp.minimum(large, REL_BUCKETS - 1)
    return jnp.where(n < max_exact, n, large)


def _bias_table(rel_bias, dist, valid):
    onehot = (_t5_bucket(dist).reshape(-1)[None, :] == jnp.arange(REL_BUCKETS, dtype=jnp.int32)[:, None]).astype(F32)
    tbl = jnp.dot(rel_bias.astype(F32).T, onehot, precision=lax.Precision.HIGHEST)
    tbl = jnp.where(valid.reshape(-1)[None, :], tbl, NEG_INF)
    return tbl.reshape(-1, dist.shape[-1])


def _swa_prompt_body(sink_ref, q_ref, kvp_ref, kvc_ref, bias_ref, o_ref, *, hd):
    i = pl.program_id(1)
    blk = WINDOW
    g = ATT_GROUP
    nkv = ATT_KV_HEADS
    nk = nkv * hd
    col = lax.broadcasted_iota(jnp.int32, (1, 1, 2 * blk), 2)
    first_mask = jnp.where(jnp.logical_and(i == 0, col < blk), NEG_INF, 0.0)
    keys = jnp.stack([jnp.concatenate([kvp_ref[:, hk * hd:(hk + 1) * hd], kvc_ref[:, hk * hd:(hk + 1) * hd]], axis=0)
                      for hk in range(nkv)])
    vals = jnp.stack([jnp.concatenate([kvp_ref[:, nk + hk * hd:nk + (hk + 1) * hd],
                                       kvc_ref[:, nk + hk * hd:nk + (hk + 1) * hd]], axis=0)
                      for hk in range(nkv)])
    qs = jnp.stack([jnp.concatenate([q_ref[:, (hk * g + gg) * hd:(hk * g + gg + 1) * hd] for gg in range(g)], axis=0)
                    for hk in range(nkv)])
    s = jnp.einsum('hqd,hkd->hqk', qs, keys, preferred_element_type=F32)
    s = s + bias_ref[...].reshape(nkv, g * blk, 2 * blk) + first_mask
    sink = jnp.stack([jnp.concatenate([jnp.full((blk, 1), sink_ref[hk * g + gg], F32) for gg in range(g)], axis=0)
                      for hk in range(nkv)])
    m = jnp.maximum(jnp.max(s, axis=-1, keepdims=True), sink)
    p = jnp.exp(s - m).astype(BF16)
    den = jnp.einsum('hqk,hkd->hqd', p, jnp.ones(vals.shape, BF16), preferred_element_type=F32) + jnp.exp(sink - m)
    o = jnp.einsum('hqk,hkd->hqd', p, vals, preferred_element_type=F32) / den
    for hk in range(nkv):
        for gg in range(g):
            o_ref[:, (hk * g + gg) * hd:(hk * g + gg + 1) * hd] = o[hk, gg * blk:(gg + 1) * blk, :].astype(BF16)


def _swa_prompt_call(proj, sinks, rel_bias, *, n_batch, seq, hd):
    blk = WINDOW
    nb = seq // blk
    nq = ATT_Q_HEADS * hd
    nkv2 = 2 * ATT_KV_HEADS * hd
    kvcol = nq // nkv2
    i_idx = jnp.arange(blk, dtype=jnp.int32)[:, None]
    j_idx = jnp.arange(2 * blk, dtype=jnp.int32)[None, :]
    dist = blk + i_idx - j_idx
    bias = _bias_table(rel_bias, dist, (dist >= 0) & (dist < WINDOW))
    body = functools.partial(_swa_prompt_body, hd=hd)
    return pl.pallas_call(
        body,
        out_shape=jax.ShapeDtypeStruct((n_batch * seq, nq), BF16),
        grid=(n_batch, nb),
        in_specs=[
            pl.BlockSpec(memory_space=pltpu.SMEM),
            pl.BlockSpec((blk, nq), lambda b, i: (b * nb + i, 0)),
            pl.BlockSpec((blk, nkv2), lambda b, i: (b * nb + jnp.maximum(i - 1, 0), kvcol)),
            pl.BlockSpec((blk, nkv2), lambda b, i: (b * nb + i, kvcol)),
            pl.BlockSpec(bias.shape, lambda b, i: (0, 0)),
        ],
        out_specs=pl.BlockSpec((blk, nq), lambda b, i: (b * nb + i, 0)),
        compiler_params=_cparams(("arbitrary", "arbitrary")),
        name="swa_prompt",
    )(sinks, proj, proj, proj, bias)


def _swa_sample_body(sink_ref, p_ref, ck_ref, cv_ref, bias_ref, o_ref, *, hd):
    s = SAMPLE_ROWS
    g = ATT_GROUP
    nkv = ATT_KV_HEADS
    nq = ATT_Q_HEADS * hd
    keys = jnp.stack([jnp.concatenate([ck_ref[:, hk * hd:(hk + 1) * hd].astype(BF16),
                                       p_ref[:, nq + hk * hd:nq + (hk + 1) * hd]], axis=0)
                      for hk in range(nkv)])
    vals = jnp.stack([jnp.concatenate([cv_ref[:, hk * hd:(hk + 1) * hd].astype(BF16),
                                       p_ref[:, nq + (nkv + hk) * hd:nq + (nkv + hk + 1) * hd]], axis=0)
                      for hk in range(nkv)])
    qs = jnp.stack([jnp.concatenate([p_ref[:, (hk * g + gg) * hd:(hk * g + gg + 1) * hd] for gg in range(g)], axis=0)
                    for hk in range(nkv)])
    sc = jnp.einsum('hqd,hkd->hqk', qs, keys, preferred_element_type=F32)
    sc = sc + bias_ref[...].reshape(nkv, g * s, keys.shape[1])
    sink = jnp.stack([jnp.concatenate([jnp.full((s, 1), sink_ref[hk * g + gg], F32) for gg in range(g)], axis=0)
                      for hk in range(nkv)])
    m = jnp.maximum(jnp.max(sc, axis=-1, keepdims=True), sink)
    p = jnp.exp(sc - m).astype(BF16)
    den = jnp.einsum('hqk,hkd->hqd', p, jnp.ones(vals.shape, BF16), preferred_element_type=F32) + jnp.exp(sink - m)
    o = jnp.einsum('hqk,hkd->hqd', p, vals, preferred_element_type=F32) / den
    for hk in range(nkv):
        for gg in range(g):
            o_ref[:, (hk * g + gg) * hd:(hk * g + gg + 1) * hd] = o[hk, gg * s:(gg + 1) * s, :].astype(BF16)


def _swa_sample_call(proj, cache_k, cache_v, sinks, rel_bias, *, n_prompt_rows, n_seq, dec_seq, hd):
    s = SAMPLE_ROWS
    nq = ATT_Q_HEADS * hd
    nkv = ATT_KV_HEADS * hd
    sc = cache_k.shape[1]
    i_idx = jnp.arange(s, dtype=jnp.int32)[:, None]
    j_idx = jnp.arange(sc + s, dtype=jnp.int32)[None, :]
    dist = sc + i_idx - j_idx
    valid = (dist >= 0) & (dist < WINDOW) & (j_idx < sc + dec_seq)
    bias = _bias_table(rel_bias, dist, valid)
    base = n_prompt_rows // s
    body = functools.partial(_swa_sample_body, hd=hd)
    return pl.pallas_call(
        body,
        out_shape=jax.ShapeDtypeStruct((n_seq * s, nq), BF16),
        grid=(n_seq,),
        in_specs=[
            pl.BlockSpec(memory_space=pltpu.SMEM),
            pl.BlockSpec((s, proj.shape[1]), lambda b: (base + b, 0)),
            pl.BlockSpec((None, sc, nkv), lambda b: (b, 0, 0)),
            pl.BlockSpec((None, sc, nkv), lambda b: (b, 0, 0)),
            pl.BlockSpec((ATT_Q_HEADS * s, sc + s), lambda b: (0, 0)),
        ],
        out_specs=pl.BlockSpec((s, nq), lambda b: (b, 0)),
        compiler_params=_cparams(("arbitrary",)),
        name="swa_sample",
    )(sinks, proj, cache_k, cache_v, bias)


MOE_TILE = 256


def _moe_body(src_ref, dst_ref, te_ref, nu_ref, h_hbm, win_ref, wout_ref, y_hbm, xg, yb, wib, wob, gsem, ssem,
              *, ff):
    t = pl.program_id(0)
    nt = pl.num_programs(0)
    n_used = nu_ref[0]
    tm = MOE_TILE
    slot = lax.rem(t, 2)

    def gather_start(tile, sl):
        base = tile * tm
        for r in range(tm):
            pltpu.make_async_copy(h_hbm.at[pl.ds(src_ref[base + r], 1), :], xg.at[sl, pl.ds(r, 1), :],
                                  gsem.at[sl]).start()

    def gather_wait(sl):
        pltpu.make_async_copy(h_hbm.at[pl.ds(0, tm), :], xg.at[sl], gsem.at[sl]).wait()

    def scatter_start(tile, sl):
        base = tile * tm
        for r in range(tm):
            pltpu.make_async_copy(yb.at[sl, pl.ds(r, 1), :], y_hbm.at[pl.ds(dst_ref[base + r], 1), :],
                                  ssem.at[sl]).start()

    def scatter_wait(sl):
        pltpu.make_async_copy(yb.at[sl], y_hbm.at[pl.ds(0, tm), :], ssem.at[sl]).wait()

    valid = t < n_used

    @pl.when(t == 0)
    def _():
        gather_start(0, 0)

    @pl.when(jnp.logical_and(valid, jnp.logical_or(t == 0, te_ref[t] != te_ref[jnp.maximum(t - 1, 0)])))
    def _():
        wib[...] = win_ref[...].astype(BF16)
        wob[...] = wout_ref[...].astype(BF16)

    def step(sl):
        @pl.when(t >= 2)
        def _():
            scatter_wait(sl)

        gather_wait(sl)

        @pl.when(t + 1 < n_used)
        def _():
            gather_start(t + 1, 1 - sl)

        gu = jnp.dot(_unpack_bf16_pairs(xg[sl]).astype(BF16), wib[...], preferred_element_type=F32)
        gate, up = gu[:, :ff], gu[:, ff:]
        act = (gate * jax.nn.sigmoid(gate) * up).astype(BF16)
        yb[sl] = _pack_bf16_pairs(jnp.dot(act, wob[...], preferred_element_type=F32))
        scatter_start(t, sl)

    for sl in range(2):
        pl.when(jnp.logical_and(valid, slot == sl))(functools.partial(step, sl))

    @pl.when(t == nt - 1)
    def _():
        scatter_wait(lax.rem(n_used - 1, 2))
        scatter_wait(lax.rem(n_used, 2))


def _moe_layer(h2, route, counts, w_in, w_out, layer):
    r = h2.shape[0]
    depth, n_exp, d, ff2 = w_in.shape
    ff = ff2 // 2
    tm = MOE_TILE
    n_tiles = 2 * r // tm + n_exp
    n_slots = n_tiles * tm

    cnt = counts[0, MOE_GROUPS:MOE_GROUPS + n_exp].astype(jnp.int32)
    padded = ((cnt + tm - 1) // tm) * tm
    ends = jnp.cumsum(padded)
    starts = ends - padded
    rt = route[:, :8].T.astype(jnp.int32)
    pos = jnp.concatenate([starts[rt[ROUTE_E1]] + rt[ROUTE_R1], starts[rt[ROUTE_E2]] + rt[ROUTE_R2]])
    tok = jnp.arange(r, dtype=jnp.int32)
    dst = (2 * r + jnp.arange(n_slots, dtype=jnp.int32)).at[pos].set(
        jnp.concatenate([tok, r + tok]), unique_indices=True, indices_are_sorted=False, mode="promise_in_bounds")
    src = jnp.where(dst < r, dst, jnp.where(dst < 2 * r, dst - r, 0))
    tile_start = jnp.arange(n_tiles, dtype=jnp.int32) * tm
    tile_expert = jnp.minimum(jnp.sum(tile_start[:, None] >= ends[None, :], axis=1), n_exp - 1).astype(jnp.int32)
    n_used = (ends[-1:] // tm).astype(jnp.int32)

    return pl.pallas_call(
        functools.partial(_moe_body, ff=ff),
        out_shape=jax.ShapeDtypeStruct((2 * r + n_slots, d // 2), jnp.uint32),
        grid_spec=pltpu.PrefetchScalarGridSpec(
            num_scalar_prefetch=4,
            grid=(n_tiles,),
            in_specs=[
                pl.BlockSpec(memory_space=pl.ANY),
                pl.BlockSpec((None, None, d, ff2), lambda t, src, dst, te, nu: (layer, te[t], 0, 0)),
                pl.BlockSpec((None, None, ff, d), lambda t, src, dst, te, nu: (layer, te[t], 0, 0)),
            ],
            out_specs=pl.BlockSpec(memory_space=pl.ANY),
            scratch_shapes=[pltpu.VMEM((2, tm, d // 2), jnp.uint32), pltpu.VMEM((2, tm, d // 2), jnp.uint32),
                            pltpu.VMEM((d, ff2), BF16), pltpu.VMEM((ff, d), BF16),
                            pltpu.SemaphoreType.DMA((2,)), pltpu.SemaphoreType.DMA((2,))],
        ),
        compiler_params=_cparams(("arbitrary",)),
        name="moe_experts",
    )(src, dst, tile_expert, n_used, h2, w_in, w_out)


def kernel(x_prompt, x_sample, c_prompt, c_sample, state_ret, cache_conv, cache_swa_k, cache_swa_v, ada_w, ada_b, norm_gain, ret_w_in, ret_gn_gain, ret_gn_bias, ret_w_out, conv_w_pw1, conv_b_pw1, conv_w_dw, conv_b_dw, conv_ln_gain, conv_ln_bias, conv_w_pw2, conv_b_pw2, att_w_qkv, att_q_gain, att_k_gain, att_sinks, att_w_o, rel_bias, moe_wg, moe_bg, moe_we, moe_be, moe_w_in, moe_w_out):
    n_batch, seq, d = x_prompt.shape
    n_seq, dec_seq, _ = x_sample.shape
    depth = ada_w.shape[0]
    s = SAMPLE_ROWS
    assert n_seq * s == ROW_TILE and seq % ROW_TILE == 0 and dec_seq <= s
    n_prompt_rows = n_batch * seq
    seq_tiles = seq // ROW_TILE
    dk = ret_w_in.shape[2] // (6 * RET_HEADS)
    dv = 2 * dk
    hd = d // ATT_Q_HEADS
    geom = dict(seq_tiles=seq_tiles, n_batch=n_batch)

    xs_pad = jnp.pad(x_sample, ((0, 0), (0, s - dec_seq), (0, 0))).reshape(n_seq * s, d)
    x = jnp.concatenate([x_prompt.reshape(n_prompt_rows, d), xs_pad], axis=0)

    n_c = n_batch + n_seq
    c_rows = ((n_c + 7) // 8) * 8
    c_all = jnp.pad(jnp.concatenate([c_prompt, c_sample], axis=0), ((0, c_rows - n_c), (0, 0)))
    mod = _ada_call(c_all, ada_w, ada_b)
    mod4 = mod.reshape(depth, c_rows, 1, 6 * d)
    modtok = jnp.repeat(mod[:, n_batch:n_c], s, axis=1)

    gains = norm_gain.astype(F32).reshape(2 * depth, 1, d)
    gng_all = ret_gn_gain.astype(F32)[:, None, :]
    gnb_all = ret_gn_bias.astype(F32)[:, None, :]
    state_all = state_ret.astype(F32)

    ret_p, conv_p, conv_s, kp_l, vp_l, ks_l, vs_l = [], [], [], [], [], [], []
    ret_s = None
    (h,) = _norm_call(x, mod4, modtok, norm=(gains, 0, 0, 0, 1), **geom)
    for l in range(depth):
        kind, j = l % 3, l // 3
        resid1 = (x, mod4, modtok, l, 2)
        if kind == 0:
            proj = _linear_call(h, ret_w_in, j, n_out=ret_w_in.shape[2], tn=2048, out_dtype=BF16,
                                name="ret_in", **geom)
            a_p, st_p = _ret_prompt_call(proj, gng_all, gnb_all, j, n_batch=n_batch, seq=seq, dk=dk, dv=dv)
            a_s, ret_s = _ret_sample_call(proj, state_all, gng_all, gnb_all, j, ret_s, n_prompt_rows=n_prompt_rows,
                                          n_seq=n_seq, dec_seq=dec_seq, dk=dk, dv=dv)
            ret_p.append(st_p)
            x = _linear_call(a_p, ret_w_out, j, h_sample=a_s, n_out=d, tn=1024, out_dtype=F32, resid=resid1,
                             single_buffer_w=True, name="ret_out", **geom)
        elif kind == 1:
            z = _linear_call(h, conv_w_pw1, j, n_out=d, tn=1024, out_dtype=BF16, bias=conv_b_pw1,
                             glu=True, name="conv_pw1", **geom)
            cargs = (conv_w_dw[j], conv_b_dw[j][None], conv_ln_gain[j][None], conv_ln_bias[j][None])
            a_p = _conv_prompt_call(z, *cargs, n_batch=n_batch, seq=seq)
            a_s = _conv_sample_call(z, cache_conv[j].astype(F32), *cargs, n_prompt_rows=n_prompt_rows, n_seq=n_seq)
            z_tail = jnp.stack([z[(b + 1) * seq - CONV_STATE:(b + 1) * seq] for b in range(n_batch)])
            conv_p.append(z_tail.astype(F32))
            z_new = z[n_prompt_rows:].reshape(n_seq, s, d)[:, :dec_seq].astype(F32)
            conv_s.append(jnp.concatenate([cache_conv[j].astype(F32), z_new], axis=1)[:, -CONV_STATE:])
            x = _linear_call(a_p, conv_w_pw2, j, h_sample=a_s, n_out=d, tn=1024, out_dtype=F32,
                             bias=conv_b_pw2, resid=resid1, name="conv_pw2", **geom)
        else:
            nkv = ATT_KV_HEADS * hd
            nq = ATT_Q_HEADS * hd
            hgain = jnp.concatenate([jnp.tile(att_q_gain[j].astype(F32) * (hd ** -0.5), ATT_Q_HEADS),
                                     jnp.tile(att_k_gain[j].astype(F32), ATT_KV_HEADS), jnp.ones((nkv,), F32)])[None]
            hmask = jnp.concatenate([jnp.ones((nq + nkv,), F32), jnp.zeros((nkv,), F32)])[None]
            proj = _linear_call(h, att_w_qkv, j, n_out=att_w_qkv.shape[2], tn=att_w_qkv.shape[2], out_dtype=BF16,
                                head_norm=(hgain, hmask, hd), single_buffer_w=True, name="att_qkv", **geom)
            sinks = att_sinks[j].astype(F32)
            a_p = _swa_prompt_call(proj, sinks, rel_bias, n_batch=n_batch, seq=seq, hd=hd)
            win = cache_swa_k.shape[2]
            ck = cache_swa_k[j].astype(F32).reshape(n_seq, win, nkv)
            cv = cache_swa_v[j].astype(F32).reshape(n_seq, win, nkv)
            a_s = _swa_sample_call(proj, ck, cv, sinks, rel_bias, n_prompt_rows=n_prompt_rows,
                                   n_seq=n_seq, dec_seq=dec_seq, hd=hd)
            kv_tail = jnp.stack([proj[(b + 1) * seq - WINDOW:(b + 1) * seq, nq:] for b in range(n_batch)]).astype(F32)
            kp_l.append(kv_tail[:, :, :nkv].reshape(n_batch, WINDOW, ATT_KV_HEADS, hd))
            vp_l.append(kv_tail[:, :, nkv:].reshape(n_batch, WINDOW, ATT_KV_HEADS, hd))
            kv_new = proj[n_prompt_rows:, nq:].reshape(n_seq, s, 2 * nkv)[:, :dec_seq].astype(F32)
            k_new, v_new = kv_new[:, :, :nkv], kv_new[:, :, nkv:]
            ks_l.append(jnp.concatenate([ck, k_new], axis=1)[:, -win:].reshape(n_seq, win, ATT_KV_HEADS, hd))
            vs_l.append(jnp.concatenate([cv, v_new], axis=1)[:, -win:].reshape(n_seq, win, ATT_KV_HEADS, hd))
            x = _linear_call(a_p, att_w_o, j, h_sample=a_s, n_out=d, tn=1024, out_dtype=F32, resid=resid1,
                             name="att_out", **geom)

        lane_pad = ROUTER_LANES - MOE_GROUPS - MOE_GROUPS * MOE_EPG
        router_w = jnp.pad(jnp.concatenate([moe_wg[l], moe_we[l]], axis=1).astype(F32), ((0, 0), (0, lane_pad)))
        router_hi = router_w.astype(BF16)
        router_w = jnp.stack([router_hi, (router_w - router_hi.astype(F32)).astype(BF16)])
        router_b = jnp.pad(jnp.concatenate([moe_bg[l], moe_be[l]]).astype(F32), (0, lane_pad))[None]
        h2, route, counts = _norm_call(x, mod4, modtok, norm=(gains, 2 * l + 1, l, 3, 4),
                                       router=(router_w, router_b), **geom)
        ypair = _moe_layer(h2, route, counts, moe_w_in, moe_w_out, l)
        if l + 1 < depth:
            x, h = _norm_call(x, mod4, modtok, resid=(ypair, route, l, 5),
                              norm=(gains, 2 * l + 2, l + 1, 0, 1), **geom)
        else:
            x_p, x_s = _norm_call(x, mod4, modtok, resid=(ypair, route, l, 5), split_out=True, **geom)

    y_prompt = x_p.reshape(n_batch, seq, d)
    y_sample = x_s.reshape(n_seq, s, d)[:, :dec_seq]
    return (y_prompt, y_sample, jnp.stack(ret_p), ret_s, jnp.stack(conv_p), jnp.stack(conv_s),
            jnp.stack(kp_l), jnp.stack(vp_l), jnp.stack(ks_l), jnp.stack(vs_l))
```
